```python
import math
import jax, jax.numpy as jnp
from jax import lax
import numpy as np

D_MODEL = 2048
BATCH = 16
SEQ = 256
DEPTH = 4
DEC_BATCH = 8
DEC_SEQ = 4096
PAST_LEN = 256

GRID_W = 64
N_MIXERS = 3
N_A = (DEPTH + 2) // N_MIXERS
N_B = (DEPTH + 1) // N_MIXERS
N_C = DEPTH // N_MIXERS
D_FF = -(-(8 * D_MODEL) // (3 * 256)) * 256
NORM_EPS = 1e-6

RW_HEAD = 64
RW_H = D_MODEL // RW_HEAD
RW_DECAY_LORA = max(32, int(round(1.8 * math.sqrt(D_MODEL) / 32)) * 32)
RW_AAA_LORA = RW_DECAY_LORA
RW_GATE_LORA = max(32, int(round(0.6 * D_MODEL ** 0.8 / 32)) * 32)
RW_LN_EPS = 64e-5

HG_K = 128
HG_H = D_MODEL // HG_K
HG_V = D_MODEL // HG_H
HG_CHUNK = 32

ML_H = 16
ML_NOPE = 128
ML_ROPE = 64
ML_V = 128
ML_Q_LORA = 512
ML_KV_LORA = 512
ML_SCALE = 1.0 / math.sqrt(ML_NOPE + ML_ROPE)
Q_BLOCK = 128
ROPE_BASE = 10000.0

F32 = jnp.float32

kernel_name = 'hybrid_rwkv7_hgrn2_mla_diffusion_step'


def _rmsnorm(x, w):
    xf = x.astype(F32)
    y = xf * lax.rsqrt(jnp.mean(xf * xf, axis=-1, keepdims=True) + NORM_EPS)
    return y.astype(x.dtype) * w


def _adaln(cond, w, b):
    m = jax.nn.silu(cond) @ w + b
    return jnp.split(m[:, None, :], 6, axis=-1)


def _swiglu(h, w_in, w_out):
    a, b = jnp.split(h @ w_in, 2, axis=-1)
    return (jax.nn.silu(a) * b) @ w_out


def _centred_shift(x):
    xp = jnp.pad(x, ((0, 0), (1, 1), (0, 0)))
    return 0.5 * (xp[:, :-2] + xp[:, 2:])


def _rev(ts):
    return tuple(jnp.flip(t, axis=1) for t in ts)


def _wkv7_scan(r, w, k, v, kk, kka, s0):
    def step(S, inp):
        r_t, w_t, k_t, v_t, kk_t, kka_t = inp
        sa = jnp.einsum('bhvk,bhk->bhv', S, kk_t)
        S = (S * w_t[:, :, None, :] - sa[..., None] * kka_t[:, :, None, :]
             + v_t[..., None] * k_t[:, :, None, :])
        return S, jnp.einsum('bhvk,bhk->bhv', S, r_t)
    xs = tuple(jnp.swapaxes(t, 0, 1) for t in (r, w, k, v, kk, kka))
    s_fin, ys = lax.scan(step, s0, xs)
    return jnp.swapaxes(ys, 0, 1), s_fin


def _rwkv7_mix(h, p, s0_f, s0_b):
    B, T, D = h.shape
    heads = lambda t: t.reshape(B, T, RW_H, RW_HEAD)
    xx = _centred_shift(h) - h
    xr, xw, xk, xv, xa, xg = (h + xx * p['mu'][i] for i in range(6))
    r = heads(xr @ p['wr'])
    k = xk @ p['wk']
    v = heads(xv @ p['wv'])
    g = jax.nn.sigmoid(xg @ p['g1']) @ p['g2']
    kk = heads(k * p['kk']).astype(F32)
    kk = kk * lax.rsqrt(jnp.sum(kk * kk, axis=-1, keepdims=True) + 1e-12)
    ys, bonuses, finals = [], [], []
    for d, s0 in enumerate((s0_f, s0_b)):
        wlog = -jax.nn.softplus(-(p['w0'][d] + jnp.tanh(xw @ p['w1'][d]) @ p['w2'][d]).astype(F32)) - 0.5
        decay = jnp.exp(-jnp.exp(wlog))
        a = jax.nn.sigmoid(p['a0'][d] + (xa @ p['a1'][d]) @ p['a2'][d])
        kd = heads(k * (1 + (a - 1) * p['ka']))
        ins = tuple(t.astype(F32) for t in (r, heads(decay), kd, v, kk, kk * heads(a).astype(F32)))
        if d == 1:
            ins = _rev(ins)
        y, s_fin = _wkv7_scan(*ins, s0.astype(F32))
        if d == 1:
            y = jnp.flip(y, axis=1)
        ys.append(y)
        bonuses.append(jnp.sum(r * kd * p['rk'][d], axis=-1, keepdims=True) * v)
        finals.append(s_fin)
    y = ys[0] + ys[1]
    mu = jnp.mean(y, axis=-1, keepdims=True)
    var = jnp.mean(jnp.square(y - mu), axis=-1, keepdims=True)
    yn = ((y - mu) * lax.rsqrt(var + RW_LN_EPS)).astype(h.dtype).reshape(B, T, D)
    yn = yn * p['lnx_w'] + p['lnx_b'] + (bonuses[0] + bonuses[1]).reshape(B, T, D)
    return (yn * g) @ p['wo'], finals[0], finals[1]


def _hgrn2_chunk_scan(q, k, gl, i, s0):
    B, T, H, K = q.shape
    V = i.shape[-1]
    C = HG_CHUNK
    n = T // C
    chunks = lambda t: t.reshape(B, n, C, H, t.shape[-1]).transpose(1, 0, 3, 2, 4)
    causal = jnp.tril(jnp.ones((C, C), dtype=bool))[:, :, None]

    def step(S, inp):
        qc, kc, gc, ic = inp
        G = jnp.cumsum(gc, axis=2)
        o = jnp.einsum('bhtk,bhkv->bhtv', qc * jnp.exp(G), S)
        dec = jnp.exp(jnp.where(causal, G[:, :, :, None, :] - G[:, :, None, :, :], -jnp.inf))
        A = jnp.einsum('bhtk,bhsk,bhtsk->bhts', qc, kc, dec)
        o = o + jnp.einsum('bhts,bhsv->bhtv', A, ic)
        G_end = G[:, :, -1:, :]
        S = (jnp.exp(G_end[:, :, 0, :, None]) * S
             + jnp.einsum('bhsk,bhsv->bhkv', kc * jnp.exp(G_end - G), ic))
        return S, o

    s_fin, o = lax.scan(step, s0, tuple(chunks(t) for t in (q, k, gl, i)))
    return o.transpose(1, 0, 3, 2, 4).reshape(B, T, H, V), s_fin


def _hgrn2_mix(h, p, lb, s0_f, s0_b):
    B, T, D = h.shape
    hk, hv = HG_H * HG_K, HG_H * HG_V
    q, f_fw, f_bw, iv, g = jnp.split(h @ p['w_in'], [hk, 2 * hk, 3 * hk, 3 * hk + hv], axis=-1)
    to_k = lambda t: t.reshape(B, T, HG_H, HG_K).astype(F32)
    to_v = lambda t: t.reshape(B, T, HG_H, HG_V).astype(F32)
    q = to_k(jax.nn.silu(q))
    iv = to_v(iv)
    outs, finals = [], []
    for d, (z, s0) in enumerate(((f_fw, s0_f), (f_bw, s0_b))):
        f = lb + (1 - lb) * jax.nn.sigmoid(to_k(z))
        ins = (q, 1 - f, jnp.log(f), iv)
        if d == 1:
            ins = _rev(ins)
        o, s_fin = _hgrn2_chunk_scan(*ins, s0.astype(F32))
        if d == 1:
            o = jnp.flip(o, axis=1)
        outs.append(o)
        finals.append(s_fin)
    o = outs[0] + outs[1]
    o = o * lax.rsqrt(jnp.mean(o * o, axis=-1, keepdims=True) + NORM_EPS) * p['norm_w'] * jax.nn.silu(to_v(g))
    return o.reshape(B, T, D).astype(h.dtype) @ p['wo'], finals[0], finals[1]


def _mla_project(h, p):
    B, T, _ = h.shape
    cq, ckv, kr = jnp.split(h @ p['w_down'], [ML_Q_LORA, ML_Q_LORA + ML_KV_LORA], axis=-1)
    q = (_rmsnorm(cq, p['qnorm_w']) @ p['w_uq']).reshape(B, T, ML_H, ML_NOPE + ML_ROPE)
    return q[..., :ML_NOPE], q[..., ML_NOPE:], _rmsnorm(ckv, p['kvnorm_w']), kr


def _mla_expand(ckv, p):
    B, L, _ = ckv.shape
    kv = (ckv @ p['w_ukv']).reshape(B, L, ML_H, ML_NOPE + ML_V)
    return kv[..., :ML_NOPE], kv[..., ML_NOPE:]


def _mla_attend(q_nope, q_rope, k_nope, k_rope, v):
    B, T, H, _ = q_nope.shape
    nb = T // Q_BLOCK
    blocks = lambda t: jnp.swapaxes(t.reshape(B, nb, Q_BLOCK, *t.shape[2:]), 0, 1)

    def one_block(qs):
        qn, qr = qs
        s = jnp.einsum('bqhd,bkhd->bhqk', qn, k_nope) + jnp.einsum('bqhd,bkd->bhqk', qr, k_rope)
        pr = jax.nn.softmax(s.astype(F32) * ML_SCALE, axis=-1).astype(v.dtype)
        return jnp.einsum('bhqk,bkhd->bqhd', pr, v)

    o = lax.map(one_block, (blocks(q_nope), blocks(q_rope)))
    return jnp.swapaxes(o, 0, 1).reshape(B, T, H * ML_V)


def _axial_angles(T):
    rows = T // GRID_W
    r = jnp.broadcast_to(jnp.arange(rows, dtype=F32)[:, None], (rows, GRID_W)).reshape(-1)
    col = jnp.broadcast_to(jnp.arange(GRID_W, dtype=F32)[None, :], (rows, GRID_W)).reshape(-1)
    nf = ML_ROPE // 4
    inv = ROPE_BASE ** (-jnp.arange(nf, dtype=F32) / nf)
    return r[:, None] * inv, col[:, None] * inv


def _rotate(x, ang):
    x1, x2 = jnp.split(x, 2, axis=-1)
    cs, sn = jnp.cos(ang), jnp.sin(ang)
    return jnp.concatenate([x1 * cs - x2 * sn, x2 * cs + x1 * sn], axis=-1)


def _axial_rope(x, ang_r, ang_c):
    xr, xc = jnp.split(x, 2, axis=-1)
    return jnp.concatenate([_rotate(xr, ang_r), _rotate(xc, ang_c)], axis=-1).astype(x.dtype)


def _mla_context(h, p):
    qn, qr, ckv, kr = _mla_project(h, p)
    kn, v = _mla_expand(ckv, p)
    return _mla_attend(qn, qr, kn, kr, v) @ p['wo'], ckv, kr


def _mla_latent(h, p, ckv_ctx, kr_ctx):
    qn, qr, ckv, kr = _mla_project(h, p)
    ang_r, ang_c = _axial_angles(h.shape[1])
    qr = _axial_rope(qr, ang_r[:, None, :], ang_c[:, None, :])
    kr = _axial_rope(kr, ang_r, ang_c)
    kn, v = _mla_expand(jnp.concatenate([ckv, ckv_ctx.astype(ckv.dtype)], axis=1), p)
    kr_all = jnp.concatenate([kr, kr_ctx.astype(kr.dtype)], axis=1)
    return _mla_attend(qn, qr, kn, kr_all, v) @ p['wo']


def setup_inputs(seed: int = 0) -> dict:
    key = jax.random.key(seed)
    ks = iter(jax.random.split(key, 64))
    nrm = lambda shape, scale=1.0: scale * jax.random.normal(next(ks), shape, F32)
    gain = lambda shape: 1.0 + nrm(shape, 0.02)
    D = D_MODEL
    LD, LA, LG = RW_DECAY_LORA, RW_AAA_LORA, RW_GATE_LORA
    inp = {}
    inp['x_prompt'] = nrm((BATCH, SEQ, D))
    inp['x_sample'] = nrm((DEC_BATCH, DEC_SEQ, D))
    inp['state_rwkv'] = nrm((DEC_BATCH, N_A, 2, RW_H, RW_HEAD, RW_HEAD), 0.5)
    inp['state_hgrn'] = nrm((DEC_BATCH, N_B, 2, HG_H, HG_K, HG_V), 0.5)
    inp['cache_ckv'] = nrm((DEC_BATCH, N_C, PAST_LEN, ML_KV_LORA))
    inp['cache_krope'] = nrm((DEC_BATCH, N_C, PAST_LEN, ML_ROPE))
    inp['c'] = nrm((DEC_BATCH, D))
    inp['c_ctx'] = nrm((D,))
    inp['ada_w'] = nrm((DEPTH, D, 6 * D), 0.5 * D ** -0.5)
    inp['ada_b'] = nrm((DEPTH, 6 * D), 0.02)
    inp['norm1_w'] = gain((DEPTH, D))
    inp['norm2_w'] = gain((DEPTH, D))
    inp['ffn_w_in'] = nrm((DEPTH, D, 2 * D_FF), D ** -0.5)
    inp['ffn_w_out'] = nrm((DEPTH, D_FF, D), D_FF ** -0.5)
    inp['final_norm_w'] = gain((D,))
    inp['rw_mu'] = jax.random.uniform(next(ks), (N_A, 6, D), F32)
    inp['rw_wr'] = nrm((N_A, D, D), D ** -0.5)
    inp['rw_wk'] = nrm((N_A, D, D), D ** -0.5)
    inp['rw_wv'] = nrm((N_A, D, D), D ** -0.5)
    inp['rw_wo'] = nrm((N_A, D, D), D ** -0.5)
    inp['rw_w0'] = nrm((N_A, 2, D), 0.3)
    inp['rw_w1'] = nrm((N_A, 2, D, LD), D ** -0.5)
    inp['rw_w2'] = nrm((N_A, 2, LD, D), 0.1 * LD ** -0.5)
    inp['rw_a0'] = nrm((N_A, 2, D), 0.3)
    inp['rw_a1'] = nrm((N_A, 2, D, LA), D ** -0.5)
    inp['rw_a2'] = nrm((N_A, 2, LA, D), 0.1 * LA ** -0.5)
    inp['rw_g1'] = nrm((N_A, D, LG), D ** -0.5)
    inp['rw_g2'] = nrm((N_A, LG, D), LG ** -0.5)
    inp['rw_kk'] = 0.85 + nrm((N_A, D), 0.02)
    inp['rw_ka'] = gain((N_A, D))
    inp['rw_rk'] = nrm((N_A, 2, RW_H, RW_HEAD), 0.1)
    inp['rw_lnx_w'] = gain((N_A, D))
    inp['rw_lnx_b'] = nrm((N_A, D), 0.02)
    inp['hg_w_in'] = nrm((N_B, D, 3 * HG_H * HG_K + 2 * HG_H * HG_V), D ** -0.5)
    inp['hg_lb'] = nrm((DEPTH, HG_H * HG_K), 0.5)
    inp['hg_norm_w'] = gain((N_B, HG_V))
    inp['hg_wo'] = nrm((N_B, HG_H * HG_V, D), (HG_H * HG_V) ** -0.5)
    inp['ml_w_down'] = nrm((N_C, D, ML_Q_LORA + ML_KV_LORA + ML_ROPE), D ** -0.5)
    inp['ml_qnorm_w'] = gain((N_C, ML_Q_LORA))
    inp['ml_kvnorm_w'] = gain((N_C, ML_KV_LORA))
    inp['ml_w_uq'] = nrm((N_C, ML_Q_LORA, ML_H * (ML_NOPE + ML_ROPE)), ML_Q_LORA ** -0.5)
    inp['ml_w_ukv'] = nrm((N_C, ML_KV_LORA, ML_H * (ML_NOPE + ML_V)), ML_KV_LORA ** -0.5)
    inp['ml_wo'] = nrm((N_C, ML_H * ML_V, D), (ML_H * ML_V) ** -0.5)
    return inp


def reference(x_prompt, x_sample, state_rwkv, state_hgrn, cache_ckv, cache_krope, c, c_ctx,
              ada_w, ada_b, norm1_w, norm2_w, ffn_w_in, ffn_w_out, final_norm_w,
              rw_mu, rw_wr, rw_wk, rw_wv, rw_wo, rw_w0, rw_w1, rw_w2, rw_a0, rw_a1, rw_a2,
              rw_g1, rw_g2, rw_kk, rw_ka, rw_rk, rw_lnx_w, rw_lnx_b,
              hg_w_in, hg_lb, hg_norm_w, hg_wo,
              ml_w_down, ml_qnorm_w, ml_kvnorm_w, ml_w_uq, ml_w_ukv, ml_wo):
    n_ctx = x_prompt.shape[0]
    lb_table = jnp.cumsum(jax.nn.softmax(hg_lb.astype(F32), axis=0), axis=0)
    lb_table = lb_table - lb_table[0]
    x_c, x_s = x_prompt, x_sample
    cond_ctx = c_ctx[None, :]
    new_rwkv, new_hgrn, new_ckv, new_krope = [], [], [], []
    for l in range(DEPTH):
        kind, j = l % N_MIXERS, l // N_MIXERS
        sh1c, sc1c, g1c, sh2c, sc2c, g2c = _adaln(cond_ctx, ada_w[l], ada_b[l])
        sh1s, sc1s, g1s, sh2s, sc2s, g2s = _adaln(c, ada_w[l], ada_b[l])
        h_c = _rmsnorm(x_c, norm1_w[l]) * (1 + sc1c) + sh1c
        h_s = _rmsnorm(x_s, norm1_w[l]) * (1 + sc1s) + sh1s
        if kind == 0:
            p = {'mu': rw_mu[j], 'wr': rw_wr[j], 'wk': rw_wk[j], 'wv': rw_wv[j], 'wo': rw_wo[j],
                 'w0': rw_w0[j], 'w1': rw_w1[j], 'w2': rw_w2[j], 'a0': rw_a0[j], 'a1': rw_a1[j],
                 'a2': rw_a2[j], 'g1': rw_g1[j], 'g2': rw_g2[j], 'kk': rw_kk[j], 'ka': rw_ka[j],
                 'rk': rw_rk[j], 'lnx_w': rw_lnx_w[j], 'lnx_b': rw_lnx_b[j]}
            zero = jnp.zeros((n_ctx, RW_H, RW_HEAD, RW_HEAD), F32)
            o_c, s_f, s_b = _rwkv7_mix(h_c, p, zero, zero)
            o_s, _, _ = _rwkv7_mix(h_s, p, state_rwkv[:, j, 0], state_rwkv[:, j, 1])
            new_rwkv.append(jnp.stack([s_f, s_b], axis=1))
        elif kind == 1:
            p = {'w_in': hg_w_in[j], 'norm_w': hg_norm_w[j], 'wo': hg_wo[j]}
            lb = lb_table[l].reshape(HG_H, HG_K)
            zero = jnp.zeros((n_ctx, HG_H, HG_K, HG_V), F32)
            o_c, s_f, s_b = _hgrn2_mix(h_c, p, lb, zero, zero)
            o_s, _, _ = _hgrn2_mix(h_s, p, lb, state_hgrn[:, j, 0], state_hgrn[:, j, 1])
            new_hgrn.append(jnp.stack([s_f, s_b], axis=1))
        else:
            p = {'w_down': ml_w_down[j], 'qnorm_w': ml_qnorm_w[j], 'kvnorm_w': ml_kvnorm_w[j],
                 'w_uq': ml_w_uq[j], 'w_ukv': ml_w_ukv[j], 'wo': ml_wo[j]}
            o_c, ckv_c, kr_c = _mla_context(h_c, p)
            o_s = _mla_latent(h_s, p, cache_ckv[:, j], cache_krope[:, j])
            new_ckv.append(ckv_c)
            new_krope.append(kr_c)
        x_c = x_c + g1c * o_c
        x_s = x_s + g1s * o_s
        x_c = x_c + g2c * _swiglu(_rmsnorm(x_c, norm2_w[l]) * (1 + sc2c) + sh2c, ffn_w_in[l], ffn_w_out[l])
        x_s = x_s + g2s * _swiglu(_rmsnorm(x_s, norm2_w[l]) * (1 + sc2s) + sh2s, ffn_w_in[l], ffn_w_out[l])
    y_prompt = _rmsnorm(x_c, final_norm_w)
    y_sample = _rmsnorm(x_s, final_norm_w)
    new_state_rwkv = jnp.stack(new_rwkv, axis=1)
    new_state_hgrn = jnp.stack(new_hgrn, axis=1)
    new_cache_ckv = jnp.stack(new_ckv, axis=1)
    new_cache_krope = jnp.stack(new_krope, axis=1)
    return (y_prompt, y_sample, new_state_rwkv, new_state_hgrn, new_cache_ckv, new_cache_krope)
```

```python
import functools
import math

import numpy as np
import jax
import jax.numpy as jnp
from jax import lax
from jax.experimental import pallas as pl
from jax.experimental.pallas import tpu as pltpu

F32 = jnp.float32
BF16 = jnp.bfloat16
HIGHEST = lax.Precision.HIGHEST

LANES_V7X = 128
SUBLANES_V7X = 8
VMEM_BYTES_V7X = 64 * 1024 * 1024
VMEM_LIMIT = 56 * 1024 * 1024

NORM_EPS = 1e-6
RW_HEAD = 64
RW_LN_EPS = 64e-5
RW_GROUP = 4
RW_LANES = RW_GROUP * RW_HEAD
RW_CHUNK = 64
HG_K = 128
HG_CHUNK_TOKENS = 64
ML_H = 16
ML_NOPE = 128
ML_ROPE = 64
ML_V = 128
ML_Q_LORA = 512
ML_KV_LORA = 512
GRID_W = 64
ROPE_BASE = 10000.0
LORA_PAD = 128


class Layout:
    def __init__(self, n_ctx, ctx_len, n_lat, lat_len):
        self.n_ctx, self.ctx_len, self.n_lat, self.lat_len = n_ctx, ctx_len, n_lat, lat_len
        self.nc = n_ctx * ctx_len
        self.n = self.nc + n_lat * lat_len
        self.tb = min(256, ctx_len)
        assert ctx_len % self.tb == 0 and lat_len % self.tb == 0 and self.tb % RW_CHUNK == 0
        self.nb = self.n // self.tb
        self.nb_ctx = self.nc // self.tb
        self.bps_ctx = ctx_len // self.tb
        self.bps_lat = lat_len // self.tb
        self.n_seq = n_ctx + n_lat

    def tile(self, want):
        t = want
        while self.nc % t or self.lat_len % t:
            t //= 2
        return t

    def cond_of_tile(self, i, tm):
        row = i * tm
        return jnp.where(row < self.nc, 0, 1 + (row - self.nc) // self.lat_len)

    def seq_info(self, blk):
        is_ctx = blk < self.nb_ctx
        lat = blk - self.nb_ctx
        seq = jnp.where(is_ctx, blk // self.bps_ctx, self.n_ctx + lat // self.bps_lat)
        pos = jnp.where(is_ctx, blk % self.bps_ctx, lat % self.bps_lat)
        cnt = jnp.where(is_ctx, self.bps_ctx, self.bps_lat)
        return seq, pos, cnt


def _cparams(n_axes):
    return pltpu.CompilerParams(dimension_semantics=("arbitrary",) * n_axes, vmem_limit_bytes=VMEM_LIMIT)


def _bdot(a, b):
    return jnp.dot(a.astype(BF16), b.astype(BF16), preferred_element_type=F32)


def _bdot_nt(a, b):
    return lax.dot_general(a.astype(BF16), b.astype(BF16), (((1,), (1,)), ((), ())),
                           preferred_element_type=F32)


def _xdot(a, b):
    return jnp.dot(a, b, preferred_element_type=F32, precision=HIGHEST)


def _silu(x):
    return x * jax.nn.sigmoid(x)


def _adaln_kernel(c_ref, w_ref, b_ref, o_ref):
    a = _silu(c_ref[...]).astype(BF16)
    o_ref[...] = jnp.dot(a, w_ref[...].astype(BF16), preferred_element_type=F32) + b_ref[...]


def adaln(cond, ada_w, ada_b):
    depth, d, d6 = ada_w.shape
    r = cond.shape[0]
    tn = 1024
    return pl.pallas_call(
        _adaln_kernel,
        grid=(depth, d6 // tn),
        in_specs=[pl.BlockSpec((r, d), lambda l, j: (0, 0)),
                  pl.BlockSpec((None, d, tn), lambda l, j: (l, 0, j)),
                  pl.BlockSpec((None, 1, tn), lambda l, j: (l, 0, j))],
        out_specs=pl.BlockSpec((None, r, tn), lambda l, j: (l, 0, j)),
        out_shape=jax.ShapeDtypeStruct((depth, r, d6), F32),
        compiler_params=_cparams(2), name="adaln",
    )(cond, ada_w, ada_b.reshape(depth, 1, d6))


def _norm_mod(x, nw, sc, sh):
    y = x * lax.rsqrt(jnp.mean(x * x, axis=-1, keepdims=True) + NORM_EPS)
    return (y * nw) * (1.0 + sc) + sh


def _norm_mod_kernel(x_ref, nw_ref, sc_ref, sh_ref, o_ref):
    o_ref[...] = _norm_mod(x_ref[...], nw_ref[...], sc_ref[...], sh_ref[...]).astype(o_ref.dtype)


def norm_mod(lay, x, nw, sc, sh, out_dtype):
    n, d = x.shape
    tm = lay.tile(512)
    cmap = lambda i: (lay.cond_of_tile(i, tm), 0, 0)
    return pl.pallas_call(
        _norm_mod_kernel,
        grid=(n // tm,),
        in_specs=[pl.BlockSpec((tm, d), lambda i: (i, 0)),
                  pl.BlockSpec((1, d), lambda i: (0, 0)),
                  pl.BlockSpec((None, 1, d), cmap),
                  pl.BlockSpec((None, 1, d), cmap)],
        out_specs=pl.BlockSpec((tm, d), lambda i: (i, 0)),
        out_shape=jax.ShapeDtypeStruct((n, d), out_dtype),
        compiler_params=_cparams(1), name="norm_mod",
    )(x, nw.reshape(1, d), sc, sh)


def _mm_kernel(*refs, n_w, n_e, epi):
    a = refs[0][...]
    accs = [jnp.dot(a, refs[1 + i][...], preferred_element_type=F32) for i in range(n_w)]
    extras = [refs[1 + n_w + i][...] for i in range(n_e)]
    outs = epi(accs, extras)
    o_refs = refs[1 + n_w + n_e:]
    for o_ref, val in zip(o_refs, outs):
        o_ref[...] = val.astype(o_ref.dtype)


def matmul(a, ws, epi, out_dtypes, *, tm, tn, extras=(), name):
    m, k = a.shape
    nw = ws[0].shape[1]
    assert m % tm == 0 and nw % tn == 0
    in_specs = [pl.BlockSpec((tm, k), lambda i, j: (i, 0))]
    in_specs += [pl.BlockSpec((k, tn), lambda i, j: (0, j)) for _ in ws]
    in_specs += [spec for _, spec in extras]
    outs = pl.pallas_call(
        functools.partial(_mm_kernel, n_w=len(ws), n_e=len(extras), epi=epi),
        grid=(m // tm, nw // tn),
        in_specs=in_specs,
        out_specs=[pl.BlockSpec((tm, tn), lambda i, j: (i, j)) for _ in out_dtypes],
        out_shape=[jax.ShapeDtypeStruct((m, nw), dt) for dt in out_dtypes],
        compiler_params=_cparams(2), name=name,
    )(a, *ws, *[arr for arr, _ in extras])
    return outs


def _epi_plain(accs, extras):
    return accs


def _epi_gated_residual(accs, extras):
    x, g = extras
    return [x + g * accs[0]]


def matmul_gated_residual(lay, a, w, x, gate, *, tm, tn, name):
    extras = ((x, pl.BlockSpec((tm, tn), lambda i, j: (i, j))),
              (gate, pl.BlockSpec((None, 1, tn), lambda i, j: (lay.cond_of_tile(i, tm), 0, j))))
    return matmul(a, [w], _epi_gated_residual, [F32], tm=tm, tn=tn, extras=extras, name=name)[0]


def _epi_swiglu(accs, extras):
    return [_silu(accs[0]) * accs[1]]


def _rwkv_prep_kernel(x_ref, xp_ref, xn_ref, nw_ref, sc_ref, sh_ref, mu_ref, *o_refs, lay):
    i = pl.program_id(0)
    _, pos, cnt = lay.seq_info(i)
    nw, sc, sh = nw_ref[...], sc_ref[...], sh_ref[...]
    h = _norm_mod(x_ref[...], nw, sc, sh)
    hp = _norm_mod(xp_ref[...], nw, sc, sh)[SUBLANES_V7X - 1:SUBLANES_V7X]
    hn = _norm_mod(xn_ref[...], nw, sc, sh)[0:1]
    hp = jnp.where(pos == 0, 0.0, hp)
    hn = jnp.where(pos == cnt - 1, 0.0, hn)
    tb = h.shape[0]
    row = lax.broadcasted_iota(jnp.int32, h.shape, 0)
    prev = jnp.where(row == 0, hp, pltpu.roll(h, 1, axis=0))
    nxt = jnp.where(row == tb - 1, hn, pltpu.roll(h, tb - 1, axis=0))
    xx = 0.5 * (prev + nxt) - h
    for idx, o_ref in enumerate(o_refs):
        o_ref[...] = (h + xx * mu_ref[idx:idx + 1, :]).astype(o_ref.dtype)


def rwkv_prep(lay, x, nw, sc, sh, mu):
    n, d = x.shape
    tb = lay.tb
    r8 = tb // SUBLANES_V7X
    last8 = n // SUBLANES_V7X - 1
    cmap = lambda i: (lay.cond_of_tile(i, tb), 0, 0)
    return pl.pallas_call(
        functools.partial(_rwkv_prep_kernel, lay=lay),
        grid=(n // tb,),
        in_specs=[pl.BlockSpec((tb, d), lambda i: (i, 0)),
                  pl.BlockSpec((SUBLANES_V7X, d), lambda i: (jnp.maximum(i * r8 - 1, 0), 0)),
                  pl.BlockSpec((SUBLANES_V7X, d), lambda i: (jnp.minimum((i + 1) * r8, last8), 0)),
                  pl.BlockSpec((1, d), lambda i: (0, 0)),
                  pl.BlockSpec((None, 1, d), cmap),
                  pl.BlockSpec((None, 1, d), cmap),
                  pl.BlockSpec((6, d), lambda i: (0, 0))],
        out_specs=[pl.BlockSpec((tb, d), lambda i: (i, 0))] * 6,
        out_shape=[jax.ShapeDtypeStruct((n, d), BF16)] * 6,
        compiler_params=_cparams(1), name="rwkv_prep",
    )(x, x, x, nw.reshape(1, d), sc, sh, mu)


def _softplus(z):
    return jnp.maximum(z, 0.0) + jnp.log(1.0 + jnp.exp(-jnp.abs(z)))


def _rwkv_lora_kernel(xw_ref, xa_ref, xg_ref, w1_ref, w2_ref, w0_ref, a1_ref, a2_ref, a0_ref,
                      g1_ref, g2_ref, lw_ref, a_ref, g_ref):
    xw, xa, xg = xw_ref[...], xa_ref[...], xg_ref[...]
    for d in range(2):
        t = jnp.tanh(jnp.dot(xw, w1_ref[d], preferred_element_type=F32))
        wl = w0_ref[d] + _bdot(t, w2_ref[d])
        wlog = -_softplus(-wl) - 0.5
        lw_ref[d] = -jnp.exp(wlog)
        t = jnp.dot(xa, a1_ref[d], preferred_element_type=F32)
        a_ref[d] = jax.nn.sigmoid(a0_ref[d] + _bdot(t, a2_ref[d]))
    t = jax.nn.sigmoid(jnp.dot(xg, g1_ref[...], preferred_element_type=F32))
    g_ref[...] = _bdot(t, g2_ref[...])


def rwkv_lora(lay, xw, xa, xg, w1, w2, w0, a1, a2, a0, g1, g2):
    n, d = xw.shape
    tm = lay.tile(128)
    full = lambda arr: pl.BlockSpec(arr.shape, lambda i: (0,) * arr.ndim)
    row = pl.BlockSpec((tm, d), lambda i: (i, 0))
    return pl.pallas_call(
        _rwkv_lora_kernel,
        grid=(n // tm,),
        in_specs=[row, row, row] + [full(t) for t in (w1, w2, w0, a1, a2, a0, g1, g2)],
        out_specs=[pl.BlockSpec((2, tm, d), lambda i: (0, i, 0)),
                   pl.BlockSpec((2, tm, d), lambda i: (0, i, 0)),
                   row],
        out_shape=[jax.ShapeDtypeStruct((2, n, d), F32), jax.ShapeDtypeStruct((2, n, d), F32),
                   jax.ShapeDtypeStruct((n, d), F32)],
        compiler_params=_cparams(1), name="rwkv_lora",
    )(xw, xa, xg, w1, w2, w0, a1, a2, a0, g1, g2)


def _wkv_constants():
    c, g, hd = RW_CHUNK, RW_GROUP, RW_HEAD
    gc, lanes = g * c, g * hd
    t = np.arange(c)
    cum = np.stack([(t[None, :] <= t[:, None]), (t[None, :] >= t[:, None])]).astype(np.float32)
    r2 = np.arange(2 * gc)[:, None]
    c2 = np.arange(2 * gc)[None, :]
    tr, tc = r2 % c, c2 % c
    masks = []
    for rev in (False, True):
        strict = (tc > tr) if rev else (tc < tr)
        incl = (tc >= tr) if rev else (tc <= tr)
        masks.append(np.where(r2 < gc, strict, incl))
    tri = np.stack(masks).astype(np.float32)
    head_rows = (np.arange(gc)[:, None] // c == np.arange(lanes)[None, :] // hd).astype(np.float32)
    bd = (np.arange(lanes)[:, None] // hd == np.arange(lanes)[None, :] // hd).astype(np.float32)
    eye = np.eye(gc, dtype=np.float32)
    return tuple(jnp.asarray(x) for x in (cum, tri, head_rows, bd, eye))


def _wkv_chunk(S, r, k, v, a, lw, kkw, kaw, rk, cum_m, tri_m, hm, bd, eye):
    c, g = RW_CHUNK, RW_GROUP
    gc = g * c
    kk = k * kkw
    kk = kk * lax.rsqrt(_xdot(kk * kk, bd) + 1e-12)
    kd = k * (1.0 + (a - 1.0) * kaw)
    kka = kk * a
    bonus = _xdot(r * kd * rk, bd) * v
    cum = _xdot(cum_m, lw)
    tot = jnp.sum(lw, axis=0, keepdims=True)
    kkq = kk * jnp.exp(cum - lw)
    rq = r * jnp.exp(cum)
    winv = jnp.exp(-cum)
    kdh, kkah = kd * winv, kka * winv
    wrest = jnp.exp(tot - cum)
    kdw, kkaw = kd * wrest, kka * wrest

    def stack(x):
        return jnp.where(hm > 0, jnp.concatenate([x] * g, axis=0), 0.0)

    def fold(x):
        out = x[0:c]
        for gi in range(1, g):
            out = out + x[gi * c:(gi + 1) * c]
        return out

    qs = jnp.concatenate([stack(kkq), stack(rq)], axis=0)
    ks = jnp.concatenate([stack(kdh), stack(kkah)], axis=0)
    sc = jnp.where(tri_m > 0, _bdot_nt(qs, ks), 0.0)
    l_d, l_a = sc[:gc, :gc], sc[:gc, gc:]
    a_d, a_a = sc[gc:, :gc], sc[gc:, gc:]
    nj = -l_a
    t_inv = eye + nj
    for _ in range(int(math.log2(c)) - 1):
        nj = _bdot(nj, nj)
        t_inv = t_inv + _bdot(t_inv, nj)
    p0 = _bdot_nt(jnp.concatenate([kkq, rq], axis=0), S)
    vbd = stack(v)
    rhs = stack(p0[:c]) + _bdot(l_d, vbd)
    ubd = _bdot(t_inv, rhs)
    u = fold(ubd)
    y = p0[c:] + _bdot(jnp.concatenate([fold(a_d), -fold(a_a)], axis=1),
                       jnp.concatenate([vbd, ubd], axis=0))
    upd = _bdot(jnp.concatenate([v, u], axis=0).T, jnp.concatenate([kdw, -kkaw], axis=0))
    s_new = S * jnp.exp(tot) + jnp.where(bd > 0, upd, 0.0)
    return s_new, y, bonus


def _wkv_kernel(r_ref, k_ref, v_ref, a_ref, lw_ref, kkw_ref, kaw_ref, rk_ref, s0_ref,
                cum_ref, tri_ref, hm_ref, bd_ref, eye_ref, y_ref, b_ref, sfin_ref, s_scr, *, lay):
    d = pl.program_id(0)
    j = pl.program_id(2)
    blk = jnp.where(d == 0, j, lay.nb - 1 - j)
    _, pos, cnt = lay.seq_info(blk)
    first = pos == jnp.where(d == 0, 0, cnt - 1)
    last = pos == jnp.where(d == 0, cnt - 1, 0)

    @pl.when(first)
    def _():
        s_scr[...] = s0_ref[...]

    n_chunks = lay.tb // RW_CHUNK
    consts = (kkw_ref[...], kaw_ref[...], rk_ref[...], cum_ref[...], tri_ref[...], hm_ref[...],
              bd_ref[...], eye_ref[...])

    def body(ci, carry):
        cc = jnp.where(d == 0, ci, n_chunks - 1 - ci)
        sl = pl.ds(pl.multiple_of(cc * RW_CHUNK, RW_CHUNK), RW_CHUNK)
        s_new, y, bonus = _wkv_chunk(s_scr[...], r_ref[sl, :], k_ref[sl, :], v_ref[sl, :],
                                     a_ref[sl, :], lw_ref[sl, :], *consts)
        s_scr[...] = s_new
        y_ref[sl, :] = y
        b_ref[sl, :] = bonus
        return carry

    lax.fori_loop(0, n_chunks, body, 0)

    @pl.when(last)
    def _():
        sfin_ref[...] = s_scr[...]


def wkv(lay, r, k, v, a, lw, kkw, kaw, rk, s0):
    n, dm = r.shape
    tb, lanes = lay.tb, RW_LANES
    ng = dm // lanes
    consts = _wkv_constants()

    def blk_of(d, j):
        return jnp.where(d == 0, j, lay.nb - 1 - j)

    tok = pl.BlockSpec((tb, lanes), lambda d, g, j: (blk_of(d, j), g))
    tok2 = pl.BlockSpec((None, tb, lanes), lambda d, g, j: (d, blk_of(d, j), g))
    par = pl.BlockSpec((1, lanes), lambda d, g, j: (0, g))

    def s0_map(d, g, j):
        seq, _, _ = lay.seq_info(blk_of(d, j))
        return (jnp.maximum(seq - lay.n_ctx + 1, 0), d, g, 0, 0)

    def sfin_map(d, g, j):
        seq, _, _ = lay.seq_info(blk_of(d, j))
        return (seq, d, g, 0, 0)

    cum_m, tri_m, hm, bd, eye = consts
    return pl.pallas_call(
        functools.partial(_wkv_kernel, lay=lay),
        grid=(2, ng, lay.nb),
        in_specs=[tok, tok, tok, tok2, tok2, par, par,
                  pl.BlockSpec((None, 1, lanes), lambda d, g, j: (d, 0, g)),
                  pl.BlockSpec((None, None, None, lanes, lanes), s0_map),
                  pl.BlockSpec((None,) + cum_m.shape[1:], lambda d, g, j: (d, 0, 0)),
                  pl.BlockSpec((None,) + tri_m.shape[1:], lambda d, g, j: (d, 0, 0)),
                  pl.BlockSpec(hm.shape, lambda d, g, j: (0, 0)),
                  pl.BlockSpec(bd.shape, lambda d, g, j: (0, 0)),
                  pl.BlockSpec(eye.shape, lambda d, g, j: (0, 0))],
        out_specs=[tok2, tok2, pl.BlockSpec((None, None, None, lanes, lanes), sfin_map)],
        out_shape=[jax.ShapeDtypeStruct((2, n, dm), F32), jax.ShapeDtypeStruct((2, n, dm), F32),
                   jax.ShapeDtypeStruct((lay.n_seq, 2, ng, lanes, lanes), F32)],
        scratch_shapes=[pltpu.VMEM((lanes, lanes), F32)],
        compiler_params=_cparams(3), name="wkv",
    )(r, k, v, a, lw, kkw, kaw, rk, s0, cum_m, tri_m, hm, bd, eye)


def _rwkv_post_kernel(y_ref, b_ref, g_ref, lnw_ref, lnb_ref, bd_ref, o_ref):
    y = y_ref[0] + y_ref[1]
    bd = bd_ref[...]
    inv = 1.0 / RW_HEAD
    mu = _xdot(y, bd) * inv
    yc = y - mu
    var = _xdot(yc * yc, bd) * inv
    yn = yc * lax.rsqrt(var + RW_LN_EPS)
    out = yn * lnw_ref[...] + lnb_ref[...] + (b_ref[0] + b_ref[1])
    o_ref[...] = (out * g_ref[...]).astype(o_ref.dtype)


def rwkv_post(lay, y, bonus, g, lnw, lnb):
    _, n, dm = y.shape
    lanes = RW_LANES
    tm = lay.tile(512)
    bd = _wkv_constants()[3]
    two = pl.BlockSpec((2, tm, lanes), lambda i, c: (0, i, c))
    par = pl.BlockSpec((1, lanes), lambda i, c: (0, c))
    return pl.pallas_call(
        _rwkv_post_kernel,
        grid=(n // tm, dm // lanes),
        in_specs=[two, two, pl.BlockSpec((tm, lanes), lambda i, c: (i, c)), par, par,
                  pl.BlockSpec(bd.shape, lambda i, c: (0, 0))],
        out_specs=pl.BlockSpec((tm, lanes), lambda i, c: (i, c)),
        out_shape=jax.ShapeDtypeStruct((n, dm), BF16),
        compiler_params=_cparams(2), name="rwkv_post",
    )(y, bonus, g, lnw, lnb, bd)


def rwkv_layer(lay, x, mods, nw, p, s0):
    n, d = x.shape
    sh1, sc1, g1 = mods[0], mods[1], mods[2]
    xr, xw, xk, xv, xa, xg = rwkv_prep(lay, x, nw, sc1, sh1, p['mu'])
    tm = lay.tile(1024)
    r = matmul(xr, [p['wr']], _epi_plain, [F32], tm=tm, tn=512, name="rwkv_r")[0]
    k = matmul(xk, [p['wk']], _epi_plain, [F32], tm=tm, tn=512, name="rwkv_k")[0]
    v = matmul(xv, [p['wv']], _epi_plain, [F32], tm=tm, tn=512, name="rwkv_v")[0]
    lw, a, g = rwkv_lora(lay, xw, xa, xg, p['w1'], p['w2'], p['w0'], p['a1'], p['a2'], p['a0'],
                         p['g1'], p['g2'])
    y, bonus, sfin = wkv(lay, r, k, v, a, lw, p['kk'], p['ka'], p['rk'], s0)
    z = rwkv_post(lay, y, bonus, g, p['lnx_w'], p['lnx_b'])
    x = matmul_gated_residual(lay, z, p['wo'], x, g1, tm=tm, tn=512, name="rwkv_o")
    return x, sfin


def _hgrn_constants():
    c = HG_CHUNK_TOKENS
    t = np.arange(c)[:, None]
    j = np.arange(c)[None, :]
    mats_all, masks_all = [], []
    for rev in (False, True):
        mats, masks = [], []
        h = 1
        while h < c:
            mid = (t // (2 * h)) * 2 * h + h
            upper = (t % (2 * h)) >= h
            same = (t // (2 * h)) == (j // (2 * h))
            if not rev:
                m = np.where(upper, (j >= mid) & (j <= t), (j > t) & (j <= mid - 1))
                mask = same & upper & ((j % (2 * h)) < h)
            else:
                m = np.where(upper, (j >= mid) & (j < t), (j >= t) & (j <= mid - 1))
                mask = same & (~upper) & ((j % (2 * h)) >= h)
            mats.append(m)
            masks.append(mask)
            h *= 2
        mats.append((j >= t) if rev else (j <= t))
        masks.append(t == j)
        mats_all.append(np.concatenate(mats, 0))
        masks_all.append(np.stack(masks, 0))
    return (jnp.asarray(np.stack(mats_all).astype(np.float32)),
            jnp.asarray(np.stack(masks_all).astype(np.float32)))


def _hgrn_chunk(St, q, f, iv, lvl_m, masks):
    c = HG_CHUNK_TOKENS
    nlev = masks.shape[0] - 1
    g = jnp.log(f)
    k = 1.0 - f
    x = _xdot(lvl_m, g)
    e = jnp.exp(x)
    a = jnp.where(masks[nlev] > 0, _bdot_nt(q, k), 0.0)
    for lv in range(nlev):
        el = e[lv * c:(lv + 1) * c]
        a = a + jnp.where(masks[lv] > 0, _bdot_nt(q * el, k * el), 0.0)
    gc = x[nlev * c:]
    tot = jnp.sum(g, axis=0, keepdims=True)
    o = _bdot_nt(q * e[nlev * c:], St) + _bdot(a, iv)
    s_new = St * jnp.exp(tot) + _bdot(iv.T, k * jnp.exp(tot - gc))
    return s_new, o


def _hgrn_kernel(q_ref, f_ref, i_ref, s0_ref, lvl_ref, mask_ref, o_ref, sfin_ref, s_scr, *, lay):
    d = pl.program_id(0)
    j = pl.program_id(2)
    blk = jnp.where(d == 0, j, lay.nb - 1 - j)
    _, pos, cnt = lay.seq_info(blk)
    first = pos == jnp.where(d == 0, 0, cnt - 1)
    last = pos == jnp.where(d == 0, cnt - 1, 0)

    @pl.when(first)
    def _():
        s_scr[...] = s0_ref[...]

    n_chunks = lay.tb // HG_CHUNK_TOKENS
    lvl_m, masks = lvl_ref[...], mask_ref[...]

    def body(ci, carry):
        cc = jnp.where(d == 0, ci, n_chunks - 1 - ci)
        sl = pl.ds(pl.multiple_of(cc * HG_CHUNK_TOKENS, HG_CHUNK_TOKENS), HG_CHUNK_TOKENS)
        s_new, o = _hgrn_chunk(s_scr[...], q_ref[sl, :], f_ref[sl, :], i_ref[sl, :], lvl_m, masks)
        s_scr[...] = s_new
        o_ref[sl, :] = o
        return carry

    lax.fori_loop(0, n_chunks, body, 0)

    @pl.when(last)
    def _():
        sfin_ref[...] = s_scr[...]


def hgrn_scan(lay, q, f, iv, s0t):
    n, dm = q.shape
    tb = lay.tb
    nh = dm // HG_K
    lvl_m, masks = _hgrn_constants()

    def blk_of(d, j):
        return jnp.where(d == 0, j, lay.nb - 1 - j)

    tok = pl.BlockSpec((tb, HG_K), lambda d, h, j: (blk_of(d, j), h))
    tok2 = pl.BlockSpec((None, tb, HG_K), lambda d, h, j: (d, blk_of(d, j), h))

    def s0_map(d, h, j):
        seq, _, _ = lay.seq_info(blk_of(d, j))
        return (jnp.maximum(seq - lay.n_ctx + 1, 0), d, h, 0, 0)

    def sfin_map(d, h, j):
        seq, _, _ = lay.seq_info(blk_of(d, j))
        return (seq, d, h, 0, 0)

    st = pl.BlockSpec((None, None, None, HG_K, HG_K), s0_map)
    return pl.pallas_call(
        functools.partial(_hgrn_kernel, lay=lay),
        grid=(2, nh, lay.nb),
        in_specs=[tok, tok2, tok, st,
                  pl.BlockSpec((None,) + lvl_m.shape[1:], lambda d, h, j: (d, 0, 0)),
                  pl.BlockSpec((None,) + masks.shape[1:], lambda d, h, j: (d, 0, 0, 0))],
        out_specs=[tok2, pl.BlockSpec((None, None, None, HG_K, HG_K), sfin_map)],
        out_shape=[jax.ShapeDtypeStruct((2, n, dm), F32),
                   jax.ShapeDtypeStruct((lay.n_seq, 2, nh, HG_K, HG_K), F32)],
        scratch_shapes=[pltpu.VMEM((HG_K, HG_K), F32)],
        compiler_params=_cparams(3), name="hgrn_scan",
    )(q, f, iv, s0t, lvl_m, masks)


def _hgrn_post_kernel(o_ref, g_ref, nw_ref, z_ref):
    o = o_ref[0] + o_ref[1]
    o = o * lax.rsqrt(jnp.mean(o * o, axis=-1, keepdims=True) + NORM_EPS) * nw_ref[...] * g_ref[...]
    z_ref[...] = o.astype(z_ref.dtype)


def hgrn_post(lay, o, gs, nw):
    _, n, dm = o.shape
    tm = lay.tile(1024)
    return pl.pallas_call(
        _hgrn_post_kernel,
        grid=(n // tm, dm // HG_K),
        in_specs=[pl.BlockSpec((2, tm, HG_K), lambda i, h: (0, i, h)),
                  pl.BlockSpec((tm, HG_K), lambda i, h: (i, h)),
                  pl.BlockSpec((1, HG_K), lambda i, h: (0, 0))],
        out_specs=pl.BlockSpec((tm, HG_K), lambda i, h: (i, h)),
        out_shape=jax.ShapeDtypeStruct((n, dm), BF16),
        compiler_params=_cparams(2), name="hgrn_post",
    )(o, gs, nw)


def _epi_hgrn_in(accs, extras):
    lb = extras[0]
    q = _silu(accs[0])
    f0 = lb + (1.0 - lb) * jax.nn.sigmoid(accs[1])
    f1 = lb + (1.0 - lb) * jax.nn.sigmoid(accs[2])
    return [q, f0, f1, accs[3], _silu(accs[4])]


def hgrn_layer(lay, x, mods, nw, p, lb, s0t):
    n, d = x.shape
    sh1, sc1, g1 = mods[0], mods[1], mods[2]
    h = norm_mod(lay, x, nw, sc1, sh1, BF16)
    tm, tn = lay.tile(512), 256
    extras = ((lb, pl.BlockSpec((1, tn), lambda i, j: (0, j))),)
    q, f0, f1, iv, gs = matmul(h, p['w_in'], _epi_hgrn_in, [F32] * 5, tm=tm, tn=tn, extras=extras,
                               name="hgrn_in")
    o, sfin = hgrn_scan(lay, q, jnp.stack([f0, f1]), iv, s0t)
    z = hgrn_post(lay, o, gs, p['norm_w'])
    x = matmul_gated_residual(lay, z, p['wo'], x, g1, tm=lay.tile(1024), tn=512, name="hgrn_o")
    return x, sfin


ML_DOWN_COLS = 1280
ML_KR_OFF = ML_Q_LORA + ML_KV_LORA
ML_KRS_OFF = ML_KR_OFF + LANES_V7X


def _rms(x, w):
    return x * lax.rsqrt(jnp.mean(x * x, axis=-1, keepdims=True) + NORM_EPS) * w


def _mla_mid_kernel(dn_ref, qw_ref, kvw_ref, cos_ref, sin_ref, qn_ref, ckv_ref, kr_ref):
    dn = dn_ref[...]
    qn_ref[...] = _rms(dn[:, :ML_Q_LORA], qw_ref[...]).astype(qn_ref.dtype)
    ckv_ref[...] = _rms(dn[:, ML_Q_LORA:ML_KR_OFF], kvw_ref[...])
    kr = dn[:, ML_KR_OFF:ML_KR_OFF + ML_ROPE]
    krs = dn[:, ML_KRS_OFF:ML_KRS_OFF + ML_ROPE]
    kr_ref[...] = kr * cos_ref[...] + krs * sin_ref[...]


def mla_mid(lay, dn, qw, kvw, cos, sin):
    n = dn.shape[0]
    tm = lay.tile(512)
    return pl.pallas_call(
        _mla_mid_kernel,
        grid=(n // tm,),
        in_specs=[pl.BlockSpec((tm, ML_DOWN_COLS), lambda i: (i, 0)),
                  pl.BlockSpec((1, ML_Q_LORA), lambda i: (0, 0)),
                  pl.BlockSpec((1, ML_KV_LORA), lambda i: (0, 0)),
                  pl.BlockSpec((tm, ML_ROPE), lambda i: (i, 0)),
                  pl.BlockSpec((tm, ML_ROPE), lambda i: (i, 0))],
        out_specs=[pl.BlockSpec((tm, ML_Q_LORA), lambda i: (i, 0)),
                   pl.BlockSpec((tm, ML_KV_LORA), lambda i: (i, 0)),
                   pl.BlockSpec((tm, ML_ROPE), lambda i: (i, 0))],
        out_shape=[jax.ShapeDtypeStruct((n, ML_Q_LORA), BF16),
                   jax.ShapeDtypeStruct((n, ML_KV_LORA), F32),
                   jax.ShapeDtypeStruct((n, ML_ROPE), F32)],
        compiler_params=_cparams(1), name="mla_mid",
    )(dn, qw, kvw, cos, sin)


def _epi_rope(accs, extras):
    cos, sin = extras
    return [accs[0] * cos + accs[1] * sin]


def _attn_kernel(qn_ref, qr_ref, kn_ref, kr_ref, v_ref, o_ref, *, scale):
    kr = kr_ref[...]
    outs = []
    for h in range(2):
        qn = qn_ref[:, h * ML_NOPE:(h + 1) * ML_NOPE]
        qr = qr_ref[:, h * ML_ROPE:(h + 1) * ML_ROPE]
        kn = kn_ref[:, h * ML_NOPE:(h + 1) * ML_NOPE]
        s = (lax.dot_general(qn, kn, (((1,), (1,)), ((), ())), preferred_element_type=F32)
             + lax.dot_general(qr, kr, (((1,), (1,)), ((), ())), preferred_element_type=F32)) * scale
        m = jnp.max(s, axis=-1, keepdims=True)
        p = jnp.exp(s - m)
        l = jnp.sum(p, axis=-1, keepdims=True)
        pr = (p / l).astype(BF16)
        outs.append(jnp.dot(pr, v_ref[:, h * ML_V:(h + 1) * ML_V], preferred_element_type=F32))
    o_ref[...] = jnp.concatenate(outs, axis=1).astype(o_ref.dtype)


def attention(qn, qr, kn, kr, v, *, n_seq, q_len, k_len, row0, tq):
    heads2 = qn.shape[1] // (2 * ML_NOPE)
    qb = q_len // tq
    rb0 = row0 // tq
    scale = 1.0 / math.sqrt(ML_NOPE + ML_ROPE)
    return pl.pallas_call(
        functools.partial(_attn_kernel, scale=scale),
        grid=(n_seq, heads2, qb),
        in_specs=[pl.BlockSpec((tq, 2 * ML_NOPE), lambda s, h, i: (rb0 + s * qb + i, h)),
                  pl.BlockSpec((tq, 2 * ML_ROPE), lambda s, h, i: (rb0 + s * qb + i, h)),
                  pl.BlockSpec((k_len, 2 * ML_NOPE), lambda s, h, i: (s, h)),
                  pl.BlockSpec((k_len, ML_ROPE), lambda s, h, i: (s, 0)),
                  pl.BlockSpec((k_len, 2 * ML_V), lambda s, h, i: (s, h))],
        out_specs=pl.BlockSpec((tq, 2 * ML_V), lambda s, h, i: (s * qb + i, h)),
        out_shape=jax.ShapeDtypeStruct((n_seq * q_len, heads2 * 2 * ML_V), BF16),
        compiler_params=_cparams(3), name="mla_attn",
    )(qn, qr, kn, kr, v)


def mla_layer(lay, x, mods, nw, p, cache_ckv, cache_kr, cos, sin, cos2, sin2):
    n, d = x.shape
    sh1, sc1, g1 = mods[0], mods[1], mods[2]
    h = norm_mod(lay, x, nw, sc1, sh1, BF16)
    tm = lay.tile(512)
    dn = matmul(h, [p['w_down']], _epi_plain, [F32], tm=tm, tn=ML_DOWN_COLS, name="mla_down")[0]
    qlat, ckv, kr = mla_mid(lay, dn, p['qnorm_w'], p['kvnorm_w'], cos, sin)
    qn = matmul(qlat, [p['w_uq_nope']], _epi_plain, [BF16], tm=tm, tn=512, name="mla_qn")[0]
    tn = 2 * ML_ROPE
    extras = ((cos2, pl.BlockSpec((tm, tn), lambda i, j: (i, 0))),
              (sin2, pl.BlockSpec((tm, tn), lambda i, j: (i, 0))))
    qr = matmul(qlat, [p['w_uq_rope'], p['w_uq_rope_sw']], _epi_rope, [BF16], tm=tm, tn=tn,
                extras=extras, name="mla_qr")[0]
    nc, past = lay.nc, cache_ckv.shape[1]
    ckv_b, kr_b = ckv.astype(BF16), kr.astype(BF16)
    kn_c, v_c = matmul(ckv_b[:nc], [p['w_ukn'], p['w_uv']], _epi_plain, [BF16, BF16],
                       tm=lay.tile(512), tn=512, name="mla_kv_ctx")
    o_c = attention(qn, qr, kn_c, kr_b[:nc], v_c, n_seq=lay.n_ctx, q_len=lay.ctx_len,
                    k_len=lay.ctx_len, row0=0, tq=min(256, lay.ctx_len))
    k_len = lay.lat_len + past
    ckv_l = jnp.concatenate([ckv_b[nc:].reshape(lay.n_lat, lay.lat_len, -1), cache_ckv.astype(BF16)],
                            axis=1).reshape(lay.n_lat * k_len, -1)
    kr_l = jnp.concatenate([kr_b[nc:].reshape(lay.n_lat, lay.lat_len, -1), cache_kr.astype(BF16)],
                           axis=1).reshape(lay.n_lat * k_len, -1)
    tk = math.gcd(k_len, 512)
    kn_l, v_l = matmul(ckv_l, [p['w_ukn'], p['w_uv']], _epi_plain, [BF16, BF16], tm=tk, tn=512,
                       name="mla_kv_lat")
    o_l = attention(qn, qr, kn_l, kr_l, v_l, n_seq=lay.n_lat, q_len=lay.lat_len, k_len=k_len,
                    row0=nc, tq=min(256, lay.lat_len))
    o = jnp.concatenate([o_c, o_l], axis=0)
    x = matmul_gated_residual(lay, o, p['wo'], x, g1, tm=lay.tile(1024), tn=512, name="mla_o")
    return x, ckv[:nc], kr[:nc]


def _rope_tables(lay):
    t = lay.lat_len
    rows = t // GRID_W
    rr = jnp.broadcast_to(jnp.arange(rows, dtype=F32)[:, None], (rows, GRID_W)).reshape(-1)
    cc = jnp.broadcast_to(jnp.arange(GRID_W, dtype=F32)[None, :], (rows, GRID_W)).reshape(-1)
    nf = ML_ROPE // 4
    inv = ROPE_BASE ** (-jnp.arange(nf, dtype=F32) / nf)
    ar, ac = rr[:, None] * inv, cc[:, None] * inv
    cos = jnp.concatenate([jnp.cos(ar), jnp.cos(ar), jnp.cos(ac), jnp.cos(ac)], axis=-1)
    sin = jnp.concatenate([-jnp.sin(ar), jnp.sin(ar), -jnp.sin(ac), jnp.sin(ac)], axis=-1)
    cos = jnp.concatenate([jnp.ones((lay.nc, ML_ROPE), F32), jnp.tile(cos, (lay.n_lat, 1))], axis=0)
    sin = jnp.concatenate([jnp.zeros((lay.nc, ML_ROPE), F32), jnp.tile(sin, (lay.n_lat, 1))], axis=0)
    return cos, sin


def _swap_cols(w):
    k, c = w.shape
    w4 = w.reshape(k, c // 32, 2, 16)
    return w4[:, :, ::-1, :].reshape(k, c)


def ffn(lay, x, mods, nw, w_a, w_b, w_out):
    sh2, sc2, g2 = mods[3], mods[4], mods[5]
    h = norm_mod(lay, x, nw, sc2, sh2, BF16)
    act = matmul(h, [w_a, w_b], _epi_swiglu, [BF16], tm=lay.tile(1024), tn=512, name="ffn_in")[0]
    return matmul_gated_residual(lay, act, w_out, x, g2, tm=lay.tile(512), tn=512, name="ffn_out")


def _block_diag_states(s):
    b, two, h, n, _ = s.shape
    s = s.reshape(b, two, h // RW_GROUP, RW_GROUP, n, n)
    eye = jnp.eye(RW_GROUP, dtype=s.dtype)
    out = jnp.einsum('bdghvk,hi->bdghvik', s, eye)
    return out.reshape(b, two, h // RW_GROUP, RW_GROUP * n, RW_GROUP * n)


def _diag_blocks(s):
    b, two, g, l, _ = s.shape
    n = l // RW_GROUP
    s = s.reshape(b, two, g, RW_GROUP, n, RW_GROUP, n)
    s = jnp.moveaxis(jnp.diagonal(s, axis1=3, axis2=5), -1, 3)
    return s.reshape(b, two, g * RW_GROUP, n, n)


def kernel(x_prompt, x_sample, state_rwkv, state_hgrn, cache_ckv, cache_krope, c, c_ctx, ada_w, ada_b, norm1_w, norm2_w, ffn_w_in, ffn_w_out, final_norm_w, rw_mu, rw_wr, rw_wk, rw_wv, rw_wo, rw_w0, rw_w1, rw_w2, rw_a0, rw_a1, rw_a2, rw_g1, rw_g2, rw_kk, rw_ka, rw_rk, rw_lnx_w, rw_lnx_b, hg_w_in, hg_lb, hg_norm_w, hg_wo, ml_w_down, ml_qnorm_w, ml_kvnorm_w, ml_w_uq, ml_w_ukv, ml_wo):
    n_ctx, ctx_len, d = x_prompt.shape
    n_lat, lat_len, _ = x_sample.shape
    depth = ada_w.shape[0]
    lay = Layout(n_ctx, ctx_len, n_lat, lat_len)
    d_ff = ffn_w_out.shape[1]
    x = jnp.concatenate([x_prompt.reshape(lay.nc, d), x_sample.reshape(n_lat * lat_len, d)], axis=0)

    n_cond = -(-(1 + n_lat) // SUBLANES_V7X) * SUBLANES_V7X
    cond = jnp.zeros((n_cond, d), F32).at[0].set(c_ctx).at[1:1 + n_lat].set(c)
    mod_all = adaln(cond, ada_w, ada_b)
    mod_all = mod_all.reshape(depth, n_cond, 6, 1, d).transpose(0, 2, 1, 3, 4)

    lb_table = jnp.cumsum(jax.nn.softmax(hg_lb.astype(F32), axis=0), axis=0)
    lb_table = lb_table - lb_table[0]
    cos, sin = _rope_tables(lay)
    cos2, sin2 = jnp.tile(cos, (1, 2)), jnp.tile(sin, (1, 2))
    bf = lambda t: t.astype(BF16)

    new_rwkv, new_hgrn, new_ckv, new_krope = [], [], [], []
    for l in range(depth):
        kind, j = l % 3, l // 3
        mods = mod_all[l]
        if kind == 0:
            pad1 = lambda w: jnp.pad(w, ((0, 0), (0, 0), (0, LORA_PAD - w.shape[2])))
            pad2 = lambda w: jnp.pad(w, ((0, 0), (0, LORA_PAD - w.shape[1]), (0, 0)))
            p = {'mu': rw_mu[j], 'wr': bf(rw_wr[j]), 'wk': bf(rw_wk[j]), 'wv': bf(rw_wv[j]),
                 'wo': bf(rw_wo[j]),
                 'w0': rw_w0[j].reshape(2, 1, d), 'w1': bf(pad1(rw_w1[j])), 'w2': bf(pad2(rw_w2[j])),
                 'a0': rw_a0[j].reshape(2, 1, d), 'a1': bf(pad1(rw_a1[j])), 'a2': bf(pad2(rw_a2[j])),
                 'g1': bf(rw_g1[j]), 'g2': bf(rw_g2[j]),
                 'kk': rw_kk[j].reshape(1, d), 'ka': rw_ka[j].reshape(1, d),
                 'rk': rw_rk[j].reshape(2, 1, d),
                 'lnx_w': rw_lnx_w[j].reshape(1, d), 'lnx_b': rw_lnx_b[j].reshape(1, d)}
            s_lat = _block_diag_states(state_rwkv[:, j].astype(F32))
            s0 = jnp.concatenate([jnp.zeros((1,) + s_lat.shape[1:], F32), s_lat], axis=0)
            x, sfin = rwkv_layer(lay, x, mods, norm1_w[l], p, s0)
            new_rwkv.append(_diag_blocks(sfin[:n_ctx]))
        elif kind == 1:
            hk = d
            w_in = hg_w_in[j]
            p = {'w_in': [bf(w_in[:, i * hk:(i + 1) * hk]) for i in range(5)],
                 'norm_w': hg_norm_w[j].reshape(1, HG_K), 'wo': bf(hg_wo[j])}
            s_lat = jnp.swapaxes(state_hgrn[:, j].astype(F32), -1, -2)
            s0t = jnp.concatenate([jnp.zeros((1,) + s_lat.shape[1:], F32), s_lat], axis=0)
            x, sfin = hgrn_layer(lay, x, mods, norm1_w[l], p, lb_table[l].reshape(1, d), s0t)
            new_hgrn.append(jnp.swapaxes(sfin[:n_ctx], -1, -2))
        else:
            wd = ml_w_down[j]
            kr_w = wd[:, ML_KR_OFF:]
            zpad = jnp.zeros((d, LANES_V7X - ML_ROPE), wd.dtype)
            w_down = jnp.concatenate([wd, zpad, _swap_cols(kr_w), zpad], axis=1)
            wq = ml_w_uq[j].reshape(ML_Q_LORA, ML_H, ML_NOPE + ML_ROPE)
            wq_n = wq[:, :, :ML_NOPE].reshape(ML_Q_LORA, ML_H * ML_NOPE)
            wq_r = wq[:, :, ML_NOPE:].reshape(ML_Q_LORA, ML_H * ML_ROPE)
            wkv = ml_w_ukv[j].reshape(ML_KV_LORA, ML_H, ML_NOPE + ML_V)
            p = {'w_down': bf(w_down), 'qnorm_w': ml_qnorm_w[j].reshape(1, -1),
                 'kvnorm_w': ml_kvnorm_w[j].reshape(1, -1),
                 'w_uq_nope': bf(wq_n), 'w_uq_rope': bf(wq_r), 'w_uq_rope_sw': bf(_swap_cols(wq_r)),
                 'w_ukn': bf(wkv[:, :, :ML_NOPE].reshape(ML_KV_LORA, ML_H * ML_NOPE)),
                 'w_uv': bf(wkv[:, :, ML_NOPE:].reshape(ML_KV_LORA, ML_H * ML_V)),
                 'wo': bf(ml_wo[j])}
            x, ckv_c, kr_c = mla_layer(lay, x, mods, norm1_w[l], p, cache_ckv[:, j], cache_krope[:, j],
                                       cos, sin, cos2, sin2)
            new_ckv.append(ckv_c.reshape(n_ctx, ctx_len, ML_KV_LORA))
            new_krope.append(kr_c.reshape(n_ctx, ctx_len, ML_ROPE))
        w_in = ffn_w_in[l]
        x = ffn(lay, x, mods, norm2_w[l], bf(w_in[:, :d_ff]), bf(w_in[:, d_ff:]), bf(ffn_w_out[l]))

    zero = jnp.zeros((n_cond, 1, d), F32)
    y = norm_mod(lay, x, final_norm_w, zero, zero, F32)
    y_prompt = y[:lay.nc].reshape(n_ctx, ctx_len, d)
    y_sample = y[lay.nc:].reshape(n_lat, lat_len, d)
    return (y_prompt, y_sample, jnp.stack(new_rwkv, axis=1), jnp.stack(new_hgrn, axis=1),
            jnp.stack(new_ckv, axis=1), jnp.stack(new_krope, axis=1))
```

```python
import functools
import math

import numpy as np
import jax
import jax.numpy as jnp
from jax import lax
from jax.experimental import pallas as pl
from jax.experimental.pallas import tpu as pltpu

F32 = jnp.float32
BF16 = jnp.bfloat16
HIGHEST = lax.Precision.HIGHEST

LANES_V7X = 128
SUBLANES_V7X = 8
VMEM_BYTES_V7X = 64 * 1024 * 1024
VMEM_LIMIT = 56 * 1024 * 1024

NORM_EPS = 1e-6
RW_HEAD = 64
RW_LN_EPS = 64e-5
RW_GROUP = 4
RW_LANES = RW_GROUP * RW_HEAD
RW_CHUNK = 64
RW_GSTEP = 2
HG_K = 128
HG_CHUNK_TOKENS = 64
ML_H = 16
ML_NOPE = 128
ML_ROPE = 64
ML_V = 128
ML_Q_LORA = 512
ML_KV_LORA = 512
GRID_W = 64
ROPE_BASE = 10000.0
LORA_PAD = 128


class Layout:
    def __init__(self, n_ctx, ctx_len, n_lat, lat_len):
        self.n_ctx, self.ctx_len, self.n_lat, self.lat_len = n_ctx, ctx_len, n_lat, lat_len
        self.nc = n_ctx * ctx_len
        self.n = self.nc + n_lat * lat_len
        self.tb = min(256, ctx_len)
        assert ctx_len % self.tb == 0 and lat_len % self.tb == 0 and self.tb % RW_CHUNK == 0
        self.nb = self.n // self.tb
        self.nb_ctx = self.nc // self.tb
        self.bps_ctx = ctx_len // self.tb
        self.bps_lat = lat_len // self.tb
        self.n_seq = n_ctx + n_lat

    def tile(self, want):
        t = want
        while self.nc % t or self.lat_len % t:
            t //= 2
        return t

    def cond_of_tile(self, i, tm):
        row = i * tm
        return jnp.where(row < self.nc, 0, 1 + (row - self.nc) // self.lat_len)

    def seq_info(self, blk):
        is_ctx = blk < self.nb_ctx
        lat = blk - self.nb_ctx
        seq = jnp.where(is_ctx, blk // self.bps_ctx, self.n_ctx + lat // self.bps_lat)
        pos = jnp.where(is_ctx, blk % self.bps_ctx, lat % self.bps_lat)
        cnt = jnp.where(is_ctx, self.bps_ctx, self.bps_lat)
        return seq, pos, cnt


def _cparams(n_axes):
    return pltpu.CompilerParams(dimension_semantics=("arbitrary",) * n_axes, vmem_limit_bytes=VMEM_LIMIT)


def _bdot(a, b):
    return jnp.dot(a.astype(BF16), b.astype(BF16), preferred_element_type=F32)


def _bdot_nt(a, b):
    return lax.dot_general(a.astype(BF16), b.astype(BF16), (((1,), (1,)), ((), ())),
                           preferred_element_type=F32)


def _xdot(a, b):
    return jnp.dot(a, b, preferred_element_type=F32, precision=HIGHEST)


def _silu(x):
    return x * jax.nn.sigmoid(x)


def _adaln_kernel(c_ref, w_ref, b_ref, o_ref):
    a = _silu(c_ref[...]).astype(BF16)
    o_ref[...] = jnp.dot(a, w_ref[...].astype(BF16), preferred_element_type=F32) + b_ref[...]


def adaln(cond, ada_w, ada_b):
    depth, d, d6 = ada_w.shape
    r = cond.shape[0]
    tn = 1024
    return pl.pallas_call(
        _adaln_kernel,
        grid=(depth, d6 // tn),
        in_specs=[pl.BlockSpec((r, d), lambda l, j: (0, 0)),
                  pl.BlockSpec((None, d, tn), lambda l, j: (l, 0, j)),
                  pl.BlockSpec((None, 1, tn), lambda l, j: (l, 0, j))],
        out_specs=pl.BlockSpec((None, r, tn), lambda l, j: (l, 0, j)),
        out_shape=jax.ShapeDtypeStruct((depth, r, d6), F32),
        compiler_params=_cparams(2), name="adaln",
    )(cond, ada_w, ada_b.reshape(depth, 1, d6))


def _norm_mod(x, nw, sc, sh):
    y = x * lax.rsqrt(jnp.mean(x * x, axis=-1, keepdims=True) + NORM_EPS)
    return (y * nw) * (1.0 + sc) + sh


def _norm_mod_kernel(x_ref, nw_ref, sc_ref, sh_ref, o_ref):
    o_ref[...] = _norm_mod(x_ref[...], nw_ref[...], sc_ref[...], sh_ref[...]).astype(o_ref.dtype)


def norm_mod(lay, x, nw, sc, sh, out_dtype):
    n, d = x.shape
    tm = lay.tile(512)
    cmap = lambda i: (lay.cond_of_tile(i, tm), 0, 0)
    return pl.pallas_call(
        _norm_mod_kernel,
        grid=(n // tm,),
        in_specs=[pl.BlockSpec((tm, d), lambda i: (i, 0)),
                  pl.BlockSpec((1, d), lambda i: (0, 0)),
                  pl.BlockSpec((None, 1, d), cmap),
                  pl.BlockSpec((None, 1, d), cmap)],
        out_specs=pl.BlockSpec((tm, d), lambda i: (i, 0)),
        out_shape=jax.ShapeDtypeStruct((n, d), out_dtype),
        compiler_params=_cparams(1), name="norm_mod",
    )(x, nw.reshape(1, d), sc, sh)


def _mm_kernel(*refs, n_w, n_e, epi):
    a = refs[0][...]
    accs = [jnp.dot(a, refs[1 + i][...], preferred_element_type=F32) for i in range(n_w)]
    extras = [refs[1 + n_w + i][...] for i in range(n_e)]
    outs = epi(accs, extras)
    o_refs = refs[1 + n_w + n_e:]
    for o_ref, val in zip(o_refs, outs):
        o_ref[...] = val.astype(o_ref.dtype)


def matmul(a, ws, epi, out_dtypes, *, tm, tn, extras=(), name):
    m, k = a.shape
    nw = ws[0].shape[1]
    assert m % tm == 0 and nw % tn == 0
    in_specs = [pl.BlockSpec((tm, k), lambda i, j: (i, 0))]
    in_specs += [pl.BlockSpec((k, tn), lambda i, j: (0, j)) for _ in ws]
    in_specs += [spec for _, spec in extras]
    outs = pl.pallas_call(
        functools.partial(_mm_kernel, n_w=len(ws), n_e=len(extras), epi=epi),
        grid=(m // tm, nw // tn),
        in_specs=in_specs,
        out_specs=[pl.BlockSpec((tm, tn), lambda i, j: (i, j)) for _ in out_dtypes],
        out_shape=[jax.ShapeDtypeStruct((m, nw), dt) for dt in out_dtypes],
        compiler_params=_cparams(2), name=name,
    )(a, *ws, *[arr for arr, _ in extras])
    return outs


def _epi_plain(accs, extras):
    return accs


def _epi_gated_residual(accs, extras):
    x, g = extras
    return [x + g * accs[0]]


def matmul_gated_residual(lay, a, w, x, gate, *, tm, tn, name):
    extras = ((x, pl.BlockSpec((tm, tn), lambda i, j: (i, j))),
              (gate, pl.BlockSpec((None, 1, tn), lambda i, j: (lay.cond_of_tile(i, tm), 0, j))))
    return matmul(a, [w], _epi_gated_residual, [F32], tm=tm, tn=tn, extras=extras, name=name)[0]


def _epi_swiglu(accs, extras):
    return [_silu(accs[0]) * accs[1]]


def _rwkv_prep_kernel(x_ref, xp_ref, xn_ref, nw_ref, sc_ref, sh_ref, mu_ref, *o_refs, lay):
    i = pl.program_id(0)
    _, pos, cnt = lay.seq_info(i)
    nw, sc, sh = nw_ref[...], sc_ref[...], sh_ref[...]
    h = _norm_mod(x_ref[...], nw, sc, sh)
    hp = _norm_mod(xp_ref[...], nw, sc, sh)[SUBLANES_V7X - 1:SUBLANES_V7X]
    hn = _norm_mod(xn_ref[...], nw, sc, sh)[0:1]
    hp = jnp.where(pos == 0, 0.0, hp)
    hn = jnp.where(pos == cnt - 1, 0.0, hn)
    tb = h.shape[0]
    row = lax.broadcasted_iota(jnp.int32, h.shape, 0)
    prev = jnp.where(row == 0, hp, pltpu.roll(h, 1, axis=0))
    nxt = jnp.where(row == tb - 1, hn, pltpu.roll(h, tb - 1, axis=0))
    xx = 0.5 * (prev + nxt) - h
    for idx, o_ref in enumerate(o_refs):
        o_ref[...] = (h + xx * mu_ref[idx:idx + 1, :]).astype(o_ref.dtype)


def rwkv_prep(lay, x, nw, sc, sh, mu):
    n, d = x.shape
    tb = lay.tb
    r8 = tb // SUBLANES_V7X
    last8 = n // SUBLANES_V7X - 1
    cmap = lambda i: (lay.cond_of_tile(i, tb), 0, 0)
    return pl.pallas_call(
        functools.partial(_rwkv_prep_kernel, lay=lay),
        grid=(n // tb,),
        in_specs=[pl.BlockSpec((tb, d), lambda i: (i, 0)),
                  pl.BlockSpec((SUBLANES_V7X, d), lambda i: (jnp.maximum(i * r8 - 1, 0), 0)),
                  pl.BlockSpec((SUBLANES_V7X, d), lambda i: (jnp.minimum((i + 1) * r8, last8), 0)),
                  pl.BlockSpec((1, d), lambda i: (0, 0)),
                  pl.BlockSpec((None, 1, d), cmap),
                  pl.BlockSpec((None, 1, d), cmap),
                  pl.BlockSpec((6, d), lambda i: (0, 0))],
        out_specs=[pl.BlockSpec((tb, d), lambda i: (i, 0))] * 6,
        out_shape=[jax.ShapeDtypeStruct((n, d), BF16)] * 6,
        compiler_params=_cparams(1), name="rwkv_prep",
    )(x, x, x, nw.reshape(1, d), sc, sh, mu)


def _softplus(z):
    return jnp.maximum(z, 0.0) + jnp.log(1.0 + jnp.exp(-jnp.abs(z)))


def _rwkv_lora_kernel(xw_ref, xa_ref, xg_ref, w1_ref, w2_ref, w0_ref, a1_ref, a2_ref, a0_ref,
                      g1_ref, g2_ref, lw_ref, a_ref, g_ref):
    xw, xa, xg = xw_ref[...], xa_ref[...], xg_ref[...]
    for d in range(2):
        t = jnp.tanh(jnp.dot(xw, w1_ref[d], preferred_element_type=F32))
        wl = w0_ref[d] + _bdot(t, w2_ref[d])
        wlog = -_softplus(-wl) - 0.5
        lw_ref[d] = -jnp.exp(wlog)
        t = jnp.dot(xa, a1_ref[d], preferred_element_type=F32)
        a_ref[d] = jax.nn.sigmoid(a0_ref[d] + _bdot(t, a2_ref[d]))
    t = jax.nn.sigmoid(jnp.dot(xg, g1_ref[...], preferred_element_type=F32))
    g_ref[...] = _bdot(t, g2_ref[...])


def rwkv_lora(lay, xw, xa, xg, w1, w2, w0, a1, a2, a0, g1, g2):
    n, d = xw.shape
    tm = lay.tile(128)
    full = lambda arr: pl.BlockSpec(arr.shape, lambda i: (0,) * arr.ndim)
    row = pl.BlockSpec((tm, d), lambda i: (i, 0))
    return pl.pallas_call(
        _rwkv_lora_kernel,
        grid=(n // tm,),
        in_specs=[row, row, row] + [full(t) for t in (w1, w2, w0, a1, a2, a0, g1, g2)],
        out_specs=[pl.BlockSpec((2, tm, d), lambda i: (0, i, 0)),
                   pl.BlockSpec((2, tm, d), lambda i: (0, i, 0)),
                   row],
        out_shape=[jax.ShapeDtypeStruct((2, n, d), F32), jax.ShapeDtypeStruct((2, n, d), F32),
                   jax.ShapeDtypeStruct((n, d), F32)],
        compiler_params=_cparams(1), name="rwkv_lora",
    )(xw, xa, xg, w1, w2, w0, a1, a2, a0, g1, g2)


def _wkv_constants():
    c, g, hd = RW_CHUNK, RW_GROUP, RW_HEAD
    gc, lanes = g * c, g * hd
    t = np.arange(c)
    cum = np.stack([(t[None, :] <= t[:, None]), (t[None, :] >= t[:, None])])
    tr = np.arange(c)[:, None]
    tc = np.arange(gc)[None, :] % c
    strict = np.stack([tc < tr, tc > tr])
    incl = np.stack([tc <= tr, tc >= tr])
    head_rows = np.arange(gc)[:, None] // c == np.arange(lanes)[None, :] // hd
    blk_rows = np.arange(gc)[:, None] // c == np.arange(gc)[None, :] // c
    bd = np.arange(lanes)[:, None] // hd == np.arange(lanes)[None, :] // hd
    eye_w = tr == tc
    return dict(cum=jnp.asarray(cum, BF16), strict=jnp.asarray(strict, F32), incl=jnp.asarray(incl, F32),
                head_rows=jnp.asarray(head_rows, BF16), blk_rows=jnp.asarray(blk_rows, BF16),
                bd=jnp.asarray(bd, F32), bd_b=jnp.asarray(bd, BF16), eye_w=jnp.asarray(eye_w, F32))


def _split_dot(m01, x, passes):
    acc, rem = None, x
    for _ in range(passes):
        part = rem.astype(BF16)
        term = jnp.dot(m01, part, preferred_element_type=F32)
        acc = term if acc is None else acc + term
        rem = rem - part.astype(F32)
    return acc


def _split_dot_r(x, m01, passes):
    acc, rem = None, x
    for _ in range(passes):
        part = rem.astype(BF16)
        term = jnp.dot(part, m01, preferred_element_type=F32)
        acc = term if acc is None else acc + term
        rem = rem - part.astype(F32)
    return acc


def _tile_rows(x, mask_b):
    return jnp.concatenate([x.astype(BF16)] * RW_GROUP, axis=0) * mask_b


def _wkv_tinv_many(items, head_rows, blk_rows, eye_w):
    c = RW_CHUNK
    n_ws = []
    for kk, a, lw, cum_m, strict_w in items:
        cum = _split_dot(cum_m, lw, 3)
        kkq = kk * jnp.exp(cum - lw)
        kkah = kk * a * jnp.exp(-cum)
        n_ws.append(-jnp.where(strict_w > 0, _bdot_nt(kkq, _tile_rows(kkah, head_rows)), 0.0))
    t_ws = [eye_w + n_w for n_w in n_ws]
    n_pows = [_bdot(n_w, _tile_rows(n_w, blk_rows)) for n_w in n_ws]
    levels = int(math.log2(c))
    for lv in range(1, levels):
        ws = [_tile_rows(n_pow, blk_rows) for n_pow in n_pows]
        if lv < levels - 1:
            boths = [_bdot(jnp.concatenate([t_w, n_pow], axis=0), w) for t_w, n_pow, w in zip(t_ws, n_pows, ws)]
            t_ws = [t_w + both[:c] for t_w, both in zip(t_ws, boths)]
            n_pows = [both[c:] for both in boths]
        else:
            t_ws = [t_w + _bdot(t_w, w) for t_w, w in zip(t_ws, ws)]
    return t_ws


def _wkv_prep(r, v, kk, kd, a, lw, cum_m, strict_w, incl_w, head_rows):
    c = RW_CHUNK
    cum = _split_dot(cum_m, lw, 3)
    tot = jnp.sum(lw, axis=0, keepdims=True)
    kka = kk * a
    e_inv, e_rest = jnp.exp(-cum), jnp.exp(tot - cum)
    kkq, rq = kk * jnp.exp(cum - lw), r * jnp.exp(cum)
    q2 = jnp.concatenate([kkq, rq], axis=0).astype(BF16)
    s1 = _bdot_nt(q2, _tile_rows(kd * e_inv, head_rows))
    l_d = jnp.where(strict_w > 0, s1[:c], 0.0).astype(BF16)
    a_d = jnp.where(incl_w > 0, s1[c:], 0.0)
    a_a = jnp.where(incl_w > 0, _bdot_nt(q2[c:], _tile_rows(kka * e_inv, head_rows)), 0.0)
    vbd = _tile_rows(v, head_rows)
    return dict(q2=q2, l_dv=_bdot(l_d, vbd), a_w=jnp.concatenate([a_d, -a_a], axis=1).astype(BF16), vbd=vbd,
                v=v, kw=jnp.concatenate([kd * e_rest, -(kka * e_rest)], axis=0).astype(BF16),
                decay=jnp.exp(tot))


def _wkv_advance(states, preps, t_ws, head_rows, bd):
    c = RW_CHUNK
    p0s = [_bdot_nt(p['q2'], s) for p, s in zip(preps, states)]
    us = [_bdot(t_w, _tile_rows(p0[:c] + p['l_dv'], head_rows))
          for p0, p, t_w in zip(p0s, preps, t_ws)]
    upds = [_bdot(jnp.concatenate([p['v'], u], axis=0).T, p['kw']) for p, u in zip(preps, us)]
    new_states = [s * p['decay'] + jnp.where(bd > 0, upd, 0.0) for s, p, upd in zip(states, preps, upds)]
    ys = [p0[c:] + _bdot(p['a_w'], jnp.concatenate([p['vbd'], _tile_rows(u, head_rows)], axis=0))
          for p0, p, u in zip(p0s, preps, us)]
    return new_states, ys


def _rwkv_mid_kernel(r_ref, k_ref, v_ref, a_ref, kkw_ref, kaw_ref, rk_ref, bd_ref, kk_ref, kd_ref, b_ref):
    r, k, v = r_ref[...], k_ref[...], v_ref[...]
    bd = bd_ref[...]
    kk = k * kkw_ref[...]
    kk_ref[...] = kk * lax.rsqrt(_split_dot_r(kk * kk, bd, 2) + 1e-12)
    bonus = None
    for d in range(2):
        kd = k * (1.0 + (a_ref[d] - 1.0) * kaw_ref[...])
        kd_ref[d] = kd
        term = _split_dot_r(r * kd * rk_ref[d], bd, 2) * v
        bonus = term if bonus is None else bonus + term
    b_ref[...] = bonus


def rwkv_mid(lay, r, k, v, a, kkw, kaw, rk):
    n, dm = r.shape
    lanes = RW_LANES
    tm = lay.tile(512)
    bd_b = _wkv_constants()['bd_b']
    tok = pl.BlockSpec((tm, lanes), lambda i, g: (i, g))
    two = pl.BlockSpec((2, tm, lanes), lambda i, g: (0, i, g))
    par = pl.BlockSpec((1, lanes), lambda i, g: (0, g))
    return pl.pallas_call(
        _rwkv_mid_kernel,
        grid=(n // tm, dm // lanes),
        in_specs=[tok, tok, tok, two, par, par, pl.BlockSpec((2, 1, lanes), lambda i, g: (0, 0, g)),
                  pl.BlockSpec(bd_b.shape, lambda i, g: (0, 0))],
        out_specs=[tok, two, tok],
        out_shape=[jax.ShapeDtypeStruct((n, dm), F32), jax.ShapeDtypeStruct((2, n, dm), F32),
                   jax.ShapeDtypeStruct((n, dm), F32)],
        compiler_params=_cparams(2), name="rwkv_mid",
    )(r, k, v, a, kkw, kaw, rk, bd_b)


def _wkv_inv_kernel(kk_ref, a_ref, lw_ref, cum_ref, strict_ref, hr_ref, br_ref, eye_ref, t_ref, *, lay):
    c = RW_CHUNK
    hr, br, eye_w = hr_ref[...], br_ref[...], eye_ref[...]
    where = [(d, slice(ci * c, (ci + 1) * c)) for d in range(2) for ci in range(lay.tb // c)]
    items = [(kk_ref[sl, :], a_ref[d, sl, :], lw_ref[d, sl, :], cum_ref[d], strict_ref[d]) for d, sl in where]
    for (d, sl), t_w in zip(where, _wkv_tinv_many(items, hr, br, eye_w)):
        t_ref[d, sl, :] = t_w.astype(t_ref.dtype)


def wkv_inv(lay, kk, a, lw):
    n, dm = kk.shape
    tb, lanes = lay.tb, RW_LANES
    k = _wkv_constants()
    full = lambda arr: pl.BlockSpec(arr.shape, lambda g, j: (0,) * arr.ndim)
    two = pl.BlockSpec((2, tb, lanes), lambda g, j: (0, j, g))
    return pl.pallas_call(
        functools.partial(_wkv_inv_kernel, lay=lay),
        grid=(dm // lanes, lay.nb),
        in_specs=[pl.BlockSpec((tb, lanes), lambda g, j: (j, g)), two, two,
                  full(k['cum']), full(k['strict']), full(k['head_rows']), full(k['blk_rows']),
                  full(k['eye_w'])],
        out_specs=two,
        out_shape=jax.ShapeDtypeStruct((2, n, dm), BF16),
        compiler_params=_cparams(2), name="wkv_inv",
    )(kk, a, lw, k['cum'], k['strict'], k['head_rows'], k['blk_rows'], k['eye_w'])


def _wkv_kernel(*refs, lay):
    (rf, vf, kkf, kdf, af, lwf, tf, s0f, rb, vb, kkb, kdb, ab, lwb, tb_, s0b,
     cum_ref, strict_ref, incl_ref, hr_ref, bd_ref, yf_ref, yb_ref, sff_ref, sfb_ref, sf_scr, sb_scr) = refs
    c = RW_CHUNK
    n_chunks = lay.tb // c
    j = pl.program_id(1)
    _, pos_f, cnt_f = lay.seq_info(j)
    _, pos_b, cnt_b = lay.seq_info(lay.nb - 1 - j)

    @pl.when(pos_f == 0)
    def _():
        sf_scr[...] = s0f[...]

    @pl.when(pos_b == cnt_b - 1)
    def _():
        sb_scr[...] = s0b[...]

    hr, bd = hr_ref[...], bd_ref[...]
    sl_f = [slice(ci * c, (ci + 1) * c) for ci in range(n_chunks)]
    sl_b = sl_f[::-1]
    cols = [slice(gi * RW_LANES, (gi + 1) * RW_LANES) for gi in range(RW_GSTEP)]
    preps = []
    for ci in range(n_chunks):
        row = []
        for cs in cols:
            sf, sb = sl_f[ci], sl_b[ci]
            row.append(_wkv_prep(rf[sf, cs], vf[sf, cs], kkf[sf, cs], kdf[sf, cs], af[sf, cs], lwf[sf, cs],
                                 cum_ref[0], strict_ref[0], incl_ref[0], hr))
            row.append(_wkv_prep(rb[sb, cs], vb[sb, cs], kkb[sb, cs], kdb[sb, cs], ab[sb, cs], lwb[sb, cs],
                                 cum_ref[1], strict_ref[1], incl_ref[1], hr))
        preps.append(row)
    states = []
    for gi in range(RW_GSTEP):
        states += [sf_scr[gi], sb_scr[gi]]
    for ci in range(n_chunks):
        t_ws = []
        for cs in cols:
            t_ws += [tf[sl_f[ci], cs], tb_[sl_b[ci], cs]]
        states, ys = _wkv_advance(states, preps[ci], t_ws, hr, bd)
        for gi, cs in enumerate(cols):
            yf_ref[sl_f[ci], cs] = ys[2 * gi]
            yb_ref[sl_b[ci], cs] = ys[2 * gi + 1]
    for gi in range(RW_GSTEP):
        sf_scr[gi] = states[2 * gi]
        sb_scr[gi] = states[2 * gi + 1]

    @pl.when(pos_f == cnt_f - 1)
    def _():
        sff_ref[...] = sf_scr[...]

    @pl.when(pos_b == 0)
    def _():
        sfb_ref[...] = sb_scr[...]


def wkv(lay, r, v, kk, kd, a, lw, t_inv, s0):
    n, dm = r.shape
    tb, lanes = lay.tb, RW_LANES
    ng = dm // lanes
    width = RW_GSTEP * lanes
    k = _wkv_constants()
    full = lambda arr: pl.BlockSpec(arr.shape, lambda g, j: (0,) * arr.ndim)

    def views(d, blk):
        tok = pl.BlockSpec((tb, width), lambda g, j: (blk(j), g))
        tok2 = pl.BlockSpec((None, tb, width), lambda g, j: (d, blk(j), g))

        def s0_map(g, j):
            seq, _, _ = lay.seq_info(blk(j))
            return (jnp.maximum(seq - lay.n_ctx + 1, 0), d, g, 0, 0)

        def sfin_map(g, j):
            seq, _, _ = lay.seq_info(blk(j))
            return (seq, g, 0, 0)

        ins = [tok, tok, tok, tok2, tok2, tok2, tok2,
               pl.BlockSpec((None, None, RW_GSTEP, lanes, lanes), s0_map)]
        return ins, tok, pl.BlockSpec((None, RW_GSTEP, lanes, lanes), sfin_map)

    in_f, y_f, sf_f = views(0, lambda j: j)
    in_b, y_b, sf_b = views(1, lambda j: lay.nb - 1 - j)
    consts = [k['cum'], k['strict'], k['incl'], k['head_rows'], k['bd']]
    args = [r, v, kk, kd, a, lw, t_inv, s0]
    return pl.pallas_call(
        functools.partial(_wkv_kernel, lay=lay),
        grid=(ng // RW_GSTEP, lay.nb),
        in_specs=in_f + in_b + [full(x) for x in consts],
        out_specs=[y_f, y_b, sf_f, sf_b],
        out_shape=[jax.ShapeDtypeStruct((n, dm), F32), jax.ShapeDtypeStruct((n, dm), F32),
                   jax.ShapeDtypeStruct((lay.n_seq, ng, lanes, lanes), F32),
                   jax.ShapeDtypeStruct((lay.n_seq, ng, lanes, lanes), F32)],
        scratch_shapes=[pltpu.VMEM((RW_GSTEP, lanes, lanes), F32), pltpu.VMEM((RW_GSTEP, lanes, lanes), F32)],
        compiler_params=_cparams(2), name="wkv",
    )(*args, *args, *consts)


def _rwkv_post_kernel(yf_ref, yb_ref, b_ref, g_ref, lnw_ref, lnb_ref, bd_ref, o_ref):
    y = yf_ref[...] + yb_ref[...]
    bd = bd_ref[...]
    inv = 1.0 / RW_HEAD
    mu = _split_dot_r(y, bd, 3) * inv
    yc = y - mu
    var = _split_dot_r(yc * yc, bd, 3) * inv
    yn = yc * lax.rsqrt(var + RW_LN_EPS)
    out = yn * lnw_ref[...] + lnb_ref[...] + b_ref[...]
    o_ref[...] = (out * g_ref[...]).astype(o_ref.dtype)


def rwkv_post(lay, y_f, y_b, bonus, g, lnw, lnb):
    n, dm = y_f.shape
    lanes = RW_LANES
    tm = lay.tile(512)
    bd_b = _wkv_constants()['bd_b']
    tok = pl.BlockSpec((tm, lanes), lambda i, c: (i, c))
    par = pl.BlockSpec((1, lanes), lambda i, c: (0, c))
    return pl.pallas_call(
        _rwkv_post_kernel,
        grid=(n // tm, dm // lanes),
        in_specs=[tok, tok, tok, tok, par, par, pl.BlockSpec(bd_b.shape, lambda i, c: (0, 0))],
        out_specs=tok,
        out_shape=jax.ShapeDtypeStruct((n, dm), BF16),
        compiler_params=_cparams(2), name="rwkv_post",
    )(y_f, y_b, bonus, g, lnw, lnb, bd_b)


def rwkv_layer(lay, x, mods, nw, p, s0):
    n, d = x.shape
    sh1, sc1, g1 = mods[0], mods[1], mods[2]
    xr, xw, xk, xv, xa, xg = rwkv_prep(lay, x, nw, sc1, sh1, p['mu'])
    tm = lay.tile(1024)
    r = matmul(xr, [p['wr']], _epi_plain, [F32], tm=tm, tn=512, name="rwkv_r")[0]
    k = matmul(xk, [p['wk']], _epi_plain, [F32], tm=tm, tn=512, name="rwkv_k")[0]
    v = matmul(xv, [p['wv']], _epi_plain, [F32], tm=tm, tn=512, name="rwkv_v")[0]
    lw, a, g = rwkv_lora(lay, xw, xa, xg, p['w1'], p['w2'], p['w0'], p['a1'], p['a2'], p['a0'],
                         p['g1'], p['g2'])
    kk, kd, bonus = rwkv_mid(lay, r, k, v, a, p['kk'], p['ka'], p['rk'])
    t_inv = wkv_inv(lay, kk, a, lw)
    y_f, y_b, sfin_f, sfin_b = wkv(lay, r, v, kk, kd, a, lw, t_inv, s0)
    z = rwkv_post(lay, y_f, y_b, bonus, g, p['lnx_w'], p['lnx_b'])
    x = matmul_gated_residual(lay, z, p['wo'], x, g1, tm=tm, tn=512, name="rwkv_o")
    return x, jnp.stack([sfin_f, sfin_b], axis=1)


def _hgrn_constants():
    c = HG_CHUNK_TOKENS
    t = np.arange(c)[:, None]
    j = np.arange(c)[None, :]
    mats_all, masks_all = [], []
    for rev in (False, True):
        mats, masks = [], []
        h = 1
        while h < c:
            mid = (t // (2 * h)) * 2 * h + h
            upper = (t % (2 * h)) >= h
            same = (t // (2 * h)) == (j // (2 * h))
            if not rev:
                m = np.where(upper, (j >= mid) & (j <= t), (j > t) & (j <= mid - 1))
                mask = same & upper & ((j % (2 * h)) < h)
            else:
                m = np.where(upper, (j >= mid) & (j < t), (j >= t) & (j <= mid - 1))
                mask = same & (~upper) & ((j % (2 * h)) >= h)
            mats.append(m)
            masks.append(mask)
            h *= 2
        mats.append((j >= t) if rev else (j <= t))
        masks.append(t == j)
        mats_all.append(np.concatenate(mats, 0))
        masks_all.append(np.stack(masks, 0))
    return (jnp.asarray(np.stack(mats_all).astype(np.float32)),
            jnp.asarray(np.stack(masks_all).astype(np.float32)))


def _hgrn_chunk(St, q, f, iv, lvl_m, masks):
    c = HG_CHUNK_TOKENS
    nlev = masks.shape[0] - 1
    g = jnp.log(f)
    k = 1.0 - f
    x = _xdot(lvl_m, g)
    e = jnp.exp(x)
    a = jnp.where(masks[nlev] > 0, _bdot_nt(q, k), 0.0)
    for lv in range(nlev):
        el = e[lv * c:(lv + 1) * c]
        a = a + jnp.where(masks[lv] > 0, _bdot_nt(q * el, k * el), 0.0)
    gc = x[nlev * c:]
    tot = jnp.sum(g, axis=0, keepdims=True)
    o = _bdot_nt(q * e[nlev * c:], St) + _bdot(a, iv)
    s_new = St * jnp.exp(tot) + _bdot(iv.T, k * jnp.exp(tot - gc))
    return s_new, o


def _hgrn_kernel(q_ref, f_ref, i_ref, s0_ref, lvl_ref, mask_ref, o_ref, sfin_ref, s_scr, *, lay):
    d = pl.program_id(0)
    j = pl.program_id(2)
    blk = jnp.where(d == 0, j, lay.nb - 1 - j)
    _, pos, cnt = lay.seq_info(blk)
    first = pos == jnp.where(d == 0, 0, cnt - 1)
    last = pos == jnp.where(d == 0, cnt - 1, 0)

    @pl.when(first)
    def _():
        s_scr[...] = s0_ref[...]

    n_chunks = lay.tb // HG_CHUNK_TOKENS
    lvl_m, masks = lvl_ref[...], mask_ref[...]

    def body(ci, carry):
        cc = jnp.where(d == 0, ci, n_chunks - 1 - ci)
        sl = pl.ds(pl.multiple_of(cc * HG_CHUNK_TOKENS, HG_CHUNK_TOKENS), HG_CHUNK_TOKENS)
        s_new, o = _hgrn_chunk(s_scr[...], q_ref[sl, :], f_ref[sl, :], i_ref[sl, :], lvl_m, masks)
        s_scr[...] = s_new
        o_ref[sl, :] = o
        return carry

    lax.fori_loop(0, n_chunks, body, 0)

    @pl.when(last)
    def _():
        sfin_ref[...] = s_scr[...]


def hgrn_scan(lay, q, f, iv, s0t):
    n, dm = q.shape
    tb = lay.tb
    nh = dm // HG_K
    lvl_m, masks = _hgrn_constants()

    def blk_of(d, j):
        return jnp.where(d == 0, j, lay.nb - 1 - j)

    tok = pl.BlockSpec((tb, HG_K), lambda d, h, j: (blk_of(d, j), h))
    tok2 = pl.BlockSpec((None, tb, HG_K), lambda d, h, j: (d, blk_of(d, j), h))

    def s0_map(d, h, j):
        seq, _, _ = lay.seq_info(blk_of(d, j))
        return (jnp.maximum(seq - lay.n_ctx + 1, 0), d, h, 0, 0)

    def sfin_map(d, h, j):
        seq, _, _ = lay.seq_info(blk_of(d, j))
        return (seq, d, h, 0, 0)

    st = pl.BlockSpec((None, None, None, HG_K, HG_K), s0_map)
    return pl.pallas_call(
        functools.partial(_hgrn_kernel, lay=lay),
        grid=(2, nh, lay.nb),
        in_specs=[tok, tok2, tok, st,
                  pl.BlockSpec((None,) + lvl_m.shape[1:], lambda d, h, j: (d, 0, 0)),
                  pl.BlockSpec((None,) + masks.shape[1:], lambda d, h, j: (d, 0, 0, 0))],
        out_specs=[tok2, pl.BlockSpec((None, None, None, HG_K, HG_K), sfin_map)],
        out_shape=[jax.ShapeDtypeStruct((2, n, dm), F32),
                   jax.ShapeDtypeStruct((lay.n_seq, 2, nh, HG_K, HG_K), F32)],
        scratch_shapes=[pltpu.VMEM((HG_K, HG_K), F32)],
        compiler_params=_cparams(3), name="hgrn_scan",
    )(q, f, iv, s0t, lvl_m, masks)


def _hgrn_post_kernel(o_ref, g_ref, nw_ref, z_ref):
    o = o_ref[0] + o_ref[1]
    o = o * lax.rsqrt(jnp.mean(o * o, axis=-1, keepdims=True) + NORM_EPS) * nw_ref[...] * g_ref[...]
    z_ref[...] = o.astype(z_ref.dtype)


def hgrn_post(lay, o, gs, nw):
    _, n, dm = o.shape
    tm = lay.tile(1024)
    return pl.pallas_call(
        _hgrn_post_kernel,
        grid=(n // tm, dm // HG_K),
        in_specs=[pl.BlockSpec((2, tm, HG_K), lambda i, h: (0, i, h)),
                  pl.BlockSpec((tm, HG_K), lambda i, h: (i, h)),
                  pl.BlockSpec((1, HG_K), lambda i, h: (0, 0))],
        out_specs=pl.BlockSpec((tm, HG_K), lambda i, h: (i, h)),
        out_shape=jax.ShapeDtypeStruct((n, dm), BF16),
        compiler_params=_cparams(2), name="hgrn_post",
    )(o, gs, nw)


def _epi_hgrn_in(accs, extras):
    lb = extras[0]
    q = _silu(accs[0])
    f0 = lb + (1.0 - lb) * jax.nn.sigmoid(accs[1])
    f1 = lb + (1.0 - lb) * jax.nn.sigmoid(accs[2])
    return [q, f0, f1, accs[3], _silu(accs[4])]


def hgrn_layer(lay, x, mods, nw, p, lb, s0t):
    n, d = x.shape
    sh1, sc1, g1 = mods[0], mods[1], mods[2]
    h = norm_mod(lay, x, nw, sc1, sh1, BF16)
    tm, tn = lay.tile(512), 256
    extras = ((lb, pl.BlockSpec((1, tn), lambda i, j: (0, j))),)
    q, f0, f1, iv, gs = matmul(h, p['w_in'], _epi_hgrn_in, [F32] * 5, tm=tm, tn=tn, extras=extras,
                               name="hgrn_in")
    o, sfin = hgrn_scan(lay, q, jnp.stack([f0, f1]), iv, s0t)
    z = hgrn_post(lay, o, gs, p['norm_w'])
    x = matmul_gated_residual(lay, z, p['wo'], x, g1, tm=lay.tile(1024), tn=512, name="hgrn_o")
    return x, sfin


ML_DOWN_COLS = 1280
ML_KR_OFF = ML_Q_LORA + ML_KV_LORA
ML_KRS_OFF = ML_KR_OFF + LANES_V7X


def _rms(x, w):
    return x * lax.rsqrt(jnp.mean(x * x, axis=-1, keepdims=True) + NORM_EPS) * w


def _mla_mid_kernel(dn_ref, qw_ref, kvw_ref, cos_ref, sin_ref, qn_ref, ckv_ref, kr_ref):
    dn = dn_ref[...]
    qn_ref[...] = _rms(dn[:, :ML_Q_LORA], qw_ref[...]).astype(qn_ref.dtype)
    ckv_ref[...] = _rms(dn[:, ML_Q_LORA:ML_KR_OFF], kvw_ref[...])
    kr = dn[:, ML_KR_OFF:ML_KR_OFF + ML_ROPE]
    krs = dn[:, ML_KRS_OFF:ML_KRS_OFF + ML_ROPE]
    kr_ref[...] = kr * cos_ref[...] + krs * sin_ref[...]


def mla_mid(lay, dn, qw, kvw, cos, sin):
    n = dn.shape[0]
    tm = lay.tile(512)
    return pl.pallas_call(
        _mla_mid_kernel,
        grid=(n // tm,),
        in_specs=[pl.BlockSpec((tm, ML_DOWN_COLS), lambda i: (i, 0)),
                  pl.BlockSpec((1, ML_Q_LORA), lambda i: (0, 0)),
                  pl.BlockSpec((1, ML_KV_LORA), lambda i: (0, 0)),
                  pl.BlockSpec((tm, ML_ROPE), lambda i: (i, 0)),
                  pl.BlockSpec((tm, ML_ROPE), lambda i: (i, 0))],
        out_specs=[pl.BlockSpec((tm, ML_Q_LORA), lambda i: (i, 0)),
                   pl.BlockSpec((tm, ML_KV_LORA), lambda i: (i, 0)),
                   pl.BlockSpec((tm, ML_ROPE), lambda i: (i, 0))],
        out_shape=[jax.ShapeDtypeStruct((n, ML_Q_LORA), BF16),
                   jax.ShapeDtypeStruct((n, ML_KV_LORA), F32),
                   jax.ShapeDtypeStruct((n, ML_ROPE), F32)],
        compiler_params=_cparams(1), name="mla_mid",
    )(dn, qw, kvw, cos, sin)


def _epi_rope(accs, extras):
    cos, sin = extras
    return [accs[0] * cos + accs[1] * sin]


def _attn_kernel(qn_ref, qr_ref, kn_ref, kr_ref, v_ref, o_ref, *, scale):
    kr = kr_ref[...]
    outs = []
    for h in range(2):
        qn = qn_ref[:, h * ML_NOPE:(h + 1) * ML_NOPE]
        qr = qr_ref[:, h * ML_ROPE:(h + 1) * ML_ROPE]
        kn = kn_ref[:, h * ML_NOPE:(h + 1) * ML_NOPE]
        s = (lax.dot_general(qn, kn, (((1,), (1,)), ((), ())), preferred_element_type=F32)
             + lax.dot_general(qr, kr, (((1,), (1,)), ((), ())), preferred_element_type=F32)) * scale
        m = jnp.max(s, axis=-1, keepdims=True)
        p = jnp.exp(s - m)
        l = jnp.sum(p, axis=-1, keepdims=True)
        pr = (p / l).astype(BF16)
        outs.append(jnp.dot(pr, v_ref[:, h * ML_V:(h + 1) * ML_V], preferred_element_type=F32))
    o_ref[...] = jnp.concatenate(outs, axis=1).astype(o_ref.dtype)


def attention(qn, qr, kn, kr, v, *, n_seq, q_len, k_len, row0, tq):
    heads2 = qn.shape[1] // (2 * ML_NOPE)
    qb = q_len // tq
    rb0 = row0 // tq
    scale = 1.0 / math.sqrt(ML_NOPE + ML_ROPE)
    return pl.pallas_call(
        functools.partial(_attn_kernel, scale=scale),
        grid=(n_seq, heads2, qb),
        in_specs=[pl.BlockSpec((tq, 2 * ML_NOPE), lambda s, h, i: (rb0 + s * qb + i, h)),
                  pl.BlockSpec((tq, 2 * ML_ROPE), lambda s, h, i: (rb0 + s * qb + i, h)),
                  pl.BlockSpec((k_len, 2 * ML_NOPE), lambda s, h, i: (s, h)),
                  pl.BlockSpec((k_len, ML_ROPE), lambda s, h, i: (s, 0)),
                  pl.BlockSpec((k_len, 2 * ML_V), lambda s, h, i: (s, h))],
        out_specs=pl.BlockSpec((tq, 2 * ML_V), lambda s, h, i: (s * qb + i, h)),
        out_shape=jax.ShapeDtypeStruct((n_seq * q_len, heads2 * 2 * ML_V), BF16),
        compiler_params=_cparams(3), name="mla_attn",
    )(qn, qr, kn, kr, v)


def mla_layer(lay, x, mods, nw, p, cache_ckv, cache_kr, cos, sin, cos2, sin2):
    n, d = x.shape
    sh1, sc1, g1 = mods[0], mods[1], mods[2]
    h = norm_mod(lay, x, nw, sc1, sh1, BF16)
    tm = lay.tile(512)
    dn = matmul(h, [p['w_down']], _epi_plain, [F32], tm=tm, tn=ML_DOWN_COLS, name="mla_down")[0]
    qlat, ckv, kr = mla_mid(lay, dn, p['qnorm_w'], p['kvnorm_w'], cos, sin)
    qn = matmul(qlat, [p['w_uq_nope']], _epi_plain, [BF16], tm=tm, tn=512, name="mla_qn")[0]
    tn = 2 * ML_ROPE
    extras = ((cos2, pl.BlockSpec((tm, tn), lambda i, j: (i, 0))),
              (sin2, pl.BlockSpec((tm, tn), lambda i, j: (i, 0))))
    qr = matmul(qlat, [p['w_uq_rope'], p['w_uq_rope_sw']], _epi_rope, [BF16], tm=tm, tn=tn,
                extras=extras, name="mla_qr")[0]
    nc, past = lay.nc, cache_ckv.shape[1]
    ckv_b, kr_b = ckv.astype(BF16), kr.astype(BF16)
    kn_c, v_c = matmul(ckv_b[:nc], [p['w_ukn'], p['w_uv']], _epi_plain, [BF16, BF16],
                       tm=lay.tile(512), tn=512, name="mla_kv_ctx")
    o_c = attention(qn, qr, kn_c, kr_b[:nc], v_c, n_seq=lay.n_ctx, q_len=lay.ctx_len,
                    k_len=lay.ctx_len, row0=0, tq=min(256, lay.ctx_len))
    k_len = lay.lat_len + past
    ckv_l = jnp.concatenate([ckv_b[nc:].reshape(lay.n_lat, lay.lat_len, -1), cache_ckv.astype(BF16)],
                            axis=1).reshape(lay.n_lat * k_len, -1)
    kr_l = jnp.concatenate([kr_b[nc:].reshape(lay.n_lat, lay.lat_len, -1), cache_kr.astype(BF16)],
                           axis=1).reshape(lay.n_lat * k_len, -1)
    tk = math.gcd(k_len, 512)
    kn_l, v_l = matmul(ckv_l, [p['w_ukn'], p['w_uv']], _epi_plain, [BF16, BF16], tm=tk, tn=512,
                       name="mla_kv_lat")
    o_l = attention(qn, qr, kn_l, kr_l, v_l, n_seq=lay.n_lat, q_len=lay.lat_len, k_len=k_len,
                    row0=nc, tq=min(256, lay.lat_len))
    o = jnp.concatenate([o_c, o_l], axis=0)
    x = matmul_gated_residual(lay, o, p['wo'], x, g1, tm=lay.tile(1024), tn=512, name="mla_o")
    return x, ckv[:nc], kr[:nc]


def _rope_tables(lay):
    t = lay.lat_len
    rows = t // GRID_W
    rr = jnp.broadcast_to(jnp.arange(rows, dtype=F32)[:, None], (rows, GRID_W)).reshape(-1)
    cc = jnp.broadcast_to(jnp.arange(GRID_W, dtype=F32)[None, :], (rows, GRID_W)).reshape(-1)
    nf = ML_ROPE // 4
    inv = ROPE_BASE ** (-jnp.arange(nf, dtype=F32) / nf)
    ar, ac = rr[:, None] * inv, cc[:, None] * inv
    cos = jnp.concatenate([jnp.cos(ar), jnp.cos(ar), jnp.cos(ac), jnp.cos(ac)], axis=-1)
    sin = jnp.concatenate([-jnp.sin(ar), jnp.sin(ar), -jnp.sin(ac), jnp.sin(ac)], axis=-1)
    cos = jnp.concatenate([jnp.ones((lay.nc, ML_ROPE), F32), jnp.tile(cos, (lay.n_lat, 1))], axis=0)
    sin = jnp.concatenate([jnp.zeros((lay.nc, ML_ROPE), F32), jnp.tile(sin, (lay.n_lat, 1))], axis=0)
    return cos, sin


def _swap_cols(w):
    k, c = w.shape
    w4 = w.reshape(k, c // 32, 2, 16)
    return w4[:, :, ::-1, :].reshape(k, c)


def ffn(lay, x, mods, nw, w_a, w_b, w_out):
    sh2, sc2, g2 = mods[3], mods[4], mods[5]
    h = norm_mod(lay, x, nw, sc2, sh2, BF16)
    act = matmul(h, [w_a, w_b], _epi_swiglu, [BF16], tm=lay.tile(1024), tn=512, name="ffn_in")[0]
    return matmul_gated_residual(lay, act, w_out, x, g2, tm=lay.tile(512), tn=512, name="ffn_out")


def _block_diag_states(s):
    b, two, h, n, _ = s.shape
    s = s.reshape(b, two, h // RW_GROUP, RW_GROUP, n, n)
    eye = jnp.eye(RW_GROUP, dtype=s.dtype)
    out = jnp.einsum('bdghvk,hi->bdghvik', s, eye)
    return out.reshape(b, two, h // RW_GROUP, RW_GROUP * n, RW_GROUP * n)


def _diag_blocks(s):
    b, two, g, l, _ = s.shape
    n = l // RW_GROUP
    s = s.reshape(b, two, g, RW_GROUP, n, RW_GROUP, n)
    s = jnp.moveaxis(jnp.diagonal(s, axis1=3, axis2=5), -1, 3)
    return s.reshape(b, two, g * RW_GROUP, n, n)


def kernel(x_prompt, x_sample, state_rwkv, state_hgrn, cache_ckv, cache_krope, c, c_ctx, ada_w, ada_b, norm1_w, norm2_w, ffn_w_in, ffn_w_out, final_norm_w, rw_mu, rw_wr, rw_wk, rw_wv, rw_wo, rw_w0, rw_w1, rw_w2, rw_a0, rw_a1, rw_a2, rw_g1, rw_g2, rw_kk, rw_ka, rw_rk, rw_lnx_w, rw_lnx_b, hg_w_in, hg_lb, hg_norm_w, hg_wo, ml_w_down, ml_qnorm_w, ml_kvnorm_w, ml_w_uq, ml_w_ukv, ml_wo):
    n_ctx, ctx_len, d = x_prompt.shape
    n_lat, lat_len, _ = x_sample.shape
    depth = ada_w.shape[0]
    lay = Layout(n_ctx, ctx_len, n_lat, lat_len)
    d_ff = ffn_w_out.shape[1]
    x = jnp.concatenate([x_prompt.reshape(lay.nc, d), x_sample.reshape(n_lat * lat_len, d)], axis=0)

    n_cond = -(-(1 + n_lat) // SUBLANES_V7X) * SUBLANES_V7X
    cond = jnp.zeros((n_cond, d), F32).at[0].set(c_ctx).at[1:1 + n_lat].set(c)
    mod_all = adaln(cond, ada_w, ada_b)
    mod_all = mod_all.reshape(depth, n_cond, 6, 1, d).transpose(0, 2, 1, 3, 4)

    lb_table = jnp.cumsum(jax.nn.softmax(hg_lb.astype(F32), axis=0), axis=0)
    lb_table = lb_table - lb_table[0]
    cos, sin = _rope_tables(lay)
    cos2, sin2 = jnp.tile(cos, (1, 2)), jnp.tile(sin, (1, 2))
    bf = lambda t: t.astype(BF16)

    new_rwkv, new_hgrn, new_ckv, new_krope = [], [], [], []
    for l in range(depth):
        kind, j = l % 3, l // 3
        mods = mod_all[l]
        if kind == 0:
            pad1 = lambda w: jnp.pad(w, ((0, 0), (0, 0), (0, LORA_PAD - w.shape[2])))
            pad2 = lambda w: jnp.pad(w, ((0, 0), (0, LORA_PAD - w.shape[1]), (0, 0)))
            p = {'mu': rw_mu[j], 'wr': bf(rw_wr[j]), 'wk': bf(rw_wk[j]), 'wv': bf(rw_wv[j]),
                 'wo': bf(rw_wo[j]),
                 'w0': rw_w0[j].reshape(2, 1, d), 'w1': bf(pad1(rw_w1[j])), 'w2': bf(pad2(rw_w2[j])),
                 'a0': rw_a0[j].reshape(2, 1, d), 'a1': bf(pad1(rw_a1[j])), 'a2': bf(pad2(rw_a2[j])),
                 'g1': bf(rw_g1[j]), 'g2': bf(rw_g2[j]),
                 'kk': rw_kk[j].reshape(1, d), 'ka': rw_ka[j].reshape(1, d),
                 'rk': rw_rk[j].reshape(2, 1, d),
                 'lnx_w': rw_lnx_w[j].reshape(1, d), 'lnx_b': rw_lnx_b[j].reshape(1, d)}
            s_lat = _block_diag_states(state_rwkv[:, j].astype(F32))
            s0 = jnp.concatenate([jnp.zeros((1,) + s_lat.shape[1:], F32), s_lat], axis=0)
            x, sfin = rwkv_layer(lay, x, mods, norm1_w[l], p, s0)
            new_rwkv.append(_diag_blocks(sfin[:n_ctx]))
        elif kind == 1:
            hk = d
            w_in = hg_w_in[j]
            p = {'w_in': [bf(w_in[:, i * hk:(i + 1) * hk]) for i in range(5)],
                 'norm_w': hg_norm_w[j].reshape(1, HG_K), 'wo': bf(hg_wo[j])}
            s_lat = jnp.swapaxes(state_hgrn[:, j].astype(F32), -1, -2)
            s0t = jnp.concatenate([jnp.zeros((1,) + s_lat.shape[1:], F32), s_lat], axis=0)
            x, sfin = hgrn_layer(lay, x, mods, norm1_w[l], p, lb_table[l].reshape(1, d), s0t)
            new_hgrn.append(jnp.swapaxes(sfin[:n_ctx], -1, -2))
        else:
            wd = ml_w_down[j]
            kr_w = wd[:, ML_KR_OFF:]
            zpad = jnp.zeros((d, LANES_V7X - ML_ROPE), wd.dtype)
            w_down = jnp.concatenate([wd, zpad, _swap_cols(kr_w), zpad], axis=1)
            wq = ml_w_uq[j].reshape(ML_Q_LORA, ML_H, ML_NOPE + ML_ROPE)
            wq_n = wq[:, :, :ML_NOPE].reshape(ML_Q_LORA, ML_H * ML_NOPE)
            wq_r = wq[:, :, ML_NOPE:].reshape(ML_Q_LORA, ML_H * ML_ROPE)
            wkv = ml_w_ukv[j].reshape(ML_KV_LORA, ML_H, ML_NOPE + ML_V)
            p = {'w_down': bf(w_down), 'qnorm_w': ml_qnorm_w[j].reshape(1, -1),
                 'kvnorm_w': ml_kvnorm_w[j].reshape(1, -1),
                 'w_uq_nope': bf(wq_n), 'w_uq_rope': bf(wq_r), 'w_uq_rope_sw': bf(_swap_cols(wq_r)),
                 'w_ukn': bf(wkv[:, :, :ML_NOPE].reshape(ML_KV_LORA, ML_H * ML_NOPE)),
                 'w_uv': bf(wkv[:, :, ML_NOPE:].reshape(ML_KV_LORA, ML_H * ML_V)),
                 'wo': bf(ml_wo[j])}
            x, ckv_c, kr_c = mla_layer(lay, x, mods, norm1_w[l], p, cache_ckv[:, j], cache_krope[:, j],
                                       cos, sin, cos2, sin2)
            new_ckv.append(ckv_c.reshape(n_ctx, ctx_len, ML_KV_LORA))
            new_krope.append(kr_c.reshape(n_ctx, ctx_len, ML_ROPE))
        w_in = ffn_w_in[l]
        x = ffn(lay, x, mods, norm2_w[l], bf(w_in[:, :d_ff]), bf(w_in[:, d_ff:]), bf(ffn_w_out[l]))

    zero = jnp.zeros((n_cond, 1, d), F32)
    y = norm_mod(lay, x, final_norm_w, zero, zero, F32)
    y_prompt = y[:lay.nc].reshape(n_ctx, ctx_len, d)
    y_sample = y[lay.nc:].reshape(n_lat, lat_len, d)
    return (y_prompt, y_sample, jnp.stack(new_rwkv, axis=1), jnp.stack(new_hgrn, axis=1),
            jnp.stack(new_ckv, axis=1), jnp.stack(new_krope, axis=1))
```

```python
import functools
import math

import numpy as np
import jax
import jax.numpy as jnp
from jax import lax
from jax.experimental import pallas as pl
from jax.experimental.pallas import tpu as pltpu

F32 = jnp.float32
BF16 = jnp.bfloat16

LANES_V7X = 128
SUBLANES_V7X = 8
VMEM_BYTES_V7X = 64 * 1024 * 1024
VMEM_LIMIT = 56 * 1024 * 1024

NORM_EPS = 1e-6
RW_HEAD = 64
RW_LN_EPS = 64e-5
RW_GROUP = 4
RW_LANES = RW_GROUP * RW_HEAD
RW_CHUNK = 64
RW_GSTEP = 2
HG_K = 128
HG_CHUNK_TOKENS = 64
HG_HSTEP = 2
ML_H = 16
ML_NOPE = 128
ML_ROPE = 64
ML_V = 128
ML_Q_LORA = 512
ML_KV_LORA = 512
GRID_W = 64
ROPE_BASE = 10000.0
LORA_PAD = 128


class Layout:
    def __init__(self, n_ctx, ctx_len, n_lat, lat_len):
        self.n_ctx, self.ctx_len, self.n_lat, self.lat_len = n_ctx, ctx_len, n_lat, lat_len
        self.nc = n_ctx * ctx_len
        self.n = self.nc + n_lat * lat_len
        self.tb = min(256, ctx_len)
        assert ctx_len % self.tb == 0 and lat_len % self.tb == 0 and self.tb % RW_CHUNK == 0
        self.nb = self.n // self.tb
        self.nb_ctx = self.nc // self.tb
        self.bps_ctx = ctx_len // self.tb
        self.bps_lat = lat_len // self.tb
        self.n_seq = n_ctx + n_lat

    def tile(self, want):
        t = want
        while self.nc % t or self.lat_len % t:
            t //= 2
        return t

    def cond_of_tile(self, i, tm):
        row = i * tm
        return jnp.where(row < self.nc, 0, 1 + (row - self.nc) // self.lat_len)

    def seq_info(self, blk):
        is_ctx = blk < self.nb_ctx
        lat = blk - self.nb_ctx
        seq = jnp.where(is_ctx, blk // self.bps_ctx, self.n_ctx + lat // self.bps_lat)
        pos = jnp.where(is_ctx, blk % self.bps_ctx, lat % self.bps_lat)
        cnt = jnp.where(is_ctx, self.bps_ctx, self.bps_lat)
        return seq, pos, cnt


def _cparams(n_axes):
    return pltpu.CompilerParams(dimension_semantics=("arbitrary",) * n_axes, vmem_limit_bytes=VMEM_LIMIT)


def _bdot(a, b):
    return jnp.dot(a.astype(BF16), b.astype(BF16), preferred_element_type=F32)


def _bdot_nt(a, b):
    return lax.dot_general(a.astype(BF16), b.astype(BF16), (((1,), (1,)), ((), ())),
                           preferred_element_type=F32)


def _silu(x):
    return x * jax.nn.sigmoid(x)


def _adaln_kernel(c_ref, w_ref, b_ref, o_ref):
    a = _silu(c_ref[...]).astype(BF16)
    o_ref[...] = jnp.dot(a, w_ref[...].astype(BF16), preferred_element_type=F32) + b_ref[...]


def adaln(cond, ada_w, ada_b):
    depth, d, d6 = ada_w.shape
    r = cond.shape[0]
    tn = 1024
    return pl.pallas_call(
        _adaln_kernel,
        grid=(depth, d6 // tn),
        in_specs=[pl.BlockSpec((r, d), lambda l, j: (0, 0)),
                  pl.BlockSpec((None, d, tn), lambda l, j: (l, 0, j)),
                  pl.BlockSpec((None, 1, tn), lambda l, j: (l, 0, j))],
        out_specs=pl.BlockSpec((None, r, tn), lambda l, j: (l, 0, j)),
        out_shape=jax.ShapeDtypeStruct((depth, r, d6), F32),
        compiler_params=_cparams(2), name="adaln",
    )(cond, ada_w, ada_b.reshape(depth, 1, d6))


def _norm_mod(x, nw, sc, sh):
    y = x * lax.rsqrt(jnp.mean(x * x, axis=-1, keepdims=True) + NORM_EPS)
    return (y * nw) * (1.0 + sc) + sh


def _norm_mod_kernel(x_ref, nw_ref, sc_ref, sh_ref, o_ref):
    o_ref[...] = _norm_mod(x_ref[...], nw_ref[...], sc_ref[...], sh_ref[...]).astype(o_ref.dtype)


def norm_mod(lay, x, nw, sc, sh, out_dtype):
    n, d = x.shape
    tm = lay.tile(512)
    cmap = lambda i: (lay.cond_of_tile(i, tm), 0, 0)
    return pl.pallas_call(
        _norm_mod_kernel,
        grid=(n // tm,),
        in_specs=[pl.BlockSpec((tm, d), lambda i: (i, 0)),
                  pl.BlockSpec((1, d), lambda i: (0, 0)),
                  pl.BlockSpec((None, 1, d), cmap),
                  pl.BlockSpec((None, 1, d), cmap)],
        out_specs=pl.BlockSpec((tm, d), lambda i: (i, 0)),
        out_shape=jax.ShapeDtypeStruct((n, d), out_dtype),
        compiler_params=_cparams(1), name="norm_mod",
    )(x, nw.reshape(1, d), sc, sh)


def _mm_kernel(*refs, n_w, n_e, epi):
    a = refs[0][...]
    accs = [jnp.dot(a, refs[1 + i][...], preferred_element_type=F32) for i in range(n_w)]
    extras = [refs[1 + n_w + i][...] for i in range(n_e)]
    outs = epi(accs, extras)
    o_refs = refs[1 + n_w + n_e:]
    for o_ref, val in zip(o_refs, outs):
        o_ref[...] = val.astype(o_ref.dtype)


def matmul(a, ws, epi, out_dtypes, *, tm, tn, extras=(), name):
    m, k = a.shape
    nw = ws[0].shape[1]
    assert m % tm == 0 and nw % tn == 0
    in_specs = [pl.BlockSpec((tm, k), lambda i, j: (i, 0))]
    in_specs += [pl.BlockSpec((k, tn), lambda i, j: (0, j)) for _ in ws]
    in_specs += [spec for _, spec in extras]
    outs = pl.pallas_call(
        functools.partial(_mm_kernel, n_w=len(ws), n_e=len(extras), epi=epi),
        grid=(m // tm, nw // tn),
        in_specs=in_specs,
        out_specs=[pl.BlockSpec((tm, tn), lambda i, j: (i, j)) for _ in out_dtypes],
        out_shape=[jax.ShapeDtypeStruct((m, nw), dt) for dt in out_dtypes],
        compiler_params=_cparams(2), name=name,
    )(a, *ws, *[arr for arr, _ in extras])
    return outs


def _epi_plain(accs, extras):
    return accs


def _epi_gated_residual(accs, extras):
    x, g = extras
    return [x + g * accs[0]]


def matmul_gated_residual(lay, a, w, x, gate, *, tm, tn, name):
    extras = ((x, pl.BlockSpec((tm, tn), lambda i, j: (i, j))),
              (gate, pl.BlockSpec((None, 1, tn), lambda i, j: (lay.cond_of_tile(i, tm), 0, j))))
    return matmul(a, [w], _epi_gated_residual, [F32], tm=tm, tn=tn, extras=extras, name=name)[0]


def _epi_swiglu(accs, extras):
    return [_silu(accs[0]) * accs[1]]


def _rwkv_prep_kernel(x_ref, xp_ref, xn_ref, nw_ref, sc_ref, sh_ref, mu_ref, *o_refs, lay):
    i = pl.program_id(0)
    _, pos, cnt = lay.seq_info(i)
    nw, sc, sh = nw_ref[...], sc_ref[...], sh_ref[...]
    h = _norm_mod(x_ref[...], nw, sc, sh)
    hp = _norm_mod(xp_ref[...], nw, sc, sh)[SUBLANES_V7X - 1:SUBLANES_V7X]
    hn = _norm_mod(xn_ref[...], nw, sc, sh)[0:1]
    hp = jnp.where(pos == 0, 0.0, hp)
    hn = jnp.where(pos == cnt - 1, 0.0, hn)
    tb = h.shape[0]
    row = lax.broadcasted_iota(jnp.int32, h.shape, 0)
    prev = jnp.where(row == 0, hp, pltpu.roll(h, 1, axis=0))
    nxt = jnp.where(row == tb - 1, hn, pltpu.roll(h, tb - 1, axis=0))
    xx = 0.5 * (prev + nxt) - h
    for idx, o_ref in enumerate(o_refs):
        o_ref[...] = (h + xx * mu_ref[idx:idx + 1, :]).astype(o_ref.dtype)


def rwkv_prep(lay, x, nw, sc, sh, mu):
    n, d = x.shape
    tb = lay.tb
    r8 = tb // SUBLANES_V7X
    last8 = n // SUBLANES_V7X - 1
    cmap = lambda i: (lay.cond_of_tile(i, tb), 0, 0)
    return pl.pallas_call(
        functools.partial(_rwkv_prep_kernel, lay=lay),
        grid=(n // tb,),
        in_specs=[pl.BlockSpec((tb, d), lambda i: (i, 0)),
                  pl.BlockSpec((SUBLANES_V7X, d), lambda i: (jnp.maximum(i * r8 - 1, 0), 0)),
                  pl.BlockSpec((SUBLANES_V7X, d), lambda i: (jnp.minimum((i + 1) * r8, last8), 0)),
                  pl.BlockSpec((1, d), lambda i: (0, 0)),
                  pl.BlockSpec((None, 1, d), cmap),
                  pl.BlockSpec((None, 1, d), cmap),
                  pl.BlockSpec((6, d), lambda i: (0, 0))],
        out_specs=[pl.BlockSpec((tb, d), lambda i: (i, 0))] * 6,
        out_shape=[jax.ShapeDtypeStruct((n, d), BF16)] * 6,
        compiler_params=_cparams(1), name="rwkv_prep",
    )(x, x, x, nw.reshape(1, d), sc, sh, mu)


def _softplus(z):
    return jnp.maximum(z, 0.0) + jnp.log(1.0 + jnp.exp(-jnp.abs(z)))


def _rwkv_lora_kernel(xw_ref, xa_ref, xg_ref, w1_ref, w2_ref, w0_ref, a1_ref, a2_ref, a0_ref,
                      g1_ref, g2_ref, lw_ref, a_ref, g_ref):
    xw, xa, xg = xw_ref[...], xa_ref[...], xg_ref[...]
    for d in range(2):
        t = jnp.tanh(jnp.dot(xw, w1_ref[d], preferred_element_type=F32))
        wl = w0_ref[d] + _bdot(t, w2_ref[d])
        wlog = -_softplus(-wl) - 0.5
        lw_ref[d] = -jnp.exp(wlog)
        t = jnp.dot(xa, a1_ref[d], preferred_element_type=F32)
        a_ref[d] = jax.nn.sigmoid(a0_ref[d] + _bdot(t, a2_ref[d]))
    t = jax.nn.sigmoid(jnp.dot(xg, g1_ref[...], preferred_element_type=F32))
    g_ref[...] = _bdot(t, g2_ref[...])


def rwkv_lora(lay, xw, xa, xg, w1, w2, w0, a1, a2, a0, g1, g2):
    n, d = xw.shape
    tm = lay.tile(128)
    full = lambda arr: pl.BlockSpec(arr.shape, lambda i: (0,) * arr.ndim)
    row = pl.BlockSpec((tm, d), lambda i: (i, 0))
    return pl.pallas_call(
        _rwkv_lora_kernel,
        grid=(n // tm,),
        in_specs=[row, row, row] + [full(t) for t in (w1, w2, w0, a1, a2, a0, g1, g2)],
        out_specs=[pl.BlockSpec((2, tm, d), lambda i: (0, i, 0)),
                   pl.BlockSpec((2, tm, d), lambda i: (0, i, 0)),
                   row],
        out_shape=[jax.ShapeDtypeStruct((2, n, d), F32), jax.ShapeDtypeStruct((2, n, d), F32),
                   jax.ShapeDtypeStruct((n, d), F32)],
        compiler_params=_cparams(1), name="rwkv_lora",
    )(xw, xa, xg, w1, w2, w0, a1, a2, a0, g1, g2)


def _wkv_constants():
    c, g, hd = RW_CHUNK, RW_GROUP, RW_HEAD
    gc, lanes = g * c, g * hd
    t = np.arange(c)
    cum = np.stack([(t[None, :] <= t[:, None]), (t[None, :] >= t[:, None])])
    tr = np.arange(c)[:, None]
    tc = np.arange(gc)[None, :] % c
    strict = np.stack([tc < tr, tc > tr])
    incl = np.stack([tc <= tr, tc >= tr])
    head_rows = np.arange(gc)[:, None] // c == np.arange(lanes)[None, :] // hd
    blk_rows = np.arange(gc)[:, None] // c == np.arange(gc)[None, :] // c
    bd = np.arange(lanes)[:, None] // hd == np.arange(lanes)[None, :] // hd
    eye_w = tr == tc
    return dict(cum=jnp.asarray(cum, BF16), strict=jnp.asarray(strict, F32), incl=jnp.asarray(incl, F32),
                head_rows=jnp.asarray(head_rows, BF16), blk_rows=jnp.asarray(blk_rows, BF16),
                bd=jnp.asarray(bd, F32), bd_b=jnp.asarray(bd, BF16), eye_w=jnp.asarray(eye_w, F32))


def _split_dot(m01, x, passes):
    acc, rem = None, x
    for _ in range(passes):
        part = rem.astype(BF16)
        term = jnp.dot(m01, part, preferred_element_type=F32)
        acc = term if acc is None else acc + term
        rem = rem - part.astype(F32)
    return acc


def _split_dot_r(x, m01, passes):
    acc, rem = None, x
    for _ in range(passes):
        part = rem.astype(BF16)
        term = jnp.dot(part, m01, preferred_element_type=F32)
        acc = term if acc is None else acc + term
        rem = rem - part.astype(F32)
    return acc


def _tile_rows(x, mask_b):
    return jnp.concatenate([x.astype(BF16)] * RW_GROUP, axis=0) * mask_b


def _wkv_tinv_many(items, head_rows, blk_rows, eye_w):
    c = RW_CHUNK
    n_ws = []
    for kk, a, lw, cum_m, strict_w in items:
        cum = _split_dot(cum_m, lw, 3)
        kkq = kk * jnp.exp(cum - lw)
        kkah = kk * a * jnp.exp(-cum)
        n_ws.append(-jnp.where(strict_w > 0, _bdot_nt(kkq, _tile_rows(kkah, head_rows)), 0.0))
    t_ws = [eye_w + n_w for n_w in n_ws]
    n_pows = [_bdot(n_w, _tile_rows(n_w, blk_rows)) for n_w in n_ws]
    levels = int(math.log2(c))
    for lv in range(1, levels):
        ws = [_tile_rows(n_pow, blk_rows) for n_pow in n_pows]
        if lv < levels - 1:
            boths = [_bdot(jnp.concatenate([t_w, n_pow], axis=0), w) for t_w, n_pow, w in zip(t_ws, n_pows, ws)]
            t_ws = [t_w + both[:c] for t_w, both in zip(t_ws, boths)]
            n_pows = [both[c:] for both in boths]
        else:
            t_ws = [t_w + _bdot(t_w, w) for t_w, w in zip(t_ws, ws)]
    return t_ws


def _wkv_prep(r, v, kk, kd, a, lw, cum_m, strict_w, incl_w, head_rows):
    c = RW_CHUNK
    cum = _split_dot(cum_m, lw, 3)
    tot = jnp.sum(lw, axis=0, keepdims=True)
    kka = kk * a
    e_inv, e_rest = jnp.exp(-cum), jnp.exp(tot - cum)
    kkq, rq = kk * jnp.exp(cum - lw), r * jnp.exp(cum)
    q2 = jnp.concatenate([kkq, rq], axis=0).astype(BF16)
    s1 = _bdot_nt(q2, _tile_rows(kd * e_inv, head_rows))
    l_d = jnp.where(strict_w > 0, s1[:c], 0.0).astype(BF16)
    a_d = jnp.where(incl_w > 0, s1[c:], 0.0)
    a_a = jnp.where(incl_w > 0, _bdot_nt(q2[c:], _tile_rows(kka * e_inv, head_rows)), 0.0)
    vbd = _tile_rows(v, head_rows)
    return dict(q2=q2, l_dv=_bdot(l_d, vbd), a_w=jnp.concatenate([a_d, -a_a], axis=1).astype(BF16), vbd=vbd,
                v=v, kw=jnp.concatenate([kd * e_rest, -(kka * e_rest)], axis=0).astype(BF16),
                decay=jnp.exp(tot))


def _wkv_advance(states, preps, t_ws, head_rows, bd):
    c = RW_CHUNK
    p0s = [_bdot_nt(p['q2'], s) for p, s in zip(preps, states)]
    us = [_bdot(t_w, _tile_rows(p0[:c] + p['l_dv'], head_rows))
          for p0, p, t_w in zip(p0s, preps, t_ws)]
    upds = [_bdot(jnp.concatenate([p['v'], u], axis=0).T, p['kw']) for p, u in zip(preps, us)]
    new_states = [s * p['decay'] + jnp.where(bd > 0, upd, 0.0) for s, p, upd in zip(states, preps, upds)]
    ys = [p0[c:] + _bdot(p['a_w'], jnp.concatenate([p['vbd'], _tile_rows(u, head_rows)], axis=0))
          for p0, p, u in zip(p0s, preps, us)]
    return new_states, ys


def _rwkv_mid_kernel(r_ref, k_ref, v_ref, a_ref, kkw_ref, kaw_ref, rk_ref, bd_ref, kk_ref, kd_ref, b_ref):
    r, k, v = r_ref[...], k_ref[...], v_ref[...]
    bd = bd_ref[...]
    kk = k * kkw_ref[...]
    kk_ref[...] = kk * lax.rsqrt(_split_dot_r(kk * kk, bd, 2) + 1e-12)
    bonus = None
    for d in range(2):
        kd = k * (1.0 + (a_ref[d] - 1.0) * kaw_ref[...])
        kd_ref[d] = kd
        term = _split_dot_r(r * kd * rk_ref[d], bd, 2) * v
        bonus = term if bonus is None else bonus + term
    b_ref[...] = bonus


def rwkv_mid(lay, r, k, v, a, kkw, kaw, rk):
    n, dm = r.shape
    lanes = RW_LANES
    tm = lay.tile(512)
    bd_b = _wkv_constants()['bd_b']
    tok = pl.BlockSpec((tm, lanes), lambda i, g: (i, g))
    two = pl.BlockSpec((2, tm, lanes), lambda i, g: (0, i, g))
    par = pl.BlockSpec((1, lanes), lambda i, g: (0, g))
    return pl.pallas_call(
        _rwkv_mid_kernel,
        grid=(n // tm, dm // lanes),
        in_specs=[tok, tok, tok, two, par, par, pl.BlockSpec((2, 1, lanes), lambda i, g: (0, 0, g)),
                  pl.BlockSpec(bd_b.shape, lambda i, g: (0, 0))],
        out_specs=[tok, two, tok],
        out_shape=[jax.ShapeDtypeStruct((n, dm), F32), jax.ShapeDtypeStruct((2, n, dm), F32),
                   jax.ShapeDtypeStruct((n, dm), F32)],
        compiler_params=_cparams(2), name="rwkv_mid",
    )(r, k, v, a, kkw, kaw, rk, bd_b)


def _wkv_inv_kernel(kk_ref, a_ref, lw_ref, cum_ref, strict_ref, hr_ref, br_ref, eye_ref, t_ref, *, lay):
    c = RW_CHUNK
    hr, br, eye_w = hr_ref[...], br_ref[...], eye_ref[...]
    where = [(d, slice(ci * c, (ci + 1) * c)) for d in range(2) for ci in range(lay.tb // c)]
    items = [(kk_ref[sl, :], a_ref[d, sl, :], lw_ref[d, sl, :], cum_ref[d], strict_ref[d]) for d, sl in where]
    for (d, sl), t_w in zip(where, _wkv_tinv_many(items, hr, br, eye_w)):
        t_ref[d, sl, :] = t_w.astype(t_ref.dtype)


def wkv_inv(lay, kk, a, lw):
    n, dm = kk.shape
    tb, lanes = lay.tb, RW_LANES
    k = _wkv_constants()
    full = lambda arr: pl.BlockSpec(arr.shape, lambda g, j: (0,) * arr.ndim)
    two = pl.BlockSpec((2, tb, lanes), lambda g, j: (0, j, g))
    return pl.pallas_call(
        functools.partial(_wkv_inv_kernel, lay=lay),
        grid=(dm // lanes, lay.nb),
        in_specs=[pl.BlockSpec((tb, lanes), lambda g, j: (j, g)), two, two,
                  full(k['cum']), full(k['strict']), full(k['head_rows']), full(k['blk_rows']),
                  full(k['eye_w'])],
        out_specs=two,
        out_shape=jax.ShapeDtypeStruct((2, n, dm), BF16),
        compiler_params=_cparams(2), name="wkv_inv",
    )(kk, a, lw, k['cum'], k['strict'], k['head_rows'], k['blk_rows'], k['eye_w'])


def _wkv_kernel(*refs, lay):
    (rf, vf, kkf, kdf, af, lwf, tf, s0f, rb, vb, kkb, kdb, ab, lwb, tb_, s0b,
     cum_ref, strict_ref, incl_ref, hr_ref, bd_ref, yf_ref, yb_ref, sff_ref, sfb_ref, sf_scr, sb_scr) = refs
    c = RW_CHUNK
    n_chunks = lay.tb // c
    j = pl.program_id(1)
    _, pos_f, cnt_f = lay.seq_info(j)
    _, pos_b, cnt_b = lay.seq_info(lay.nb - 1 - j)

    @pl.when(pos_f == 0)
    def _():
        sf_scr[...] = s0f[...]

    @pl.when(pos_b == cnt_b - 1)
    def _():
        sb_scr[...] = s0b[...]

    hr, bd = hr_ref[...], bd_ref[...]
    sl_f = [slice(ci * c, (ci + 1) * c) for ci in range(n_chunks)]
    sl_b = sl_f[::-1]
    cols = [slice(gi * RW_LANES, (gi + 1) * RW_LANES) for gi in range(RW_GSTEP)]
    preps = []
    for ci in range(n_chunks):
        row = []
        for cs in cols:
            sf, sb = sl_f[ci], sl_b[ci]
            row.append(_wkv_prep(rf[sf, cs], vf[sf, cs], kkf[sf, cs], kdf[sf, cs], af[sf, cs], lwf[sf, cs],
                                 cum_ref[0], strict_ref[0], incl_ref[0], hr))
            row.append(_wkv_prep(rb[sb, cs], vb[sb, cs], kkb[sb, cs], kdb[sb, cs], ab[sb, cs], lwb[sb, cs],
                                 cum_ref[1], strict_ref[1], incl_ref[1], hr))
        preps.append(row)
    states = []
    for gi in range(RW_GSTEP):
        states += [sf_scr[gi], sb_scr[gi]]
    for ci in range(n_chunks):
        t_ws = []
        for cs in cols:
            t_ws += [tf[sl_f[ci], cs], tb_[sl_b[ci], cs]]
        states, ys = _wkv_advance(states, preps[ci], t_ws, hr, bd)
        for gi, cs in enumerate(cols):
            yf_ref[sl_f[ci], cs] = ys[2 * gi]
            yb_ref[sl_b[ci], cs] = ys[2 * gi + 1]
    for gi in range(RW_GSTEP):
        sf_scr[gi] = states[2 * gi]
        sb_scr[gi] = states[2 * gi + 1]

    @pl.when(pos_f == cnt_f - 1)
    def _():
        sff_ref[...] = sf_scr[...]

    @pl.when(pos_b == 0)
    def _():
        sfb_ref[...] = sb_scr[...]


def wkv(lay, r, v, kk, kd, a, lw, t_inv, s0):
    n, dm = r.shape
    tb, lanes = lay.tb, RW_LANES
    ng = dm // lanes
    width = RW_GSTEP * lanes
    k = _wkv_constants()
    full = lambda arr: pl.BlockSpec(arr.shape, lambda g, j: (0,) * arr.ndim)

    def views(d, blk):
        tok = pl.BlockSpec((tb, width), lambda g, j: (blk(j), g))
        tok2 = pl.BlockSpec((None, tb, width), lambda g, j: (d, blk(j), g))

        def s0_map(g, j):
            seq, _, _ = lay.seq_info(blk(j))
            return (jnp.maximum(seq - lay.n_ctx + 1, 0), d, g, 0, 0)

        def sfin_map(g, j):
            seq, _, _ = lay.seq_info(blk(j))
            return (seq, g, 0, 0)

        ins = [tok, tok, tok, tok2, tok2, tok2, tok2,
               pl.BlockSpec((None, None, RW_GSTEP, lanes, lanes), s0_map)]
        return ins, tok, pl.BlockSpec((None, RW_GSTEP, lanes, lanes), sfin_map)

    in_f, y_f, sf_f = views(0, lambda j: j)
    in_b, y_b, sf_b = views(1, lambda j: lay.nb - 1 - j)
    consts = [k['cum'], k['strict'], k['incl'], k['head_rows'], k['bd']]
    args = [r, v, kk, kd, a, lw, t_inv, s0]
    return pl.pallas_call(
        functools.partial(_wkv_kernel, lay=lay),
        grid=(ng // RW_GSTEP, lay.nb),
        in_specs=in_f + in_b + [full(x) for x in consts],
        out_specs=[y_f, y_b, sf_f, sf_b],
        out_shape=[jax.ShapeDtypeStruct((n, dm), F32), jax.ShapeDtypeStruct((n, dm), F32),
                   jax.ShapeDtypeStruct((lay.n_seq, ng, lanes, lanes), F32),
                   jax.ShapeDtypeStruct((lay.n_seq, ng, lanes, lanes), F32)],
        scratch_shapes=[pltpu.VMEM((RW_GSTEP, lanes, lanes), F32), pltpu.VMEM((RW_GSTEP, lanes, lanes), F32)],
        compiler_params=_cparams(2), name="wkv",
    )(*args, *args, *consts)


def _rwkv_post_kernel(yf_ref, yb_ref, b_ref, g_ref, lnw_ref, lnb_ref, bd_ref, o_ref):
    y = yf_ref[...] + yb_ref[...]
    bd = bd_ref[...]
    inv = 1.0 / RW_HEAD
    mu = _split_dot_r(y, bd, 3) * inv
    yc = y - mu
    var = _split_dot_r(yc * yc, bd, 3) * inv
    yn = yc * lax.rsqrt(var + RW_LN_EPS)
    out = yn * lnw_ref[...] + lnb_ref[...] + b_ref[...]
    o_ref[...] = (out * g_ref[...]).astype(o_ref.dtype)


def rwkv_post(lay, y_f, y_b, bonus, g, lnw, lnb):
    n, dm = y_f.shape
    lanes = RW_LANES
    tm = lay.tile(512)
    bd_b = _wkv_constants()['bd_b']
    tok = pl.BlockSpec((tm, lanes), lambda i, c: (i, c))
    par = pl.BlockSpec((1, lanes), lambda i, c: (0, c))
    return pl.pallas_call(
        _rwkv_post_kernel,
        grid=(n // tm, dm // lanes),
        in_specs=[tok, tok, tok, tok, par, par, pl.BlockSpec(bd_b.shape, lambda i, c: (0, 0))],
        out_specs=tok,
        out_shape=jax.ShapeDtypeStruct((n, dm), BF16),
        compiler_params=_cparams(2), name="rwkv_post",
    )(y_f, y_b, bonus, g, lnw, lnb, bd_b)


def rwkv_layer(lay, x, mods, nw, p, s0):
    n, d = x.shape
    sh1, sc1, g1 = mods[0], mods[1], mods[2]
    xr, xw, xk, xv, xa, xg = rwkv_prep(lay, x, nw, sc1, sh1, p['mu'])
    tm = lay.tile(1024)
    r = matmul(xr, [p['wr']], _epi_plain, [F32], tm=tm, tn=512, name="rwkv_r")[0]
    k = matmul(xk, [p['wk']], _epi_plain, [F32], tm=tm, tn=512, name="rwkv_k")[0]
    v = matmul(xv, [p['wv']], _epi_plain, [F32], tm=tm, tn=512, name="rwkv_v")[0]
    lw, a, g = rwkv_lora(lay, xw, xa, xg, p['w1'], p['w2'], p['w0'], p['a1'], p['a2'], p['a0'],
                         p['g1'], p['g2'])
    kk, kd, bonus = rwkv_mid(lay, r, k, v, a, p['kk'], p['ka'], p['rk'])
    t_inv = wkv_inv(lay, kk, a, lw)
    y_f, y_b, sfin_f, sfin_b = wkv(lay, r, v, kk, kd, a, lw, t_inv, s0)
    z = rwkv_post(lay, y_f, y_b, bonus, g, p['lnx_w'], p['lnx_b'])
    x = matmul_gated_residual(lay, z, p['wo'], x, g1, tm=tm, tn=512, name="rwkv_o")
    return x, jnp.stack([sfin_f, sfin_b], axis=1)


def _hgrn_constants():
    c = HG_CHUNK_TOKENS
    t = np.arange(c)[:, None]
    j = np.arange(c)[None, :]
    cums, masks_all = [], []
    for rev in (False, True):
        masks = []
        h = 1
        while h < c:
            upper = (t % (2 * h)) >= h
            same = (t // (2 * h)) == (j // (2 * h))
            if not rev:
                mask = same & upper & ((j % (2 * h)) < h)
            else:
                mask = same & (~upper) & ((j % (2 * h)) >= h)
            masks.append(mask)
            h *= 2
        masks.append(t == j)
        cums.append((j >= t) if rev else (j <= t))
        masks_all.append(np.stack(masks, 0))
    return jnp.asarray(np.stack(cums), BF16), jnp.asarray(np.stack(masks_all).astype(np.float32))


def _hgrn_level_exponents(g, gcum, rev):
    c, kdim = g.shape
    row = lax.broadcasted_iota(jnp.int32, g.shape, 0)
    nxt = pltpu.roll(g, c - 1, axis=0)
    prv = pltpu.roll(g, 1, axis=0)
    r2, r4 = row & 1, row & 3
    if not rev:
        x1 = jnp.where(r2 == 1, g, 0.0)
        x2 = jnp.where(r4 == 0, nxt, jnp.where(r4 == 2, g, jnp.where(r4 == 3, prv + g, 0.0)))
    else:
        x1 = jnp.where(r2 == 0, g, 0.0)
        x2 = jnp.where(r4 == 0, g + nxt, jnp.where(r4 == 1, g, jnp.where(r4 == 3, prv, 0.0)))
    xs = [x1, x2]
    h = 4
    while h < c:
        gr = gcum.reshape(c // (2 * h), 2 * h, kdim)
        ref = gr[:, h:h + 1, :] if rev else gr[:, h - 1:h, :]
        upper = lax.broadcasted_iota(jnp.int32, gr.shape, 1) >= h
        diff = gr - ref
        x = jnp.where(upper, -diff, diff) if rev else jnp.where(upper, diff, -diff)
        xs.append(x.reshape(c, kdim))
        h *= 2
    return xs


def _hgrn_units(units, masks_by_dir, cum_by_dir):
    c = HG_CHUNK_TOKENS
    gs = [jnp.log(f) for _, f, _, _ in units]
    gcums = [_split_dot(cum_by_dir[rev], g, 3) for g, (_, _, _, rev) in zip(gs, units)]
    outs = []
    pend = []
    for (q, f, iv, rev), g, gcum in zip(units, gs, gcums):
        k = 1.0 - f
        tot = gcum[0:1] if rev else gcum[c - 1:c]
        es = [jnp.exp(x) for x in _hgrn_level_exponents(g, gcum, rev)]
        pend.append((q, k, iv, rev, es, jnp.exp(gcum), jnp.exp(tot - gcum), jnp.exp(tot)))
    for q, k, iv, rev, es, eg, erest, etot in pend:
        masks = masks_by_dir[rev]
        a = jnp.where(masks[len(es)] > 0, _bdot_nt(q, k), 0.0)
        for lv, el in enumerate(es):
            a = a + jnp.where(masks[lv] > 0, _bdot_nt(q * el, k * el), 0.0)
        outs.append(dict(a=a, iv=iv, qe=(q * eg).astype(BF16), kdec=(k * erest).astype(BF16), decay=etot))
    for u in outs:
        u['av'] = _bdot(u['a'], u['iv'])
    for u in outs:
        u['upd'] = _bdot(u['iv'].T, u['kdec'])
    return outs


def _hgrn_kernel(*refs, lay):
    (qf, ff, if_, s0f, qb, fb, ib, s0b, cum_ref, mask_ref, of_ref, ob_ref, sff_ref, sfb_ref,
     sf_scr, sb_scr) = refs
    c = HG_CHUNK_TOKENS
    n_chunks = lay.tb // c
    j = pl.program_id(1)
    _, pos_f, cnt_f = lay.seq_info(j)
    _, pos_b, cnt_b = lay.seq_info(lay.nb - 1 - j)

    @pl.when(pos_f == 0)
    def _():
        sf_scr[...] = s0f[...]

    @pl.when(pos_b == cnt_b - 1)
    def _():
        sb_scr[...] = s0b[...]

    masks_by_dir = [mask_ref[0], mask_ref[1]]
    cum_by_dir = [cum_ref[0], cum_ref[1]]
    sl_f = [slice(ci * c, (ci + 1) * c) for ci in range(n_chunks)]
    sl_b = sl_f[::-1]
    cols = [slice(hi * HG_K, (hi + 1) * HG_K) for hi in range(HG_HSTEP)]
    chains = []
    for hi, cs in enumerate(cols):
        chains.append((sf_scr, hi, of_ref, [(sl, cs) for sl in sl_f],
                       [(qf[sl, cs], ff[sl, cs], if_[sl, cs], False) for sl in sl_f]))
        chains.append((sb_scr, hi, ob_ref, [(sl, cs) for sl in sl_b],
                       [(qb[sl, cs], fb[sl, cs], ib[sl, cs], True) for sl in sl_b]))
    done = _hgrn_units([u for ch in chains for u in ch[4]], masks_by_dir, cum_by_dir)
    pend = []
    for n_ch, (scr, hi, o_ref, where, _) in enumerate(chains):
        s = scr[hi]
        for ci in range(n_chunks):
            u = done[n_ch * n_chunks + ci]
            pend.append((o_ref, where[ci], u, s))
            s = s * u['decay'] + u['upd']
        scr[hi] = s
    for o_ref, (sl, cs), u, s_prev in pend:
        o_ref[sl, cs] = u['av'] + _bdot_nt(u['qe'], s_prev)

    @pl.when(pos_f == cnt_f - 1)
    def _():
        sff_ref[...] = sf_scr[...]

    @pl.when(pos_b == 0)
    def _():
        sfb_ref[...] = sb_scr[...]


def hgrn_scan(lay, q, f_fwd, f_bwd, iv, s0t):
    n, dm = q.shape
    tb = lay.tb
    nh = dm // HG_K
    width = HG_HSTEP * HG_K
    cum_m, masks = _hgrn_constants()
    full = lambda arr: pl.BlockSpec(arr.shape, lambda h, j: (0,) * arr.ndim)

    def views(d, blk):
        tok = pl.BlockSpec((tb, width), lambda h, j: (blk(j), h))

        def s0_map(h, j):
            seq, _, _ = lay.seq_info(blk(j))
            return (jnp.maximum(seq - lay.n_ctx + 1, 0), d, h, 0, 0)

        def sfin_map(h, j):
            seq, _, _ = lay.seq_info(blk(j))
            return (seq, h, 0, 0)

        ins = [tok, tok, tok, pl.BlockSpec((None, None, HG_HSTEP, HG_K, HG_K), s0_map)]
        return ins, tok, pl.BlockSpec((None, HG_HSTEP, HG_K, HG_K), sfin_map)

    in_f, o_f, sf_f = views(0, lambda j: j)
    in_b, o_b, sf_b = views(1, lambda j: lay.nb - 1 - j)
    return pl.pallas_call(
        functools.partial(_hgrn_kernel, lay=lay),
        grid=(nh // HG_HSTEP, lay.nb),
        in_specs=in_f + in_b + [full(cum_m), full(masks)],
        out_specs=[o_f, o_b, sf_f, sf_b],
        out_shape=[jax.ShapeDtypeStruct((n, dm), F32), jax.ShapeDtypeStruct((n, dm), F32),
                   jax.ShapeDtypeStruct((lay.n_seq, nh, HG_K, HG_K), F32),
                   jax.ShapeDtypeStruct((lay.n_seq, nh, HG_K, HG_K), F32)],
        scratch_shapes=[pltpu.VMEM((HG_HSTEP, HG_K, HG_K), F32), pltpu.VMEM((HG_HSTEP, HG_K, HG_K), F32)],
        compiler_params=_cparams(2), name="hgrn_scan",
    )(q, f_fwd, iv, s0t, q, f_bwd, iv, s0t, cum_m, masks)


def _hgrn_post_kernel(of_ref, ob_ref, g_ref, nw_ref, z_ref):
    o = of_ref[...] + ob_ref[...]
    o = o * lax.rsqrt(jnp.mean(o * o, axis=-1, keepdims=True) + NORM_EPS) * nw_ref[...] * g_ref[...]
    z_ref[...] = o.astype(z_ref.dtype)


def hgrn_post(lay, o_f, o_b, gs, nw):
    n, dm = o_f.shape
    tm = lay.tile(1024)
    tok = pl.BlockSpec((tm, HG_K), lambda i, h: (i, h))
    return pl.pallas_call(
        _hgrn_post_kernel,
        grid=(n // tm, dm // HG_K),
        in_specs=[tok, tok, tok, pl.BlockSpec((1, HG_K), lambda i, h: (0, 0))],
        out_specs=tok,
        out_shape=jax.ShapeDtypeStruct((n, dm), BF16),
        compiler_params=_cparams(2), name="hgrn_post",
    )(o_f, o_b, gs, nw)


def _epi_hgrn_in(accs, extras):
    lb = extras[0]
    q = _silu(accs[0])
    f0 = lb + (1.0 - lb) * jax.nn.sigmoid(accs[1])
    f1 = lb + (1.0 - lb) * jax.nn.sigmoid(accs[2])
    return [q, f0, f1, accs[3], _silu(accs[4])]


def hgrn_layer(lay, x, mods, nw, p, lb, s0t):
    n, d = x.shape
    sh1, sc1, g1 = mods[0], mods[1], mods[2]
    h = norm_mod(lay, x, nw, sc1, sh1, BF16)
    tm, tn = lay.tile(512), 256
    extras = ((lb, pl.BlockSpec((1, tn), lambda i, j: (0, j))),)
    q, f0, f1, iv, gs = matmul(h, p['w_in'], _epi_hgrn_in, [F32] * 5, tm=tm, tn=tn, extras=extras,
                               name="hgrn_in")
    o_f, o_b, sfin_f, sfin_b = hgrn_scan(lay, q, f0, f1, iv, s0t)
    z = hgrn_post(lay, o_f, o_b, gs, p['norm_w'])
    x = matmul_gated_residual(lay, z, p['wo'], x, g1, tm=lay.tile(1024), tn=512, name="hgrn_o")
    return x, jnp.stack([sfin_f, sfin_b], axis=1)


ML_DOWN_COLS = 1280
ML_KR_OFF = ML_Q_LORA + ML_KV_LORA
ML_KRS_OFF = ML_KR_OFF + LANES_V7X


def _rms(x, w):
    return x * lax.rsqrt(jnp.mean(x * x, axis=-1, keepdims=True) + NORM_EPS) * w


def _mla_mid_kernel(dn_ref, qw_ref, kvw_ref, cos_ref, sin_ref, qn_ref, ckv_ref, kr_ref):
    dn = dn_ref[...]
    qn_ref[...] = _rms(dn[:, :ML_Q_LORA], qw_ref[...]).astype(qn_ref.dtype)
    ckv_ref[...] = _rms(dn[:, ML_Q_LORA:ML_KR_OFF], kvw_ref[...])
    kr = dn[:, ML_KR_OFF:ML_KR_OFF + ML_ROPE]
    krs = dn[:, ML_KRS_OFF:ML_KRS_OFF + ML_ROPE]
    kr_ref[...] = kr * cos_ref[...] + krs * sin_ref[...]


def mla_mid(lay, dn, qw, kvw, cos, sin):
    n = dn.shape[0]
    tm = lay.tile(512)
    return pl.pallas_call(
        _mla_mid_kernel,
        grid=(n // tm,),
        in_specs=[pl.BlockSpec((tm, ML_DOWN_COLS), lambda i: (i, 0)),
                  pl.BlockSpec((1, ML_Q_LORA), lambda i: (0, 0)),
                  pl.BlockSpec((1, ML_KV_LORA), lambda i: (0, 0)),
                  pl.BlockSpec((tm, ML_ROPE), lambda i: (i, 0)),
                  pl.BlockSpec((tm, ML_ROPE), lambda i: (i, 0))],
        out_specs=[pl.BlockSpec((tm, ML_Q_LORA), lambda i: (i, 0)),
                   pl.BlockSpec((tm, ML_KV_LORA), lambda i: (i, 0)),
                   pl.BlockSpec((tm, ML_ROPE), lambda i: (i, 0))],
        out_shape=[jax.ShapeDtypeStruct((n, ML_Q_LORA), BF16),
                   jax.ShapeDtypeStruct((n, ML_KV_LORA), F32),
                   jax.ShapeDtypeStruct((n, ML_ROPE), F32)],
        compiler_params=_cparams(1), name="mla_mid",
    )(dn, qw, kvw, cos, sin)


ML_QSCALE = math.log2(math.e) / math.sqrt(ML_NOPE + ML_ROPE)


def _epi_qscale(accs, extras):
    return [accs[0] * ML_QSCALE]


def _epi_rope(accs, extras):
    cos, sin = extras
    return [(accs[0] * cos + accs[1] * sin) * ML_QSCALE]


def _attn_kernel(qn_ref, qr_ref, kn_ref, kr_ref, v_ref, o_ref, kc_scr):
    @pl.when(pl.program_id(2) == 0)
    def _():
        for h in range(2):
            kc_scr[h, :, :ML_NOPE] = kn_ref[:, h * ML_NOPE:(h + 1) * ML_NOPE]
            kc_scr[h, :, ML_NOPE:] = kr_ref[...]

    outs = []
    for h in range(2):
        q = jnp.concatenate([qn_ref[:, h * ML_NOPE:(h + 1) * ML_NOPE],
                             qr_ref[:, h * ML_ROPE:(h + 1) * ML_ROPE]], axis=1)
        s = lax.dot_general(q, kc_scr[h], (((1,), (1,)), ((), ())), preferred_element_type=F32)
        m = jnp.max(s, axis=-1, keepdims=True)
        p = jnp.exp2(s - m)
        l = jnp.sum(p, axis=-1, keepdims=True)
        o = jnp.dot(p.astype(BF16), v_ref[:, h * ML_V:(h + 1) * ML_V], preferred_element_type=F32)
        outs.append(o / l)
    o_ref[...] = jnp.concatenate(outs, axis=1).astype(o_ref.dtype)


def attention(qn, qr, kn, kr, v, *, n_seq, q_len, k_len, row0, tq):
    heads2 = qn.shape[1] // (2 * ML_NOPE)
    qb = q_len // tq
    rb0 = row0 // tq
    return pl.pallas_call(
        _attn_kernel,
        grid=(n_seq, heads2, qb),
        in_specs=[pl.BlockSpec((tq, 2 * ML_NOPE), lambda s, h, i: (rb0 + s * qb + i, h)),
                  pl.BlockSpec((tq, 2 * ML_ROPE), lambda s, h, i: (rb0 + s * qb + i, h)),
                  pl.BlockSpec((k_len, 2 * ML_NOPE), lambda s, h, i: (s, h)),
                  pl.BlockSpec((k_len, ML_ROPE), lambda s, h, i: (s, 0)),
                  pl.BlockSpec((k_len, 2 * ML_V), lambda s, h, i: (s, h))],
        out_specs=pl.BlockSpec((tq, 2 * ML_V), lambda s, h, i: (s * qb + i, h)),
        out_shape=jax.ShapeDtypeStruct((n_seq * q_len, heads2 * 2 * ML_V), BF16),
        scratch_shapes=[pltpu.VMEM((2, k_len, ML_NOPE + ML_ROPE), BF16)],
        compiler_params=_cparams(3), name="mla_attn",
    )(qn, qr, kn, kr, v)


def mla_layer(lay, x, mods, nw, p, cache_ckv, cache_kr, cos, sin, cos2, sin2):
    n, d = x.shape
    sh1, sc1, g1 = mods[0], mods[1], mods[2]
    h = norm_mod(lay, x, nw, sc1, sh1, BF16)
    tm = lay.tile(512)
    dn = matmul(h, [p['w_down']], _epi_plain, [F32], tm=tm, tn=ML_DOWN_COLS, name="mla_down")[0]
    qlat, ckv, kr = mla_mid(lay, dn, p['qnorm_w'], p['kvnorm_w'], cos, sin)
    qn = matmul(qlat, [p['w_uq_nope']], _epi_qscale, [BF16], tm=tm, tn=512, name="mla_qn")[0]
    tn = 2 * ML_ROPE
    extras = ((cos2, pl.BlockSpec((tm, tn), lambda i, j: (i, 0))),
              (sin2, pl.BlockSpec((tm, tn), lambda i, j: (i, 0))))
    qr = matmul(qlat, [p['w_uq_rope'], p['w_uq_rope_sw']], _epi_rope, [BF16], tm=tm, tn=tn,
                extras=extras, name="mla_qr")[0]
    nc, past = lay.nc, cache_ckv.shape[1]
    ckv_b, kr_b = ckv.astype(BF16), kr.astype(BF16)
    kn_c, v_c = matmul(ckv_b[:nc], [p['w_ukn'], p['w_uv']], _epi_plain, [BF16, BF16],
                       tm=lay.tile(512), tn=512, name="mla_kv_ctx")
    o_c = attention(qn, qr, kn_c, kr_b[:nc], v_c, n_seq=lay.n_ctx, q_len=lay.ctx_len,
                    k_len=lay.ctx_len, row0=0, tq=min(256, lay.ctx_len))
    k_len = lay.lat_len + past
    ckv_l = jnp.concatenate([ckv_b[nc:].reshape(lay.n_lat, lay.lat_len, -1), cache_ckv.astype(BF16)],
                            axis=1).reshape(lay.n_lat * k_len, -1)
    kr_l = jnp.concatenate([kr_b[nc:].reshape(lay.n_lat, lay.lat_len, -1), cache_kr.astype(BF16)],
                           axis=1).reshape(lay.n_lat * k_len, -1)
    tk = math.gcd(k_len, 512)
    kn_l, v_l = matmul(ckv_l, [p['w_ukn'], p['w_uv']], _epi_plain, [BF16, BF16], tm=tk, tn=512,
                       name="mla_kv_lat")
    o_l = attention(qn, qr, kn_l, kr_l, v_l, n_seq=lay.n_lat, q_len=lay.lat_len, k_len=k_len,
                    row0=nc, tq=min(256, lay.lat_len))
    o = jnp.concatenate([o_c, o_l], axis=0)
    x = matmul_gated_residual(lay, o, p['wo'], x, g1, tm=lay.tile(1024), tn=512, name="mla_o")
    return x, ckv[:nc], kr[:nc]


def _rope_tables(lay):
    t = lay.lat_len
    rows = t // GRID_W
    rr = jnp.broadcast_to(jnp.arange(rows, dtype=F32)[:, None], (rows, GRID_W)).reshape(-1)
    cc = jnp.broadcast_to(jnp.arange(GRID_W, dtype=F32)[None, :], (rows, GRID_W)).reshape(-1)
    nf = ML_ROPE // 4
    inv = ROPE_BASE ** (-jnp.arange(nf, dtype=F32) / nf)
    ar, ac = rr[:, None] * inv, cc[:, None] * inv
    cos = jnp.concatenate([jnp.cos(ar), jnp.cos(ar), jnp.cos(ac), jnp.cos(ac)], axis=-1)
    sin = jnp.concatenate([-jnp.sin(ar), jnp.sin(ar), -jnp.sin(ac), jnp.sin(ac)], axis=-1)
    cos = jnp.concatenate([jnp.ones((lay.nc, ML_ROPE), F32), jnp.tile(cos, (lay.n_lat, 1))], axis=0)
    sin = jnp.concatenate([jnp.zeros((lay.nc, ML_ROPE), F32), jnp.tile(sin, (lay.n_lat, 1))], axis=0)
    return cos, sin


def _swap_cols(w):
    k, c = w.shape
    w4 = w.reshape(k, c // 32, 2, 16)
    return w4[:, :, ::-1, :].reshape(k, c)


def ffn(lay, x, mods, nw, w_a, w_b, w_out):
    sh2, sc2, g2 = mods[3], mods[4], mods[5]
    h = norm_mod(lay, x, nw, sc2, sh2, BF16)
    act = matmul(h, [w_a, w_b], _epi_swiglu, [BF16], tm=lay.tile(1024), tn=512, name="ffn_in")[0]
    return matmul_gated_residual(lay, act, w_out, x, g2, tm=lay.tile(512), tn=512, name="ffn_out")


def _block_diag_states(s):
    b, two, h, n, _ = s.shape
    s = s.reshape(b, two, h // RW_GROUP, RW_GROUP, n, n)
    eye = jnp.eye(RW_GROUP, dtype=s.dtype)
    out = jnp.einsum('bdghvk,hi->bdghvik', s, eye)
    return out.reshape(b, two, h // RW_GROUP, RW_GROUP * n, RW_GROUP * n)


def _diag_blocks(s):
    b, two, g, l, _ = s.shape
    n = l // RW_GROUP
    s = s.reshape(b, two, g, RW_GROUP, n, RW_GROUP, n)
    s = jnp.moveaxis(jnp.diagonal(s, axis1=3, axis2=5), -1, 3)
    return s.reshape(b, two, g * RW_GROUP, n, n)


def kernel(x_prompt, x_sample, state_rwkv, state_hgrn, cache_ckv, cache_krope, c, c_ctx, ada_w, ada_b, norm1_w, norm2_w, ffn_w_in, ffn_w_out, final_norm_w, rw_mu, rw_wr, rw_wk, rw_wv, rw_wo, rw_w0, rw_w1, rw_w2, rw_a0, rw_a1, rw_a2, rw_g1, rw_g2, rw_kk, rw_ka, rw_rk, rw_lnx_w, rw_lnx_b, hg_w_in, hg_lb, hg_norm_w, hg_wo, ml_w_down, ml_qnorm_w, ml_kvnorm_w, ml_w_uq, ml_w_ukv, ml_wo):
    n_ctx, ctx_len, d = x_prompt.shape
    n_lat, lat_len, _ = x_sample.shape
    depth = ada_w.shape[0]
    lay = Layout(n_ctx, ctx_len, n_lat, lat_len)
    d_ff = ffn_w_out.shape[1]
    x = jnp.concatenate([x_prompt.reshape(lay.nc, d), x_sample.reshape(n_lat * lat_len, d)], axis=0)

    n_cond = -(-(1 + n_lat) // SUBLANES_V7X) * SUBLANES_V7X
    cond = jnp.zeros((n_cond, d), F32).at[0].set(c_ctx).at[1:1 + n_lat].set(c)
    mod_all = adaln(cond, ada_w, ada_b)
    mod_all = mod_all.reshape(depth, n_cond, 6, 1, d).transpose(0, 2, 1, 3, 4)

    lb_table = jnp.cumsum(jax.nn.softmax(hg_lb.astype(F32), axis=0), axis=0)
    lb_table = lb_table - lb_table[0]
    cos, sin = _rope_tables(lay)
    cos2, sin2 = jnp.tile(cos, (1, 2)), jnp.tile(sin, (1, 2))
    bf = lambda t: t.astype(BF16)

    new_rwkv, new_hgrn, new_ckv, new_krope = [], [], [], []
    for l in range(depth):
        kind, j = l % 3, l // 3
        mods = mod_all[l]
        if kind == 0:
            pad1 = lambda w: jnp.pad(w, ((0, 0), (0, 0), (0, LORA_PAD - w.shape[2])))
            pad2 = lambda w: jnp.pad(w, ((0, 0), (0, LORA_PAD - w.shape[1]), (0, 0)))
            p = {'mu': rw_mu[j], 'wr': bf(rw_wr[j]), 'wk': bf(rw_wk[j]), 'wv': bf(rw_wv[j]),
                 'wo': bf(rw_wo[j]),
                 'w0': rw_w0[j].reshape(2, 1, d), 'w1': bf(pad1(rw_w1[j])), 'w2': bf(pad2(rw_w2[j])),
                 'a0': rw_a0[j].reshape(2, 1, d), 'a1': bf(pad1(rw_a1[j])), 'a2': bf(pad2(rw_a2[j])),
                 'g1': bf(rw_g1[j]), 'g2': bf(rw_g2[j]),
                 'kk': rw_kk[j].reshape(1, d), 'ka': rw_ka[j].reshape(1, d),
                 'rk': rw_rk[j].reshape(2, 1, d),
                 'lnx_w': rw_lnx_w[j].reshape(1, d), 'lnx_b': rw_lnx_b[j].reshape(1, d)}
            s_lat = _block_diag_states(state_rwkv[:, j].astype(F32))
            s0 = jnp.concatenate([jnp.zeros((1,) + s_lat.shape[1:], F32), s_lat], axis=0)
            x, sfin = rwkv_layer(lay, x, mods, norm1_w[l], p, s0)
            new_rwkv.append(_diag_blocks(sfin[:n_ctx]))
        elif kind == 1:
            hk = d
            w_in = hg_w_in[j]
            p = {'w_in': [bf(w_in[:, i * hk:(i + 1) * hk]) for i in range(5)],
                 'norm_w': hg_norm_w[j].reshape(1, HG_K), 'wo': bf(hg_wo[j])}
            s_lat = jnp.swapaxes(state_hgrn[:, j].astype(F32), -1, -2)
            s0t = jnp.concatenate([jnp.zeros((1,) + s_lat.shape[1:], F32), s_lat], axis=0)
            x, sfin = hgrn_layer(lay, x, mods, norm1_w[l], p, lb_table[l].reshape(1, d), s0t)
            new_hgrn.append(jnp.swapaxes(sfin[:n_ctx], -1, -2))
        else:
            wd = ml_w_down[j]
            kr_w = wd[:, ML_KR_OFF:]
            zpad = jnp.zeros((d, LANES_V7X - ML_ROPE), wd.dtype)
            w_down = jnp.concatenate([wd, zpad, _swap_cols(kr_w), zpad], axis=1)
            wq = ml_w_uq[j].reshape(ML_Q_LORA, ML_H, ML_NOPE + ML_ROPE)
            wq_n = wq[:, :, :ML_NOPE].reshape(ML_Q_LORA, ML_H * ML_NOPE)
            wq_r = wq[:, :, ML_NOPE:].reshape(ML_Q_LORA, ML_H * ML_ROPE)
            wkv = ml_w_ukv[j].reshape(ML_KV_LORA, ML_H, ML_NOPE + ML_V)
            p = {'w_down': bf(w_down), 'qnorm_w': ml_qnorm_w[j].reshape(1, -1),
                 'kvnorm_w': ml_kvnorm_w[j].reshape(1, -1),
                 'w_uq_nope': bf(wq_n), 'w_uq_rope': bf(wq_r), 'w_uq_rope_sw': bf(_swap_cols(wq_r)),
                 'w_ukn': bf(wkv[:, :, :ML_NOPE].reshape(ML_KV_LORA, ML_H * ML_NOPE)),
                 'w_uv': bf(wkv[:, :, ML_NOPE:].reshape(ML_KV_LORA, ML_H * ML_V)),
                 'wo': bf(ml_wo[j])}
            x, ckv_c, kr_c = mla_layer(lay, x, mods, norm1_w[l], p, cache_ckv[:, j], cache_krope[:, j],
                                       cos, sin, cos2, sin2)
            new_ckv.append(ckv_c.reshape(n_ctx, ctx_len, ML_KV_LORA))
            new_krope.append(kr_c.reshape(n_ctx, ctx_len, ML_ROPE))
        w_in = ffn_w_in[l]
        x = ffn(lay, x, mods, norm2_w[l], bf(w_in[:, :d_ff]), bf(w_in[:, d_ff:]), bf(ffn_w_out[l]))

    zero = jnp.zeros((n_cond, 1, d), F32)
    y = norm_mod(lay, x, final_norm_w, zero, zero, F32)
    y_prompt = y[:lay.nc].reshape(n_ctx, ctx_len, d)
    y_sample = y[lay.nc:].reshape(n_lat, lat_len, d)
    return (y_prompt, y_sample, jnp.stack(new_rwkv, axis=1), jnp.stack(new_hgrn, axis=1),
            jnp.stack(new_ckv, axis=1), jnp.stack(new_krope, axis=1))
```

```python
import functools
import math

import numpy as np
import jax
import jax.numpy as jnp
from jax import lax
from jax.experimental import pallas as pl
from jax.experimental.pallas import tpu as pltpu

F32 = jnp.float32
BF16 = jnp.bfloat16

LANES_V7X = 128
SUBLANES_V7X = 8
VMEM_BYTES_V7X = 64 * 1024 * 1024
VMEM_LIMIT = 56 * 1024 * 1024

NORM_EPS = 1e-6
RW_HEAD = 64
RW_LN_EPS = 64e-5
RW_GROUP = 4
RW_LANES = RW_GROUP * RW_HEAD
RW_CHUNK = 64
RW_GSTEP = 2
HG_K = 128
HG_CHUNK_TOKENS = 64
HG_HSTEP = 2
ML_H = 16
ML_NOPE = 128
ML_ROPE = 64
ML_V = 128
ML_Q_LORA = 512
ML_KV_LORA = 512
GRID_W = 64
ROPE_BASE = 10000.0
LORA_PAD = 128


class Layout:
    def __init__(self, n_ctx, ctx_len, n_lat, lat_len):
        self.n_ctx, self.ctx_len, self.n_lat, self.lat_len = n_ctx, ctx_len, n_lat, lat_len
        self.nc = n_ctx * ctx_len
        self.n = self.nc + n_lat * lat_len
        self.tb = min(256, ctx_len)
        assert ctx_len % self.tb == 0 and lat_len % self.tb == 0 and self.tb % RW_CHUNK == 0
        self.nb = self.n // self.tb
        self.nb_ctx = self.nc // self.tb
        self.bps_ctx = ctx_len // self.tb
        self.bps_lat = lat_len // self.tb
        self.n_seq = n_ctx + n_lat

    def tile(self, want):
        t = want
        while self.nc % t or self.lat_len % t:
            t //= 2
        return t

    def cond_of_tile(self, i, tm):
        row = i * tm
        return jnp.where(row < self.nc, 0, 1 + (row - self.nc) // self.lat_len)

    def seq_info(self, blk):
        is_ctx = blk < self.nb_ctx
        lat = blk - self.nb_ctx
        seq = jnp.where(is_ctx, blk // self.bps_ctx, self.n_ctx + lat // self.bps_lat)
        pos = jnp.where(is_ctx, blk % self.bps_ctx, lat % self.bps_lat)
        cnt = jnp.where(is_ctx, self.bps_ctx, self.bps_lat)
        return seq, pos, cnt


def _cparams(n_axes):
    return pltpu.CompilerParams(dimension_semantics=("arbitrary",) * n_axes, vmem_limit_bytes=VMEM_LIMIT)


def _bdot(a, b):
    return jnp.dot(a.astype(BF16), b.astype(BF16), preferred_element_type=F32)


def _bdot_nt(a, b):
    return lax.dot_general(a.astype(BF16), b.astype(BF16), (((1,), (1,)), ((), ())),
                           preferred_element_type=F32)


def _silu(x):
    return x * jax.nn.sigmoid(x)


def _adaln_kernel(c_ref, w_ref, b_ref, o_ref):
    a = _silu(c_ref[...]).astype(BF16)
    o_ref[...] = jnp.dot(a, w_ref[...].astype(BF16), preferred_element_type=F32) + b_ref[...]


def adaln(cond, ada_w, ada_b):
    depth, d, d6 = ada_w.shape
    r = cond.shape[0]
    tn = 1024
    return pl.pallas_call(
        _adaln_kernel,
        grid=(depth, d6 // tn),
        in_specs=[pl.BlockSpec((r, d), lambda l, j: (0, 0)),
                  pl.BlockSpec((None, d, tn), lambda l, j: (l, 0, j)),
                  pl.BlockSpec((None, 1, tn), lambda l, j: (l, 0, j))],
        out_specs=pl.BlockSpec((None, r, tn), lambda l, j: (l, 0, j)),
        out_shape=jax.ShapeDtypeStruct((depth, r, d6), F32),
        compiler_params=_cparams(2), name="adaln",
    )(cond, ada_w, ada_b.reshape(depth, 1, d6))


def _norm_mod(x, nw, sc, sh):
    y = x * lax.rsqrt(jnp.mean(x * x, axis=-1, keepdims=True) + NORM_EPS)
    return (y * nw) * (1.0 + sc) + sh


def _norm_mod_kernel(x_ref, nw_ref, sc_ref, sh_ref, o_ref):
    o_ref[...] = _norm_mod(x_ref[...], nw_ref[...], sc_ref[...], sh_ref[...]).astype(o_ref.dtype)


def norm_mod(lay, x, nw, sc, sh, out_dtype):
    n, d = x.shape
    tm = lay.tile(512)
    cmap = lambda i: (lay.cond_of_tile(i, tm), 0, 0)
    return pl.pallas_call(
        _norm_mod_kernel,
        grid=(n // tm,),
        in_specs=[pl.BlockSpec((tm, d), lambda i: (i, 0)),
                  pl.BlockSpec((1, d), lambda i: (0, 0)),
                  pl.BlockSpec((None, 1, d), cmap),
                  pl.BlockSpec((None, 1, d), cmap)],
        out_specs=pl.BlockSpec((tm, d), lambda i: (i, 0)),
        out_shape=jax.ShapeDtypeStruct((n, d), out_dtype),
        compiler_params=_cparams(1), name="norm_mod",
    )(x, nw.reshape(1, d), sc, sh)


def _mm_kernel(*refs, n_w, n_e, epi):
    a = refs[0][...]
    accs = [jnp.dot(a, refs[1 + i][...], preferred_element_type=F32) for i in range(n_w)]
    extras = [refs[1 + n_w + i][...] for i in range(n_e)]
    outs = epi(accs, extras)
    o_refs = refs[1 + n_w + n_e:]
    for o_ref, val in zip(o_refs, outs):
        o_ref[...] = val.astype(o_ref.dtype)


def matmul(a, ws, epi, out_dtypes, *, tm, tn, extras=(), name):
    m, k = a.shape
    nw = ws[0].shape[1]
    assert m % tm == 0 and nw % tn == 0
    in_specs = [pl.BlockSpec((tm, k), lambda i, j: (i, 0))]
    in_specs += [pl.BlockSpec((k, tn), lambda i, j: (0, j)) for _ in ws]
    in_specs += [spec for _, spec in extras]
    outs = pl.pallas_call(
        functools.partial(_mm_kernel, n_w=len(ws), n_e=len(extras), epi=epi),
        grid=(m // tm, nw // tn),
        in_specs=in_specs,
        out_specs=[pl.BlockSpec((tm, tn), lambda i, j: (i, j)) for _ in out_dtypes],
        out_shape=[jax.ShapeDtypeStruct((m, nw), dt) for dt in out_dtypes],
        compiler_params=_cparams(2), name=name,
    )(a, *ws, *[arr for arr, _ in extras])
    return outs


def _epi_plain(accs, extras):
    return accs


def _epi_gated_residual(accs, extras):
    x, g = extras
    return [x + g * accs[0]]


def matmul_gated_residual(lay, a, w, x, gate, *, tm, tn, name):
    extras = ((x, pl.BlockSpec((tm, tn), lambda i, j: (i, j))),
              (gate, pl.BlockSpec((None, 1, tn), lambda i, j: (lay.cond_of_tile(i, tm), 0, j))))
    return matmul(a, [w], _epi_gated_residual, [F32], tm=tm, tn=tn, extras=extras, name=name)[0]


def _epi_swiglu(accs, extras):
    return [_silu(accs[0]) * accs[1]]


def _rwkv_prep_kernel(x_ref, xp_ref, xn_ref, nw_ref, sc_ref, sh_ref, mu_ref, *o_refs, lay):
    i = pl.program_id(0)
    _, pos, cnt = lay.seq_info(i)
    nw, sc, sh = nw_ref[...], sc_ref[...], sh_ref[...]
    h = _norm_mod(x_ref[...], nw, sc, sh)
    hp = _norm_mod(xp_ref[...], nw, sc, sh)[SUBLANES_V7X - 1:SUBLANES_V7X]
    hn = _norm_mod(xn_ref[...], nw, sc, sh)[0:1]
    hp = jnp.where(pos == 0, 0.0, hp)
    hn = jnp.where(pos == cnt - 1, 0.0, hn)
    tb = h.shape[0]
    row = lax.broadcasted_iota(jnp.int32, h.shape, 0)
    prev = jnp.where(row == 0, hp, pltpu.roll(h, 1, axis=0))
    nxt = jnp.where(row == tb - 1, hn, pltpu.roll(h, tb - 1, axis=0))
    xx = 0.5 * (prev + nxt) - h
    for idx, o_ref in enumerate(o_refs):
        o_ref[...] = (h + xx * mu_ref[idx:idx + 1, :]).astype(o_ref.dtype)


def rwkv_prep(lay, x, nw, sc, sh, mu):
    n, d = x.shape
    tb = lay.tb
    r8 = tb // SUBLANES_V7X
    last8 = n // SUBLANES_V7X - 1
    cmap = lambda i: (lay.cond_of_tile(i, tb), 0, 0)
    return pl.pallas_call(
        functools.partial(_rwkv_prep_kernel, lay=lay),
        grid=(n // tb,),
        in_specs=[pl.BlockSpec((tb, d), lambda i: (i, 0)),
                  pl.BlockSpec((SUBLANES_V7X, d), lambda i: (jnp.maximum(i * r8 - 1, 0), 0)),
                  pl.BlockSpec((SUBLANES_V7X, d), lambda i: (jnp.minimum((i + 1) * r8, last8), 0)),
                  pl.BlockSpec((1, d), lambda i: (0, 0)),
                  pl.BlockSpec((None, 1, d), cmap),
                  pl.BlockSpec((None, 1, d), cmap),
                  pl.BlockSpec((6, d), lambda i: (0, 0))],
        out_specs=[pl.BlockSpec((tb, d), lambda i: (i, 0))] * 6,
        out_shape=[jax.ShapeDtypeStruct((n, d), BF16)] * 6,
        compiler_params=_cparams(1), name="rwkv_prep",
    )(x, x, x, nw.reshape(1, d), sc, sh, mu)


def _softplus(z):
    return jnp.maximum(z, 0.0) + jnp.log(1.0 + jnp.exp(-jnp.abs(z)))


def _rwkv_lora_kernel(xw_ref, xa_ref, xg_ref, w1_ref, w2_ref, w0_ref, a1_ref, a2_ref, a0_ref,
                      g1_ref, g2_ref, lw_ref, a_ref, g_ref):
    xw, xa, xg = xw_ref[...], xa_ref[...], xg_ref[...]
    for d in range(2):
        t = jnp.tanh(jnp.dot(xw, w1_ref[d], preferred_element_type=F32))
        wl = w0_ref[d] + _bdot(t, w2_ref[d])
        wlog = -_softplus(-wl) - 0.5
        lw_ref[d] = -jnp.exp(wlog)
        t = jnp.dot(xa, a1_ref[d], preferred_element_type=F32)
        a_ref[d] = jax.nn.sigmoid(a0_ref[d] + _bdot(t, a2_ref[d]))
    t = jax.nn.sigmoid(jnp.dot(xg, g1_ref[...], preferred_element_type=F32))
    g_ref[...] = _bdot(t, g2_ref[...])


def rwkv_lora(lay, xw, xa, xg, w1, w2, w0, a1, a2, a0, g1, g2):
    n, d = xw.shape
    tm = lay.tile(128)
    full = lambda arr: pl.BlockSpec(arr.shape, lambda i: (0,) * arr.ndim)
    row = pl.BlockSpec((tm, d), lambda i: (i, 0))
    return pl.pallas_call(
        _rwkv_lora_kernel,
        grid=(n // tm,),
        in_specs=[row, row, row] + [full(t) for t in (w1, w2, w0, a1, a2, a0, g1, g2)],
        out_specs=[pl.BlockSpec((2, tm, d), lambda i: (0, i, 0)),
                   pl.BlockSpec((2, tm, d), lambda i: (0, i, 0)),
                   row],
        out_shape=[jax.ShapeDtypeStruct((2, n, d), F32), jax.ShapeDtypeStruct((2, n, d), F32),
                   jax.ShapeDtypeStruct((n, d), F32)],
        compiler_params=_cparams(1), name="rwkv_lora",
    )(xw, xa, xg, w1, w2, w0, a1, a2, a0, g1, g2)


def _wkv_constants():
    c, g, hd = RW_CHUNK, RW_GROUP, RW_HEAD
    gc, lanes = g * c, g * hd
    t = np.arange(c)
    cum = np.stack([(t[None, :] <= t[:, None]), (t[None, :] >= t[:, None])])
    tr = np.arange(c)[:, None]
    tc = np.arange(gc)[None, :] % c
    strict = np.stack([tc < tr, tc > tr])
    incl = np.stack([tc <= tr, tc >= tr])
    head_rows = np.arange(gc)[:, None] // c == np.arange(lanes)[None, :] // hd
    blk_rows = np.arange(gc)[:, None] // c == np.arange(gc)[None, :] // c
    bd = np.arange(lanes)[:, None] // hd == np.arange(lanes)[None, :] // hd
    eye_w = tr == tc
    return dict(cum=jnp.asarray(cum, BF16), strict=jnp.asarray(strict, F32), incl=jnp.asarray(incl, F32),
                head_rows=jnp.asarray(head_rows, BF16), blk_rows=jnp.asarray(blk_rows, BF16),
                bd=jnp.asarray(bd, F32), bd_b=jnp.asarray(bd, BF16), eye_w=jnp.asarray(eye_w, F32))


def _split_dot(m01, x, passes):
    acc, rem = None, x
    for _ in range(passes):
        part = rem.astype(BF16)
        term = jnp.dot(m01, part, preferred_element_type=F32)
        acc = term if acc is None else acc + term
        rem = rem - part.astype(F32)
    return acc


def _split_dot_r(x, m01, passes):
    acc, rem = None, x
    for _ in range(passes):
        part = rem.astype(BF16)
        term = jnp.dot(part, m01, preferred_element_type=F32)
        acc = term if acc is None else acc + term
        rem = rem - part.astype(F32)
    return acc


def _tile_rows(x, mask_b):
    return jnp.concatenate([x.astype(BF16)] * RW_GROUP, axis=0) * mask_b


def _wkv_tinv_many(items, head_rows, blk_rows, eye_w):
    c = RW_CHUNK
    n_ws = []
    for kk, a, lw, cum_m, strict_w in items:
        cum = _split_dot(cum_m, lw, 3)
        kkq = kk * jnp.exp(cum - lw)
        kkah = kk * a * jnp.exp(-cum)
        n_ws.append(-jnp.where(strict_w > 0, _bdot_nt(kkq, _tile_rows(kkah, head_rows)), 0.0))
    t_ws = [eye_w + n_w for n_w in n_ws]
    n_pows = [_bdot(n_w, _tile_rows(n_w, blk_rows)) for n_w in n_ws]
    levels = int(math.log2(c))
    for lv in range(1, levels):
        ws = [_tile_rows(n_pow, blk_rows) for n_pow in n_pows]
        if lv < levels - 1:
            boths = [_bdot(jnp.concatenate([t_w, n_pow], axis=0), w) for t_w, n_pow, w in zip(t_ws, n_pows, ws)]
            t_ws = [t_w + both[:c] for t_w, both in zip(t_ws, boths)]
            n_pows = [both[c:] for both in boths]
        else:
            t_ws = [t_w + _bdot(t_w, w) for t_w, w in zip(t_ws, ws)]
    return t_ws


def _wkv_prep1(raw):
    return [_split_dot(u[6], u[5], 3) for u in raw]


def _wkv_prep2(raw, cums, head_rows):
    c = RW_CHUNK
    mids = []
    for (r, v, kk, kd, a, lw, _, _, _), cum in zip(raw, cums):
        tot = jnp.sum(lw, axis=0, keepdims=True)
        kka = kk * a
        e_inv, e_rest = jnp.exp(-cum), jnp.exp(tot - cum)
        q2 = jnp.concatenate([kk * jnp.exp(cum - lw), r * jnp.exp(cum)], axis=0).astype(BF16)
        mids.append(dict(q2=q2, kdh=_tile_rows(kd * e_inv, head_rows), kkah=_tile_rows(kka * e_inv, head_rows),
                         vbd=_tile_rows(v, head_rows), v=v, decay=jnp.exp(tot),
                         kw=jnp.concatenate([kd * e_rest, -(kka * e_rest)], axis=0).astype(BF16)))
    s1s = [_bdot_nt(m['q2'], m['kdh']) for m in mids]
    s2s = [_bdot_nt(m['q2'][c:], m['kkah']) for m in mids]
    for m, u, s1, s2 in zip(mids, raw, s1s, s2s):
        strict_w, incl_w = u[7], u[8]
        m['lad'] = jnp.concatenate([jnp.where(strict_w > 0, s1[:c], 0.0),
                                    jnp.where(incl_w > 0, s1[c:], 0.0)], axis=0).astype(BF16)
        m['a_a'] = jnp.where(incl_w > 0, s2, 0.0).astype(BF16)
    return mids


def _wkv_prep3(mids):
    for m, lav in zip(mids, [_bdot(m['lad'], m['vbd']) for m in mids]):
        m['lav'] = lav
    return mids


def _wkv_adv1(states, preps):
    return [_bdot_nt(p['q2'], s) for p, s in zip(preps, states)]


def _wkv_adv2(p0s, preps, t_ws, head_rows):
    c = RW_CHUNK
    return [_bdot(t_w, _tile_rows(p0[:c] + p['lav'][:c], head_rows))
            for p0, p, t_w in zip(p0s, preps, t_ws)]


def _wkv_adv3(states, p0s, us, preps, head_rows, bd):
    c = RW_CHUNK
    upds = [_bdot(jnp.concatenate([p['v'], u], axis=0).T, p['kw']) for p, u in zip(preps, us)]
    aus = [_bdot(p['a_a'], _tile_rows(u, head_rows)) for p, u in zip(preps, us)]
    new_states = [s * p['decay'] + jnp.where(bd > 0, upd, 0.0) for s, p, upd in zip(states, preps, upds)]
    ys = [p0[c:] + p['lav'][c:] - au for p0, p, au in zip(p0s, preps, aus)]
    return new_states, ys


def _rwkv_mid_kernel(r_ref, k_ref, v_ref, a_ref, kkw_ref, kaw_ref, rk_ref, bd_ref, kk_ref, kd_ref, b_ref):
    r, k, v = r_ref[...], k_ref[...], v_ref[...]
    bd = bd_ref[...]
    kk = k * kkw_ref[...]
    kk_ref[...] = kk * lax.rsqrt(_split_dot_r(kk * kk, bd, 2) + 1e-12)
    bonus = None
    for d in range(2):
        kd = k * (1.0 + (a_ref[d] - 1.0) * kaw_ref[...])
        kd_ref[d] = kd
        term = _split_dot_r(r * kd * rk_ref[d], bd, 2) * v
        bonus = term if bonus is None else bonus + term
    b_ref[...] = bonus


def rwkv_mid(lay, r, k, v, a, kkw, kaw, rk):
    n, dm = r.shape
    lanes = RW_LANES
    tm = lay.tile(512)
    bd_b = _wkv_constants()['bd_b']
    tok = pl.BlockSpec((tm, lanes), lambda i, g: (i, g))
    two = pl.BlockSpec((2, tm, lanes), lambda i, g: (0, i, g))
    par = pl.BlockSpec((1, lanes), lambda i, g: (0, g))
    return pl.pallas_call(
        _rwkv_mid_kernel,
        grid=(n // tm, dm // lanes),
        in_specs=[tok, tok, tok, two, par, par, pl.BlockSpec((2, 1, lanes), lambda i, g: (0, 0, g)),
                  pl.BlockSpec(bd_b.shape, lambda i, g: (0, 0))],
        out_specs=[tok, two, tok],
        out_shape=[jax.ShapeDtypeStruct((n, dm), F32), jax.ShapeDtypeStruct((2, n, dm), F32),
                   jax.ShapeDtypeStruct((n, dm), F32)],
        compiler_params=_cparams(2), name="rwkv_mid",
    )(r, k, v, a, kkw, kaw, rk, bd_b)


def _wkv_inv_kernel(kk_ref, a_ref, lw_ref, cum_ref, strict_ref, hr_ref, br_ref, eye_ref, t_ref, *, lay):
    c = RW_CHUNK
    hr, br, eye_w = hr_ref[...], br_ref[...], eye_ref[...]
    where = [(d, slice(ci * c, (ci + 1) * c), slice(gi * RW_LANES, (gi + 1) * RW_LANES))
             for gi in range(RW_GSTEP) for d in range(2) for ci in range(lay.tb // c)]
    items = [(kk_ref[sl, cs], a_ref[d, sl, cs], lw_ref[d, sl, cs], cum_ref[d], strict_ref[d])
             for d, sl, cs in where]
    for (d, sl, cs), t_w in zip(where, _wkv_tinv_many(items, hr, br, eye_w)):
        t_ref[d, sl, cs] = t_w.astype(t_ref.dtype)


def wkv_inv(lay, kk, a, lw):
    n, dm = kk.shape
    tb, lanes = lay.tb, RW_LANES
    k = _wkv_constants()
    full = lambda arr: pl.BlockSpec(arr.shape, lambda g, j: (0,) * arr.ndim)
    width = RW_GSTEP * lanes
    two = pl.BlockSpec((2, tb, width), lambda g, j: (0, j, g))
    return pl.pallas_call(
        functools.partial(_wkv_inv_kernel, lay=lay),
        grid=(dm // width, lay.nb),
        in_specs=[pl.BlockSpec((tb, width), lambda g, j: (j, g)), two, two,
                  full(k['cum']), full(k['strict']), full(k['head_rows']), full(k['blk_rows']),
                  full(k['eye_w'])],
        out_specs=two,
        out_shape=jax.ShapeDtypeStruct((2, n, dm), BF16),
        compiler_params=_cparams(2), name="wkv_inv",
    )(kk, a, lw, k['cum'], k['strict'], k['head_rows'], k['blk_rows'], k['eye_w'])


def _wkv_kernel(*refs, lay):
    (rf, vf, kkf, kdf, af, lwf, tf, s0f, rb, vb, kkb, kdb, ab, lwb, tb_, s0b,
     cum_ref, strict_ref, incl_ref, hr_ref, bd_ref, yf_ref, yb_ref, sff_ref, sfb_ref, sf_scr, sb_scr) = refs
    c = RW_CHUNK
    n_chunks = lay.tb // c
    j = pl.program_id(1)
    _, pos_f, cnt_f = lay.seq_info(j)
    _, pos_b, cnt_b = lay.seq_info(lay.nb - 1 - j)

    @pl.when(pos_f == 0)
    def _():
        sf_scr[...] = s0f[...]

    @pl.when(pos_b == cnt_b - 1)
    def _():
        sb_scr[...] = s0b[...]

    hr, bd = hr_ref[...], bd_ref[...]
    sl_f = [slice(ci * c, (ci + 1) * c) for ci in range(n_chunks)]
    sl_b = sl_f[::-1]
    cols = [slice(gi * RW_LANES, (gi + 1) * RW_LANES) for gi in range(RW_GSTEP)]
    def raw(ci):
        out = []
        for cs in cols:
            sf, sb = sl_f[ci], sl_b[ci]
            out.append((rf[sf, cs], vf[sf, cs], kkf[sf, cs], kdf[sf, cs], af[sf, cs], lwf[sf, cs],
                        cum_ref[0], strict_ref[0], incl_ref[0]))
            out.append((rb[sb, cs], vb[sb, cs], kkb[sb, cs], kdb[sb, cs], ab[sb, cs], lwb[sb, cs],
                        cum_ref[1], strict_ref[1], incl_ref[1]))
        return out

    states = []
    for gi in range(RW_GSTEP):
        states += [sf_scr[gi], sb_scr[gi]]
    raw_n = raw(0)
    preps = _wkv_prep3(_wkv_prep2(raw_n, _wkv_prep1(raw_n), hr))
    for ci in range(n_chunks):
        more = ci + 1 < n_chunks
        t_ws = []
        for cs in cols:
            t_ws += [tf[sl_f[ci], cs], tb_[sl_b[ci], cs]]
        p0s = _wkv_adv1(states, preps)
        if more:
            raw_n = raw(ci + 1)
            cums_n = _wkv_prep1(raw_n)
        us = _wkv_adv2(p0s, preps, t_ws, hr)
        if more:
            mids_n = _wkv_prep2(raw_n, cums_n, hr)
        states, ys = _wkv_adv3(states, p0s, us, preps, hr, bd)
        if more:
            preps = _wkv_prep3(mids_n)
        for gi, cs in enumerate(cols):
            yf_ref[sl_f[ci], cs] = ys[2 * gi]
            yb_ref[sl_b[ci], cs] = ys[2 * gi + 1]
    for gi in range(RW_GSTEP):
        sf_scr[gi] = states[2 * gi]
        sb_scr[gi] = states[2 * gi + 1]

    @pl.when(pos_f == cnt_f - 1)
    def _():
        sff_ref[...] = sf_scr[...]

    @pl.when(pos_b == 0)
    def _():
        sfb_ref[...] = sb_scr[...]


def wkv(lay, r, v, kk, kd, a, lw, t_inv, s0):
    n, dm = r.shape
    tb, lanes = lay.tb, RW_LANES
    ng = dm // lanes
    width = RW_GSTEP * lanes
    k = _wkv_constants()
    full = lambda arr: pl.BlockSpec(arr.shape, lambda g, j: (0,) * arr.ndim)

    def views(d, blk):
        tok = pl.BlockSpec((tb, width), lambda g, j: (blk(j), g))
        tok2 = pl.BlockSpec((None, tb, width), lambda g, j: (d, blk(j), g))

        def s0_map(g, j):
            seq, _, _ = lay.seq_info(blk(j))
            return (jnp.maximum(seq - lay.n_ctx + 1, 0), d, g, 0, 0)

        def sfin_map(g, j):
            seq, _, _ = lay.seq_info(blk(j))
            return (seq, g, 0, 0)

        ins = [tok, tok, tok, tok2, tok2, tok2, tok2,
               pl.BlockSpec((None, None, RW_GSTEP, lanes, lanes), s0_map)]
        return ins, tok, pl.BlockSpec((None, RW_GSTEP, lanes, lanes), sfin_map)

    in_f, y_f, sf_f = views(0, lambda j: j)
    in_b, y_b, sf_b = views(1, lambda j: lay.nb - 1 - j)
    consts = [k['cum'], k['strict'], k['incl'], k['head_rows'], k['bd']]
    args = [r, v, kk, kd, a, lw, t_inv, s0]
    return pl.pallas_call(
        functools.partial(_wkv_kernel, lay=lay),
        grid=(ng // RW_GSTEP, lay.nb),
        in_specs=in_f + in_b + [full(x) for x in consts],
        out_specs=[y_f, y_b, sf_f, sf_b],
        out_shape=[jax.ShapeDtypeStruct((n, dm), F32), jax.ShapeDtypeStruct((n, dm), F32),
                   jax.ShapeDtypeStruct((lay.n_seq, ng, lanes, lanes), F32),
                   jax.ShapeDtypeStruct((lay.n_seq, ng, lanes, lanes), F32)],
        scratch_shapes=[pltpu.VMEM((RW_GSTEP, lanes, lanes), F32), pltpu.VMEM((RW_GSTEP, lanes, lanes), F32)],
        compiler_params=_cparams(2), name="wkv",
    )(*args, *args, *consts)


def _rwkv_post_kernel(yf_ref, yb_ref, b_ref, g_ref, lnw_ref, lnb_ref, bd_ref, o_ref):
    y = yf_ref[...] + yb_ref[...]
    bd = bd_ref[...]
    inv = 1.0 / RW_HEAD
    mu = _split_dot_r(y, bd, 3) * inv
    yc = y - mu
    var = _split_dot_r(yc * yc, bd, 3) * inv
    yn = yc * lax.rsqrt(var + RW_LN_EPS)
    out = yn * lnw_ref[...] + lnb_ref[...] + b_ref[...]
    o_ref[...] = (out * g_ref[...]).astype(o_ref.dtype)


def rwkv_post(lay, y_f, y_b, bonus, g, lnw, lnb):
    n, dm = y_f.shape
    lanes = RW_LANES
    tm = lay.tile(512)
    bd_b = _wkv_constants()['bd_b']
    tok = pl.BlockSpec((tm, lanes), lambda i, c: (i, c))
    par = pl.BlockSpec((1, lanes), lambda i, c: (0, c))
    return pl.pallas_call(
        _rwkv_post_kernel,
        grid=(n // tm, dm // lanes),
        in_specs=[tok, tok, tok, tok, par, par, pl.BlockSpec(bd_b.shape, lambda i, c: (0, 0))],
        out_specs=tok,
        out_shape=jax.ShapeDtypeStruct((n, dm), BF16),
        compiler_params=_cparams(2), name="rwkv_post",
    )(y_f, y_b, bonus, g, lnw, lnb, bd_b)


def rwkv_layer(lay, x, mods, nw, p, s0):
    n, d = x.shape
    sh1, sc1, g1 = mods[0], mods[1], mods[2]
    xr, xw, xk, xv, xa, xg = rwkv_prep(lay, x, nw, sc1, sh1, p['mu'])
    tm = lay.tile(1024)
    r = matmul(xr, [p['wr']], _epi_plain, [F32], tm=tm, tn=512, name="rwkv_r")[0]
    k = matmul(xk, [p['wk']], _epi_plain, [F32], tm=tm, tn=512, name="rwkv_k")[0]
    v = matmul(xv, [p['wv']], _epi_plain, [F32], tm=tm, tn=512, name="rwkv_v")[0]
    lw, a, g = rwkv_lora(lay, xw, xa, xg, p['w1'], p['w2'], p['w0'], p['a1'], p['a2'], p['a0'],
                         p['g1'], p['g2'])
    kk, kd, bonus = rwkv_mid(lay, r, k, v, a, p['kk'], p['ka'], p['rk'])
    t_inv = wkv_inv(lay, kk, a, lw)
    y_f, y_b, sfin_f, sfin_b = wkv(lay, r, v, kk, kd, a, lw, t_inv, s0)
    z = rwkv_post(lay, y_f, y_b, bonus, g, p['lnx_w'], p['lnx_b'])
    x = matmul_gated_residual(lay, z, p['wo'], x, g1, tm=tm, tn=512, name="rwkv_o")
    return x, jnp.stack([sfin_f, sfin_b], axis=1)


def _hgrn_constants():
    c = HG_CHUNK_TOKENS
    t = np.arange(c)[:, None]
    j = np.arange(c)[None, :]
    cums, masks_all = [], []
    for rev in (False, True):
        masks = []
        h = 1
        while h < c:
            upper = (t % (2 * h)) >= h
            same = (t // (2 * h)) == (j // (2 * h))
            if not rev:
                mask = same & upper & ((j % (2 * h)) < h)
            else:
                mask = same & (~upper) & ((j % (2 * h)) >= h)
            masks.append(mask)
            h *= 2
        masks.append(t == j)
        cums.append((j >= t) if rev else (j <= t))
        masks_all.append(np.stack(masks, 0))
    return jnp.asarray(np.stack(cums), BF16), jnp.asarray(np.stack(masks_all).astype(np.float32))


def _hgrn_level_exponents(g, gcum, rev):
    c, kdim = g.shape
    row = lax.broadcasted_iota(jnp.int32, g.shape, 0)
    nxt = pltpu.roll(g, c - 1, axis=0)
    prv = pltpu.roll(g, 1, axis=0)
    r2, r4 = row & 1, row & 3
    if not rev:
        x1 = jnp.where(r2 == 1, g, 0.0)
        x2 = jnp.where(r4 == 0, nxt, jnp.where(r4 == 2, g, jnp.where(r4 == 3, prv + g, 0.0)))
    else:
        x1 = jnp.where(r2 == 0, g, 0.0)
        x2 = jnp.where(r4 == 0, g + nxt, jnp.where(r4 == 1, g, jnp.where(r4 == 3, prv, 0.0)))
    xs = [x1, x2]
    h = 4
    while h < c:
        gr = gcum.reshape(c // (2 * h), 2 * h, kdim)
        ref = gr[:, h:h + 1, :] if rev else gr[:, h - 1:h, :]
        upper = lax.broadcasted_iota(jnp.int32, gr.shape, 1) >= h
        diff = gr - ref
        x = jnp.where(upper, -diff, diff) if rev else jnp.where(upper, diff, -diff)
        xs.append(x.reshape(c, kdim))
        h *= 2
    return xs


def _hgrn_units(units, masks_by_dir, cum_by_dir):
    c = HG_CHUNK_TOKENS
    gs = [jnp.log(f) for _, f, _, _ in units]
    gcums = [_split_dot(cum_by_dir[rev], g, 3) for g, (_, _, _, rev) in zip(gs, units)]
    outs = []
    pend = []
    for (q, f, iv, rev), g, gcum in zip(units, gs, gcums):
        k = 1.0 - f
        tot = gcum[0:1] if rev else gcum[c - 1:c]
        es = [jnp.exp(x) for x in _hgrn_level_exponents(g, gcum, rev)]
        pend.append((q, k, iv, rev, es, jnp.exp(gcum), jnp.exp(tot - gcum), jnp.exp(tot)))
    for q, k, iv, rev, es, eg, erest, etot in pend:
        masks = masks_by_dir[rev]
        a = jnp.where(masks[len(es)] > 0, _bdot_nt(q, k), 0.0)
        for lv, el in enumerate(es):
            a = a + jnp.where(masks[lv] > 0, _bdot_nt(q * el, k * el), 0.0)
        outs.append(dict(a=a, iv=iv, qe=(q * eg).astype(BF16), kdec=(k * erest).astype(BF16), decay=etot))
    for u in outs:
        u['av'] = _bdot(u['a'], u['iv'])
    for u in outs:
        u['upd'] = _bdot(u['iv'].T, u['kdec'])
    return outs


def _hgrn_kernel(*refs, lay):
    (qf, ff, if_, s0f, qb, fb, ib, s0b, cum_ref, mask_ref, of_ref, ob_ref, sff_ref, sfb_ref,
     sf_scr, sb_scr) = refs
    c = HG_CHUNK_TOKENS
    n_chunks = lay.tb // c
    j = pl.program_id(1)
    _, pos_f, cnt_f = lay.seq_info(j)
    _, pos_b, cnt_b = lay.seq_info(lay.nb - 1 - j)

    @pl.when(pos_f == 0)
    def _():
        sf_scr[...] = s0f[...]

    @pl.when(pos_b == cnt_b - 1)
    def _():
        sb_scr[...] = s0b[...]

    masks_by_dir = [mask_ref[0], mask_ref[1]]
    cum_by_dir = [cum_ref[0], cum_ref[1]]
    sl_f = [slice(ci * c, (ci + 1) * c) for ci in range(n_chunks)]
    sl_b = sl_f[::-1]
    cols = [slice(hi * HG_K, (hi + 1) * HG_K) for hi in range(HG_HSTEP)]
    chains = []
    for hi, cs in enumerate(cols):
        chains.append((sf_scr, hi, of_ref, [(sl, cs) for sl in sl_f],
                       [(qf[sl, cs], ff[sl, cs], if_[sl, cs], False) for sl in sl_f]))
        chains.append((sb_scr, hi, ob_ref, [(sl, cs) for sl in sl_b],
                       [(qb[sl, cs], fb[sl, cs], ib[sl, cs], True) for sl in sl_b]))
    done = _hgrn_units([u for ch in chains for u in ch[4]], masks_by_dir, cum_by_dir)
    pend = []
    for n_ch, (scr, hi, o_ref, where, _) in enumerate(chains):
        s = scr[hi]
        for ci in range(n_chunks):
            u = done[n_ch * n_chunks + ci]
            pend.append((o_ref, where[ci], u, s))
            s = s * u['decay'] + u['upd']
        scr[hi] = s
    for o_ref, (sl, cs), u, s_prev in pend:
        o_ref[sl, cs] = u['av'] + _bdot_nt(u['qe'], s_prev)

    @pl.when(pos_f == cnt_f - 1)
    def _():
        sff_ref[...] = sf_scr[...]

    @pl.when(pos_b == 0)
    def _():
        sfb_ref[...] = sb_scr[...]


def hgrn_scan(lay, q, f_fwd, f_bwd, iv, s0t):
    n, dm = q.shape
    tb = lay.tb
    nh = dm // HG_K
    width = HG_HSTEP * HG_K
    cum_m, masks = _hgrn_constants()
    full = lambda arr: pl.BlockSpec(arr.shape, lambda h, j: (0,) * arr.ndim)

    def views(d, blk):
        tok = pl.BlockSpec((tb, width), lambda h, j: (blk(j), h))

        def s0_map(h, j):
            seq, _, _ = lay.seq_info(blk(j))
            return (jnp.maximum(seq - lay.n_ctx + 1, 0), d, h, 0, 0)

        def sfin_map(h, j):
            seq, _, _ = lay.seq_info(blk(j))
            return (seq, h, 0, 0)

        ins = [tok, tok, tok, pl.BlockSpec((None, None, HG_HSTEP, HG_K, HG_K), s0_map)]
        return ins, tok, pl.BlockSpec((None, HG_HSTEP, HG_K, HG_K), sfin_map)

    in_f, o_f, sf_f = views(0, lambda j: j)
    in_b, o_b, sf_b = views(1, lambda j: lay.nb - 1 - j)
    return pl.pallas_call(
        functools.partial(_hgrn_kernel, lay=lay),
        grid=(nh // HG_HSTEP, lay.nb),
        in_specs=in_f + in_b + [full(cum_m), full(masks)],
        out_specs=[o_f, o_b, sf_f, sf_b],
        out_shape=[jax.ShapeDtypeStruct((n, dm), F32), jax.ShapeDtypeStruct((n, dm), F32),
                   jax.ShapeDtypeStruct((lay.n_seq, nh, HG_K, HG_K), F32),
                   jax.ShapeDtypeStruct((lay.n_seq, nh, HG_K, HG_K), F32)],
        scratch_shapes=[pltpu.VMEM((HG_HSTEP, HG_K, HG_K), F32), pltpu.VMEM((HG_HSTEP, HG_K, HG_K), F32)],
        compiler_params=_cparams(2), name="hgrn_scan",
    )(q, f_fwd, iv, s0t, q, f_bwd, iv, s0t, cum_m, masks)


def _hgrn_post_kernel(of_ref, ob_ref, g_ref, nw_ref, z_ref):
    o = of_ref[...] + ob_ref[...]
    o = o * lax.rsqrt(jnp.mean(o * o, axis=-1, keepdims=True) + NORM_EPS) * nw_ref[...] * g_ref[...]
    z_ref[...] = o.astype(z_ref.dtype)


def hgrn_post(lay, o_f, o_b, gs, nw):
    n, dm = o_f.shape
    tm = lay.tile(1024)
    tok = pl.BlockSpec((tm, HG_K), lambda i, h: (i, h))
    return pl.pallas_call(
        _hgrn_post_kernel,
        grid=(n // tm, dm // HG_K),
        in_specs=[tok, tok, tok, pl.BlockSpec((1, HG_K), lambda i, h: (0, 0))],
        out_specs=tok,
        out_shape=jax.ShapeDtypeStruct((n, dm), BF16),
        compiler_params=_cparams(2), name="hgrn_post",
    )(o_f, o_b, gs, nw)


def _epi_hgrn_in(accs, extras):
    lb = extras[0]
    q = _silu(accs[0])
    f0 = lb + (1.0 - lb) * jax.nn.sigmoid(accs[1])
    f1 = lb + (1.0 - lb) * jax.nn.sigmoid(accs[2])
    return [q, f0, f1, accs[3], _silu(accs[4])]


def hgrn_layer(lay, x, mods, nw, p, lb, s0t):
    n, d = x.shape
    sh1, sc1, g1 = mods[0], mods[1], mods[2]
    h = norm_mod(lay, x, nw, sc1, sh1, BF16)
    tm, tn = lay.tile(512), 256
    extras = ((lb, pl.BlockSpec((1, tn), lambda i, j: (0, j))),)
    q, f0, f1, iv, gs = matmul(h, p['w_in'], _epi_hgrn_in, [F32] * 5, tm=tm, tn=tn, extras=extras,
                               name="hgrn_in")
    o_f, o_b, sfin_f, sfin_b = hgrn_scan(lay, q, f0, f1, iv, s0t)
    z = hgrn_post(lay, o_f, o_b, gs, p['norm_w'])
    x = matmul_gated_residual(lay, z, p['wo'], x, g1, tm=lay.tile(1024), tn=512, name="hgrn_o")
    return x, jnp.stack([sfin_f, sfin_b], axis=1)


ML_DOWN_COLS = 1280
ML_KR_OFF = ML_Q_LORA + ML_KV_LORA
ML_KRS_OFF = ML_KR_OFF + LANES_V7X


def _rms(x, w):
    return x * lax.rsqrt(jnp.mean(x * x, axis=-1, keepdims=True) + NORM_EPS) * w


def _mla_mid_kernel(dn_ref, qw_ref, kvw_ref, cos_ref, sin_ref, qn_ref, ckv_ref, kr_ref):
    dn = dn_ref[...]
    qn_ref[...] = _rms(dn[:, :ML_Q_LORA], qw_ref[...]).astype(qn_ref.dtype)
    ckv_ref[...] = _rms(dn[:, ML_Q_LORA:ML_KR_OFF], kvw_ref[...])
    kr = dn[:, ML_KR_OFF:ML_KR_OFF + ML_ROPE]
    krs = dn[:, ML_KRS_OFF:ML_KRS_OFF + ML_ROPE]
    kr_ref[...] = kr * cos_ref[...] + krs * sin_ref[...]


def mla_mid(lay, dn, qw, kvw, cos, sin):
    n = dn.shape[0]
    tm = lay.tile(512)
    return pl.pallas_call(
        _mla_mid_kernel,
        grid=(n // tm,),
        in_specs=[pl.BlockSpec((tm, ML_DOWN_COLS), lambda i: (i, 0)),
                  pl.BlockSpec((1, ML_Q_LORA), lambda i: (0, 0)),
                  pl.BlockSpec((1, ML_KV_LORA), lambda i: (0, 0)),
                  pl.BlockSpec((tm, ML_ROPE), lambda i: (i, 0)),
                  pl.BlockSpec((tm, ML_ROPE), lambda i: (i, 0))],
        out_specs=[pl.BlockSpec((tm, ML_Q_LORA), lambda i: (i, 0)),
                   pl.BlockSpec((tm, ML_KV_LORA), lambda i: (i, 0)),
                   pl.BlockSpec((tm, ML_ROPE), lambda i: (i, 0))],
        out_shape=[jax.ShapeDtypeStruct((n, ML_Q_LORA), BF16),
                   jax.ShapeDtypeStruct((n, ML_KV_LORA), F32),
                   jax.ShapeDtypeStruct((n, ML_ROPE), F32)],
        compiler_params=_cparams(1), name="mla_mid",
    )(dn, qw, kvw, cos, sin)


ML_QSCALE = math.log2(math.e) / math.sqrt(ML_NOPE + ML_ROPE)


def _epi_qscale(accs, extras):
    return [accs[0] * ML_QSCALE]


def _epi_rope(accs, extras):
    cos, sin = extras
    return [(accs[0] * cos + accs[1] * sin) * ML_QSCALE]


def _attn_kernel(qn_ref, qr_ref, kn_ref, kr_ref, v_ref, o_ref, kc_scr):
    @pl.when(pl.program_id(2) == 0)
    def _():
        for h in range(2):
            kc_scr[h, :, :ML_NOPE] = kn_ref[:, h * ML_NOPE:(h + 1) * ML_NOPE]
            kc_scr[h, :, ML_NOPE:] = kr_ref[...]

    outs = []
    for h in range(2):
        q = jnp.concatenate([qn_ref[:, h * ML_NOPE:(h + 1) * ML_NOPE],
                             qr_ref[:, h * ML_ROPE:(h + 1) * ML_ROPE]], axis=1)
        s = lax.dot_general(q, kc_scr[h], (((1,), (1,)), ((), ())), preferred_element_type=F32)
        m = jnp.max(s, axis=-1, keepdims=True)
        p = jnp.exp2(s - m)
        l = jnp.sum(p, axis=-1, keepdims=True)
        o = jnp.dot(p.astype(BF16), v_ref[:, h * ML_V:(h + 1) * ML_V], preferred_element_type=F32)
        outs.append(o / l)
    o_ref[...] = jnp.concatenate(outs, axis=1).astype(o_ref.dtype)


def attention(qn, qr, kn, kr, v, *, n_seq, q_len, k_len, row0, tq):
    heads2 = qn.shape[1] // (2 * ML_NOPE)
    qb = q_len // tq
    rb0 = row0 // tq
    return pl.pallas_call(
        _attn_kernel,
        grid=(n_seq, heads2, qb),
        in_specs=[pl.BlockSpec((tq, 2 * ML_NOPE), lambda s, h, i: (rb0 + s * qb + i, h)),
                  pl.BlockSpec((tq, 2 * ML_ROPE), lambda s, h, i: (rb0 + s * qb + i, h)),
                  pl.BlockSpec((k_len, 2 * ML_NOPE), lambda s, h, i: (s, h)),
                  pl.BlockSpec((k_len, ML_ROPE), lambda s, h, i: (s, 0)),
                  pl.BlockSpec((k_len, 2 * ML_V), lambda s, h, i: (s, h))],
        out_specs=pl.BlockSpec((tq, 2 * ML_V), lambda s, h, i: (s * qb + i, h)),
        out_shape=jax.ShapeDtypeStruct((n_seq * q_len, heads2 * 2 * ML_V), BF16),
        scratch_shapes=[pltpu.VMEM((2, k_len, ML_NOPE + ML_ROPE), BF16)],
        compiler_params=_cparams(3), name="mla_attn",
    )(qn, qr, kn, kr, v)


def mla_layer(lay, x, mods, nw, p, cache_ckv, cache_kr, cos, sin, cos2, sin2):
    n, d = x.shape
    sh1, sc1, g1 = mods[0], mods[1], mods[2]
    h = norm_mod(lay, x, nw, sc1, sh1, BF16)
    tm = lay.tile(512)
    dn = matmul(h, [p['w_down']], _epi_plain, [F32], tm=tm, tn=ML_DOWN_COLS, name="mla_down")[0]
    qlat, ckv, kr = mla_mid(lay, dn, p['qnorm_w'], p['kvnorm_w'], cos, sin)
    qn = matmul(qlat, [p['w_uq_nope']], _epi_qscale, [BF16], tm=tm, tn=512, name="mla_qn")[0]
    tn = 2 * ML_ROPE
    extras = ((cos2, pl.BlockSpec((tm, tn), lambda i, j: (i, 0))),
              (sin2, pl.BlockSpec((tm, tn), lambda i, j: (i, 0))))
    qr = matmul(qlat, [p['w_uq_rope'], p['w_uq_rope_sw']], _epi_rope, [BF16], tm=tm, tn=tn,
                extras=extras, name="mla_qr")[0]
    nc, past = lay.nc, cache_ckv.shape[1]
    ckv_b, kr_b = ckv.astype(BF16), kr.astype(BF16)
    kn_c, v_c = matmul(ckv_b[:nc], [p['w_ukn'], p['w_uv']], _epi_plain, [BF16, BF16],
                       tm=lay.tile(512), tn=512, name="mla_kv_ctx")
    o_c = attention(qn, qr, kn_c, kr_b[:nc], v_c, n_seq=lay.n_ctx, q_len=lay.ctx_len,
                    k_len=lay.ctx_len, row0=0, tq=min(256, lay.ctx_len))
    k_len = lay.lat_len + past
    ckv_l = jnp.concatenate([ckv_b[nc:].reshape(lay.n_lat, lay.lat_len, -1), cache_ckv.astype(BF16)],
                            axis=1).reshape(lay.n_lat * k_len, -1)
    kr_l = jnp.concatenate([kr_b[nc:].reshape(lay.n_lat, lay.lat_len, -1), cache_kr.astype(BF16)],
                           axis=1).reshape(lay.n_lat * k_len, -1)
    tk = math.gcd(k_len, 512)
    kn_l, v_l = matmul(ckv_l, [p['w_ukn'], p['w_uv']], _epi_plain, [BF16, BF16], tm=tk, tn=512,
                       name="mla_kv_lat")
    o_l = attention(qn, qr, kn_l, kr_l, v_l, n_seq=lay.n_lat, q_len=lay.lat_len, k_len=k_len,
                    row0=nc, tq=min(256, lay.lat_len))
    o = jnp.concatenate([o_c, o_l], axis=0)
    x = matmul_gated_residual(lay, o, p['wo'], x, g1, tm=lay.tile(1024), tn=512, name="mla_o")
    return x, ckv[:nc], kr[:nc]


def _rope_tables(lay):
    t = lay.lat_len
    rows = t // GRID_W
    rr = jnp.broadcast_to(jnp.arange(rows, dtype=F32)[:, None], (rows, GRID_W)).reshape(-1)
    cc = jnp.broadcast_to(jnp.arange(GRID_W, dtype=F32)[None, :], (rows, GRID_W)).reshape(-1)
    nf = ML_ROPE // 4
    inv = ROPE_BASE ** (-jnp.arange(nf, dtype=F32) / nf)
    ar, ac = rr[:, None] * inv, cc[:, None] * inv
    cos = jnp.concatenate([jnp.cos(ar), jnp.cos(ar), jnp.cos(ac), jnp.cos(ac)], axis=-1)
    sin = jnp.concatenate([-jnp.sin(ar), jnp.sin(ar), -jnp.sin(ac), jnp.sin(ac)], axis=-1)
    cos = jnp.concatenate([jnp.ones((lay.nc, ML_ROPE), F32), jnp.tile(cos, (lay.n_lat, 1))], axis=0)
    sin = jnp.concatenate([jnp.zeros((lay.nc, ML_ROPE), F32), jnp.tile(sin, (lay.n_lat, 1))], axis=0)
    return cos, sin


def _swap_cols(w):
    k, c = w.shape
    w4 = w.reshape(k, c // 32, 2, 16)
    return w4[:, :, ::-1, :].reshape(k, c)


def ffn(lay, x, mods, nw, w_a, w_b, w_out):
    sh2, sc2, g2 = mods[3], mods[4], mods[5]
    h = norm_mod(lay, x, nw, sc2, sh2, BF16)
    act = matmul(h, [w_a, w_b], _epi_swiglu, [BF16], tm=lay.tile(1024), tn=512, name="ffn_in")[0]
    return matmul_gated_residual(lay, act, w_out, x, g2, tm=lay.tile(512), tn=512, name="ffn_out")


def _block_diag_states(s):
    b, two, h, n, _ = s.shape
    s = s.reshape(b, two, h // RW_GROUP, RW_GROUP, n, n)
    eye = jnp.eye(RW_GROUP, dtype=s.dtype)
    out = jnp.einsum('bdghvk,hi->bdghvik', s, eye)
    return out.reshape(b, two, h // RW_GROUP, RW_GROUP * n, RW_GROUP * n)


def _diag_blocks(s):
    b, two, g, l, _ = s.shape
    n = l // RW_GROUP
    s = s.reshape(b, two, g, RW_GROUP, n, RW_GROUP, n)
    s = jnp.moveaxis(jnp.diagonal(s, axis1=3, axis2=5), -1, 3)
    return s.reshape(b, two, g * RW_GROUP, n, n)


def kernel(x_prompt, x_sample, state_rwkv, state_hgrn, cache_ckv, cache_krope, c, c_ctx, ada_w, ada_b, norm1_w, norm2_w, ffn_w_in, ffn_w_out, final_norm_w, rw_mu, rw_wr, rw_wk, rw_wv, rw_wo, rw_w0, rw_w1, rw_w2, rw_a0, rw_a1, rw_a2, rw_g1, rw_g2, rw_kk, rw_ka, rw_rk, rw_lnx_w, rw_lnx_b, hg_w_in, hg_lb, hg_norm_w, hg_wo, ml_w_down, ml_qnorm_w, ml_kvnorm_w, ml_w_uq, ml_w_ukv, ml_wo):
    n_ctx, ctx_len, d = x_prompt.shape
    n_lat, lat_len, _ = x_sample.shape
    depth = ada_w.shape[0]
    lay = Layout(n_ctx, ctx_len, n_lat, lat_len)
    d_ff = ffn_w_out.shape[1]
    x = jnp.concatenate([x_prompt.reshape(lay.nc, d), x_sample.reshape(n_lat * lat_len, d)], axis=0)

    n_cond = -(-(1 + n_lat) // SUBLANES_V7X) * SUBLANES_V7X
    cond = jnp.zeros((n_cond, d), F32).at[0].set(c_ctx).at[1:1 + n_lat].set(c)
    mod_all = adaln(cond, ada_w, ada_b)
    mod_all = mod_all.reshape(depth, n_cond, 6, 1, d).transpose(0, 2, 1, 3, 4)

    lb_table = jnp.cumsum(jax.nn.softmax(hg_lb.astype(F32), axis=0), axis=0)
    lb_table = lb_table - lb_table[0]
    cos, sin = _rope_tables(lay)
    cos2, sin2 = jnp.tile(cos, (1, 2)), jnp.tile(sin, (1, 2))
    bf = lambda t: t.astype(BF16)

    new_rwkv, new_hgrn, new_ckv, new_krope = [], [], [], []
    for l in range(depth):
        kind, j = l % 3, l // 3
        mods = mod_all[l]
        if kind == 0:
            pad1 = lambda w: jnp.pad(w, ((0, 0), (0, 0), (0, LORA_PAD - w.shape[2])))
            pad2 = lambda w: jnp.pad(w, ((0, 0), (0, LORA_PAD - w.shape[1]), (0, 0)))
            p = {'mu': rw_mu[j], 'wr': bf(rw_wr[j]), 'wk': bf(rw_wk[j]), 'wv': bf(rw_wv[j]),
                 'wo': bf(rw_wo[j]),
                 'w0': rw_w0[j].reshape(2, 1, d), 'w1': bf(pad1(rw_w1[j])), 'w2': bf(pad2(rw_w2[j])),
                 'a0': rw_a0[j].reshape(2, 1, d), 'a1': bf(pad1(rw_a1[j])), 'a2': bf(pad2(rw_a2[j])),
                 'g1': bf(rw_g1[j]), 'g2': bf(rw_g2[j]),
                 'kk': rw_kk[j].reshape(1, d), 'ka': rw_ka[j].reshape(1, d),
                 'rk': rw_rk[j].reshape(2, 1, d),
                 'lnx_w': rw_lnx_w[j].reshape(1, d), 'lnx_b': rw_lnx_b[j].reshape(1, d)}
            s_lat = _block_diag_states(state_rwkv[:, j].astype(F32))
            s0 = jnp.concatenate([jnp.zeros((1,) + s_lat.shape[1:], F32), s_lat], axis=0)
            x, sfin = rwkv_layer(lay, x, mods, norm1_w[l], p, s0)
            new_rwkv.append(_diag_blocks(sfin[:n_ctx]))
        elif kind == 1:
            hk = d
            w_in = hg_w_in[j]
            p = {'w_in': [bf(w_in[:, i * hk:(i + 1) * hk]) for i in range(5)],
                 'norm_w': hg_norm_w[j].reshape(1, HG_K), 'wo': bf(hg_wo[j])}
            s_lat = jnp.swapaxes(state_hgrn[:, j].astype(F32), -1, -2)
            s0t = jnp.concatenate([jnp.zeros((1,) + s_lat.shape[1:], F32), s_lat], axis=0)
            x, sfin = hgrn_layer(lay, x, mods, norm1_w[l], p, lb_table[l].reshape(1, d), s0t)
            new_hgrn.append(jnp.swapaxes(sfin[:n_ctx], -1, -2))
        else:
            wd = ml_w_down[j]
            kr_w = wd[:, ML_KR_OFF:]
            zpad = jnp.zeros((d, LANES_V7X - ML_ROPE), wd.dtype)
            w_down = jnp.concatenate([wd, zpad, _swap_cols(kr_w), zpad], axis=1)
            wq = ml_w_uq[j].reshape(ML_Q_LORA, ML_H, ML_NOPE + ML_ROPE)
            wq_n = wq[:, :, :ML_NOPE].reshape(ML_Q_LORA, ML_H * ML_NOPE)
            wq_r = wq[:, :, ML_NOPE:].reshape(ML_Q_LORA, ML_H * ML_ROPE)
            wkv = ml_w_ukv[j].reshape(ML_KV_LORA, ML_H, ML_NOPE + ML_V)
            p = {'w_down': bf(w_down), 'qnorm_w': ml_qnorm_w[j].reshape(1, -1),
                 'kvnorm_w': ml_kvnorm_w[j].reshape(1, -1),
                 'w_uq_nope': bf(wq_n), 'w_uq_rope': bf(wq_r), 'w_uq_rope_sw': bf(_swap_cols(wq_r)),
                 'w_ukn': bf(wkv[:, :, :ML_NOPE].reshape(ML_KV_LORA, ML_H * ML_NOPE)),
                 'w_uv': bf(wkv[:, :, ML_NOPE:].reshape(ML_KV_LORA, ML_H * ML_V)),
                 'wo': bf(ml_wo[j])}
            x, ckv_c, kr_c = mla_layer(lay, x, mods, norm1_w[l], p, cache_ckv[:, j], cache_krope[:, j],
                                       cos, sin, cos2, sin2)
            new_ckv.append(ckv_c.reshape(n_ctx, ctx_len, ML_KV_LORA))
            new_krope.append(kr_c.reshape(n_ctx, ctx_len, ML_ROPE))
        w_in = ffn_w_in[l]
        x = ffn(lay, x, mods, norm2_w[l], bf(w_in[:, :d_ff]), bf(w_in[:, d_ff:]), bf(ffn_w_out[l]))

    zero = jnp.zeros((n_cond, 1, d), F32)
    y = norm_mod(lay, x, final_norm_w, zero, zero, F32)
    y_prompt = y[:lay.nc].reshape(n_ctx, ctx_len, d)
    y_sample = y[lay.nc:].reshape(n_lat, lat_len, d)
    return (y_prompt, y_sample, jnp.stack(new_rwkv, axis=1), jnp.stack(new_hgrn, axis=1),
            jnp.stack(new_ckv, axis=1), jnp.stack(new_krope, axis=1))
```

```python
import functools
import math

import numpy as np
import jax
import jax.numpy as jnp
from jax import lax
from jax.experimental import pallas as pl
from jax.experimental.pallas import tpu as pltpu

F32 = jnp.float32
BF16 = jnp.bfloat16

LANES_V7X = 128
SUBLANES_V7X = 8
VMEM_BYTES_V7X = 64 * 1024 * 1024
VMEM_LIMIT = 56 * 1024 * 1024

NORM_EPS = 1e-6
RW_HEAD = 64
RW_LN_EPS = 64e-5
RW_GROUP = 4
RW_LANES = RW_GROUP * RW_HEAD
RW_CHUNK = 64
RW_GSTEP = 2
HG_K = 128
HG_CHUNK_TOKENS = 64
HG_HSTEP = 2
ML_H = 16
ML_NOPE = 128
ML_ROPE = 64
ML_V = 128
ML_Q_LORA = 512
ML_KV_LORA = 512
GRID_W = 64
ROPE_BASE = 10000.0
LORA_PAD = 128


class Layout:
    def __init__(self, n_ctx, ctx_len, n_lat, lat_len):
        self.n_ctx, self.ctx_len, self.n_lat, self.lat_len = n_ctx, ctx_len, n_lat, lat_len
        self.nc = n_ctx * ctx_len
        self.n = self.nc + n_lat * lat_len
        self.tb = min(256, ctx_len)
        assert ctx_len % self.tb == 0 and lat_len % self.tb == 0 and self.tb % RW_CHUNK == 0
        self.nb = self.n // self.tb
        self.nb_ctx = self.nc // self.tb
        self.bps_ctx = ctx_len // self.tb
        self.bps_lat = lat_len // self.tb
        self.n_seq = n_ctx + n_lat

    def tile(self, want):
        t = want
        while self.nc % t or self.lat_len % t:
            t //= 2
        return t

    def cond_of_tile(self, i, tm):
        row = i * tm
        return jnp.where(row < self.nc, 0, 1 + (row - self.nc) // self.lat_len)

    def seq_info(self, blk):
        is_ctx = blk < self.nb_ctx
        lat = blk - self.nb_ctx
        seq = jnp.where(is_ctx, blk // self.bps_ctx, self.n_ctx + lat // self.bps_lat)
        pos = jnp.where(is_ctx, blk % self.bps_ctx, lat % self.bps_lat)
        cnt = jnp.where(is_ctx, self.bps_ctx, self.bps_lat)
        return seq, pos, cnt


def _cparams(n_axes):
    return pltpu.CompilerParams(dimension_semantics=("arbitrary",) * n_axes, vmem_limit_bytes=VMEM_LIMIT)


def _bdot(a, b):
    return jnp.dot(a.astype(BF16), b.astype(BF16), preferred_element_type=F32)


def _bdot_nt(a, b):
    return lax.dot_general(a.astype(BF16), b.astype(BF16), (((1,), (1,)), ((), ())),
                           preferred_element_type=F32)


def _silu(x):
    return x * jax.nn.sigmoid(x)


def _adaln_kernel(c_ref, w_ref, b_ref, o_ref):
    a = _silu(c_ref[...]).astype(BF16)
    o_ref[...] = jnp.dot(a, w_ref[...].astype(BF16), preferred_element_type=F32) + b_ref[...]


def adaln(cond, ada_w, ada_b):
    depth, d, d6 = ada_w.shape
    r = cond.shape[0]
    tn = 1024
    return pl.pallas_call(
        _adaln_kernel,
        grid=(depth, d6 // tn),
        in_specs=[pl.BlockSpec((r, d), lambda l, j: (0, 0)),
                  pl.BlockSpec((None, d, tn), lambda l, j: (l, 0, j)),
                  pl.BlockSpec((None, 1, tn), lambda l, j: (l, 0, j))],
        out_specs=pl.BlockSpec((None, r, tn), lambda l, j: (l, 0, j)),
        out_shape=jax.ShapeDtypeStruct((depth, r, d6), F32),
        compiler_params=_cparams(2), name="adaln",
    )(cond, ada_w, ada_b.reshape(depth, 1, d6))


def _norm_mod(x, nw, sc, sh):
    y = x * lax.rsqrt(jnp.mean(x * x, axis=-1, keepdims=True) + NORM_EPS)
    return (y * nw) * (1.0 + sc) + sh


def _norm_mod_kernel(x_ref, nw_ref, sc_ref, sh_ref, o_ref):
    o_ref[...] = _norm_mod(x_ref[...], nw_ref[...], sc_ref[...], sh_ref[...]).astype(o_ref.dtype)


def norm_mod(lay, x, nw, sc, sh, out_dtype):
    n, d = x.shape
    tm = lay.tile(512)
    cmap = lambda i: (lay.cond_of_tile(i, tm), 0, 0)
    return pl.pallas_call(
        _norm_mod_kernel,
        grid=(n // tm,),
        in_specs=[pl.BlockSpec((tm, d), lambda i: (i, 0)),
                  pl.BlockSpec((1, d), lambda i: (0, 0)),
                  pl.BlockSpec((None, 1, d), cmap),
                  pl.BlockSpec((None, 1, d), cmap)],
        out_specs=pl.BlockSpec((tm, d), lambda i: (i, 0)),
        out_shape=jax.ShapeDtypeStruct((n, d), out_dtype),
        compiler_params=_cparams(1), name="norm_mod",
    )(x, nw.reshape(1, d), sc, sh)


def _rmsnorm_kernel(x_ref, w_ref, o_ref):
    x = x_ref[...]
    o_ref[...] = x * lax.rsqrt(jnp.mean(x * x, axis=-1, keepdims=True) + NORM_EPS) * w_ref[...]


def rmsnorm_rows(lay, x, w, row0, rows):
    d = x.shape[1]
    tm = lay.tile(512)
    b0 = row0 // tm
    return pl.pallas_call(
        _rmsnorm_kernel,
        grid=(rows // tm,),
        in_specs=[pl.BlockSpec((tm, d), lambda i: (b0 + i, 0)), pl.BlockSpec((1, d), lambda i: (0, 0))],
        out_specs=pl.BlockSpec((tm, d), lambda i: (i, 0)),
        out_shape=jax.ShapeDtypeStruct((rows, d), x.dtype),
        compiler_params=_cparams(1), name="final_norm",
    )(x, w.reshape(1, d))


def _mm_kernel(*refs, n_w, n_e, epi):
    a = refs[0][...]
    accs = [jnp.dot(a, refs[1 + i][...], preferred_element_type=F32) for i in range(n_w)]
    extras = [refs[1 + n_w + i][...] for i in range(n_e)]
    outs = epi(accs, extras)
    o_refs = refs[1 + n_w + n_e:]
    for o_ref, val in zip(o_refs, outs):
        o_ref[...] = val.astype(o_ref.dtype)


def matmul(a, ws, epi, out_dtypes, *, tm, tn, extras=(), name):
    m, k = a.shape
    nw = ws[0].shape[1]
    assert m % tm == 0 and nw % tn == 0
    in_specs = [pl.BlockSpec((tm, k), lambda i, j: (i, 0))]
    in_specs += [pl.BlockSpec((k, tn), lambda i, j: (0, j)) for _ in ws]
    in_specs += [spec for _, spec in extras]
    outs = pl.pallas_call(
        functools.partial(_mm_kernel, n_w=len(ws), n_e=len(extras), epi=epi),
        grid=(m // tm, nw // tn),
        in_specs=in_specs,
        out_specs=[pl.BlockSpec((tm, tn), lambda i, j: (i, j)) for _ in out_dtypes],
        out_shape=[jax.ShapeDtypeStruct((m, nw), dt) for dt in out_dtypes],
        compiler_params=_cparams(2), name=name,
    )(a, *ws, *[arr for arr, _ in extras])
    return outs


def _epi_plain(accs, extras):
    return accs


def _epi_gated_residual(accs, extras):
    x, g = extras
    return [x + g * accs[0]]


def matmul_gated_residual(lay, a, w, x, gate, *, tm, tn, name):
    extras = ((x, pl.BlockSpec((tm, tn), lambda i, j: (i, j))),
              (gate, pl.BlockSpec((None, 1, tn), lambda i, j: (lay.cond_of_tile(i, tm), 0, j))))
    return matmul(a, [w], _epi_gated_residual, [F32], tm=tm, tn=tn, extras=extras, name=name)[0]


def _epi_swiglu(accs, extras):
    return [_silu(accs[0]) * accs[1]]


def _rwkv_prep_kernel(x_ref, xp_ref, xn_ref, nw_ref, sc_ref, sh_ref, mu_ref, *o_refs, lay):
    i = pl.program_id(0)
    _, pos, cnt = lay.seq_info(i)
    nw, sc, sh = nw_ref[...], sc_ref[...], sh_ref[...]
    h = _norm_mod(x_ref[...], nw, sc, sh)
    hp = _norm_mod(xp_ref[...], nw, sc, sh)[SUBLANES_V7X - 1:SUBLANES_V7X]
    hn = _norm_mod(xn_ref[...], nw, sc, sh)[0:1]
    hp = jnp.where(pos == 0, 0.0, hp)
    hn = jnp.where(pos == cnt - 1, 0.0, hn)
    tb = h.shape[0]
    row = lax.broadcasted_iota(jnp.int32, h.shape, 0)
    prev = jnp.where(row == 0, hp, pltpu.roll(h, 1, axis=0))
    nxt = jnp.where(row == tb - 1, hn, pltpu.roll(h, tb - 1, axis=0))
    xx = 0.5 * (prev + nxt) - h
    for idx, o_ref in enumerate(o_refs):
        o_ref[...] = (h + xx * mu_ref[idx:idx + 1, :]).astype(o_ref.dtype)


def rwkv_prep(lay, x, nw, sc, sh, mu):
    n, d = x.shape
    tb = lay.tb
    r8 = tb // SUBLANES_V7X
    last8 = n // SUBLANES_V7X - 1
    cmap = lambda i: (lay.cond_of_tile(i, tb), 0, 0)
    return pl.pallas_call(
        functools.partial(_rwkv_prep_kernel, lay=lay),
        grid=(n // tb,),
        in_specs=[pl.BlockSpec((tb, d), lambda i: (i, 0)),
                  pl.BlockSpec((SUBLANES_V7X, d), lambda i: (jnp.maximum(i * r8 - 1, 0), 0)),
                  pl.BlockSpec((SUBLANES_V7X, d), lambda i: (jnp.minimum((i + 1) * r8, last8), 0)),
                  pl.BlockSpec((1, d), lambda i: (0, 0)),
                  pl.BlockSpec((None, 1, d), cmap),
                  pl.BlockSpec((None, 1, d), cmap),
                  pl.BlockSpec((6, d), lambda i: (0, 0))],
        out_specs=[pl.BlockSpec((tb, d), lambda i: (i, 0))] * 6,
        out_shape=[jax.ShapeDtypeStruct((n, d), BF16)] * 6,
        compiler_params=_cparams(1), name="rwkv_prep",
    )(x, x, x, nw.reshape(1, d), sc, sh, mu)


def _softplus(z):
    return jnp.maximum(z, 0.0) + jnp.log(1.0 + jnp.exp(-jnp.abs(z)))


def _rwkv_lora_kernel(xw_ref, xa_ref, xg_ref, w1_ref, w2_ref, w0_ref, a1_ref, a2_ref, a0_ref,
                      g1_ref, g2_ref, lw_ref, a_ref, g_ref):
    xw, xa, xg = xw_ref[...], xa_ref[...], xg_ref[...]
    for d in range(2):
        t = jnp.tanh(jnp.dot(xw, w1_ref[d], preferred_element_type=F32))
        wl = w0_ref[d] + _bdot(t, w2_ref[d])
        wlog = -_softplus(-wl) - 0.5
        lw_ref[d] = -jnp.exp(wlog)
        t = jnp.dot(xa, a1_ref[d], preferred_element_type=F32)
        a_ref[d] = jax.nn.sigmoid(a0_ref[d] + _bdot(t, a2_ref[d]))
    t = jax.nn.sigmoid(jnp.dot(xg, g1_ref[...], preferred_element_type=F32))
    g_ref[...] = _bdot(t, g2_ref[...])


def rwkv_lora(lay, xw, xa, xg, w1, w2, w0, a1, a2, a0, g1, g2):
    n, d = xw.shape
    tm = lay.tile(128)
    full = lambda arr: pl.BlockSpec(arr.shape, lambda i: (0,) * arr.ndim)
    row = pl.BlockSpec((tm, d), lambda i: (i, 0))
    return pl.pallas_call(
        _rwkv_lora_kernel,
        grid=(n // tm,),
        in_specs=[row, row, row] + [full(t) for t in (w1, w2, w0, a1, a2, a0, g1, g2)],
        out_specs=[pl.BlockSpec((2, tm, d), lambda i: (0, i, 0)),
                   pl.BlockSpec((2, tm, d), lambda i: (0, i, 0)),
                   row],
        out_shape=[jax.ShapeDtypeStruct((2, n, d), F32), jax.ShapeDtypeStruct((2, n, d), F32),
                   jax.ShapeDtypeStruct((n, d), F32)],
        compiler_params=_cparams(1), name="rwkv_lora",
    )(xw, xa, xg, w1, w2, w0, a1, a2, a0, g1, g2)


def _wkv_constants():
    c, g, hd = RW_CHUNK, RW_GROUP, RW_HEAD
    gc, lanes = g * c, g * hd
    t = np.arange(c)
    cum = np.stack([(t[None, :] <= t[:, None]), (t[None, :] >= t[:, None])])
    tr = np.arange(c)[:, None]
    tc = np.arange(gc)[None, :] % c
    strict = np.stack([tc < tr, tc > tr])
    incl = np.stack([tc <= tr, tc >= tr])
    head_rows = np.arange(gc)[:, None] // c == np.arange(lanes)[None, :] // hd
    blk_rows = np.arange(gc)[:, None] // c == np.arange(gc)[None, :] // c
    bd = np.arange(lanes)[:, None] // hd == np.arange(lanes)[None, :] // hd
    eye_w = tr == tc
    return dict(cum=jnp.asarray(cum, BF16), strict=jnp.asarray(strict, F32), incl=jnp.asarray(incl, F32),
                head_rows=jnp.asarray(head_rows, BF16), blk_rows=jnp.asarray(blk_rows, BF16),
                bd=jnp.asarray(bd, F32), bd_b=jnp.asarray(bd, BF16), eye_w=jnp.asarray(eye_w, F32))


def _split_dot(m01, x, passes):
    acc, rem = None, x
    for _ in range(passes):
        part = rem.astype(BF16)
        term = jnp.dot(m01, part, preferred_element_type=F32)
        acc = term if acc is None else acc + term
        rem = rem - part.astype(F32)
    return acc


def _split_dot_r(x, m01, passes):
    acc, rem = None, x
    for _ in range(passes):
        part = rem.astype(BF16)
        term = jnp.dot(part, m01, preferred_element_type=F32)
        acc = term if acc is None else acc + term
        rem = rem - part.astype(F32)
    return acc


def _tile_rows(x, mask_b):
    return jnp.concatenate([x.astype(BF16)] * RW_GROUP, axis=0) * mask_b


def _wkv_tinv_many(items, head_rows, blk_rows, eye_w):
    c = RW_CHUNK
    n_ws = []
    for kk, a, lw, cum_m, strict_w in items:
        cum = _split_dot(cum_m, lw, 3)
        kkq = kk * jnp.exp(cum - lw)
        kkah = kk * a * jnp.exp(-cum)
        n_ws.append(-jnp.where(strict_w > 0, _bdot_nt(kkq, _tile_rows(kkah, head_rows)), 0.0))
    t_ws = [eye_w + n_w for n_w in n_ws]
    n_pows = [_bdot(n_w, _tile_rows(n_w, blk_rows)) for n_w in n_ws]
    levels = int(math.log2(c))
    for lv in range(1, levels):
        ws = [_tile_rows(n_pow, blk_rows) for n_pow in n_pows]
        if lv < levels - 1:
            boths = [_bdot(jnp.concatenate([t_w, n_pow], axis=0), w) for t_w, n_pow, w in zip(t_ws, n_pows, ws)]
            t_ws = [t_w + both[:c] for t_w, both in zip(t_ws, boths)]
            n_pows = [both[c:] for both in boths]
        else:
            t_ws = [t_w + _bdot(t_w, w) for t_w, w in zip(t_ws, ws)]
    return t_ws


def _wkv_prep1(raw):
    return [_split_dot(u[6], u[5], 3) for u in raw]


def _wkv_prep2(raw, cums, head_rows):
    c = RW_CHUNK
    mids = []
    for (r, v, kk, kd, a, lw, _, _, _), cum in zip(raw, cums):
        tot = jnp.sum(lw, axis=0, keepdims=True)
        kka = kk * a
        e_inv, e_rest = jnp.exp(-cum), jnp.exp(tot - cum)
        q2 = jnp.concatenate([kk * jnp.exp(cum - lw), r * jnp.exp(cum)], axis=0).astype(BF16)
        mids.append(dict(q2=q2, kdh=_tile_rows(kd * e_inv, head_rows), kkah=_tile_rows(kka * e_inv, head_rows),
                         vbd=_tile_rows(v, head_rows), v=v, decay=jnp.exp(tot),
                         kw=jnp.concatenate([kd * e_rest, -(kka * e_rest)], axis=0).astype(BF16)))
    s1s = [_bdot_nt(m['q2'], m['kdh']) for m in mids]
    s2s = [_bdot_nt(m['q2'][c:], m['kkah']) for m in mids]
    for m, u, s1, s2 in zip(mids, raw, s1s, s2s):
        strict_w, incl_w = u[7], u[8]
        m['lad'] = jnp.concatenate([jnp.where(strict_w > 0, s1[:c], 0.0),
                                    jnp.where(incl_w > 0, s1[c:], 0.0)], axis=0).astype(BF16)
        m['a_a'] = jnp.where(incl_w > 0, s2, 0.0).astype(BF16)
    return mids


def _wkv_prep3(mids):
    for m, lav in zip(mids, [_bdot(m['lad'], m['vbd']) for m in mids]):
        m['lav'] = lav
    return mids


def _wkv_adv1(states, preps):
    return [_bdot_nt(p['q2'], s) for p, s in zip(preps, states)]


def _wkv_adv2(p0s, preps, t_ws, head_rows):
    c = RW_CHUNK
    return [_bdot(t_w, _tile_rows(p0[:c] + p['lav'][:c], head_rows))
            for p0, p, t_w in zip(p0s, preps, t_ws)]


def _wkv_adv3(states, p0s, us, preps, head_rows, bd):
    c = RW_CHUNK
    upds = [_bdot(jnp.concatenate([p['v'].astype(F32), u], axis=0).T, p['kw']) for p, u in zip(preps, us)]
    aus = [_bdot(p['a_a'], _tile_rows(u, head_rows)) for p, u in zip(preps, us)]
    new_states = [s * p['decay'] + jnp.where(bd > 0, upd, 0.0) for s, p, upd in zip(states, preps, upds)]
    ys = [p0[c:] + p['lav'][c:] - au for p0, p, au in zip(p0s, preps, aus)]
    return new_states, ys


RKV_TN = 2 * RW_LANES


def _rwkv_rkv_kernel(xr_ref, xk_ref, xv_ref, wr_ref, wk_ref, wv_ref, a_ref, kkw_ref, kaw_ref, rk_ref, bd_ref,
                     r_ref, v_ref, kk_ref, kd_ref, b_ref):
    r = jnp.dot(xr_ref[...], wr_ref[...], preferred_element_type=F32)
    k = jnp.dot(xk_ref[...], wk_ref[...], preferred_element_type=F32)
    v = jnp.dot(xv_ref[...], wv_ref[...], preferred_element_type=F32)
    r_ref[...] = r
    v_ref[...] = v.astype(v_ref.dtype)
    bd = bd_ref[...]
    kk = k * kkw_ref[...]
    mix = None
    for d in range(2):
        kd = k * (1.0 + (a_ref[d] - 1.0) * kaw_ref[...])
        kd_ref[d] = kd
        term = kd * rk_ref[d]
        mix = term if mix is None else mix + term
    rm = r * mix
    for h in range(RKV_TN // RW_LANES):
        cs = slice(h * RW_LANES, (h + 1) * RW_LANES)
        kkh = kk[:, cs]
        kk_ref[:, cs] = kkh * lax.rsqrt(_split_dot_r(kkh * kkh, bd, 2) + 1e-12)
        b_ref[:, cs] = _split_dot_r(rm[:, cs], bd, 2) * v[:, cs]


def rwkv_rkv(lay, xr, xk, xv, wr, wk, wv, a, kkw, kaw, rk):
    n, kdim = xr.shape
    dm = wr.shape[1]
    tm, tn = lay.tile(512), RKV_TN
    bd_b = _wkv_constants()['bd_b']
    lhs = pl.BlockSpec((tm, kdim), lambda i, j: (i, 0))
    rhs = pl.BlockSpec((kdim, tn), lambda i, j: (0, j))
    tok = pl.BlockSpec((tm, tn), lambda i, j: (i, j))
    two = pl.BlockSpec((2, tm, tn), lambda i, j: (0, i, j))
    par = pl.BlockSpec((1, tn), lambda i, j: (0, j))
    return pl.pallas_call(
        _rwkv_rkv_kernel,
        grid=(n // tm, dm // tn),
        in_specs=[lhs, lhs, lhs, rhs, rhs, rhs, two, par, par,
                  pl.BlockSpec((2, 1, tn), lambda i, j: (0, 0, j)),
                  pl.BlockSpec(bd_b.shape, lambda i, j: (0, 0))],
        out_specs=[tok, tok, tok, two, tok],
        out_shape=[jax.ShapeDtypeStruct((n, dm), F32), jax.ShapeDtypeStruct((n, dm), BF16),
                   jax.ShapeDtypeStruct((n, dm), F32), jax.ShapeDtypeStruct((2, n, dm), F32),
                   jax.ShapeDtypeStruct((n, dm), F32)],
        compiler_params=_cparams(2), name="rwkv_rkv",
    )(xr, xk, xv, wr, wk, wv, a, kkw, kaw, rk, bd_b)


def _wkv_inv_kernel(kk_ref, a_ref, lw_ref, cum_ref, strict_ref, hr_ref, br_ref, eye_ref, t_ref, *, lay):
    c = RW_CHUNK
    hr, br, eye_w = hr_ref[...], br_ref[...], eye_ref[...]
    where = [(d, slice(ci * c, (ci + 1) * c), slice(gi * RW_LANES, (gi + 1) * RW_LANES))
             for gi in range(RW_GSTEP) for d in range(2) for ci in range(lay.tb // c)]
    items = [(kk_ref[sl, cs], a_ref[d, sl, cs], lw_ref[d, sl, cs], cum_ref[d], strict_ref[d])
             for d, sl, cs in where]
    for (d, sl, cs), t_w in zip(where, _wkv_tinv_many(items, hr, br, eye_w)):
        t_ref[d, sl, cs] = t_w.astype(t_ref.dtype)


def wkv_inv(lay, kk, a, lw):
    n, dm = kk.shape
    tb, lanes = lay.tb, RW_LANES
    k = _wkv_constants()
    full = lambda arr: pl.BlockSpec(arr.shape, lambda g, j: (0,) * arr.ndim)
    width = RW_GSTEP * lanes
    two = pl.BlockSpec((2, tb, width), lambda g, j: (0, j, g))
    return pl.pallas_call(
        functools.partial(_wkv_inv_kernel, lay=lay),
        grid=(dm // width, lay.nb),
        in_specs=[pl.BlockSpec((tb, width), lambda g, j: (j, g)), two, two,
                  full(k['cum']), full(k['strict']), full(k['head_rows']), full(k['blk_rows']),
                  full(k['eye_w'])],
        out_specs=two,
        out_shape=jax.ShapeDtypeStruct((2, n, dm), BF16),
        compiler_params=_cparams(2), name="wkv_inv",
    )(kk, a, lw, k['cum'], k['strict'], k['head_rows'], k['blk_rows'], k['eye_w'])


def _wkv_kernel(*refs, lay):
    (rf, vf, kkf, kdf, af, lwf, tf, s0f, rb, vb, kkb, kdb, ab, lwb, tb_, s0b,
     cum_ref, strict_ref, incl_ref, hr_ref, bd_ref, yf_ref, yb_ref, sff_ref, sfb_ref, sf_scr, sb_scr) = refs
    c = RW_CHUNK
    n_chunks = lay.tb // c
    j = pl.program_id(1)
    _, pos_f, cnt_f = lay.seq_info(j)
    _, pos_b, cnt_b = lay.seq_info(lay.nb - 1 - j)

    @pl.when(pos_f == 0)
    def _():
        sf_scr[...] = s0f[...]

    @pl.when(pos_b == cnt_b - 1)
    def _():
        sb_scr[...] = s0b[...]

    hr, bd = hr_ref[...], bd_ref[...]
    sl_f = [slice(ci * c, (ci + 1) * c) for ci in range(n_chunks)]
    sl_b = sl_f[::-1]
    cols = [slice(gi * RW_LANES, (gi + 1) * RW_LANES) for gi in range(RW_GSTEP)]
    def raw(ci):
        out = []
        for cs in cols:
            sf, sb = sl_f[ci], sl_b[ci]
            out.append((rf[sf, cs], vf[sf, cs], kkf[sf, cs], kdf[sf, cs], af[sf, cs], lwf[sf, cs],
                        cum_ref[0], strict_ref[0], incl_ref[0]))
            out.append((rb[sb, cs], vb[sb, cs], kkb[sb, cs], kdb[sb, cs], ab[sb, cs], lwb[sb, cs],
                        cum_ref[1], strict_ref[1], incl_ref[1]))
        return out

    states = []
    for gi in range(RW_GSTEP):
        states += [sf_scr[gi], sb_scr[gi]]
    raw_n = raw(0)
    preps = _wkv_prep3(_wkv_prep2(raw_n, _wkv_prep1(raw_n), hr))
    for ci in range(n_chunks):
        more = ci + 1 < n_chunks
        t_ws = []
        for cs in cols:
            t_ws += [tf[sl_f[ci], cs], tb_[sl_b[ci], cs]]
        p0s = _wkv_adv1(states, preps)
        if more:
            raw_n = raw(ci + 1)
            cums_n = _wkv_prep1(raw_n)
        us = _wkv_adv2(p0s, preps, t_ws, hr)
        if more:
            mids_n = _wkv_prep2(raw_n, cums_n, hr)
        states, ys = _wkv_adv3(states, p0s, us, preps, hr, bd)
        if more:
            preps = _wkv_prep3(mids_n)
        for gi, cs in enumerate(cols):
            yf_ref[sl_f[ci], cs] = ys[2 * gi]
            yb_ref[sl_b[ci], cs] = ys[2 * gi + 1]
    for gi in range(RW_GSTEP):
        sf_scr[gi] = states[2 * gi]
        sb_scr[gi] = states[2 * gi + 1]

    @pl.when(pos_f == cnt_f - 1)
    def _():
        sff_ref[...] = sf_scr[...]

    @pl.when(pos_b == 0)
    def _():
        sfb_ref[...] = sb_scr[...]


def wkv(lay, r, v, kk, kd, a, lw, t_inv, s0):
    n, dm = r.shape
    tb, lanes = lay.tb, RW_LANES
    ng = dm // lanes
    width = RW_GSTEP * lanes
    k = _wkv_constants()
    full = lambda arr: pl.BlockSpec(arr.shape, lambda g, j: (0,) * arr.ndim)

    def views(d, blk):
        tok = pl.BlockSpec((tb, width), lambda g, j: (blk(j), g))
        tok2 = pl.BlockSpec((None, tb, width), lambda g, j: (d, blk(j), g))

        def s0_map(g, j):
            seq, _, _ = lay.seq_info(blk(j))
            return (jnp.maximum(seq - lay.n_ctx + 1, 0), d, g, 0, 0)

        def sfin_map(g, j):
            seq, _, _ = lay.seq_info(blk(j))
            return (seq, g, 0, 0)

        ins = [tok, tok, tok, tok2, tok2, tok2, tok2,
               pl.BlockSpec((None, None, RW_GSTEP, lanes, lanes), s0_map)]
        return ins, tok, pl.BlockSpec((None, RW_GSTEP, lanes, lanes), sfin_map)

    in_f, y_f, sf_f = views(0, lambda j: j)
    in_b, y_b, sf_b = views(1, lambda j: lay.nb - 1 - j)
    consts = [k['cum'], k['strict'], k['incl'], k['head_rows'], k['bd']]
    args = [r, v, kk, kd, a, lw, t_inv, s0]
    return pl.pallas_call(
        functools.partial(_wkv_kernel, lay=lay),
        grid=(ng // RW_GSTEP, lay.nb),
        in_specs=in_f + in_b + [full(x) for x in consts],
        out_specs=[y_f, y_b, sf_f, sf_b],
        out_shape=[jax.ShapeDtypeStruct((n, dm), F32), jax.ShapeDtypeStruct((n, dm), F32),
                   jax.ShapeDtypeStruct((lay.n_seq, ng, lanes, lanes), F32),
                   jax.ShapeDtypeStruct((lay.n_seq, ng, lanes, lanes), F32)],
        scratch_shapes=[pltpu.VMEM((RW_GSTEP, lanes, lanes), F32), pltpu.VMEM((RW_GSTEP, lanes, lanes), F32)],
        compiler_params=_cparams(2), name="wkv",
    )(*args, *args, *consts)


def _rwkv_post_kernel(yf_ref, yb_ref, b_ref, g_ref, lnw_ref, lnb_ref, bd_ref, o_ref):
    y = yf_ref[...] + yb_ref[...]
    bd = bd_ref[...]
    inv = 1.0 / RW_HEAD
    mu = _split_dot_r(y, bd, 2) * inv
    yc = y - mu
    var = _split_dot_r(yc * yc, bd, 2) * inv
    yn = yc * lax.rsqrt(var + RW_LN_EPS)
    out = yn * lnw_ref[...] + lnb_ref[...] + b_ref[...]
    o_ref[...] = (out * g_ref[...]).astype(o_ref.dtype)


def rwkv_post(lay, y_f, y_b, bonus, g, lnw, lnb):
    n, dm = y_f.shape
    lanes = RW_LANES
    tm = lay.tile(512)
    bd_b = _wkv_constants()['bd_b']
    tok = pl.BlockSpec((tm, lanes), lambda i, c: (i, c))
    par = pl.BlockSpec((1, lanes), lambda i, c: (0, c))
    return pl.pallas_call(
        _rwkv_post_kernel,
        grid=(n // tm, dm // lanes),
        in_specs=[tok, tok, tok, tok, par, par, pl.BlockSpec(bd_b.shape, lambda i, c: (0, 0))],
        out_specs=tok,
        out_shape=jax.ShapeDtypeStruct((n, dm), BF16),
        compiler_params=_cparams(2), name="rwkv_post",
    )(y_f, y_b, bonus, g, lnw, lnb, bd_b)


def rwkv_layer(lay, x, mods, nw, p, s0):
    n, d = x.shape
    sh1, sc1, g1 = mods[0], mods[1], mods[2]
    xr, xw, xk, xv, xa, xg = rwkv_prep(lay, x, nw, sc1, sh1, p['mu'])
    tm = lay.tile(1024)
    lw, a, g = rwkv_lora(lay, xw, xa, xg, p['w1'], p['w2'], p['w0'], p['a1'], p['a2'], p['a0'],
                         p['g1'], p['g2'])
    r, v, kk, kd, bonus = rwkv_rkv(lay, xr, xk, xv, p['wr'], p['wk'], p['wv'], a, p['kk'], p['ka'], p['rk'])
    t_inv = wkv_inv(lay, kk, a, lw)
    y_f, y_b, sfin_f, sfin_b = wkv(lay, r, v, kk, kd, a, lw, t_inv, s0)
    z = rwkv_post(lay, y_f, y_b, bonus, g, p['lnx_w'], p['lnx_b'])
    x = matmul_gated_residual(lay, z, p['wo'], x, g1, tm=tm, tn=512, name="rwkv_o")
    return x, jnp.stack([sfin_f, sfin_b], axis=1)


def _hgrn_constants():
    c = HG_CHUNK_TOKENS
    t = np.arange(c)[:, None]
    j = np.arange(c)[None, :]
    cums, masks_all = [], []
    for rev in (False, True):
        masks = []
        h = 1
        while h < c:
            upper = (t % (2 * h)) >= h
            same = (t // (2 * h)) == (j // (2 * h))
            if not rev:
                mask = same & upper & ((j % (2 * h)) < h)
            else:
                mask = same & (~upper) & ((j % (2 * h)) >= h)
            masks.append(mask)
            h *= 2
        masks.append(t == j)
        cums.append((j >= t) if rev else (j <= t))
        masks_all.append(np.stack(masks, 0))
    return jnp.asarray(np.stack(cums), BF16), jnp.asarray(np.stack(masks_all).astype(np.float32))


def _hgrn_level_exponents(g, gcum, rev):
    c, kdim = g.shape
    row = lax.broadcasted_iota(jnp.int32, g.shape, 0)
    nxt = pltpu.roll(g, c - 1, axis=0)
    prv = pltpu.roll(g, 1, axis=0)
    r2, r4 = row & 1, row & 3
    if not rev:
        x1 = jnp.where(r2 == 1, g, 0.0)
        x2 = jnp.where(r4 == 0, nxt, jnp.where(r4 == 2, g, jnp.where(r4 == 3, prv + g, 0.0)))
    else:
        x1 = jnp.where(r2 == 0, g, 0.0)
        x2 = jnp.where(r4 == 0, g + nxt, jnp.where(r4 == 1, g, jnp.where(r4 == 3, prv, 0.0)))
    xs = [x1, x2]
    h = 4
    while h < c:
        gr = gcum.reshape(c // (2 * h), 2 * h, kdim)
        ref = gr[:, h:h + 1, :] if rev else gr[:, h - 1:h, :]
        upper = lax.broadcasted_iota(jnp.int32, gr.shape, 1) >= h
        diff = gr - ref
        x = jnp.where(upper, -diff, diff) if rev else jnp.where(upper, diff, -diff)
        xs.append(x.reshape(c, kdim))
        h *= 2
    return xs


def _hgrn_units(units, masks_by_dir, cum_by_dir):
    c = HG_CHUNK_TOKENS
    gs = [jnp.log(f) for _, f, _, _ in units]
    gcums = [_split_dot(cum_by_dir[rev], g, 3) for g, (_, _, _, rev) in zip(gs, units)]
    outs = []
    pend = []
    for (q, f, iv, rev), g, gcum in zip(units, gs, gcums):
        k = 1.0 - f
        tot = gcum[0:1] if rev else gcum[c - 1:c]
        es = [jnp.exp(x) for x in _hgrn_level_exponents(g, gcum, rev)]
        pend.append((q, k, iv, rev, es, jnp.exp(gcum), jnp.exp(tot - gcum), jnp.exp(tot)))
    for q, k, iv, rev, es, eg, erest, etot in pend:
        masks = masks_by_dir[rev]
        a = jnp.where(masks[len(es)] > 0, _bdot_nt(q, k), 0.0)
        for lv, el in enumerate(es):
            a = a + jnp.where(masks[lv] > 0, _bdot_nt(q * el, k * el), 0.0)
        outs.append(dict(a=a, iv=iv, qe=(q * eg).astype(BF16), kdec=(k * erest).astype(BF16), decay=etot))
    for u in outs:
        u['av'] = _bdot(u['a'], u['iv'])
    for u in outs:
        u['upd'] = _bdot(u['iv'].T, u['kdec'])
    return outs


def _hgrn_kernel(*refs, lay):
    (qf, ff, if_, s0f, qb, fb, ib, s0b, cum_ref, mask_ref, of_ref, ob_ref, sff_ref, sfb_ref,
     sf_scr, sb_scr) = refs
    c = HG_CHUNK_TOKENS
    n_chunks = lay.tb // c
    j = pl.program_id(1)
    _, pos_f, cnt_f = lay.seq_info(j)
    _, pos_b, cnt_b = lay.seq_info(lay.nb - 1 - j)

    @pl.when(pos_f == 0)
    def _():
        sf_scr[...] = s0f[...]

    @pl.when(pos_b == cnt_b - 1)
    def _():
        sb_scr[...] = s0b[...]

    masks_by_dir = [mask_ref[0], mask_ref[1]]
    cum_by_dir = [cum_ref[0], cum_ref[1]]
    sl_f = [slice(ci * c, (ci + 1) * c) for ci in range(n_chunks)]
    sl_b = sl_f[::-1]
    cols = [slice(hi * HG_K, (hi + 1) * HG_K) for hi in range(HG_HSTEP)]
    chains = []
    for hi, cs in enumerate(cols):
        chains.append((sf_scr, hi, of_ref, [(sl, cs) for sl in sl_f],
                       [(qf[sl, cs], ff[sl, cs], if_[sl, cs], False) for sl in sl_f]))
        chains.append((sb_scr, hi, ob_ref, [(sl, cs) for sl in sl_b],
                       [(qb[sl, cs], fb[sl, cs], ib[sl, cs], True) for sl in sl_b]))
    done = _hgrn_units([u for ch in chains for u in ch[4]], masks_by_dir, cum_by_dir)
    pend = []
    for n_ch, (scr, hi, o_ref, where, _) in enumerate(chains):
        s = scr[hi]
        for ci in range(n_chunks):
            u = done[n_ch * n_chunks + ci]
            pend.append((o_ref, where[ci], u, s))
            s = s * u['decay'] + u['upd']
        scr[hi] = s
    for o_ref, (sl, cs), u, s_prev in pend:
        o_ref[sl, cs] = u['av'] + _bdot_nt(u['qe'], s_prev)

    @pl.when(pos_f == cnt_f - 1)
    def _():
        sff_ref[...] = sf_scr[...]

    @pl.when(pos_b == 0)
    def _():
        sfb_ref[...] = sb_scr[...]


def hgrn_scan(lay, q, f_fwd, f_bwd, iv, s0t):
    n, dm = q.shape
    tb = lay.tb
    nh = dm // HG_K
    width = HG_HSTEP * HG_K
    cum_m, masks = _hgrn_constants()
    full = lambda arr: pl.BlockSpec(arr.shape, lambda h, j: (0,) * arr.ndim)

    def views(d, blk):
        tok = pl.BlockSpec((tb, width), lambda h, j: (blk(j), h))

        def s0_map(h, j):
            seq, _, _ = lay.seq_info(blk(j))
            return (jnp.maximum(seq - lay.n_ctx + 1, 0), d, h, 0, 0)

        def sfin_map(h, j):
            seq, _, _ = lay.seq_info(blk(j))
            return (seq, h, 0, 0)

        ins = [tok, tok, tok, pl.BlockSpec((None, None, HG_HSTEP, HG_K, HG_K), s0_map)]
        return ins, tok, pl.BlockSpec((None, HG_HSTEP, HG_K, HG_K), sfin_map)

    in_f, o_f, sf_f = views(0, lambda j: j)
    in_b, o_b, sf_b = views(1, lambda j: lay.nb - 1 - j)
    return pl.pallas_call(
        functools.partial(_hgrn_kernel, lay=lay),
        grid=(nh // HG_HSTEP, lay.nb),
        in_specs=in_f + in_b + [full(cum_m), full(masks)],
        out_specs=[o_f, o_b, sf_f, sf_b],
        out_shape=[jax.ShapeDtypeStruct((n, dm), F32), jax.ShapeDtypeStruct((n, dm), F32),
                   jax.ShapeDtypeStruct((lay.n_seq, nh, HG_K, HG_K), F32),
                   jax.ShapeDtypeStruct((lay.n_seq, nh, HG_K, HG_K), F32)],
        scratch_shapes=[pltpu.VMEM((HG_HSTEP, HG_K, HG_K), F32), pltpu.VMEM((HG_HSTEP, HG_K, HG_K), F32)],
        compiler_params=_cparams(2), name="hgrn_scan",
    )(q, f_fwd, iv, s0t, q, f_bwd, iv, s0t, cum_m, masks)


def _hgrn_post_kernel(of_ref, ob_ref, g_ref, nw_ref, z_ref):
    o = of_ref[...] + ob_ref[...]
    o = o * lax.rsqrt(jnp.mean(o * o, axis=-1, keepdims=True) + NORM_EPS) * nw_ref[...] * g_ref[...]
    z_ref[...] = o.astype(z_ref.dtype)


def hgrn_post(lay, o_f, o_b, gs, nw):
    n, dm = o_f.shape
    tm = lay.tile(1024)
    tok = pl.BlockSpec((tm, HG_K), lambda i, h: (i, h))
    return pl.pallas_call(
        _hgrn_post_kernel,
        grid=(n // tm, dm // HG_K),
        in_specs=[tok, tok, tok, pl.BlockSpec((1, HG_K), lambda i, h: (0, 0))],
        out_specs=tok,
        out_shape=jax.ShapeDtypeStruct((n, dm), BF16),
        compiler_params=_cparams(2), name="hgrn_post",
    )(o_f, o_b, gs, nw)


def _epi_hgrn_in(accs, extras):
    lb = extras[0]
    q = _silu(accs[0])
    f0 = lb + (1.0 - lb) * jax.nn.sigmoid(accs[1])
    f1 = lb + (1.0 - lb) * jax.nn.sigmoid(accs[2])
    return [q, f0, f1, accs[3], _silu(accs[4])]


def hgrn_layer(lay, x, mods, nw, p, lb, s0t):
    n, d = x.shape
    sh1, sc1, g1 = mods[0], mods[1], mods[2]
    h = norm_mod(lay, x, nw, sc1, sh1, BF16)
    tm, tn = lay.tile(1024), 256
    extras = ((lb, pl.BlockSpec((1, tn), lambda i, j: (0, j))),)
    q, f0, f1, iv, gs = matmul(h, p['w_in'], _epi_hgrn_in, [F32] * 5, tm=tm, tn=tn, extras=extras,
                               name="hgrn_in")
    o_f, o_b, sfin_f, sfin_b = hgrn_scan(lay, q, f0, f1, iv, s0t)
    z = hgrn_post(lay, o_f, o_b, gs, p['norm_w'])
    x = matmul_gated_residual(lay, z, p['wo'], x, g1, tm=lay.tile(1024), tn=512, name="hgrn_o")
    return x, jnp.stack([sfin_f, sfin_b], axis=1)


ML_DOWN_COLS = 1280
ML_KR_OFF = ML_Q_LORA + ML_KV_LORA
ML_KRS_OFF = ML_KR_OFF + LANES_V7X


def _rms(x, w):
    return x * lax.rsqrt(jnp.mean(x * x, axis=-1, keepdims=True) + NORM_EPS) * w


def _mla_mid_kernel(dn_ref, qw_ref, kvw_ref, cos_ref, sin_ref, qn_ref, ckv_ref, kr_ref):
    dn = dn_ref[...]
    qn_ref[...] = _rms(dn[:, :ML_Q_LORA], qw_ref[...]).astype(qn_ref.dtype)
    ckv_ref[...] = _rms(dn[:, ML_Q_LORA:ML_KR_OFF], kvw_ref[...])
    kr = dn[:, ML_KR_OFF:ML_KR_OFF + ML_ROPE]
    krs = dn[:, ML_KRS_OFF:ML_KRS_OFF + ML_ROPE]
    kr_ref[...] = kr * cos_ref[...] + krs * sin_ref[...]


def mla_mid(lay, dn, qw, kvw, cos, sin):
    n = dn.shape[0]
    tm = lay.tile(512)
    return pl.pallas_call(
        _mla_mid_kernel,
        grid=(n // tm,),
        in_specs=[pl.BlockSpec((tm, ML_DOWN_COLS), lambda i: (i, 0)),
                  pl.BlockSpec((1, ML_Q_LORA), lambda i: (0, 0)),
                  pl.BlockSpec((1, ML_KV_LORA), lambda i: (0, 0)),
                  pl.BlockSpec((tm, ML_ROPE), lambda i: (i, 0)),
                  pl.BlockSpec((tm, ML_ROPE), lambda i: (i, 0))],
        out_specs=[pl.BlockSpec((tm, ML_Q_LORA), lambda i: (i, 0)),
                   pl.BlockSpec((tm, ML_KV_LORA), lambda i: (i, 0)),
                   pl.BlockSpec((tm, ML_ROPE), lambda i: (i, 0))],
        out_shape=[jax.ShapeDtypeStruct((n, ML_Q_LORA), BF16),
                   jax.ShapeDtypeStruct((n, ML_KV_LORA), F32),
                   jax.ShapeDtypeStruct((n, ML_ROPE), F32)],
        compiler_params=_cparams(1), name="mla_mid",
    )(dn, qw, kvw, cos, sin)


ML_QSCALE = math.log2(math.e) / math.sqrt(ML_NOPE + ML_ROPE)


def _epi_qscale(accs, extras):
    return [accs[0] * ML_QSCALE]


def _epi_rope(accs, extras):
    cos, sin = extras
    return [(accs[0] * cos + accs[1] * sin) * ML_QSCALE]


def _attn_kernel(qn_ref, qr_ref, kn_ref, kr_ref, v_ref, o_ref, kc_scr):
    @pl.when(pl.program_id(2) == 0)
    def _():
        for h in range(2):
            kc_scr[h, :, :ML_NOPE] = kn_ref[:, h * ML_NOPE:(h + 1) * ML_NOPE]
            kc_scr[h, :, ML_NOPE:] = kr_ref[...]

    outs = []
    for h in range(2):
        q = jnp.concatenate([qn_ref[:, h * ML_NOPE:(h + 1) * ML_NOPE],
                             qr_ref[:, h * ML_ROPE:(h + 1) * ML_ROPE]], axis=1)
        s = lax.dot_general(q, kc_scr[h], (((1,), (1,)), ((), ())), preferred_element_type=F32)
        m = jnp.max(s, axis=-1, keepdims=True)
        p = jnp.exp2(s - m)
        l = jnp.sum(p, axis=-1, keepdims=True)
        o = jnp.dot(p.astype(BF16), v_ref[:, h * ML_V:(h + 1) * ML_V], preferred_element_type=F32)
        outs.append(o / l)
    o_ref[...] = jnp.concatenate(outs, axis=1).astype(o_ref.dtype)


def attention(qn, qr, kn, kr, v, *, n_seq, q_len, k_len, row0, tq):
    heads2 = qn.shape[1] // (2 * ML_NOPE)
    qb = q_len // tq
    rb0 = row0 // tq
    return pl.pallas_call(
        _attn_kernel,
        grid=(n_seq, heads2, qb),
        in_specs=[pl.BlockSpec((tq, 2 * ML_NOPE), lambda s, h, i: (rb0 + s * qb + i, h)),
                  pl.BlockSpec((tq, 2 * ML_ROPE), lambda s, h, i: (rb0 + s * qb + i, h)),
                  pl.BlockSpec((k_len, 2 * ML_NOPE), lambda s, h, i: (s, h)),
                  pl.BlockSpec((k_len, ML_ROPE), lambda s, h, i: (s, 0)),
                  pl.BlockSpec((k_len, 2 * ML_V), lambda s, h, i: (s, h))],
        out_specs=pl.BlockSpec((tq, 2 * ML_V), lambda s, h, i: (s * qb + i, h)),
        out_shape=jax.ShapeDtypeStruct((n_seq * q_len, heads2 * 2 * ML_V), BF16),
        scratch_shapes=[pltpu.VMEM((2, k_len, ML_NOPE + ML_ROPE), BF16)],
        compiler_params=_cparams(3), name="mla_attn",
    )(qn, qr, kn, kr, v)


def mla_layer(lay, x, mods, nw, p, cache_ckv, cache_kr, cos, sin, cos2, sin2):
    n, d = x.shape
    sh1, sc1, g1 = mods[0], mods[1], mods[2]
    h = norm_mod(lay, x, nw, sc1, sh1, BF16)
    tm = lay.tile(512)
    dn = matmul(h, [p['w_down']], _epi_plain, [F32], tm=tm, tn=ML_DOWN_COLS, name="mla_down")[0]
    qlat, ckv, kr = mla_mid(lay, dn, p['qnorm_w'], p['kvnorm_w'], cos, sin)
    qn = matmul(qlat, [p['w_uq_nope']], _epi_qscale, [BF16], tm=tm, tn=512, name="mla_qn")[0]
    tn = 2 * ML_ROPE
    extras = ((cos2, pl.BlockSpec((tm, tn), lambda i, j: (i, 0))),
              (sin2, pl.BlockSpec((tm, tn), lambda i, j: (i, 0))))
    qr = matmul(qlat, [p['w_uq_rope'], p['w_uq_rope_sw']], _epi_rope, [BF16], tm=tm, tn=tn,
                extras=extras, name="mla_qr")[0]
    nc, past = lay.nc, cache_ckv.shape[1]
    ckv_b, kr_b = ckv.astype(BF16), kr.astype(BF16)
    kn_c, v_c = matmul(ckv_b[:nc], [p['w_ukn'], p['w_uv']], _epi_plain, [BF16, BF16],
                       tm=lay.tile(512), tn=512, name="mla_kv_ctx")
    o_c = attention(qn, qr, kn_c, kr_b[:nc], v_c, n_seq=lay.n_ctx, q_len=lay.ctx_len,
                    k_len=lay.ctx_len, row0=0, tq=min(256, lay.ctx_len))
    k_len = lay.lat_len + past
    ckv_l = jnp.concatenate([ckv_b[nc:].reshape(lay.n_lat, lay.lat_len, -1), cache_ckv.astype(BF16)],
                            axis=1).reshape(lay.n_lat * k_len, -1)
    kr_l = jnp.concatenate([kr_b[nc:].reshape(lay.n_lat, lay.lat_len, -1), cache_kr.astype(BF16)],
                           axis=1).reshape(lay.n_lat * k_len, -1)
    tk = math.gcd(k_len, 512)
    kn_l, v_l = matmul(ckv_l, [p['w_ukn'], p['w_uv']], _epi_plain, [BF16, BF16], tm=tk, tn=512,
                       name="mla_kv_lat")
    o_l = attention(qn, qr, kn_l, kr_l, v_l, n_seq=lay.n_lat, q_len=lay.lat_len, k_len=k_len,
                    row0=nc, tq=min(256, lay.lat_len))
    o = jnp.concatenate([o_c, o_l], axis=0)
    x = matmul_gated_residual(lay, o, p['wo'], x, g1, tm=lay.tile(1024), tn=512, name="mla_o")
    return x, ckv[:nc], kr[:nc]


def _rope_tables(lay):
    t = lay.lat_len
    rows = t // GRID_W
    rr = jnp.broadcast_to(jnp.arange(rows, dtype=F32)[:, None], (rows, GRID_W)).reshape(-1)
    cc = jnp.broadcast_to(jnp.arange(GRID_W, dtype=F32)[None, :], (rows, GRID_W)).reshape(-1)
    nf = ML_ROPE // 4
    inv = ROPE_BASE ** (-jnp.arange(nf, dtype=F32) / nf)
    ar, ac = rr[:, None] * inv, cc[:, None] * inv
    cos = jnp.concatenate([jnp.cos(ar), jnp.cos(ar), jnp.cos(ac), jnp.cos(ac)], axis=-1)
    sin = jnp.concatenate([-jnp.sin(ar), jnp.sin(ar), -jnp.sin(ac), jnp.sin(ac)], axis=-1)
    cos = jnp.concatenate([jnp.ones((lay.nc, ML_ROPE), F32), jnp.tile(cos, (lay.n_lat, 1))], axis=0)
    sin = jnp.concatenate([jnp.zeros((lay.nc, ML_ROPE), F32), jnp.tile(sin, (lay.n_lat, 1))], axis=0)
    return cos, sin


def _swap_cols(w):
    k, c = w.shape
    w4 = w.reshape(k, c // 32, 2, 16)
    return w4[:, :, ::-1, :].reshape(k, c)


def ffn(lay, x, mods, nw, w_a, w_b, w_out):
    sh2, sc2, g2 = mods[3], mods[4], mods[5]
    h = norm_mod(lay, x, nw, sc2, sh2, BF16)
    act = matmul(h, [w_a, w_b], _epi_swiglu, [BF16], tm=lay.tile(1024), tn=512, name="ffn_in")[0]
    return matmul_gated_residual(lay, act, w_out, x, g2, tm=lay.tile(1024), tn=512, name="ffn_out")


def _block_diag_states(s):
    b, two, h, n, _ = s.shape
    s = s.reshape(b, two, h // RW_GROUP, RW_GROUP, n, n)
    eye = jnp.eye(RW_GROUP, dtype=s.dtype)
    out = jnp.einsum('bdghvk,hi->bdghvik', s, eye)
    return out.reshape(b, two, h // RW_GROUP, RW_GROUP * n, RW_GROUP * n)


def _diag_blocks(s):
    b, two, g, l, _ = s.shape
    n = l // RW_GROUP
    s = s.reshape(b, two, g, RW_GROUP, n, RW_GROUP, n)
    s = jnp.moveaxis(jnp.diagonal(s, axis1=3, axis2=5), -1, 3)
    return s.reshape(b, two, g * RW_GROUP, n, n)


def kernel(x_prompt, x_sample, state_rwkv, state_hgrn, cache_ckv, cache_krope, c, c_ctx, ada_w, ada_b, norm1_w, norm2_w, ffn_w_in, ffn_w_out, final_norm_w, rw_mu, rw_wr, rw_wk, rw_wv, rw_wo, rw_w0, rw_w1, rw_w2, rw_a0, rw_a1, rw_a2, rw_g1, rw_g2, rw_kk, rw_ka, rw_rk, rw_lnx_w, rw_lnx_b, hg_w_in, hg_lb, hg_norm_w, hg_wo, ml_w_down, ml_qnorm_w, ml_kvnorm_w, ml_w_uq, ml_w_ukv, ml_wo):
    n_ctx, ctx_len, d = x_prompt.shape
    n_lat, lat_len, _ = x_sample.shape
    depth = ada_w.shape[0]
    lay = Layout(n_ctx, ctx_len, n_lat, lat_len)
    d_ff = ffn_w_out.shape[1]
    x = jnp.concatenate([x_prompt.reshape(lay.nc, d), x_sample.reshape(n_lat * lat_len, d)], axis=0)

    n_cond = -(-(1 + n_lat) // SUBLANES_V7X) * SUBLANES_V7X
    cond = jnp.zeros((n_cond, d), F32).at[0].set(c_ctx).at[1:1 + n_lat].set(c)
    mod_all = adaln(cond, ada_w, ada_b)
    mod_all = mod_all.reshape(depth, n_cond, 6, 1, d).transpose(0, 2, 1, 3, 4)

    lb_table = jnp.cumsum(jax.nn.softmax(hg_lb.astype(F32), axis=0), axis=0)
    lb_table = lb_table - lb_table[0]
    cos, sin = _rope_tables(lay)
    cos2, sin2 = jnp.tile(cos, (1, 2)), jnp.tile(sin, (1, 2))
    bf = lambda t: t.astype(BF16)

    new_rwkv, new_hgrn, new_ckv, new_krope = [], [], [], []
    for l in range(depth):
        kind, j = l % 3, l // 3
        mods = mod_all[l]
        if kind == 0:
            pad1 = lambda w: jnp.pad(w, ((0, 0), (0, 0), (0, LORA_PAD - w.shape[2])))
            pad2 = lambda w: jnp.pad(w, ((0, 0), (0, LORA_PAD - w.shape[1]), (0, 0)))
            p = {'mu': rw_mu[j], 'wr': bf(rw_wr[j]), 'wk': bf(rw_wk[j]), 'wv': bf(rw_wv[j]),
                 'wo': bf(rw_wo[j]),
                 'w0': rw_w0[j].reshape(2, 1, d), 'w1': bf(pad1(rw_w1[j])), 'w2': bf(pad2(rw_w2[j])),
                 'a0': rw_a0[j].reshape(2, 1, d), 'a1': bf(pad1(rw_a1[j])), 'a2': bf(pad2(rw_a2[j])),
                 'g1': bf(rw_g1[j]), 'g2': bf(rw_g2[j]),
                 'kk': rw_kk[j].reshape(1, d), 'ka': rw_ka[j].reshape(1, d),
                 'rk': rw_rk[j].reshape(2, 1, d),
                 'lnx_w': rw_lnx_w[j].reshape(1, d), 'lnx_b': rw_lnx_b[j].reshape(1, d)}
            s_lat = _block_diag_states(state_rwkv[:, j].astype(F32))
            s0 = jnp.concatenate([jnp.zeros((1,) + s_lat.shape[1:], F32), s_lat], axis=0)
            x, sfin = rwkv_layer(lay, x, mods, norm1_w[l], p, s0)
            new_rwkv.append(_diag_blocks(sfin[:n_ctx]))
        elif kind == 1:
            hk = d
            w_in = hg_w_in[j]
            p = {'w_in': [bf(w_in[:, i * hk:(i + 1) * hk]) for i in range(5)],
                 'norm_w': hg_norm_w[j].reshape(1, HG_K), 'wo': bf(hg_wo[j])}
            s_lat = jnp.swapaxes(state_hgrn[:, j].astype(F32), -1, -2)
            s0t = jnp.concatenate([jnp.zeros((1,) + s_lat.shape[1:], F32), s_lat], axis=0)
            x, sfin = hgrn_layer(lay, x, mods, norm1_w[l], p, lb_table[l].reshape(1, d), s0t)
            new_hgrn.append(jnp.swapaxes(sfin[:n_ctx], -1, -2))
        else:
            wd = ml_w_down[j]
            kr_w = wd[:, ML_KR_OFF:]
            zpad = jnp.zeros((d, LANES_V7X - ML_ROPE), wd.dtype)
            w_down = jnp.concatenate([wd, zpad, _swap_cols(kr_w), zpad], axis=1)
            wq = ml_w_uq[j].reshape(ML_Q_LORA, ML_H, ML_NOPE + ML_ROPE)
            wq_n = wq[:, :, :ML_NOPE].reshape(ML_Q_LORA, ML_H * ML_NOPE)
            wq_r = wq[:, :, ML_NOPE:].reshape(ML_Q_LORA, ML_H * ML_ROPE)
            wkv = ml_w_ukv[j].reshape(ML_KV_LORA, ML_H, ML_NOPE + ML_V)
            p = {'w_down': bf(w_down), 'qnorm_w': ml_qnorm_w[j].reshape(1, -1),
                 'kvnorm_w': ml_kvnorm_w[j].reshape(1, -1),
                 'w_uq_nope': bf(wq_n), 'w_uq_rope': bf(wq_r), 'w_uq_rope_sw': bf(_swap_cols(wq_r)),
                 'w_ukn': bf(wkv[:, :, :ML_NOPE].reshape(ML_KV_LORA, ML_H * ML_NOPE)),
                 'w_uv': bf(wkv[:, :, ML_NOPE:].reshape(ML_KV_LORA, ML_H * ML_V)),
                 'wo': bf(ml_wo[j])}
            x, ckv_c, kr_c = mla_layer(lay, x, mods, norm1_w[l], p, cache_ckv[:, j], cache_krope[:, j],
                                       cos, sin, cos2, sin2)
            new_ckv.append(ckv_c.reshape(n_ctx, ctx_len, ML_KV_LORA))
            new_krope.append(kr_c.reshape(n_ctx, ctx_len, ML_ROPE))
        w_in = ffn_w_in[l]
        x = ffn(lay, x, mods, norm2_w[l], bf(w_in[:, :d_ff]), bf(w_in[:, d_ff:]), bf(ffn_w_out[l]))

    y_prompt = rmsnorm_rows(lay, x, final_norm_w, 0, lay.nc).reshape(n_ctx, ctx_len, d)
    y_sample = rmsnorm_rows(lay, x, final_norm_w, lay.nc, lay.n - lay.nc).reshape(n_lat, lat_len, d)
    return (y_prompt, y_sample, jnp.stack(new_rwkv, axis=1), jnp.stack(new_hgrn, axis=1),
            jnp.stack(new_ckv, axis=1), jnp.stack(new_krope, axis=1))
```

```python
import functools
import math

import numpy as np
import jax
import jax.numpy as jnp
from jax import lax
from jax.experimental import pallas as pl
from jax.experimental.pallas import tpu as pltpu

F32 = jnp.float32
BF16 = jnp.bfloat16

LANES_V7X = 128
SUBLANES_V7X = 8
VMEM_BYTES_V7X = 64 * 1024 * 1024
VMEM_LIMIT = 56 * 1024 * 1024

NORM_EPS = 1e-6
RW_HEAD = 64
RW_LN_EPS = 64e-5
RW_GROUP = 4
RW_LANES = RW_GROUP * RW_HEAD
RW_CHUNK = 64
RW_GSTEP = 4
RW_INV_GSTEP = 2
HG_K = 128
HG_CHUNK_TOKENS = 64
HG_HSTEP = 2
ML_H = 16
ML_NOPE = 128
ML_ROPE = 64
ML_V = 128
ML_Q_LORA = 512
ML_KV_LORA = 512
GRID_W = 64
ROPE_BASE = 10000.0
LORA_PAD = 128


class Layout:
    def __init__(self, n_ctx, ctx_len, n_lat, lat_len):
        self.n_ctx, self.ctx_len, self.n_lat, self.lat_len = n_ctx, ctx_len, n_lat, lat_len
        self.nc = n_ctx * ctx_len
        self.n = self.nc + n_lat * lat_len
        self.tb = min(256, ctx_len)
        assert ctx_len % self.tb == 0 and lat_len % self.tb == 0 and self.tb % RW_CHUNK == 0
        self.nb = self.n // self.tb
        self.nb_ctx = self.nc // self.tb
        self.bps_ctx = ctx_len // self.tb
        self.bps_lat = lat_len // self.tb
        self.n_seq = n_ctx + n_lat

    def tile(self, want):
        t = want
        while self.nc % t or self.lat_len % t:
            t //= 2
        return t

    def cond_of_tile(self, i, tm):
        row = i * tm
        return jnp.where(row < self.nc, 0, 1 + (row - self.nc) // self.lat_len)

    def seq_info(self, blk):
        is_ctx = blk < self.nb_ctx
        lat = blk - self.nb_ctx
        seq = jnp.where(is_ctx, blk // self.bps_ctx, self.n_ctx + lat // self.bps_lat)
        pos = jnp.where(is_ctx, blk % self.bps_ctx, lat % self.bps_lat)
        cnt = jnp.where(is_ctx, self.bps_ctx, self.bps_lat)
        return seq, pos, cnt


def _cparams(n_axes):
    return pltpu.CompilerParams(dimension_semantics=("arbitrary",) * n_axes, vmem_limit_bytes=VMEM_LIMIT)


def _bdot(a, b):
    return jnp.dot(a.astype(BF16), b.astype(BF16), preferred_element_type=F32)


def _bdot_nt(a, b):
    return lax.dot_general(a.astype(BF16), b.astype(BF16), (((1,), (1,)), ((), ())),
                           preferred_element_type=F32)


def _silu(x):
    return x * jax.nn.sigmoid(x)


def _adaln_kernel(c_ref, w_ref, b_ref, o_ref):
    a = _silu(c_ref[...]).astype(BF16)
    o_ref[...] = jnp.dot(a, w_ref[...].astype(BF16), preferred_element_type=F32) + b_ref[...]


def adaln(cond, ada_w, ada_b):
    depth, d, d6 = ada_w.shape
    r = cond.shape[0]
    tn = 1024
    return pl.pallas_call(
        _adaln_kernel,
        grid=(depth, d6 // tn),
        in_specs=[pl.BlockSpec((r, d), lambda l, j: (0, 0)),
                  pl.BlockSpec((None, d, tn), lambda l, j: (l, 0, j)),
                  pl.BlockSpec((None, 1, tn), lambda l, j: (l, 0, j))],
        out_specs=pl.BlockSpec((None, r, tn), lambda l, j: (l, 0, j)),
        out_shape=jax.ShapeDtypeStruct((depth, r, d6), F32),
        compiler_params=_cparams(2), name="adaln",
    )(cond, ada_w, ada_b.reshape(depth, 1, d6))


def _norm_mod(x, nw, sc, sh):
    y = x * lax.rsqrt(jnp.mean(x * x, axis=-1, keepdims=True) + NORM_EPS)
    return (y * nw) * (1.0 + sc) + sh


def _norm_mod_kernel(x_ref, nw_ref, sc_ref, sh_ref, o_ref):
    o_ref[...] = _norm_mod(x_ref[...], nw_ref[...], sc_ref[...], sh_ref[...]).astype(o_ref.dtype)


def norm_mod(lay, x, nw, sc, sh, out_dtype):
    n, d = x.shape
    tm = lay.tile(512)
    cmap = lambda i: (lay.cond_of_tile(i, tm), 0, 0)
    return pl.pallas_call(
        _norm_mod_kernel,
        grid=(n // tm,),
        in_specs=[pl.BlockSpec((tm, d), lambda i: (i, 0)),
                  pl.BlockSpec((1, d), lambda i: (0, 0)),
                  pl.BlockSpec((None, 1, d), cmap),
                  pl.BlockSpec((None, 1, d), cmap)],
        out_specs=pl.BlockSpec((tm, d), lambda i: (i, 0)),
        out_shape=jax.ShapeDtypeStruct((n, d), out_dtype),
        compiler_params=_cparams(1), name="norm_mod",
    )(x, nw.reshape(1, d), sc, sh)


def _rmsnorm_kernel(x_ref, w_ref, o_ref):
    x = x_ref[...]
    o_ref[...] = x * lax.rsqrt(jnp.mean(x * x, axis=-1, keepdims=True) + NORM_EPS) * w_ref[...]


def rmsnorm_rows(lay, x, w, row0, rows):
    d = x.shape[1]
    tm = lay.tile(512)
    b0 = row0 // tm
    return pl.pallas_call(
        _rmsnorm_kernel,
        grid=(rows // tm,),
        in_specs=[pl.BlockSpec((tm, d), lambda i: (b0 + i, 0)), pl.BlockSpec((1, d), lambda i: (0, 0))],
        out_specs=pl.BlockSpec((tm, d), lambda i: (i, 0)),
        out_shape=jax.ShapeDtypeStruct((rows, d), x.dtype),
        compiler_params=_cparams(1), name="final_norm",
    )(x, w.reshape(1, d))


def _mm_kernel(*refs, n_w, n_e, epi):
    a = refs[0][...]
    accs = [jnp.dot(a, refs[1 + i][...], preferred_element_type=F32) for i in range(n_w)]
    extras = [refs[1 + n_w + i][...] for i in range(n_e)]
    outs = epi(accs, extras)
    o_refs = refs[1 + n_w + n_e:]
    for o_ref, val in zip(o_refs, outs):
        o_ref[...] = val.astype(o_ref.dtype)


def matmul(a, ws, epi, out_dtypes, *, tm, tn, extras=(), name):
    m, k = a.shape
    nw = ws[0].shape[1]
    assert m % tm == 0 and nw % tn == 0
    in_specs = [pl.BlockSpec((tm, k), lambda i, j: (i, 0))]
    in_specs += [pl.BlockSpec((k, tn), lambda i, j: (0, j)) for _ in ws]
    in_specs += [spec for _, spec in extras]
    outs = pl.pallas_call(
        functools.partial(_mm_kernel, n_w=len(ws), n_e=len(extras), epi=epi),
        grid=(m // tm, nw // tn),
        in_specs=in_specs,
        out_specs=[pl.BlockSpec((tm, tn), lambda i, j: (i, j)) for _ in out_dtypes],
        out_shape=[jax.ShapeDtypeStruct((m, nw), dt) for dt in out_dtypes],
        compiler_params=_cparams(2), name=name,
    )(a, *ws, *[arr for arr, _ in extras])
    return outs


def _epi_plain(accs, extras):
    return accs


def _epi_gated_residual(accs, extras):
    x, g = extras
    return [x + g * accs[0]]


def matmul_gated_residual(lay, a, w, x, gate, *, tm, tn, name):
    extras = ((x, pl.BlockSpec((tm, tn), lambda i, j: (i, j))),
              (gate, pl.BlockSpec((None, 1, tn), lambda i, j: (lay.cond_of_tile(i, tm), 0, j))))
    return matmul(a, [w], _epi_gated_residual, [F32], tm=tm, tn=tn, extras=extras, name=name)[0]


def _epi_swiglu(accs, extras):
    return [_silu(accs[0]) * accs[1]]


def _rwkv_prep_kernel(x_ref, xp_ref, xn_ref, nw_ref, sc_ref, sh_ref, mu_ref, *o_refs, lay):
    i = pl.program_id(0)
    _, pos, cnt = lay.seq_info(i)
    nw, sc, sh = nw_ref[...], sc_ref[...], sh_ref[...]
    h = _norm_mod(x_ref[...], nw, sc, sh)
    hp = _norm_mod(xp_ref[...], nw, sc, sh)[SUBLANES_V7X - 1:SUBLANES_V7X]
    hn = _norm_mod(xn_ref[...], nw, sc, sh)[0:1]
    hp = jnp.where(pos == 0, 0.0, hp)
    hn = jnp.where(pos == cnt - 1, 0.0, hn)
    tb = h.shape[0]
    row = lax.broadcasted_iota(jnp.int32, h.shape, 0)
    prev = jnp.where(row == 0, hp, pltpu.roll(h, 1, axis=0))
    nxt = jnp.where(row == tb - 1, hn, pltpu.roll(h, tb - 1, axis=0))
    xx = 0.5 * (prev + nxt) - h
    for idx, o_ref in enumerate(o_refs):
        o_ref[...] = (h + xx * mu_ref[idx:idx + 1, :]).astype(o_ref.dtype)


def rwkv_prep(lay, x, nw, sc, sh, mu):
    n, d = x.shape
    tb = lay.tb
    r8 = tb // SUBLANES_V7X
    last8 = n // SUBLANES_V7X - 1
    cmap = lambda i: (lay.cond_of_tile(i, tb), 0, 0)
    return pl.pallas_call(
        functools.partial(_rwkv_prep_kernel, lay=lay),
        grid=(n // tb,),
        in_specs=[pl.BlockSpec((tb, d), lambda i: (i, 0)),
                  pl.BlockSpec((SUBLANES_V7X, d), lambda i: (jnp.maximum(i * r8 - 1, 0), 0)),
                  pl.BlockSpec((SUBLANES_V7X, d), lambda i: (jnp.minimum((i + 1) * r8, last8), 0)),
                  pl.BlockSpec((1, d), lambda i: (0, 0)),
                  pl.BlockSpec((None, 1, d), cmap),
                  pl.BlockSpec((None, 1, d), cmap),
                  pl.BlockSpec((6, d), lambda i: (0, 0))],
        out_specs=[pl.BlockSpec((tb, d), lambda i: (i, 0))] * 6,
        out_shape=[jax.ShapeDtypeStruct((n, d), BF16)] * 6,
        compiler_params=_cparams(1), name="rwkv_prep",
    )(x, x, x, nw.reshape(1, d), sc, sh, mu)


RW_LOG_DECAY_SCALE = -math.exp(-0.5)


def _rwkv_lora_kernel(xw_ref, xa_ref, xg_ref, w1_ref, w2_ref, w0_ref, a1_ref, a2_ref, a0_ref,
                      g1_ref, g2_ref, lw_ref, a_ref, g_ref):
    xw, xa, xg = xw_ref[...], xa_ref[...], xg_ref[...]
    for d in range(2):
        t = jnp.tanh(jnp.dot(xw, w1_ref[d], preferred_element_type=F32))
        wl = w0_ref[d] + _bdot(t, w2_ref[d])
        lw_ref[d] = RW_LOG_DECAY_SCALE * jax.nn.sigmoid(wl)
        t = jnp.dot(xa, a1_ref[d], preferred_element_type=F32)
        a_ref[d] = jax.nn.sigmoid(a0_ref[d] + _bdot(t, a2_ref[d]))
    t = jax.nn.sigmoid(jnp.dot(xg, g1_ref[...], preferred_element_type=F32))
    g_ref[...] = _bdot(t, g2_ref[...])


def rwkv_lora(lay, xw, xa, xg, w1, w2, w0, a1, a2, a0, g1, g2):
    n, d = xw.shape
    tm = lay.tile(128)
    full = lambda arr: pl.BlockSpec(arr.shape, lambda i: (0,) * arr.ndim)
    row = pl.BlockSpec((tm, d), lambda i: (i, 0))
    return pl.pallas_call(
        _rwkv_lora_kernel,
        grid=(n // tm,),
        in_specs=[row, row, row] + [full(t) for t in (w1, w2, w0, a1, a2, a0, g1, g2)],
        out_specs=[pl.BlockSpec((2, tm, d), lambda i: (0, i, 0)),
                   pl.BlockSpec((2, tm, d), lambda i: (0, i, 0)),
                   row],
        out_shape=[jax.ShapeDtypeStruct((2, n, d), F32), jax.ShapeDtypeStruct((2, n, d), F32),
                   jax.ShapeDtypeStruct((n, d), F32)],
        compiler_params=_cparams(1), name="rwkv_lora",
    )(xw, xa, xg, w1, w2, w0, a1, a2, a0, g1, g2)


def _wkv_constants():
    c, g, hd = RW_CHUNK, RW_GROUP, RW_HEAD
    gc, lanes = g * c, g * hd
    t = np.arange(c)
    cum = np.stack([(t[None, :] <= t[:, None]), (t[None, :] >= t[:, None])])
    tr = np.arange(c)[:, None]
    tc = np.arange(gc)[None, :] % c
    strict = np.stack([tc < tr, tc > tr])
    incl = np.stack([tc <= tr, tc >= tr])
    head_rows = np.arange(gc)[:, None] // c == np.arange(lanes)[None, :] // hd
    blk_rows = np.arange(gc)[:, None] // c == np.arange(gc)[None, :] // c
    bd = np.arange(lanes)[:, None] // hd == np.arange(lanes)[None, :] // hd
    eye_w = tr == tc
    return dict(cum=jnp.asarray(cum, BF16), strict=jnp.asarray(strict, F32), incl=jnp.asarray(incl, F32),
                head_rows=jnp.asarray(head_rows, BF16), blk_rows=jnp.asarray(blk_rows, BF16),
                bd=jnp.asarray(bd, F32), bd_b=jnp.asarray(bd, BF16), eye_w=jnp.asarray(eye_w, F32))


def _split_dot(m01, x, passes):
    acc, rem = None, x
    for _ in range(passes):
        part = rem.astype(BF16)
        term = jnp.dot(m01, part, preferred_element_type=F32)
        acc = term if acc is None else acc + term
        rem = rem - part.astype(F32)
    return acc


def _split_dot_r(x, m01, passes):
    acc, rem = None, x
    for _ in range(passes):
        part = rem.astype(BF16)
        term = jnp.dot(part, m01, preferred_element_type=F32)
        acc = term if acc is None else acc + term
        rem = rem - part.astype(F32)
    return acc


def _tile_rows(x, mask_b):
    return jnp.concatenate([x.astype(BF16)] * RW_GROUP, axis=0) * mask_b


def _wkv_tinv_many(items, head_rows, blk_rows, eye_w):
    c = RW_CHUNK
    n_ws = []
    for kk, a, lw, cum_m, strict_w in items:
        cum = _split_dot(cum_m, lw, 3)
        kkq = kk * jnp.exp(cum - lw)
        kkah = kk * a * jnp.exp(-cum)
        n_ws.append(-jnp.where(strict_w > 0, _bdot_nt(kkq, _tile_rows(kkah, head_rows)), 0.0))
    t_ws = [eye_w + n_w for n_w in n_ws]
    n_pows = [_bdot(n_w, _tile_rows(n_w, blk_rows)) for n_w in n_ws]
    levels = int(math.log2(c))
    for lv in range(1, levels):
        ws = [_tile_rows(n_pow, blk_rows) for n_pow in n_pows]
        if lv < levels - 1:
            boths = [_bdot(jnp.concatenate([t_w, n_pow], axis=0), w) for t_w, n_pow, w in zip(t_ws, n_pows, ws)]
            t_ws = [t_w + both[:c] for t_w, both in zip(t_ws, boths)]
            n_pows = [both[c:] for both in boths]
        else:
            t_ws = [t_w + _bdot(t_w, w) for t_w, w in zip(t_ws, ws)]
    return t_ws


def _wkv_prep1(raw):
    return [_split_dot(u[6], u[5], 3) for u in raw]


def _wkv_prep2(raw, cums, head_rows):
    c = RW_CHUNK
    mids = []
    for (r, v, kk, kd, a, lw, _, _, _), cum in zip(raw, cums):
        tot = jnp.sum(lw, axis=0, keepdims=True)
        kka = kk * a
        e_inv, e_rest = jnp.exp(-cum), jnp.exp(tot - cum)
        q2 = jnp.concatenate([kk * jnp.exp(cum - lw), r * jnp.exp(cum)], axis=0).astype(BF16)
        mids.append(dict(q2=q2, kdh=_tile_rows(kd * e_inv, head_rows), kkah=_tile_rows(kka * e_inv, head_rows),
                         vbd=_tile_rows(v, head_rows), v=v, decay=jnp.exp(tot),
                         kw=jnp.concatenate([kd * e_rest, -(kka * e_rest)], axis=0).astype(BF16)))
    s1s = [_bdot_nt(m['q2'], m['kdh']) for m in mids]
    s2s = [_bdot_nt(m['q2'][c:], m['kkah']) for m in mids]
    for m, u, s1, s2 in zip(mids, raw, s1s, s2s):
        strict_w, incl_w = u[7], u[8]
        m['lad'] = jnp.concatenate([jnp.where(strict_w > 0, s1[:c], 0.0),
                                    jnp.where(incl_w > 0, s1[c:], 0.0)], axis=0).astype(BF16)
        m['a_a'] = jnp.where(incl_w > 0, s2, 0.0).astype(BF16)
    return mids


def _wkv_prep3(mids):
    for m, lav in zip(mids, [_bdot(m['lad'], m['vbd']) for m in mids]):
        m['lav'] = lav
    return mids


def _wkv_adv1(states, preps):
    return [_bdot_nt(p['q2'], s) for p, s in zip(preps, states)]


def _wkv_adv2(p0s, preps, t_ws, head_rows):
    c = RW_CHUNK
    return [_bdot(t_w, _tile_rows(p0[:c] + p['lav'][:c], head_rows))
            for p0, p, t_w in zip(p0s, preps, t_ws)]


def _wkv_adv3(states, p0s, us, preps, head_rows, bd):
    c = RW_CHUNK
    upds = [_bdot(jnp.concatenate([p['v'].astype(F32), u], axis=0).T, p['kw']) for p, u in zip(preps, us)]
    aus = [_bdot(p['a_a'], _tile_rows(u, head_rows)) for p, u in zip(preps, us)]
    new_states = [s * p['decay'] + jnp.where(bd > 0, upd, 0.0) for s, p, upd in zip(states, preps, upds)]
    ys = [p0[c:] + p['lav'][c:] - au for p0, p, au in zip(p0s, preps, aus)]
    return new_states, ys


RKV_TN = RW_LANES


def _rwkv_rkv_kernel(xr_ref, xk_ref, xv_ref, wr_ref, wk_ref, wv_ref, a_ref, kkw_ref, kaw_ref, rk_ref, bd_ref,
                     r_ref, v_ref, kk_ref, kd_ref, b_ref):
    r = jnp.dot(xr_ref[...], wr_ref[...], preferred_element_type=F32)
    k = jnp.dot(xk_ref[...], wk_ref[...], preferred_element_type=F32)
    v = jnp.dot(xv_ref[...], wv_ref[...], preferred_element_type=F32)
    r_ref[...] = r
    v_ref[...] = v.astype(v_ref.dtype)
    bd = bd_ref[...]
    kk = k * kkw_ref[...]
    mix = None
    for d in range(2):
        kd = k * (1.0 + (a_ref[d] - 1.0) * kaw_ref[...])
        kd_ref[d] = kd
        term = kd * rk_ref[d]
        mix = term if mix is None else mix + term
    rm = r * mix
    for h in range(RKV_TN // RW_LANES):
        cs = slice(h * RW_LANES, (h + 1) * RW_LANES)
        kkh = kk[:, cs]
        kk_ref[:, cs] = kkh * lax.rsqrt(_split_dot_r(kkh * kkh, bd, 2) + 1e-12)
        b_ref[:, cs] = _split_dot_r(rm[:, cs], bd, 2) * v[:, cs]


def rwkv_rkv(lay, xr, xk, xv, wr, wk, wv, a, kkw, kaw, rk):
    n, kdim = xr.shape
    dm = wr.shape[1]
    tm, tn = lay.tile(1024), RKV_TN
    bd_b = _wkv_constants()['bd_b']
    lhs = pl.BlockSpec((tm, kdim), lambda i, j: (i, 0))
    rhs = pl.BlockSpec((kdim, tn), lambda i, j: (0, j))
    tok = pl.BlockSpec((tm, tn), lambda i, j: (i, j))
    two = pl.BlockSpec((2, tm, tn), lambda i, j: (0, i, j))
    par = pl.BlockSpec((1, tn), lambda i, j: (0, j))
    return pl.pallas_call(
        _rwkv_rkv_kernel,
        grid=(n // tm, dm // tn),
        in_specs=[lhs, lhs, lhs, rhs, rhs, rhs, two, par, par,
                  pl.BlockSpec((2, 1, tn), lambda i, j: (0, 0, j)),
                  pl.BlockSpec(bd_b.shape, lambda i, j: (0, 0))],
        out_specs=[tok, tok, tok, two, tok],
        out_shape=[jax.ShapeDtypeStruct((n, dm), F32), jax.ShapeDtypeStruct((n, dm), BF16),
                   jax.ShapeDtypeStruct((n, dm), F32), jax.ShapeDtypeStruct((2, n, dm), F32),
                   jax.ShapeDtypeStruct((n, dm), F32)],
        compiler_params=_cparams(2), name="rwkv_rkv",
    )(xr, xk, xv, wr, wk, wv, a, kkw, kaw, rk, bd_b)


def _wkv_inv_kernel(kk_ref, a_ref, lw_ref, cum_ref, strict_ref, hr_ref, br_ref, eye_ref, t_ref, *, lay):
    c = RW_CHUNK
    hr, br, eye_w = hr_ref[...], br_ref[...], eye_ref[...]
    where = [(d, slice(ci * c, (ci + 1) * c), slice(gi * RW_LANES, (gi + 1) * RW_LANES))
             for gi in range(RW_INV_GSTEP) for d in range(2) for ci in range(lay.tb // c)]
    items = [(kk_ref[sl, cs], a_ref[d, sl, cs], lw_ref[d, sl, cs], cum_ref[d], strict_ref[d])
             for d, sl, cs in where]
    for (d, sl, cs), t_w in zip(where, _wkv_tinv_many(items, hr, br, eye_w)):
        t_ref[d, sl, cs] = t_w.astype(t_ref.dtype)


def wkv_inv(lay, kk, a, lw):
    n, dm = kk.shape
    tb, lanes = lay.tb, RW_LANES
    k = _wkv_constants()
    full = lambda arr: pl.BlockSpec(arr.shape, lambda g, j: (0,) * arr.ndim)
    width = RW_INV_GSTEP * lanes
    two = pl.BlockSpec((2, tb, width), lambda g, j: (0, j, g))
    return pl.pallas_call(
        functools.partial(_wkv_inv_kernel, lay=lay),
        grid=(dm // width, lay.nb),
        in_specs=[pl.BlockSpec((tb, width), lambda g, j: (j, g)), two, two,
                  full(k['cum']), full(k['strict']), full(k['head_rows']), full(k['blk_rows']),
                  full(k['eye_w'])],
        out_specs=two,
        out_shape=jax.ShapeDtypeStruct((2, n, dm), BF16),
        compiler_params=_cparams(2), name="wkv_inv",
    )(kk, a, lw, k['cum'], k['strict'], k['head_rows'], k['blk_rows'], k['eye_w'])


def _wkv_kernel(*refs, lay):
    (rf, vf, kkf, kdf, af, lwf, tf, s0f, rb, vb, kkb, kdb, ab, lwb, tb_, s0b,
     cum_ref, strict_ref, incl_ref, hr_ref, bd_ref, yf_ref, yb_ref, sff_ref, sfb_ref, sf_scr, sb_scr) = refs
    c = RW_CHUNK
    n_chunks = lay.tb // c
    j = pl.program_id(1)
    _, pos_f, cnt_f = lay.seq_info(j)
    _, pos_b, cnt_b = lay.seq_info(lay.nb - 1 - j)

    @pl.when(pos_f == 0)
    def _():
        sf_scr[...] = s0f[...]

    @pl.when(pos_b == cnt_b - 1)
    def _():
        sb_scr[...] = s0b[...]

    hr, bd = hr_ref[...], bd_ref[...]
    sl_f = [slice(ci * c, (ci + 1) * c) for ci in range(n_chunks)]
    sl_b = sl_f[::-1]
    cols = [slice(gi * RW_LANES, (gi + 1) * RW_LANES) for gi in range(RW_GSTEP)]
    def raw(ci):
        out = []
        for cs in cols:
            sf, sb = sl_f[ci], sl_b[ci]
            out.append((rf[sf, cs], vf[sf, cs], kkf[sf, cs], kdf[sf, cs], af[sf, cs], lwf[sf, cs],
                        cum_ref[0], strict_ref[0], incl_ref[0]))
            out.append((rb[sb, cs], vb[sb, cs], kkb[sb, cs], kdb[sb, cs], ab[sb, cs], lwb[sb, cs],
                        cum_ref[1], strict_ref[1], incl_ref[1]))
        return out

    states = []
    for gi in range(RW_GSTEP):
        states += [sf_scr[gi], sb_scr[gi]]
    raw_n = raw(0)
    preps = _wkv_prep3(_wkv_prep2(raw_n, _wkv_prep1(raw_n), hr))
    for ci in range(n_chunks):
        more = ci + 1 < n_chunks
        t_ws = []
        for cs in cols:
            t_ws += [tf[sl_f[ci], cs], tb_[sl_b[ci], cs]]
        p0s = _wkv_adv1(states, preps)
        if more:
            raw_n = raw(ci + 1)
            cums_n = _wkv_prep1(raw_n)
        us = _wkv_adv2(p0s, preps, t_ws, hr)
        if more:
            mids_n = _wkv_prep2(raw_n, cums_n, hr)
        states, ys = _wkv_adv3(states, p0s, us, preps, hr, bd)
        if more:
            preps = _wkv_prep3(mids_n)
        for gi, cs in enumerate(cols):
            yf_ref[sl_f[ci], cs] = ys[2 * gi]
            yb_ref[sl_b[ci], cs] = ys[2 * gi + 1]
    for gi in range(RW_GSTEP):
        sf_scr[gi] = states[2 * gi]
        sb_scr[gi] = states[2 * gi + 1]

    @pl.when(pos_f == cnt_f - 1)
    def _():
        sff_ref[...] = sf_scr[...]

    @pl.when(pos_b == 0)
    def _():
        sfb_ref[...] = sb_scr[...]


def wkv(lay, r, v, kk, kd, a, lw, t_inv, s0):
    n, dm = r.shape
    tb, lanes = lay.tb, RW_LANES
    ng = dm // lanes
    width = RW_GSTEP * lanes
    k = _wkv_constants()
    full = lambda arr: pl.BlockSpec(arr.shape, lambda g, j: (0,) * arr.ndim)

    def views(d, blk):
        tok = pl.BlockSpec((tb, width), lambda g, j: (blk(j), g))
        tok2 = pl.BlockSpec((None, tb, width), lambda g, j: (d, blk(j), g))

        def s0_map(g, j):
            seq, _, _ = lay.seq_info(blk(j))
            return (jnp.maximum(seq - lay.n_ctx + 1, 0), d, g, 0, 0)

        def sfin_map(g, j):
            seq, _, _ = lay.seq_info(blk(j))
            return (seq, g, 0, 0)

        ins = [tok, tok, tok, tok2, tok2, tok2, tok2,
               pl.BlockSpec((None, None, RW_GSTEP, lanes, lanes), s0_map)]
        return ins, tok, pl.BlockSpec((None, RW_GSTEP, lanes, lanes), sfin_map)

    in_f, y_f, sf_f = views(0, lambda j: j)
    in_b, y_b, sf_b = views(1, lambda j: lay.nb - 1 - j)
    consts = [k['cum'], k['strict'], k['incl'], k['head_rows'], k['bd']]
    args = [r, v, kk, kd, a, lw, t_inv, s0]
    return pl.pallas_call(
        functools.partial(_wkv_kernel, lay=lay),
        grid=(ng // RW_GSTEP, lay.nb),
        in_specs=in_f + in_b + [full(x) for x in consts],
        out_specs=[y_f, y_b, sf_f, sf_b],
        out_shape=[jax.ShapeDtypeStruct((n, dm), F32), jax.ShapeDtypeStruct((n, dm), F32),
                   jax.ShapeDtypeStruct((lay.n_seq, ng, lanes, lanes), F32),
                   jax.ShapeDtypeStruct((lay.n_seq, ng, lanes, lanes), F32)],
        scratch_shapes=[pltpu.VMEM((RW_GSTEP, lanes, lanes), F32), pltpu.VMEM((RW_GSTEP, lanes, lanes), F32)],
        compiler_params=_cparams(2), name="wkv",
    )(*args, *args, *consts)


def _rwkv_post_kernel(yf_ref, yb_ref, b_ref, g_ref, lnw_ref, lnb_ref, bd_ref, o_ref):
    y = yf_ref[...] + yb_ref[...]
    bd = bd_ref[...]
    inv = 1.0 / RW_HEAD
    mu = _split_dot_r(y, bd, 2) * inv
    yc = y - mu
    var = _split_dot_r(yc * yc, bd, 2) * inv
    yn = yc * lax.rsqrt(var + RW_LN_EPS)
    out = yn * lnw_ref[...] + lnb_ref[...] + b_ref[...]
    o_ref[...] = (out * g_ref[...]).astype(o_ref.dtype)


def rwkv_post(lay, y_f, y_b, bonus, g, lnw, lnb):
    n, dm = y_f.shape
    lanes = RW_LANES
    tm = lay.tile(512)
    bd_b = _wkv_constants()['bd_b']
    tok = pl.BlockSpec((tm, lanes), lambda i, c: (i, c))
    par = pl.BlockSpec((1, lanes), lambda i, c: (0, c))
    return pl.pallas_call(
        _rwkv_post_kernel,
        grid=(n // tm, dm // lanes),
        in_specs=[tok, tok, tok, tok, par, par, pl.BlockSpec(bd_b.shape, lambda i, c: (0, 0))],
        out_specs=tok,
        out_shape=jax.ShapeDtypeStruct((n, dm), BF16),
        compiler_params=_cparams(2), name="rwkv_post",
    )(y_f, y_b, bonus, g, lnw, lnb, bd_b)


def rwkv_layer(lay, x, mods, nw, p, s0):
    n, d = x.shape
    sh1, sc1, g1 = mods[0], mods[1], mods[2]
    xr, xw, xk, xv, xa, xg = rwkv_prep(lay, x, nw, sc1, sh1, p['mu'])
    tm = lay.tile(1024)
    lw, a, g = rwkv_lora(lay, xw, xa, xg, p['w1'], p['w2'], p['w0'], p['a1'], p['a2'], p['a0'],
                         p['g1'], p['g2'])
    r, v, kk, kd, bonus = rwkv_rkv(lay, xr, xk, xv, p['wr'], p['wk'], p['wv'], a, p['kk'], p['ka'], p['rk'])
    t_inv = wkv_inv(lay, kk, a, lw)
    y_f, y_b, sfin_f, sfin_b = wkv(lay, r, v, kk, kd, a, lw, t_inv, s0)
    z = rwkv_post(lay, y_f, y_b, bonus, g, p['lnx_w'], p['lnx_b'])
    x = matmul_gated_residual(lay, z, p['wo'], x, g1, tm=tm, tn=512, name="rwkv_o")
    return x, jnp.stack([sfin_f, sfin_b], axis=1)


def _hgrn_constants():
    c = HG_CHUNK_TOKENS
    t = np.arange(c)[:, None]
    j = np.arange(c)[None, :]
    cums, masks_all = [], []
    for rev in (False, True):
        masks = []
        h = 1
        while h < c:
            upper = (t % (2 * h)) >= h
            same = (t // (2 * h)) == (j // (2 * h))
            if not rev:
                mask = same & upper & ((j % (2 * h)) < h)
            else:
                mask = same & (~upper) & ((j % (2 * h)) >= h)
            masks.append(mask)
            h *= 2
        masks.append(t == j)
        cums.append((j >= t) if rev else (j <= t))
        masks_all.append(np.stack(masks, 0))
    return jnp.asarray(np.stack(cums), BF16), jnp.asarray(np.stack(masks_all).astype(np.float32))


def _hgrn_level_exponents(g, gcum, rev):
    c, kdim = g.shape
    row = lax.broadcasted_iota(jnp.int32, g.shape, 0)
    nxt = pltpu.roll(g, c - 1, axis=0)
    prv = pltpu.roll(g, 1, axis=0)
    r2, r4 = row & 1, row & 3
    if not rev:
        x1 = jnp.where(r2 == 1, g, 0.0)
        x2 = jnp.where(r4 == 0, nxt, jnp.where(r4 == 2, g, jnp.where(r4 == 3, prv + g, 0.0)))
    else:
        x1 = jnp.where(r2 == 0, g, 0.0)
        x2 = jnp.where(r4 == 0, g + nxt, jnp.where(r4 == 1, g, jnp.where(r4 == 3, prv, 0.0)))
    xs = [x1, x2]
    h = 4
    while h < c:
        gr = gcum.reshape(c // (2 * h), 2 * h, kdim)
        ref = gr[:, h:h + 1, :] if rev else gr[:, h - 1:h, :]
        upper = lax.broadcasted_iota(jnp.int32, gr.shape, 1) >= h
        diff = gr - ref
        x = jnp.where(upper, -diff, diff) if rev else jnp.where(upper, diff, -diff)
        xs.append(x.reshape(c, kdim))
        h *= 2
    return xs


def _hgrn_units(units, masks_by_dir, cum_by_dir):
    c = HG_CHUNK_TOKENS
    gs = [jnp.log(f) for _, f, _, _ in units]
    gcums = [_split_dot(cum_by_dir[rev], g, 3) for g, (_, _, _, rev) in zip(gs, units)]
    outs = []
    pend = []
    for (q, f, iv, rev), g, gcum in zip(units, gs, gcums):
        k = 1.0 - f
        tot = gcum[0:1] if rev else gcum[c - 1:c]
        es = [jnp.exp(x) for x in _hgrn_level_exponents(g, gcum, rev)]
        pend.append((q, k, iv, rev, es, jnp.exp(gcum), jnp.exp(tot - gcum), jnp.exp(tot)))
    for q, k, iv, rev, es, eg, erest, etot in pend:
        masks = masks_by_dir[rev]
        a = jnp.where(masks[len(es)] > 0, _bdot_nt(q, k), 0.0)
        for lv, el in enumerate(es):
            a = a + jnp.where(masks[lv] > 0, _bdot_nt(q * el, k * el), 0.0)
        outs.append(dict(a=a, iv=iv, qe=(q * eg).astype(BF16), kdec=(k * erest).astype(BF16), decay=etot))
    for u in outs:
        u['av'] = _bdot(u['a'], u['iv'])
    for u in outs:
        u['upd'] = _bdot(u['iv'].T, u['kdec'])
    return outs


def _hgrn_kernel(*refs, lay):
    (qf, ff, if_, s0f, qb, fb, ib, s0b, cum_ref, mask_ref, of_ref, ob_ref, sff_ref, sfb_ref,
     sf_scr, sb_scr) = refs
    c = HG_CHUNK_TOKENS
    n_chunks = lay.tb // c
    j = pl.program_id(1)
    _, pos_f, cnt_f = lay.seq_info(j)
    _, pos_b, cnt_b = lay.seq_info(lay.nb - 1 - j)

    @pl.when(pos_f == 0)
    def _():
        sf_scr[...] = s0f[...]

    @pl.when(pos_b == cnt_b - 1)
    def _():
        sb_scr[...] = s0b[...]

    masks_by_dir = [mask_ref[0], mask_ref[1]]
    cum_by_dir = [cum_ref[0], cum_ref[1]]
    sl_f = [slice(ci * c, (ci + 1) * c) for ci in range(n_chunks)]
    sl_b = sl_f[::-1]
    cols = [slice(hi * HG_K, (hi + 1) * HG_K) for hi in range(HG_HSTEP)]
    chains = []
    for hi, cs in enumerate(cols):
        chains.append((sf_scr, hi, of_ref, [(sl, cs) for sl in sl_f],
                       [(qf[sl, cs], ff[sl, cs], if_[sl, cs], False) for sl in sl_f]))
        chains.append((sb_scr, hi, ob_ref, [(sl, cs) for sl in sl_b],
                       [(qb[sl, cs], fb[sl, cs], ib[sl, cs], True) for sl in sl_b]))
    done = _hgrn_units([u for ch in chains for u in ch[4]], masks_by_dir, cum_by_dir)
    pend = []
    for n_ch, (scr, hi, o_ref, where, _) in enumerate(chains):
        s = scr[hi]
        for ci in range(n_chunks):
            u = done[n_ch * n_chunks + ci]
            pend.append((o_ref, where[ci], u, s))
            s = s * u['decay'] + u['upd']
        scr[hi] = s
    for o_ref, (sl, cs), u, s_prev in pend:
        o_ref[sl, cs] = u['av'] + _bdot_nt(u['qe'], s_prev)

    @pl.when(pos_f == cnt_f - 1)
    def _():
        sff_ref[...] = sf_scr[...]

    @pl.when(pos_b == 0)
    def _():
        sfb_ref[...] = sb_scr[...]


def hgrn_scan(lay, q, f_fwd, f_bwd, iv, s0t):
    n, dm = q.shape
    tb = lay.tb
    nh = dm // HG_K
    width = HG_HSTEP * HG_K
    cum_m, masks = _hgrn_constants()
    full = lambda arr: pl.BlockSpec(arr.shape, lambda h, j: (0,) * arr.ndim)

    def views(d, blk):
        tok = pl.BlockSpec((tb, width), lambda h, j: (blk(j), h))

        def s0_map(h, j):
            seq, _, _ = lay.seq_info(blk(j))
            return (jnp.maximum(seq - lay.n_ctx + 1, 0), d, h, 0, 0)

        def sfin_map(h, j):
            seq, _, _ = lay.seq_info(blk(j))
            return (seq, h, 0, 0)

        ins = [tok, tok, tok, pl.BlockSpec((None, None, HG_HSTEP, HG_K, HG_K), s0_map)]
        return ins, tok, pl.BlockSpec((None, HG_HSTEP, HG_K, HG_K), sfin_map)

    in_f, o_f, sf_f = views(0, lambda j: j)
    in_b, o_b, sf_b = views(1, lambda j: lay.nb - 1 - j)
    return pl.pallas_call(
        functools.partial(_hgrn_kernel, lay=lay),
        grid=(nh // HG_HSTEP, lay.nb),
        in_specs=in_f + in_b + [full(cum_m), full(masks)],
        out_specs=[o_f, o_b, sf_f, sf_b],
        out_shape=[jax.ShapeDtypeStruct((n, dm), F32), jax.ShapeDtypeStruct((n, dm), F32),
                   jax.ShapeDtypeStruct((lay.n_seq, nh, HG_K, HG_K), F32),
                   jax.ShapeDtypeStruct((lay.n_seq, nh, HG_K, HG_K), F32)],
        scratch_shapes=[pltpu.VMEM((HG_HSTEP, HG_K, HG_K), F32), pltpu.VMEM((HG_HSTEP, HG_K, HG_K), F32)],
        compiler_params=_cparams(2), name="hgrn_scan",
    )(q, f_fwd, iv, s0t, q, f_bwd, iv, s0t, cum_m, masks)


def _hgrn_post_kernel(of_ref, ob_ref, g_ref, nw_ref, z_ref):
    o = of_ref[...] + ob_ref[...]
    o = o * lax.rsqrt(jnp.mean(o * o, axis=-1, keepdims=True) + NORM_EPS) * nw_ref[...] * g_ref[...]
    z_ref[...] = o.astype(z_ref.dtype)


def hgrn_post(lay, o_f, o_b, gs, nw):
    n, dm = o_f.shape
    tm = lay.tile(1024)
    tok = pl.BlockSpec((tm, HG_K), lambda i, h: (i, h))
    return pl.pallas_call(
        _hgrn_post_kernel,
        grid=(n // tm, dm // HG_K),
        in_specs=[tok, tok, tok, pl.BlockSpec((1, HG_K), lambda i, h: (0, 0))],
        out_specs=tok,
        out_shape=jax.ShapeDtypeStruct((n, dm), BF16),
        compiler_params=_cparams(2), name="hgrn_post",
    )(o_f, o_b, gs, nw)


def _epi_hgrn_in(accs, extras):
    lb = extras[0]
    q = _silu(accs[0])
    f0 = lb + (1.0 - lb) * jax.nn.sigmoid(accs[1])
    f1 = lb + (1.0 - lb) * jax.nn.sigmoid(accs[2])
    return [q, f0, f1, accs[3], _silu(accs[4])]


def hgrn_layer(lay, x, mods, nw, p, lb, s0t):
    n, d = x.shape
    sh1, sc1, g1 = mods[0], mods[1], mods[2]
    h = norm_mod(lay, x, nw, sc1, sh1, BF16)
    tm, tn = lay.tile(1024), 256
    extras = ((lb, pl.BlockSpec((1, tn), lambda i, j: (0, j))),)
    q, f0, f1, iv, gs = matmul(h, p['w_in'], _epi_hgrn_in, [F32] * 5, tm=tm, tn=tn, extras=extras,
                               name="hgrn_in")
    o_f, o_b, sfin_f, sfin_b = hgrn_scan(lay, q, f0, f1, iv, s0t)
    z = hgrn_post(lay, o_f, o_b, gs, p['norm_w'])
    x = matmul_gated_residual(lay, z, p['wo'], x, g1, tm=lay.tile(1024), tn=512, name="hgrn_o")
    return x, jnp.stack([sfin_f, sfin_b], axis=1)


ML_DOWN_COLS = 1280
ML_KR_OFF = ML_Q_LORA + ML_KV_LORA
ML_KRS_OFF = ML_KR_OFF + LANES_V7X


def _rms(x, w):
    return x * lax.rsqrt(jnp.mean(x * x, axis=-1, keepdims=True) + NORM_EPS) * w


def _mla_mid_kernel(dn_ref, qw_ref, kvw_ref, cos_ref, sin_ref, qn_ref, ckv_ref, kr_ref):
    dn = dn_ref[...]
    qn_ref[...] = _rms(dn[:, :ML_Q_LORA], qw_ref[...]).astype(qn_ref.dtype)
    ckv_ref[...] = _rms(dn[:, ML_Q_LORA:ML_KR_OFF], kvw_ref[...])
    kr = dn[:, ML_KR_OFF:ML_KR_OFF + ML_ROPE]
    krs = dn[:, ML_KRS_OFF:ML_KRS_OFF + ML_ROPE]
    kr_ref[...] = kr * cos_ref[...] + krs * sin_ref[...]


def mla_mid(lay, dn, qw, kvw, cos, sin):
    n = dn.shape[0]
    tm = lay.tile(512)
    return pl.pallas_call(
        _mla_mid_kernel,
        grid=(n // tm,),
        in_specs=[pl.BlockSpec((tm, ML_DOWN_COLS), lambda i: (i, 0)),
                  pl.BlockSpec((1, ML_Q_LORA), lambda i: (0, 0)),
                  pl.BlockSpec((1, ML_KV_LORA), lambda i: (0, 0)),
                  pl.BlockSpec((tm, ML_ROPE), lambda i: (i, 0)),
                  pl.BlockSpec((tm, ML_ROPE), lambda i: (i, 0))],
        out_specs=[pl.BlockSpec((tm, ML_Q_LORA), lambda i: (i, 0)),
                   pl.BlockSpec((tm, ML_KV_LORA), lambda i: (i, 0)),
                   pl.BlockSpec((tm, ML_ROPE), lambda i: (i, 0))],
        out_shape=[jax.ShapeDtypeStruct((n, ML_Q_LORA), BF16),
                   jax.ShapeDtypeStruct((n, ML_KV_LORA), F32),
                   jax.ShapeDtypeStruct((n, ML_ROPE), F32)],
        compiler_params=_cparams(1), name="mla_mid",
    )(dn, qw, kvw, cos, sin)


ML_QSCALE = math.log2(math.e) / math.sqrt(ML_NOPE + ML_ROPE)


def _epi_qscale(accs, extras):
    return [accs[0] * ML_QSCALE]


def _epi_rope(accs, extras):
    cos, sin = extras
    return [(accs[0] * cos + accs[1] * sin) * ML_QSCALE]


def _attn_kernel(qn_ref, qr_ref, kn_ref, kr_ref, v_ref, o_ref, kc_scr):
    @pl.when(pl.program_id(2) == 0)
    def _():
        for h in range(2):
            kc_scr[h, :, :ML_NOPE] = kn_ref[:, h * ML_NOPE:(h + 1) * ML_NOPE]
            kc_scr[h, :, ML_NOPE:] = kr_ref[...]

    scores = []
    for h in range(2):
        q = jnp.concatenate([qn_ref[:, h * ML_NOPE:(h + 1) * ML_NOPE],
                             qr_ref[:, h * ML_ROPE:(h + 1) * ML_ROPE]], axis=1)
        scores.append(lax.dot_general(q, kc_scr[h], (((1,), (1,)), ((), ())), preferred_element_type=F32))
    outs = []
    for h, s in enumerate(scores):
        m = jnp.max(s, axis=-1, keepdims=True)
        p = jnp.exp2(s - m)
        l = jnp.sum(p, axis=-1, keepdims=True)
        o = jnp.dot(p.astype(BF16), v_ref[:, h * ML_V:(h + 1) * ML_V], preferred_element_type=F32)
        outs.append(o / l)
    o_ref[...] = jnp.concatenate(outs, axis=1).astype(o_ref.dtype)


def attention(qn, qr, kn, kr, v, *, n_seq, q_len, k_len, row0, tq):
    heads2 = qn.shape[1] // (2 * ML_NOPE)
    qb = q_len // tq
    rb0 = row0 // tq
    return pl.pallas_call(
        _attn_kernel,
        grid=(n_seq, heads2, qb),
        in_specs=[pl.BlockSpec((tq, 2 * ML_NOPE), lambda s, h, i: (rb0 + s * qb + i, h)),
                  pl.BlockSpec((tq, 2 * ML_ROPE), lambda s, h, i: (rb0 + s * qb + i, h)),
                  pl.BlockSpec((k_len, 2 * ML_NOPE), lambda s, h, i: (s, h)),
                  pl.BlockSpec((k_len, ML_ROPE), lambda s, h, i: (s, 0)),
                  pl.BlockSpec((k_len, 2 * ML_V), lambda s, h, i: (s, h))],
        out_specs=pl.BlockSpec((tq, 2 * ML_V), lambda s, h, i: (s * qb + i, h)),
        out_shape=jax.ShapeDtypeStruct((n_seq * q_len, heads2 * 2 * ML_V), BF16),
        scratch_shapes=[pltpu.VMEM((2, k_len, ML_NOPE + ML_ROPE), BF16)],
        compiler_params=_cparams(3), name="mla_attn",
    )(qn, qr, kn, kr, v)


def mla_layer(lay, x, mods, nw, p, cache_ckv, cache_kr, cos, sin, cos2, sin2):
    n, d = x.shape
    sh1, sc1, g1 = mods[0], mods[1], mods[2]
    h = norm_mod(lay, x, nw, sc1, sh1, BF16)
    tm = lay.tile(512)
    dn = matmul(h, [p['w_down']], _epi_plain, [F32], tm=tm, tn=ML_DOWN_COLS, name="mla_down")[0]
    qlat, ckv, kr = mla_mid(lay, dn, p['qnorm_w'], p['kvnorm_w'], cos, sin)
    qn = matmul(qlat, [p['w_uq_nope']], _epi_qscale, [BF16], tm=tm, tn=512, name="mla_qn")[0]
    tn = 2 * ML_ROPE
    extras = ((cos2, pl.BlockSpec((tm, tn), lambda i, j: (i, 0))),
              (sin2, pl.BlockSpec((tm, tn), lambda i, j: (i, 0))))
    qr = matmul(qlat, [p['w_uq_rope'], p['w_uq_rope_sw']], _epi_rope, [BF16], tm=tm, tn=tn,
                extras=extras, name="mla_qr")[0]
    nc, past = lay.nc, cache_ckv.shape[1]
    ckv_b, kr_b = ckv.astype(BF16), kr.astype(BF16)
    kn_c, v_c = matmul(ckv_b[:nc], [p['w_ukn'], p['w_uv']], _epi_plain, [BF16, BF16],
                       tm=lay.tile(512), tn=512, name="mla_kv_ctx")
    o_c = attention(qn, qr, kn_c, kr_b[:nc], v_c, n_seq=lay.n_ctx, q_len=lay.ctx_len,
                    k_len=lay.ctx_len, row0=0, tq=min(256, lay.ctx_len))
    k_len = lay.lat_len + past
    ckv_l = jnp.concatenate([ckv_b[nc:].reshape(lay.n_lat, lay.lat_len, -1), cache_ckv.astype(BF16)],
                            axis=1).reshape(lay.n_lat * k_len, -1)
    kr_l = jnp.concatenate([kr_b[nc:].reshape(lay.n_lat, lay.lat_len, -1), cache_kr.astype(BF16)],
                           axis=1).reshape(lay.n_lat * k_len, -1)
    tk = math.gcd(k_len, 512)
    kn_l, v_l = matmul(ckv_l, [p['w_ukn'], p['w_uv']], _epi_plain, [BF16, BF16], tm=tk, tn=512,
                       name="mla_kv_lat")
    o_l = attention(qn, qr, kn_l, kr_l, v_l, n_seq=lay.n_lat, q_len=lay.lat_len, k_len=k_len,
                    row0=nc, tq=min(256, lay.lat_len))
    o = jnp.concatenate([o_c, o_l], axis=0)
    x = matmul_gated_residual(lay, o, p['wo'], x, g1, tm=lay.tile(1024), tn=512, name="mla_o")
    return x, ckv[:nc], kr[:nc]


def _rope_tables(lay):
    t = lay.lat_len
    rows = t // GRID_W
    rr = jnp.broadcast_to(jnp.arange(rows, dtype=F32)[:, None], (rows, GRID_W)).reshape(-1)
    cc = jnp.broadcast_to(jnp.arange(GRID_W, dtype=F32)[None, :], (rows, GRID_W)).reshape(-1)
    nf = ML_ROPE // 4
    inv = ROPE_BASE ** (-jnp.arange(nf, dtype=F32) / nf)
    ar, ac = rr[:, None] * inv, cc[:, None] * inv
    cos = jnp.concatenate([jnp.cos(ar), jnp.cos(ar), jnp.cos(ac), jnp.cos(ac)], axis=-1)
    sin = jnp.concatenate([-jnp.sin(ar), jnp.sin(ar), -jnp.sin(ac), jnp.sin(ac)], axis=-1)
    cos = jnp.concatenate([jnp.ones((lay.nc, ML_ROPE), F32), jnp.tile(cos, (lay.n_lat, 1))], axis=0)
    sin = jnp.concatenate([jnp.zeros((lay.nc, ML_ROPE), F32), jnp.tile(sin, (lay.n_lat, 1))], axis=0)
    return cos, sin


def _swap_cols(w):
    k, c = w.shape
    w4 = w.reshape(k, c // 32, 2, 16)
    return w4[:, :, ::-1, :].reshape(k, c)


def ffn(lay, x, mods, nw, w_a, w_b, w_out):
    sh2, sc2, g2 = mods[3], mods[4], mods[5]
    h = norm_mod(lay, x, nw, sc2, sh2, BF16)
    act = matmul(h, [w_a, w_b], _epi_swiglu, [BF16], tm=lay.tile(1024), tn=512, name="ffn_in")[0]
    return matmul_gated_residual(lay, act, w_out, x, g2, tm=lay.tile(1024), tn=512, name="ffn_out")


def _block_diag_states(s):
    b, two, h, n, _ = s.shape
    s = s.reshape(b, two, h // RW_GROUP, RW_GROUP, n, n)
    eye = jnp.eye(RW_GROUP, dtype=s.dtype)
    out = jnp.einsum('bdghvk,hi->bdghvik', s, eye)
    return out.reshape(b, two, h // RW_GROUP, RW_GROUP * n, RW_GROUP * n)


def _diag_blocks(s):
    b, two, g, l, _ = s.shape
    n = l // RW_GROUP
    s = s.reshape(b, two, g, RW_GROUP, n, RW_GROUP, n)
    s = jnp.moveaxis(jnp.diagonal(s, axis1=3, axis2=5), -1, 3)
    return s.reshape(b, two, g * RW_GROUP, n, n)


def kernel(x_prompt, x_sample, state_rwkv, state_hgrn, cache_ckv, cache_krope, c, c_ctx, ada_w, ada_b, norm1_w, norm2_w, ffn_w_in, ffn_w_out, final_norm_w, rw_mu, rw_wr, rw_wk, rw_wv, rw_wo, rw_w0, rw_w1, rw_w2, rw_a0, rw_a1, rw_a2, rw_g1, rw_g2, rw_kk, rw_ka, rw_rk, rw_lnx_w, rw_lnx_b, hg_w_in, hg_lb, hg_norm_w, hg_wo, ml_w_down, ml_qnorm_w, ml_kvnorm_w, ml_w_uq, ml_w_ukv, ml_wo):
    n_ctx, ctx_len, d = x_prompt.shape
    n_lat, lat_len, _ = x_sample.shape
    depth = ada_w.shape[0]
    lay = Layout(n_ctx, ctx_len, n_lat, lat_len)
    d_ff = ffn_w_out.shape[1]
    x = jnp.concatenate([x_prompt.reshape(lay.nc, d), x_sample.reshape(n_lat * lat_len, d)], axis=0)

    n_cond = -(-(1 + n_lat) // SUBLANES_V7X) * SUBLANES_V7X
    cond = jnp.zeros((n_cond, d), F32).at[0].set(c_ctx).at[1:1 + n_lat].set(c)
    mod_all = adaln(cond, ada_w, ada_b)
    mod_all = mod_all.reshape(depth, n_cond, 6, 1, d).transpose(0, 2, 1, 3, 4)

    lb_table = jnp.cumsum(jax.nn.softmax(hg_lb.astype(F32), axis=0), axis=0)
    lb_table = lb_table - lb_table[0]
    cos, sin = _rope_tables(lay)
    cos2, sin2 = jnp.tile(cos, (1, 2)), jnp.tile(sin, (1, 2))
    bf = lambda t: t.astype(BF16)

    new_rwkv, new_hgrn, new_ckv, new_krope = [], [], [], []
    for l in range(depth):
        kind, j = l % 3, l // 3
        mods = mod_all[l]
        if kind == 0:
            pad1 = lambda w: jnp.pad(w, ((0, 0), (0, 0), (0, LORA_PAD - w.shape[2])))
            pad2 = lambda w: jnp.pad(w, ((0, 0), (0, LORA_PAD - w.shape[1]), (0, 0)))
            p = {'mu': rw_mu[j], 'wr': bf(rw_wr[j]), 'wk': bf(rw_wk[j]), 'wv': bf(rw_wv[j]),
                 'wo': bf(rw_wo[j]),
                 'w0': rw_w0[j].reshape(2, 1, d), 'w1': bf(pad1(rw_w1[j])), 'w2': bf(pad2(rw_w2[j])),
                 'a0': rw_a0[j].reshape(2, 1, d), 'a1': bf(pad1(rw_a1[j])), 'a2': bf(pad2(rw_a2[j])),
                 'g1': bf(rw_g1[j]), 'g2': bf(rw_g2[j]),
                 'kk': rw_kk[j].reshape(1, d), 'ka': rw_ka[j].reshape(1, d),
                 'rk': rw_rk[j].reshape(2, 1, d),
                 'lnx_w': rw_lnx_w[j].reshape(1, d), 'lnx_b': rw_lnx_b[j].reshape(1, d)}
            s_lat = _block_diag_states(state_rwkv[:, j].astype(F32))
            s0 = jnp.concatenate([jnp.zeros((1,) + s_lat.shape[1:], F32), s_lat], axis=0)
            x, sfin = rwkv_layer(lay, x, mods, norm1_w[l], p, s0)
            new_rwkv.append(_diag_blocks(sfin[:n_ctx]))
        elif kind == 1:
            hk = d
            w_in = hg_w_in[j]
            p = {'w_in': [bf(w_in[:, i * hk:(i + 1) * hk]) for i in range(5)],
                 'norm_w': hg_norm_w[j].reshape(1, HG_K), 'wo': bf(hg_wo[j])}
            s_lat = jnp.swapaxes(state_hgrn[:, j].astype(F32), -1, -2)
            s0t = jnp.concatenate([jnp.zeros((1,) + s_lat.shape[1:], F32), s_lat], axis=0)
            x, sfin = hgrn_layer(lay, x, mods, norm1_w[l], p, lb_table[l].reshape(1, d), s0t)
            new_hgrn.append(jnp.swapaxes(sfin[:n_ctx], -1, -2))
        else:
            wd = ml_w_down[j]
            kr_w = wd[:, ML_KR_OFF:]
            zpad = jnp.zeros((d, LANES_V7X - ML_ROPE), wd.dtype)
            w_down = jnp.concatenate([wd, zpad, _swap_cols(kr_w), zpad], axis=1)
            wq = ml_w_uq[j].reshape(ML_Q_LORA, ML_H, ML_NOPE + ML_ROPE)
            wq_n = wq[:, :, :ML_NOPE].reshape(ML_Q_LORA, ML_H * ML_NOPE)
            wq_r = wq[:, :, ML_NOPE:].reshape(ML_Q_LORA, ML_H * ML_ROPE)
            wkv = ml_w_ukv[j].reshape(ML_KV_LORA, ML_H, ML_NOPE + ML_V)
            p = {'w_down': bf(w_down), 'qnorm_w': ml_qnorm_w[j].reshape(1, -1),
                 'kvnorm_w': ml_kvnorm_w[j].reshape(1, -1),
                 'w_uq_nope': bf(wq_n), 'w_uq_rope': bf(wq_r), 'w_uq_rope_sw': bf(_swap_cols(wq_r)),
                 'w_ukn': bf(wkv[:, :, :ML_NOPE].reshape(ML_KV_LORA, ML_H * ML_NOPE)),
                 'w_uv': bf(wkv[:, :, ML_NOPE:].reshape(ML_KV_LORA, ML_H * ML_V)),
                 'wo': bf(ml_wo[j])}
            x, ckv_c, kr_c = mla_layer(lay, x, mods, norm1_w[l], p, cache_ckv[:, j], cache_krope[:, j],
                                       cos, sin, cos2, sin2)
            new_ckv.append(ckv_c.reshape(n_ctx, ctx_len, ML_KV_LORA))
            new_krope.append(kr_c.reshape(n_ctx, ctx_len, ML_ROPE))
        w_in = ffn_w_in[l]
        x = ffn(lay, x, mods, norm2_w[l], bf(w_in[:, :d_ff]), bf(w_in[:, d_ff:]), bf(ffn_w_out[l]))

    y_prompt = rmsnorm_rows(lay, x, final_norm_w, 0, lay.nc).reshape(n_ctx, ctx_len, d)
    y_sample = rmsnorm_rows(lay, x, final_norm_w, lay.nc, lay.n - lay.nc).reshape(n_lat, lat_len, d)
    return (y_prompt, y_sample, jnp.stack(new_rwkv, axis=1), jnp.stack(new_hgrn, axis=1),
            jnp.stack(new_ckv, axis=1), jnp.stack(new_krope, axis=1))
```

```python
import functools
import math

import numpy as np
import jax
import jax.numpy as jnp
from jax import lax
from jax.experimental import pallas as pl
from jax.experimental.pallas import tpu as pltpu

F32 = jnp.float32
BF16 = jnp.bfloat16

LANES_V7X = 128
SUBLANES_V7X = 8
VMEM_BYTES_V7X = 64 * 1024 * 1024
VMEM_LIMIT = 56 * 1024 * 1024

NORM_EPS = 1e-6
RW_HEAD = 64
RW_LN_EPS = 64e-5
RW_GROUP = 4
RW_LANES = RW_GROUP * RW_HEAD
RW_CHUNK = 64
RW_GSTEP = 2
RW_INV_GSTEP = 2
HG_K = 128
HG_CHUNK_TOKENS = 64
HG_HSTEP = 2
ML_H = 16
ML_NOPE = 128
ML_ROPE = 64
ML_V = 128
ML_Q_LORA = 512
ML_KV_LORA = 512
GRID_W = 64
ROPE_BASE = 10000.0
LORA_PAD = 128


class Layout:
    def __init__(self, n_ctx, ctx_len, n_lat, lat_len):
        self.n_ctx, self.ctx_len, self.n_lat, self.lat_len = n_ctx, ctx_len, n_lat, lat_len
        self.nc = n_ctx * ctx_len
        self.n = self.nc + n_lat * lat_len
        self.tb = min(256, ctx_len)
        assert ctx_len % self.tb == 0 and lat_len % self.tb == 0 and self.tb % RW_CHUNK == 0
        self.nb = self.n // self.tb
        self.nb_ctx = self.nc // self.tb
        self.bps_ctx = ctx_len // self.tb
        self.bps_lat = lat_len // self.tb
        self.n_seq = n_ctx + n_lat

    def tile(self, want):
        t = want
        while self.nc % t or self.lat_len % t:
            t //= 2
        return t

    def cond_of_tile(self, i, tm):
        row = i * tm
        return jnp.where(row < self.nc, 0, 1 + (row - self.nc) // self.lat_len)

    def seq_info(self, blk):
        is_ctx = blk < self.nb_ctx
        lat = blk - self.nb_ctx
        seq = jnp.where(is_ctx, blk // self.bps_ctx, self.n_ctx + lat // self.bps_lat)
        pos = jnp.where(is_ctx, blk % self.bps_ctx, lat % self.bps_lat)
        cnt = jnp.where(is_ctx, self.bps_ctx, self.bps_lat)
        return seq, pos, cnt


def _cparams(n_axes):
    return pltpu.CompilerParams(dimension_semantics=("arbitrary",) * n_axes, vmem_limit_bytes=VMEM_LIMIT)


def _bdot(a, b):
    return jnp.dot(a.astype(BF16), b.astype(BF16), preferred_element_type=F32)


def _bdot_nt(a, b):
    return lax.dot_general(a.astype(BF16), b.astype(BF16), (((1,), (1,)), ((), ())),
                           preferred_element_type=F32)


def _silu(x):
    return x * jax.nn.sigmoid(x)


def _adaln_kernel(c_ref, w_ref, b_ref, o_ref):
    a = _silu(c_ref[...]).astype(BF16)
    o_ref[...] = jnp.dot(a, w_ref[...].astype(BF16), preferred_element_type=F32) + b_ref[...]


def adaln(cond, ada_w, ada_b):
    depth, d, d6 = ada_w.shape
    r = cond.shape[0]
    tn = 1024
    return pl.pallas_call(
        _adaln_kernel,
        grid=(depth, d6 // tn),
        in_specs=[pl.BlockSpec((r, d), lambda l, j: (0, 0)),
                  pl.BlockSpec((None, d, tn), lambda l, j: (l, 0, j)),
                  pl.BlockSpec((None, 1, tn), lambda l, j: (l, 0, j))],
        out_specs=pl.BlockSpec((None, r, tn), lambda l, j: (l, 0, j)),
        out_shape=jax.ShapeDtypeStruct((depth, r, d6), F32),
        compiler_params=_cparams(2), name="adaln",
    )(cond, ada_w, ada_b.reshape(depth, 1, d6))


def _norm_mod(x, nw, sc, sh):
    y = x * lax.rsqrt(jnp.mean(x * x, axis=-1, keepdims=True) + NORM_EPS)
    return (y * nw) * (1.0 + sc) + sh


def _norm_mod_kernel(x_ref, nw_ref, sc_ref, sh_ref, o_ref):
    o_ref[...] = _norm_mod(x_ref[...], nw_ref[...], sc_ref[...], sh_ref[...]).astype(o_ref.dtype)


def norm_mod(lay, x, nw, sc, sh, out_dtype):
    n, d = x.shape
    tm = lay.tile(512)
    cmap = lambda i: (lay.cond_of_tile(i, tm), 0, 0)
    return pl.pallas_call(
        _norm_mod_kernel,
        grid=(n // tm,),
        in_specs=[pl.BlockSpec((tm, d), lambda i: (i, 0)),
                  pl.BlockSpec((1, d), lambda i: (0, 0)),
                  pl.BlockSpec((None, 1, d), cmap),
                  pl.BlockSpec((None, 1, d), cmap)],
        out_specs=pl.BlockSpec((tm, d), lambda i: (i, 0)),
        out_shape=jax.ShapeDtypeStruct((n, d), out_dtype),
        compiler_params=_cparams(1), name="norm_mod",
    )(x, nw.reshape(1, d), sc, sh)


def _rmsnorm_kernel(x_ref, w_ref, o_ref):
    x = x_ref[...]
    o_ref[...] = x * lax.rsqrt(jnp.mean(x * x, axis=-1, keepdims=True) + NORM_EPS) * w_ref[...]


def rmsnorm_rows(lay, x, w, row0, rows):
    d = x.shape[1]
    tm = lay.tile(512)
    b0 = row0 // tm
    return pl.pallas_call(
        _rmsnorm_kernel,
        grid=(rows // tm,),
        in_specs=[pl.BlockSpec((tm, d), lambda i: (b0 + i, 0)), pl.BlockSpec((1, d), lambda i: (0, 0))],
        out_specs=pl.BlockSpec((tm, d), lambda i: (i, 0)),
        out_shape=jax.ShapeDtypeStruct((rows, d), x.dtype),
        compiler_params=_cparams(1), name="final_norm",
    )(x, w.reshape(1, d))


def _mm_kernel(*refs, n_w, n_e, epi):
    a = refs[0][...]
    accs = [jnp.dot(a, refs[1 + i][...], preferred_element_type=F32) for i in range(n_w)]
    extras = [refs[1 + n_w + i][...] for i in range(n_e)]
    outs = epi(accs, extras)
    o_refs = refs[1 + n_w + n_e:]
    for o_ref, val in zip(o_refs, outs):
        o_ref[...] = val.astype(o_ref.dtype)


def matmul(a, ws, epi, out_dtypes, *, tm, tn, extras=(), name):
    m, k = a.shape
    nw = ws[0].shape[1]
    assert m % tm == 0 and nw % tn == 0
    in_specs = [pl.BlockSpec((tm, k), lambda i, j: (i, 0))]
    in_specs += [pl.BlockSpec((k, tn), lambda i, j: (0, j)) for _ in ws]
    in_specs += [spec for _, spec in extras]
    outs = pl.pallas_call(
        functools.partial(_mm_kernel, n_w=len(ws), n_e=len(extras), epi=epi),
        grid=(m // tm, nw // tn),
        in_specs=in_specs,
        out_specs=[pl.BlockSpec((tm, tn), lambda i, j: (i, j)) for _ in out_dtypes],
        out_shape=[jax.ShapeDtypeStruct((m, nw), dt) for dt in out_dtypes],
        compiler_params=_cparams(2), name=name,
    )(a, *ws, *[arr for arr, _ in extras])
    return outs


def _epi_plain(accs, extras):
    return accs


def _epi_gated_residual(accs, extras):
    x, g = extras
    return [x + g * accs[0]]


def matmul_gated_residual(lay, a, w, x, gate, *, tm, tn, name):
    extras = ((x, pl.BlockSpec((tm, tn), lambda i, j: (i, j))),
              (gate, pl.BlockSpec((None, 1, tn), lambda i, j: (lay.cond_of_tile(i, tm), 0, j))))
    return matmul(a, [w], _epi_gated_residual, [F32], tm=tm, tn=tn, extras=extras, name=name)[0]


def _epi_swiglu(accs, extras):
    return [_silu(accs[0]) * accs[1]]


def _rwkv_prep_kernel(x_ref, xp_ref, xn_ref, nw_ref, sc_ref, sh_ref, mu_ref, *o_refs, lay):
    i = pl.program_id(0)
    _, pos, cnt = lay.seq_info(i)
    nw, sc, sh = nw_ref[...], sc_ref[...], sh_ref[...]
    h = _norm_mod(x_ref[...], nw, sc, sh)
    hp = _norm_mod(xp_ref[...], nw, sc, sh)[SUBLANES_V7X - 1:SUBLANES_V7X]
    hn = _norm_mod(xn_ref[...], nw, sc, sh)[0:1]
    hp = jnp.where(pos == 0, 0.0, hp)
    hn = jnp.where(pos == cnt - 1, 0.0, hn)
    tb = h.shape[0]
    row = lax.broadcasted_iota(jnp.int32, h.shape, 0)
    prev = jnp.where(row == 0, hp, pltpu.roll(h, 1, axis=0))
    nxt = jnp.where(row == tb - 1, hn, pltpu.roll(h, tb - 1, axis=0))
    xx = 0.5 * (prev + nxt) - h
    for idx, o_ref in enumerate(o_refs):
        o_ref[...] = (h + xx * mu_ref[idx:idx + 1, :]).astype(o_ref.dtype)


def rwkv_prep(lay, x, nw, sc, sh, mu):
    n, d = x.shape
    tb = lay.tb
    r8 = tb // SUBLANES_V7X
    last8 = n // SUBLANES_V7X - 1
    cmap = lambda i: (lay.cond_of_tile(i, tb), 0, 0)
    return pl.pallas_call(
        functools.partial(_rwkv_prep_kernel, lay=lay),
        grid=(n // tb,),
        in_specs=[pl.BlockSpec((tb, d), lambda i: (i, 0)),
                  pl.BlockSpec((SUBLANES_V7X, d), lambda i: (jnp.maximum(i * r8 - 1, 0), 0)),
                  pl.BlockSpec((SUBLANES_V7X, d), lambda i: (jnp.minimum((i + 1) * r8, last8), 0)),
                  pl.BlockSpec((1, d), lambda i: (0, 0)),
                  pl.BlockSpec((None, 1, d), cmap),
                  pl.BlockSpec((None, 1, d), cmap),
                  pl.BlockSpec((6, d), lambda i: (0, 0))],
        out_specs=[pl.BlockSpec((tb, d), lambda i: (i, 0))] * 6,
        out_shape=[jax.ShapeDtypeStruct((n, d), BF16)] * 6,
        compiler_params=_cparams(1), name="rwkv_prep",
    )(x, x, x, nw.reshape(1, d), sc, sh, mu)


RW_LOG_DECAY_SCALE = -math.exp(-0.5)


def _rwkv_lora_kernel(xw_ref, xa_ref, xg_ref, w1_ref, w2_ref, w0_ref, a1_ref, a2_ref, a0_ref,
                      g1_ref, g2_ref, lw_ref, a_ref, g_ref):
    xw, xa, xg = xw_ref[...], xa_ref[...], xg_ref[...]
    for d in range(2):
        t = jnp.tanh(jnp.dot(xw, w1_ref[d], preferred_element_type=F32))
        wl = w0_ref[d] + _bdot(t, w2_ref[d])
        lw_ref[d] = RW_LOG_DECAY_SCALE * jax.nn.sigmoid(wl)
        t = jnp.dot(xa, a1_ref[d], preferred_element_type=F32)
        a_ref[d] = jax.nn.sigmoid(a0_ref[d] + _bdot(t, a2_ref[d]))
    t = jax.nn.sigmoid(jnp.dot(xg, g1_ref[...], preferred_element_type=F32))
    g_ref[...] = _bdot(t, g2_ref[...])


def rwkv_lora(lay, xw, xa, xg, w1, w2, w0, a1, a2, a0, g1, g2):
    n, d = xw.shape
    tm = lay.tile(128)
    full = lambda arr: pl.BlockSpec(arr.shape, lambda i: (0,) * arr.ndim)
    row = pl.BlockSpec((tm, d), lambda i: (i, 0))
    return pl.pallas_call(
        _rwkv_lora_kernel,
        grid=(n // tm,),
        in_specs=[row, row, row] + [full(t) for t in (w1, w2, w0, a1, a2, a0, g1, g2)],
        out_specs=[pl.BlockSpec((2, tm, d), lambda i: (0, i, 0)),
                   pl.BlockSpec((2, tm, d), lambda i: (0, i, 0)),
                   row],
        out_shape=[jax.ShapeDtypeStruct((2, n, d), F32), jax.ShapeDtypeStruct((2, n, d), F32),
                   jax.ShapeDtypeStruct((n, d), F32)],
        compiler_params=_cparams(1), name="rwkv_lora",
    )(xw, xa, xg, w1, w2, w0, a1, a2, a0, g1, g2)


def _wkv_constants():
    c, g, hd = RW_CHUNK, RW_GROUP, RW_HEAD
    gc, lanes = g * c, g * hd
    t = np.arange(c)
    cum = np.stack([(t[None, :] <= t[:, None]), (t[None, :] >= t[:, None])])
    tr = np.arange(c)[:, None]
    tc = np.arange(gc)[None, :] % c
    strict = np.stack([tc < tr, tc > tr])
    incl = np.stack([tc <= tr, tc >= tr])
    head_rows = np.arange(gc)[:, None] // c == np.arange(lanes)[None, :] // hd
    blk_rows = np.arange(gc)[:, None] // c == np.arange(gc)[None, :] // c
    bd = np.arange(lanes)[:, None] // hd == np.arange(lanes)[None, :] // hd
    eye_w = tr == tc
    return dict(cum=jnp.asarray(cum, BF16), strict=jnp.asarray(strict, F32), incl=jnp.asarray(incl, F32),
                head_rows=jnp.asarray(head_rows, BF16), blk_rows=jnp.asarray(blk_rows, BF16),
                bd=jnp.asarray(bd, F32), bd_b=jnp.asarray(bd, BF16), eye_w=jnp.asarray(eye_w, F32))


def _split_dot(m01, x, passes):
    acc, rem = None, x
    for _ in range(passes):
        part = rem.astype(BF16)
        term = jnp.dot(m01, part, preferred_element_type=F32)
        acc = term if acc is None else acc + term
        rem = rem - part.astype(F32)
    return acc


def _split_dot_r(x, m01, passes):
    acc, rem = None, x
    for _ in range(passes):
        part = rem.astype(BF16)
        term = jnp.dot(part, m01, preferred_element_type=F32)
        acc = term if acc is None else acc + term
        rem = rem - part.astype(F32)
    return acc


def _tile_rows(x, mask_b):
    return jnp.concatenate([x.astype(BF16)] * RW_GROUP, axis=0) * mask_b


def _wkv_tinv_many(items, head_rows, blk_rows, eye_w):
    c = RW_CHUNK
    n_ws = []
    for kk, a, lw, cum_m, strict_w in items:
        cum = _split_dot(cum_m, lw, 3)
        kkq = kk * jnp.exp(cum - lw)
        kkah = kk * a * jnp.exp(-cum)
        n_ws.append(-jnp.where(strict_w > 0, _bdot_nt(kkq, _tile_rows(kkah, head_rows)), 0.0))
    t_ws = [eye_w + n_w for n_w in n_ws]
    n_pows = [_bdot(n_w, _tile_rows(n_w, blk_rows)) for n_w in n_ws]
    levels = int(math.log2(c))
    for lv in range(1, levels):
        ws = [_tile_rows(n_pow, blk_rows) for n_pow in n_pows]
        if lv < levels - 1:
            boths = [_bdot(jnp.concatenate([t_w, n_pow], axis=0), w) for t_w, n_pow, w in zip(t_ws, n_pows, ws)]
            t_ws = [t_w + both[:c] for t_w, both in zip(t_ws, boths)]
            n_pows = [both[c:] for both in boths]
        else:
            t_ws = [t_w + _bdot(t_w, w) for t_w, w in zip(t_ws, ws)]
    return t_ws


def _wkv_prep1(raw):
    return [_split_dot(u[6], u[5], 3) for u in raw]


def _wkv_prep2(raw, cums, head_rows):
    c = RW_CHUNK
    mids = []
    for (r, v, kk, kd, a, lw, _, _, _), cum in zip(raw, cums):
        tot = jnp.sum(lw, axis=0, keepdims=True)
        kka = kk * a
        e_inv, e_rest = jnp.exp(-cum), jnp.exp(tot - cum)
        q2 = jnp.concatenate([kk * jnp.exp(cum - lw), r * jnp.exp(cum)], axis=0).astype(BF16)
        mids.append(dict(q2=q2, kdh=_tile_rows(kd * e_inv, head_rows), kkah=_tile_rows(kka * e_inv, head_rows),
                         vbd=_tile_rows(v, head_rows), v=v, decay=jnp.exp(tot),
                         kw=jnp.concatenate([kd * e_rest, -(kka * e_rest)], axis=0).astype(BF16)))
    s1s = [_bdot_nt(m['q2'], m['kdh']) for m in mids]
    s2s = [_bdot_nt(m['q2'][c:], m['kkah']) for m in mids]
    for m, u, s1, s2 in zip(mids, raw, s1s, s2s):
        strict_w, incl_w = u[7], u[8]
        m['lad'] = jnp.concatenate([jnp.where(strict_w > 0, s1[:c], 0.0),
                                    jnp.where(incl_w > 0, s1[c:], 0.0)], axis=0).astype(BF16)
        m['a_a'] = jnp.where(incl_w > 0, s2, 0.0).astype(BF16)
    return mids


def _wkv_prep3(mids):
    for m, lav in zip(mids, [_bdot(m['lad'], m['vbd']) for m in mids]):
        m['lav'] = lav
    return mids


def _wkv_adv1(states, preps):
    return [_bdot_nt(p['q2'], s) for p, s in zip(preps, states)]


def _wkv_adv2(p0s, preps, t_ws, head_rows):
    c = RW_CHUNK
    return [_bdot(t_w, _tile_rows(p0[:c] + p['lav'][:c], head_rows))
            for p0, p, t_w in zip(p0s, preps, t_ws)]


def _wkv_adv3(states, p0s, us, preps, head_rows, bd):
    c = RW_CHUNK
    upds = [_bdot(jnp.concatenate([p['v'].astype(F32), u], axis=0).T, p['kw']) for p, u in zip(preps, us)]
    aus = [_bdot(p['a_a'], _tile_rows(u, head_rows)) for p, u in zip(preps, us)]
    new_states = [s * p['decay'] + jnp.where(bd > 0, upd, 0.0) for s, p, upd in zip(states, preps, upds)]
    ys = [p0[c:] + p['lav'][c:] - au for p0, p, au in zip(p0s, preps, aus)]
    return new_states, ys


RKV_TN = RW_LANES


def _rwkv_rkv_kernel(xr_ref, xk_ref, xv_ref, wr_ref, wk_ref, wv_ref, a_ref, kkw_ref, kaw_ref, rk_ref, bd_ref,
                     r_ref, v_ref, kk_ref, kd_ref, b_ref):
    r = jnp.dot(xr_ref[...], wr_ref[...], preferred_element_type=F32)
    k = jnp.dot(xk_ref[...], wk_ref[...], preferred_element_type=F32)
    v = jnp.dot(xv_ref[...], wv_ref[...], preferred_element_type=F32)
    r_ref[...] = r
    v_ref[...] = v.astype(v_ref.dtype)
    bd = bd_ref[...]
    kk = k * kkw_ref[...]
    mix = None
    for d in range(2):
        kd = k * (1.0 + (a_ref[d] - 1.0) * kaw_ref[...])
        kd_ref[d] = kd
        term = kd * rk_ref[d]
        mix = term if mix is None else mix + term
    rm = r * mix
    for h in range(RKV_TN // RW_LANES):
        cs = slice(h * RW_LANES, (h + 1) * RW_LANES)
        kkh = kk[:, cs]
        kk_ref[:, cs] = kkh * lax.rsqrt(_split_dot_r(kkh * kkh, bd, 2) + 1e-12)
        b_ref[:, cs] = _split_dot_r(rm[:, cs], bd, 2) * v[:, cs]


def rwkv_rkv(lay, xr, xk, xv, wr, wk, wv, a, kkw, kaw, rk):
    n, kdim = xr.shape
    dm = wr.shape[1]
    tm, tn = lay.tile(1024), RKV_TN
    bd_b = _wkv_constants()['bd_b']
    lhs = pl.BlockSpec((tm, kdim), lambda i, j: (i, 0))
    rhs = pl.BlockSpec((kdim, tn), lambda i, j: (0, j))
    tok = pl.BlockSpec((tm, tn), lambda i, j: (i, j))
    two = pl.BlockSpec((2, tm, tn), lambda i, j: (0, i, j))
    par = pl.BlockSpec((1, tn), lambda i, j: (0, j))
    return pl.pallas_call(
        _rwkv_rkv_kernel,
        grid=(n // tm, dm // tn),
        in_specs=[lhs, lhs, lhs, rhs, rhs, rhs, two, par, par,
                  pl.BlockSpec((2, 1, tn), lambda i, j: (0, 0, j)),
                  pl.BlockSpec(bd_b.shape, lambda i, j: (0, 0))],
        out_specs=[tok, tok, tok, two, tok],
        out_shape=[jax.ShapeDtypeStruct((n, dm), F32), jax.ShapeDtypeStruct((n, dm), BF16),
                   jax.ShapeDtypeStruct((n, dm), F32), jax.ShapeDtypeStruct((2, n, dm), F32),
                   jax.ShapeDtypeStruct((n, dm), F32)],
        compiler_params=_cparams(2), name="rwkv_rkv",
    )(xr, xk, xv, wr, wk, wv, a, kkw, kaw, rk, bd_b)


def _wkv_inv_kernel(kk_ref, a_ref, lw_ref, cum_ref, strict_ref, hr_ref, br_ref, eye_ref, t_ref, *, lay):
    c = RW_CHUNK
    hr, br, eye_w = hr_ref[...], br_ref[...], eye_ref[...]
    where = [(d, slice(ci * c, (ci + 1) * c), slice(gi * RW_LANES, (gi + 1) * RW_LANES))
             for gi in range(RW_INV_GSTEP) for d in range(2) for ci in range(lay.tb // c)]
    items = [(kk_ref[sl, cs], a_ref[d, sl, cs], lw_ref[d, sl, cs], cum_ref[d], strict_ref[d])
             for d, sl, cs in where]
    for (d, sl, cs), t_w in zip(where, _wkv_tinv_many(items, hr, br, eye_w)):
        t_ref[d, sl, cs] = t_w.astype(t_ref.dtype)


def wkv_inv(lay, kk, a, lw):
    n, dm = kk.shape
    tb, lanes = lay.tb, RW_LANES
    k = _wkv_constants()
    full = lambda arr: pl.BlockSpec(arr.shape, lambda g, j: (0,) * arr.ndim)
    width = RW_INV_GSTEP * lanes
    two = pl.BlockSpec((2, tb, width), lambda g, j: (0, j, g))
    return pl.pallas_call(
        functools.partial(_wkv_inv_kernel, lay=lay),
        grid=(dm // width, lay.nb),
        in_specs=[pl.BlockSpec((tb, width), lambda g, j: (j, g)), two, two,
                  full(k['cum']), full(k['strict']), full(k['head_rows']), full(k['blk_rows']),
                  full(k['eye_w'])],
        out_specs=two,
        out_shape=jax.ShapeDtypeStruct((2, n, dm), BF16),
        compiler_params=_cparams(2), name="wkv_inv",
    )(kk, a, lw, k['cum'], k['strict'], k['head_rows'], k['blk_rows'], k['eye_w'])


def _wkv_kernel(*refs, lay):
    (rf, vf, kkf, kdf, af, lwf, tf, s0f, rb, vb, kkb, kdb, ab, lwb, tb_, s0b,
     cum_ref, strict_ref, incl_ref, hr_ref, bd_ref, yf_ref, yb_ref, sff_ref, sfb_ref, sf_scr, sb_scr) = refs
    c = RW_CHUNK
    n_chunks = lay.tb // c
    j = pl.program_id(1)
    _, pos_f, cnt_f = lay.seq_info(j)
    _, pos_b, cnt_b = lay.seq_info(lay.nb - 1 - j)

    @pl.when(pos_f == 0)
    def _():
        sf_scr[...] = s0f[...]

    @pl.when(pos_b == cnt_b - 1)
    def _():
        sb_scr[...] = s0b[...]

    hr, bd = hr_ref[...], bd_ref[...]
    sl_f = [slice(ci * c, (ci + 1) * c) for ci in range(n_chunks)]
    sl_b = sl_f[::-1]
    cols = [slice(gi * RW_LANES, (gi + 1) * RW_LANES) for gi in range(RW_GSTEP)]
    def raw(ci):
        out = []
        for cs in cols:
            sf, sb = sl_f[ci], sl_b[ci]
            out.append((rf[sf, cs], vf[sf, cs], kkf[sf, cs], kdf[sf, cs], af[sf, cs], lwf[sf, cs],
                        cum_ref[0], strict_ref[0], incl_ref[0]))
            out.append((rb[sb, cs], vb[sb, cs], kkb[sb, cs], kdb[sb, cs], ab[sb, cs], lwb[sb, cs],
                        cum_ref[1], strict_ref[1], incl_ref[1]))
        return out

    states = []
    for gi in range(RW_GSTEP):
        states += [sf_scr[gi], sb_scr[gi]]
    raw_n = raw(0)
    preps = _wkv_prep3(_wkv_prep2(raw_n, _wkv_prep1(raw_n), hr))
    for ci in range(n_chunks):
        more = ci + 1 < n_chunks
        t_ws = []
        for cs in cols:
            t_ws += [tf[sl_f[ci], cs], tb_[sl_b[ci], cs]]
        p0s = _wkv_adv1(states, preps)
        if more:
            raw_n = raw(ci + 1)
            cums_n = _wkv_prep1(raw_n)
        us = _wkv_adv2(p0s, preps, t_ws, hr)
        if more:
            mids_n = _wkv_prep2(raw_n, cums_n, hr)
        states, ys = _wkv_adv3(states, p0s, us, preps, hr, bd)
        if more:
            preps = _wkv_prep3(mids_n)
        for gi, cs in enumerate(cols):
            yf_ref[sl_f[ci], cs] = ys[2 * gi]
            yb_ref[sl_b[ci], cs] = ys[2 * gi + 1]
    for gi in range(RW_GSTEP):
        sf_scr[gi] = states[2 * gi]
        sb_scr[gi] = states[2 * gi + 1]

    @pl.when(pos_f == cnt_f - 1)
    def _():
        sff_ref[...] = sf_scr[...]

    @pl.when(pos_b == 0)
    def _():
        sfb_ref[...] = sb_scr[...]


def wkv(lay, r, v, kk, kd, a, lw, t_inv, s0):
    n, dm = r.shape
    tb, lanes = lay.tb, RW_LANES
    ng = dm // lanes
    width = RW_GSTEP * lanes
    k = _wkv_constants()
    full = lambda arr: pl.BlockSpec(arr.shape, lambda g, j: (0,) * arr.ndim)

    def views(d, blk):
        tok = pl.BlockSpec((tb, width), lambda g, j: (blk(j), g))
        tok2 = pl.BlockSpec((None, tb, width), lambda g, j: (d, blk(j), g))

        def s0_map(g, j):
            seq, _, _ = lay.seq_info(blk(j))
            return (jnp.maximum(seq - lay.n_ctx + 1, 0), d, g, 0, 0)

        def sfin_map(g, j):
            seq, _, _ = lay.seq_info(blk(j))
            return (seq, g, 0, 0)

        ins = [tok, tok, tok, tok2, tok2, tok2, tok2,
               pl.BlockSpec((None, None, RW_GSTEP, lanes, lanes), s0_map)]
        return ins, tok, pl.BlockSpec((None, RW_GSTEP, lanes, lanes), sfin_map)

    in_f, y_f, sf_f = views(0, lambda j: j)
    in_b, y_b, sf_b = views(1, lambda j: lay.nb - 1 - j)
    consts = [k['cum'], k['strict'], k['incl'], k['head_rows'], k['bd']]
    args = [r, v, kk, kd, a, lw, t_inv, s0]
    return pl.pallas_call(
        functools.partial(_wkv_kernel, lay=lay),
        grid=(ng // RW_GSTEP, lay.nb),
        in_specs=in_f + in_b + [full(x) for x in consts],
        out_specs=[y_f, y_b, sf_f, sf_b],
        out_shape=[jax.ShapeDtypeStruct((n, dm), F32), jax.ShapeDtypeStruct((n, dm), F32),
                   jax.ShapeDtypeStruct((lay.n_seq, ng, lanes, lanes), F32),
                   jax.ShapeDtypeStruct((lay.n_seq, ng, lanes, lanes), F32)],
        scratch_shapes=[pltpu.VMEM((RW_GSTEP, lanes, lanes), F32), pltpu.VMEM((RW_GSTEP, lanes, lanes), F32)],
        compiler_params=_cparams(2), name="wkv",
    )(*args, *args, *consts)


def _tinv_begin(items, head_rows, blk_rows, eye_w):
    cums = [_split_dot(it[3], it[2], 3) for it in items]
    ops = [(kk * jnp.exp(cum - lw), _tile_rows(kk * a * jnp.exp(-cum), head_rows))
           for (kk, a, lw, _, _), cum in zip(items, cums)]
    n_ws = [-jnp.where(it[4] > 0, _bdot_nt(q, w), 0.0) for it, (q, w) in zip(items, ops)]
    n_pows = [_bdot(n_w, _tile_rows(n_w, blk_rows)) for n_w in n_ws]
    return [eye_w + n_w for n_w in n_ws], n_pows


def _tinv_level(t_ws, n_pows, last, blk_rows):
    c = RW_CHUNK
    ws = [_tile_rows(n_pow, blk_rows) for n_pow in n_pows]
    if last:
        return [t_w + _bdot(t_w, w) for t_w, w in zip(t_ws, ws)], None
    boths = [_bdot(jnp.concatenate([t_w, n_pow], axis=0), w) for t_w, n_pow, w in zip(t_ws, n_pows, ws)]
    return [t_w + both[:c] for t_w, both in zip(t_ws, boths)], [both[c:] for both in boths]


RW_TINV_LEVELS = int(math.log2(RW_CHUNK))


def _wkv_fused_kernel(*refs, lay):
    (rf, vf, kkf, kdf, af, lwf, s0f, kkfn, afn, lwfn,
     rb, vb, kkb, kdb, ab, lwb, s0b, kkbn, abn, lwbn,
     cum_ref, strict_ref, incl_ref, hr_ref, br_ref, eye_ref, bd_ref,
     yf_ref, yb_ref, sff_ref, sfb_ref, sf_scr, sb_scr, t_scr) = refs
    c = RW_CHUNK
    n_chunks = lay.tb // c
    j = pl.program_id(1)
    cur = lax.rem(j, 2)
    nxt = 1 - cur
    _, pos_f, cnt_f = lay.seq_info(j)
    _, pos_b, cnt_b = lay.seq_info(lay.nb - 1 - j)

    @pl.when(pos_f == 0)
    def _():
        sf_scr[...] = s0f[...]

    @pl.when(pos_b == cnt_b - 1)
    def _():
        sb_scr[...] = s0b[...]

    hr, br, eye_w, bd = hr_ref[...], br_ref[...], eye_ref[...], bd_ref[...]
    sl_f = [slice(ci * c, (ci + 1) * c) for ci in range(n_chunks)]
    sl_b = sl_f[::-1]
    cols = [slice(gi * RW_LANES, (gi + 1) * RW_LANES) for gi in range(RW_GSTEP)]

    def inv_items(kk_f, a_f, lw_f, kk_b, a_b, lw_b, sl):
        out = []
        for cs in cols:
            out.append((kk_f[sl, cs], a_f[sl, cs], lw_f[sl, cs], cum_ref[0], strict_ref[0]))
            out.append((kk_b[sl, cs], a_b[sl, cs], lw_b[sl, cs], cum_ref[1], strict_ref[1]))
        return out

    def inv_store(slot, sl, t_ws):
        for gi, cs in enumerate(cols):
            t_scr[slot, 0, sl, cs] = t_ws[2 * gi].astype(t_scr.dtype)
            t_scr[slot, 1, sl, cs] = t_ws[2 * gi + 1].astype(t_scr.dtype)

    @pl.when(j == 0)
    def _():
        def body(ci, carry):
            sl = pl.ds(pl.multiple_of(ci * c, c), c)
            t_ws, n_pows = _tinv_begin(inv_items(kkf, af, lwf, kkb, ab, lwb, sl), hr, br, eye_w)
            for lv in range(1, RW_TINV_LEVELS):
                t_ws, n_pows = _tinv_level(t_ws, n_pows, lv == RW_TINV_LEVELS - 1, br)
            inv_store(cur, sl, t_ws)
            return carry
        lax.fori_loop(0, n_chunks, body, 0)

    def raw(ci):
        out = []
        for cs in cols:
            sf, sb = sl_f[ci], sl_b[ci]
            out.append((rf[sf, cs], vf[sf, cs], kkf[sf, cs], kdf[sf, cs], af[sf, cs], lwf[sf, cs],
                        cum_ref[0], strict_ref[0], incl_ref[0]))
            out.append((rb[sb, cs], vb[sb, cs], kkb[sb, cs], kdb[sb, cs], ab[sb, cs], lwb[sb, cs],
                        cum_ref[1], strict_ref[1], incl_ref[1]))
        return out

    states = []
    for gi in range(RW_GSTEP):
        states += [sf_scr[gi], sb_scr[gi]]
    raw_n = raw(0)
    preps = _wkv_prep3(_wkv_prep2(raw_n, _wkv_prep1(raw_n), hr))
    inv = {'lv': 0, 't': None, 'n': None}

    def inv_step():
        lv = inv['lv']
        if lv == 0:
            items = []
            for sl in sl_f:
                items += inv_items(kkfn, afn, lwfn, kkbn, abn, lwbn, sl)
            inv['t'], inv['n'] = _tinv_begin(items, hr, br, eye_w)
        elif lv < RW_TINV_LEVELS:
            inv['t'], inv['n'] = _tinv_level(inv['t'], inv['n'], lv == RW_TINV_LEVELS - 1, br)
            if lv == RW_TINV_LEVELS - 1:
                per = 2 * RW_GSTEP
                for ci2, sl in enumerate(sl_f):
                    inv_store(nxt, sl, inv['t'][ci2 * per:(ci2 + 1) * per])
        inv['lv'] = lv + 1

    for ci in range(n_chunks):
        more = ci + 1 < n_chunks
        t_ws = []
        for cs in cols:
            t_ws += [t_scr[cur, 0, sl_f[ci], cs], t_scr[cur, 1, sl_b[ci], cs]]
        p0s = _wkv_adv1(states, preps)
        if more:
            raw_n = raw(ci + 1)
            cums_n = _wkv_prep1(raw_n)
        inv_step()
        us = _wkv_adv2(p0s, preps, t_ws, hr)
        if more:
            mids_n = _wkv_prep2(raw_n, cums_n, hr)
        inv_step()
        states, ys = _wkv_adv3(states, p0s, us, preps, hr, bd)
        if more:
            preps = _wkv_prep3(mids_n)
        for gi, cs in enumerate(cols):
            yf_ref[sl_f[ci], cs] = ys[2 * gi]
            yb_ref[sl_b[ci], cs] = ys[2 * gi + 1]
    while inv['lv'] < RW_TINV_LEVELS:
        inv_step()
    for gi in range(RW_GSTEP):
        sf_scr[gi] = states[2 * gi]
        sb_scr[gi] = states[2 * gi + 1]

    @pl.when(pos_f == cnt_f - 1)
    def _():
        sff_ref[...] = sf_scr[...]

    @pl.when(pos_b == 0)
    def _():
        sfb_ref[...] = sb_scr[...]


def wkv_fused(lay, r, v, kk, kd, a, lw, s0):
    n, dm = r.shape
    tb, lanes = lay.tb, RW_LANES
    ng = dm // lanes
    width = RW_GSTEP * lanes
    k = _wkv_constants()
    full = lambda arr: pl.BlockSpec(arr.shape, lambda g, j: (0,) * arr.ndim)

    def views(d, blk, blk_next):
        tok = pl.BlockSpec((tb, width), lambda g, j: (blk(j), g))
        tok2 = pl.BlockSpec((None, tb, width), lambda g, j: (d, blk(j), g))
        tok_n = pl.BlockSpec((tb, width), lambda g, j: (blk_next(j), g))
        tok2_n = pl.BlockSpec((None, tb, width), lambda g, j: (d, blk_next(j), g))

        def s0_map(g, j):
            seq, _, _ = lay.seq_info(blk(j))
            return (jnp.maximum(seq - lay.n_ctx + 1, 0), d, g, 0, 0)

        def sfin_map(g, j):
            seq, _, _ = lay.seq_info(blk(j))
            return (seq, g, 0, 0)

        ins = [tok, tok, tok, tok2, tok2, tok2, pl.BlockSpec((None, None, RW_GSTEP, lanes, lanes), s0_map),
               tok_n, tok2_n, tok2_n]
        return ins, tok, pl.BlockSpec((None, RW_GSTEP, lanes, lanes), sfin_map)

    in_f, y_f, sf_f = views(0, lambda j: j, lambda j: jnp.minimum(j + 1, lay.nb - 1))
    in_b, y_b, sf_b = views(1, lambda j: lay.nb - 1 - j, lambda j: jnp.maximum(lay.nb - 2 - j, 0))
    consts = [k['cum'], k['strict'], k['incl'], k['head_rows'], k['blk_rows'], k['eye_w'], k['bd']]
    args = [r, v, kk, kd, a, lw, s0, kk, a, lw]
    return pl.pallas_call(
        functools.partial(_wkv_fused_kernel, lay=lay),
        grid=(ng // RW_GSTEP, lay.nb),
        in_specs=in_f + in_b + [full(x) for x in consts],
        out_specs=[y_f, y_b, sf_f, sf_b],
        out_shape=[jax.ShapeDtypeStruct((n, dm), F32), jax.ShapeDtypeStruct((n, dm), F32),
                   jax.ShapeDtypeStruct((lay.n_seq, ng, lanes, lanes), F32),
                   jax.ShapeDtypeStruct((lay.n_seq, ng, lanes, lanes), F32)],
        scratch_shapes=[pltpu.VMEM((RW_GSTEP, lanes, lanes), F32), pltpu.VMEM((RW_GSTEP, lanes, lanes), F32),
                        pltpu.VMEM((2, 2, tb, width), BF16)],
        compiler_params=_cparams(2), name="wkv_fused",
    )(*args, *args, *consts)


def _rwkv_post_kernel(yf_ref, yb_ref, b_ref, g_ref, lnw_ref, lnb_ref, bd_ref, o_ref):
    y = yf_ref[...] + yb_ref[...]
    bd = bd_ref[...]
    inv = 1.0 / RW_HEAD
    mu = _split_dot_r(y, bd, 2) * inv
    yc = y - mu
    var = _split_dot_r(yc * yc, bd, 2) * inv
    yn = yc * lax.rsqrt(var + RW_LN_EPS)
    out = yn * lnw_ref[...] + lnb_ref[...] + b_ref[...]
    o_ref[...] = (out * g_ref[...]).astype(o_ref.dtype)


def rwkv_post(lay, y_f, y_b, bonus, g, lnw, lnb):
    n, dm = y_f.shape
    lanes = RW_LANES
    tm = lay.tile(512)
    bd_b = _wkv_constants()['bd_b']
    tok = pl.BlockSpec((tm, lanes), lambda i, c: (i, c))
    par = pl.BlockSpec((1, lanes), lambda i, c: (0, c))
    return pl.pallas_call(
        _rwkv_post_kernel,
        grid=(n // tm, dm // lanes),
        in_specs=[tok, tok, tok, tok, par, par, pl.BlockSpec(bd_b.shape, lambda i, c: (0, 0))],
        out_specs=tok,
        out_shape=jax.ShapeDtypeStruct((n, dm), BF16),
        compiler_params=_cparams(2), name="rwkv_post",
    )(y_f, y_b, bonus, g, lnw, lnb, bd_b)


def rwkv_layer(lay, x, mods, nw, p, s0):
    n, d = x.shape
    sh1, sc1, g1 = mods[0], mods[1], mods[2]
    xr, xw, xk, xv, xa, xg = rwkv_prep(lay, x, nw, sc1, sh1, p['mu'])
    tm = lay.tile(1024)
    lw, a, g = rwkv_lora(lay, xw, xa, xg, p['w1'], p['w2'], p['w0'], p['a1'], p['a2'], p['a0'],
                         p['g1'], p['g2'])
    r, v, kk, kd, bonus = rwkv_rkv(lay, xr, xk, xv, p['wr'], p['wk'], p['wv'], a, p['kk'], p['ka'], p['rk'])
    y_f, y_b, sfin_f, sfin_b = wkv_fused(lay, r, v, kk, kd, a, lw, s0)
    z = rwkv_post(lay, y_f, y_b, bonus, g, p['lnx_w'], p['lnx_b'])
    x = matmul_gated_residual(lay, z, p['wo'], x, g1, tm=tm, tn=512, name="rwkv_o")
    return x, jnp.stack([sfin_f, sfin_b], axis=1)


def _hgrn_constants():
    c = HG_CHUNK_TOKENS
    t = np.arange(c)[:, None]
    j = np.arange(c)[None, :]
    cums, masks_all = [], []
    for rev in (False, True):
        masks = []
        h = 1
        while h < c:
            upper = (t % (2 * h)) >= h
            same = (t // (2 * h)) == (j // (2 * h))
            if not rev:
                mask = same & upper & ((j % (2 * h)) < h)
            else:
                mask = same & (~upper) & ((j % (2 * h)) >= h)
            masks.append(mask)
            h *= 2
        masks.append(t == j)
        cums.append((j >= t) if rev else (j <= t))
        masks_all.append(np.stack(masks, 0))
    return jnp.asarray(np.stack(cums), BF16), jnp.asarray(np.stack(masks_all).astype(np.float32))


def _hgrn_level_exponents(g, gcum, rev):
    c, kdim = g.shape
    row = lax.broadcasted_iota(jnp.int32, g.shape, 0)
    nxt = pltpu.roll(g, c - 1, axis=0)
    prv = pltpu.roll(g, 1, axis=0)
    r2, r4 = row & 1, row & 3
    if not rev:
        x1 = jnp.where(r2 == 1, g, 0.0)
        x2 = jnp.where(r4 == 0, nxt, jnp.where(r4 == 2, g, jnp.where(r4 == 3, prv + g, 0.0)))
    else:
        x1 = jnp.where(r2 == 0, g, 0.0)
        x2 = jnp.where(r4 == 0, g + nxt, jnp.where(r4 == 1, g, jnp.where(r4 == 3, prv, 0.0)))
    xs = [x1, x2]
    h = 4
    while h < c:
        gr = gcum.reshape(c // (2 * h), 2 * h, kdim)
        ref = gr[:, h:h + 1, :] if rev else gr[:, h - 1:h, :]
        upper = lax.broadcasted_iota(jnp.int32, gr.shape, 1) >= h
        diff = gr - ref
        x = jnp.where(upper, -diff, diff) if rev else jnp.where(upper, diff, -diff)
        xs.append(x.reshape(c, kdim))
        h *= 2
    return xs


def _hgrn_units(units, masks_by_dir, cum_by_dir):
    c = HG_CHUNK_TOKENS
    gs = [jnp.log(f) for _, f, _, _ in units]
    gcums = [_split_dot(cum_by_dir[rev], g, 3) for g, (_, _, _, rev) in zip(gs, units)]
    outs = []
    pend = []
    for (q, f, iv, rev), g, gcum in zip(units, gs, gcums):
        k = 1.0 - f
        tot = gcum[0:1] if rev else gcum[c - 1:c]
        es = [jnp.exp(x) for x in _hgrn_level_exponents(g, gcum, rev)]
        pend.append((q, k, iv, rev, es, jnp.exp(gcum), jnp.exp(tot - gcum), jnp.exp(tot)))
    for q, k, iv, rev, es, eg, erest, etot in pend:
        masks = masks_by_dir[rev]
        a = jnp.where(masks[len(es)] > 0, _bdot_nt(q, k), 0.0)
        for lv, el in enumerate(es):
            a = a + jnp.where(masks[lv] > 0, _bdot_nt(q * el, k * el), 0.0)
        outs.append(dict(a=a, iv=iv, qe=(q * eg).astype(BF16), kdec=(k * erest).astype(BF16), decay=etot))
    for u in outs:
        u['av'] = _bdot(u['a'], u['iv'])
    for u in outs:
        u['upd'] = _bdot(u['iv'].T, u['kdec'])
    return outs


def _hgrn_kernel(*refs, lay):
    (qf, ff, if_, s0f, qb, fb, ib, s0b, cum_ref, mask_ref, of_ref, ob_ref, sff_ref, sfb_ref,
     sf_scr, sb_scr) = refs
    c = HG_CHUNK_TOKENS
    n_chunks = lay.tb // c
    j = pl.program_id(1)
    _, pos_f, cnt_f = lay.seq_info(j)
    _, pos_b, cnt_b = lay.seq_info(lay.nb - 1 - j)

    @pl.when(pos_f == 0)
    def _():
        sf_scr[...] = s0f[...]

    @pl.when(pos_b == cnt_b - 1)
    def _():
        sb_scr[...] = s0b[...]

    masks_by_dir = [mask_ref[0], mask_ref[1]]
    cum_by_dir = [cum_ref[0], cum_ref[1]]
    sl_f = [slice(ci * c, (ci + 1) * c) for ci in range(n_chunks)]
    sl_b = sl_f[::-1]
    cols = [slice(hi * HG_K, (hi + 1) * HG_K) for hi in range(HG_HSTEP)]
    chains = []
    for hi, cs in enumerate(cols):
        chains.append((sf_scr, hi, of_ref, [(sl, cs) for sl in sl_f],
                       [(qf[sl, cs], ff[sl, cs], if_[sl, cs], False) for sl in sl_f]))
        chains.append((sb_scr, hi, ob_ref, [(sl, cs) for sl in sl_b],
                       [(qb[sl, cs], fb[sl, cs], ib[sl, cs], True) for sl in sl_b]))
    done = _hgrn_units([u for ch in chains for u in ch[4]], masks_by_dir, cum_by_dir)
    pend = []
    for n_ch, (scr, hi, o_ref, where, _) in enumerate(chains):
        s = scr[hi]
        for ci in range(n_chunks):
            u = done[n_ch * n_chunks + ci]
            pend.append((o_ref, where[ci], u, s))
            s = s * u['decay'] + u['upd']
        scr[hi] = s
    for o_ref, (sl, cs), u, s_prev in pend:
        o_ref[sl, cs] = u['av'] + _bdot_nt(u['qe'], s_prev)

    @pl.when(pos_f == cnt_f - 1)
    def _():
        sff_ref[...] = sf_scr[...]

    @pl.when(pos_b == 0)
    def _():
        sfb_ref[...] = sb_scr[...]


def hgrn_scan(lay, q, f_fwd, f_bwd, iv, s0t):
    n, dm = q.shape
    tb = lay.tb
    nh = dm // HG_K
    width = HG_HSTEP * HG_K
    cum_m, masks = _hgrn_constants()
    full = lambda arr: pl.BlockSpec(arr.shape, lambda h, j: (0,) * arr.ndim)

    def views(d, blk):
        tok = pl.BlockSpec((tb, width), lambda h, j: (blk(j), h))

        def s0_map(h, j):
            seq, _, _ = lay.seq_info(blk(j))
            return (jnp.maximum(seq - lay.n_ctx + 1, 0), d, h, 0, 0)

        def sfin_map(h, j):
            seq, _, _ = lay.seq_info(blk(j))
            return (seq, h, 0, 0)

        ins = [tok, tok, tok, pl.BlockSpec((None, None, HG_HSTEP, HG_K, HG_K), s0_map)]
        return ins, tok, pl.BlockSpec((None, HG_HSTEP, HG_K, HG_K), sfin_map)

    in_f, o_f, sf_f = views(0, lambda j: j)
    in_b, o_b, sf_b = views(1, lambda j: lay.nb - 1 - j)
    return pl.pallas_call(
        functools.partial(_hgrn_kernel, lay=lay),
        grid=(nh // HG_HSTEP, lay.nb),
        in_specs=in_f + in_b + [full(cum_m), full(masks)],
        out_specs=[o_f, o_b, sf_f, sf_b],
        out_shape=[jax.ShapeDtypeStruct((n, dm), F32), jax.ShapeDtypeStruct((n, dm), F32),
                   jax.ShapeDtypeStruct((lay.n_seq, nh, HG_K, HG_K), F32),
                   jax.ShapeDtypeStruct((lay.n_seq, nh, HG_K, HG_K), F32)],
        scratch_shapes=[pltpu.VMEM((HG_HSTEP, HG_K, HG_K), F32), pltpu.VMEM((HG_HSTEP, HG_K, HG_K), F32)],
        compiler_params=_cparams(2), name="hgrn_scan",
    )(q, f_fwd, iv, s0t, q, f_bwd, iv, s0t, cum_m, masks)


def _hgrn_post_kernel(of_ref, ob_ref, g_ref, nw_ref, z_ref):
    o = of_ref[...] + ob_ref[...]
    o = o * lax.rsqrt(jnp.mean(o * o, axis=-1, keepdims=True) + NORM_EPS) * nw_ref[...] * g_ref[...]
    z_ref[...] = o.astype(z_ref.dtype)


def hgrn_post(lay, o_f, o_b, gs, nw):
    n, dm = o_f.shape
    tm = lay.tile(1024)
    tok = pl.BlockSpec((tm, HG_K), lambda i, h: (i, h))
    return pl.pallas_call(
        _hgrn_post_kernel,
        grid=(n // tm, dm // HG_K),
        in_specs=[tok, tok, tok, pl.BlockSpec((1, HG_K), lambda i, h: (0, 0))],
        out_specs=tok,
        out_shape=jax.ShapeDtypeStruct((n, dm), BF16),
        compiler_params=_cparams(2), name="hgrn_post",
    )(o_f, o_b, gs, nw)


def _epi_hgrn_in(accs, extras):
    lb = extras[0]
    q = _silu(accs[0])
    f0 = lb + (1.0 - lb) * jax.nn.sigmoid(accs[1])
    f1 = lb + (1.0 - lb) * jax.nn.sigmoid(accs[2])
    return [q, f0, f1, accs[3], _silu(accs[4])]


def hgrn_layer(lay, x, mods, nw, p, lb, s0t):
    n, d = x.shape
    sh1, sc1, g1 = mods[0], mods[1], mods[2]
    h = norm_mod(lay, x, nw, sc1, sh1, BF16)
    tm, tn = lay.tile(1024), 256
    extras = ((lb, pl.BlockSpec((1, tn), lambda i, j: (0, j))),)
    q, f0, f1, iv, gs = matmul(h, p['w_in'], _epi_hgrn_in, [F32] * 5, tm=tm, tn=tn, extras=extras,
                               name="hgrn_in")
    o_f, o_b, sfin_f, sfin_b = hgrn_scan(lay, q, f0, f1, iv, s0t)
    z = hgrn_post(lay, o_f, o_b, gs, p['norm_w'])
    x = matmul_gated_residual(lay, z, p['wo'], x, g1, tm=lay.tile(1024), tn=512, name="hgrn_o")
    return x, jnp.stack([sfin_f, sfin_b], axis=1)


ML_DOWN_COLS = 1280
ML_KR_OFF = ML_Q_LORA + ML_KV_LORA
ML_KRS_OFF = ML_KR_OFF + LANES_V7X


def _rms(x, w):
    return x * lax.rsqrt(jnp.mean(x * x, axis=-1, keepdims=True) + NORM_EPS) * w


def _mla_mid_kernel(dn_ref, qw_ref, kvw_ref, cos_ref, sin_ref, qn_ref, ckv_ref, kr_ref):
    dn = dn_ref[...]
    qn_ref[...] = _rms(dn[:, :ML_Q_LORA], qw_ref[...]).astype(qn_ref.dtype)
    ckv_ref[...] = _rms(dn[:, ML_Q_LORA:ML_KR_OFF], kvw_ref[...])
    kr = dn[:, ML_KR_OFF:ML_KR_OFF + ML_ROPE]
    krs = dn[:, ML_KRS_OFF:ML_KRS_OFF + ML_ROPE]
    kr_ref[...] = kr * cos_ref[...] + krs * sin_ref[...]


def mla_mid(lay, dn, qw, kvw, cos, sin):
    n = dn.shape[0]
    tm = lay.tile(512)
    return pl.pallas_call(
        _mla_mid_kernel,
        grid=(n // tm,),
        in_specs=[pl.BlockSpec((tm, ML_DOWN_COLS), lambda i: (i, 0)),
                  pl.BlockSpec((1, ML_Q_LORA), lambda i: (0, 0)),
                  pl.BlockSpec((1, ML_KV_LORA), lambda i: (0, 0)),
                  pl.BlockSpec((tm, ML_ROPE), lambda i: (i, 0)),
                  pl.BlockSpec((tm, ML_ROPE), lambda i: (i, 0))],
        out_specs=[pl.BlockSpec((tm, ML_Q_LORA), lambda i: (i, 0)),
                   pl.BlockSpec((tm, ML_KV_LORA), lambda i: (i, 0)),
                   pl.BlockSpec((tm, ML_ROPE), lambda i: (i, 0))],
        out_shape=[jax.ShapeDtypeStruct((n, ML_Q_LORA), BF16),
                   jax.ShapeDtypeStruct((n, ML_KV_LORA), F32),
                   jax.ShapeDtypeStruct((n, ML_ROPE), F32)],
        compiler_params=_cparams(1), name="mla_mid",
    )(dn, qw, kvw, cos, sin)


ML_QSCALE = math.log2(math.e) / math.sqrt(ML_NOPE + ML_ROPE)


def _epi_qscale(accs, extras):
    return [accs[0] * ML_QSCALE]


def _epi_rope(accs, extras):
    cos, sin = extras
    return [(accs[0] * cos + accs[1] * sin) * ML_QSCALE]


def _attn_kernel(qn_ref, qr_ref, kn_ref, kr_ref, v_ref, o_ref, kc_scr):
    @pl.when(pl.program_id(2) == 0)
    def _():
        for h in range(2):
            kc_scr[h, :, :ML_NOPE] = kn_ref[:, h * ML_NOPE:(h + 1) * ML_NOPE]
            kc_scr[h, :, ML_NOPE:] = kr_ref[...]

    scores = []
    for h in range(2):
        q = jnp.concatenate([qn_ref[:, h * ML_NOPE:(h + 1) * ML_NOPE],
                             qr_ref[:, h * ML_ROPE:(h + 1) * ML_ROPE]], axis=1)
        scores.append(lax.dot_general(q, kc_scr[h], (((1,), (1,)), ((), ())), preferred_element_type=F32))
    outs = []
    for h, s in enumerate(scores):
        m = jnp.max(s, axis=-1, keepdims=True)
        p = jnp.exp2(s - m)
        l = jnp.sum(p, axis=-1, keepdims=True)
        o = jnp.dot(p.astype(BF16), v_ref[:, h * ML_V:(h + 1) * ML_V], preferred_element_type=F32)
        outs.append(o / l)
    o_ref[...] = jnp.concatenate(outs, axis=1).astype(o_ref.dtype)


def attention(qn, qr, kn, kr, v, *, n_seq, q_len, k_len, row0, tq):
    heads2 = qn.shape[1] // (2 * ML_NOPE)
    qb = q_len // tq
    rb0 = row0 // tq
    return pl.pallas_call(
        _attn_kernel,
        grid=(n_seq, heads2, qb),
        in_specs=[pl.BlockSpec((tq, 2 * ML_NOPE), lambda s, h, i: (rb0 + s * qb + i, h)),
                  pl.BlockSpec((tq, 2 * ML_ROPE), lambda s, h, i: (rb0 + s * qb + i, h)),
                  pl.BlockSpec((k_len, 2 * ML_NOPE), lambda s, h, i: (s, h)),
                  pl.BlockSpec((k_len, ML_ROPE), lambda s, h, i: (s, 0)),
                  pl.BlockSpec((k_len, 2 * ML_V), lambda s, h, i: (s, h))],
        out_specs=pl.BlockSpec((tq, 2 * ML_V), lambda s, h, i: (s * qb + i, h)),
        out_shape=jax.ShapeDtypeStruct((n_seq * q_len, heads2 * 2 * ML_V), BF16),
        scratch_shapes=[pltpu.VMEM((2, k_len, ML_NOPE + ML_ROPE), BF16)],
        compiler_params=_cparams(3), name="mla_attn",
    )(qn, qr, kn, kr, v)


def mla_layer(lay, x, mods, nw, p, cache_ckv, cache_kr, cos, sin, cos2, sin2):
    n, d = x.shape
    sh1, sc1, g1 = mods[0], mods[1], mods[2]
    h = norm_mod(lay, x, nw, sc1, sh1, BF16)
    tm = lay.tile(512)
    dn = matmul(h, [p['w_down']], _epi_plain, [F32], tm=tm, tn=ML_DOWN_COLS, name="mla_down")[0]
    qlat, ckv, kr = mla_mid(lay, dn, p['qnorm_w'], p['kvnorm_w'], cos, sin)
    qn = matmul(qlat, [p['w_uq_nope']], _epi_qscale, [BF16], tm=tm, tn=512, name="mla_qn")[0]
    tn = 2 * ML_ROPE
    extras = ((cos2, pl.BlockSpec((tm, tn), lambda i, j: (i, 0))),
              (sin2, pl.BlockSpec((tm, tn), lambda i, j: (i, 0))))
    qr = matmul(qlat, [p['w_uq_rope'], p['w_uq_rope_sw']], _epi_rope, [BF16], tm=tm, tn=tn,
                extras=extras, name="mla_qr")[0]
    nc, past = lay.nc, cache_ckv.shape[1]
    ckv_b, kr_b = ckv.astype(BF16), kr.astype(BF16)
    kn_c, v_c = matmul(ckv_b[:nc], [p['w_ukn'], p['w_uv']], _epi_plain, [BF16, BF16],
                       tm=lay.tile(512), tn=512, name="mla_kv_ctx")
    o_c = attention(qn, qr, kn_c, kr_b[:nc], v_c, n_seq=lay.n_ctx, q_len=lay.ctx_len,
                    k_len=lay.ctx_len, row0=0, tq=min(256, lay.ctx_len))
    k_len = lay.lat_len + past
    ckv_l = jnp.concatenate([ckv_b[nc:].reshape(lay.n_lat, lay.lat_len, -1), cache_ckv.astype(BF16)],
                            axis=1).reshape(lay.n_lat * k_len, -1)
    kr_l = jnp.concatenate([kr_b[nc:].reshape(lay.n_lat, lay.lat_len, -1), cache_kr.astype(BF16)],
                           axis=1).reshape(lay.n_lat * k_len, -1)
    tk = math.gcd(k_len, 512)
    kn_l, v_l = matmul(ckv_l, [p['w_ukn'], p['w_uv']], _epi_plain, [BF16, BF16], tm=tk, tn=512,
                       name="mla_kv_lat")
    o_l = attention(qn, qr, kn_l, kr_l, v_l, n_seq=lay.n_lat, q_len=lay.lat_len, k_len=k_len,
                    row0=nc, tq=min(256, lay.lat_len))
    o = jnp.concatenate([o_c, o_l], axis=0)
    x = matmul_gated_residual(lay, o, p['wo'], x, g1, tm=lay.tile(1024), tn=512, name="mla_o")
    return x, ckv[:nc], kr[:nc]


def _rope_tables(lay):
    t = lay.lat_len
    rows = t // GRID_W
    rr = jnp.broadcast_to(jnp.arange(rows, dtype=F32)[:, None], (rows, GRID_W)).reshape(-1)
    cc = jnp.broadcast_to(jnp.arange(GRID_W, dtype=F32)[None, :], (rows, GRID_W)).reshape(-1)
    nf = ML_ROPE // 4
    inv = ROPE_BASE ** (-jnp.arange(nf, dtype=F32) / nf)
    ar, ac = rr[:, None] * inv, cc[:, None] * inv
    cos = jnp.concatenate([jnp.cos(ar), jnp.cos(ar), jnp.cos(ac), jnp.cos(ac)], axis=-1)
    sin = jnp.concatenate([-jnp.sin(ar), jnp.sin(ar), -jnp.sin(ac), jnp.sin(ac)], axis=-1)
    cos = jnp.concatenate([jnp.ones((lay.nc, ML_ROPE), F32), jnp.tile(cos, (lay.n_lat, 1))], axis=0)
    sin = jnp.concatenate([jnp.zeros((lay.nc, ML_ROPE), F32), jnp.tile(sin, (lay.n_lat, 1))], axis=0)
    return cos, sin


def _swap_cols(w):
    k, c = w.shape
    w4 = w.reshape(k, c // 32, 2, 16)
    return w4[:, :, ::-1, :].reshape(k, c)


def ffn(lay, x, mods, nw, w_a, w_b, w_out):
    sh2, sc2, g2 = mods[3], mods[4], mods[5]
    h = norm_mod(lay, x, nw, sc2, sh2, BF16)
    act = matmul(h, [w_a, w_b], _epi_swiglu, [BF16], tm=lay.tile(1024), tn=512, name="ffn_in")[0]
    return matmul_gated_residual(lay, act, w_out, x, g2, tm=lay.tile(1024), tn=512, name="ffn_out")


def _block_diag_states(s):
    b, two, h, n, _ = s.shape
    s = s.reshape(b, two, h // RW_GROUP, RW_GROUP, n, n)
    eye = jnp.eye(RW_GROUP, dtype=s.dtype)
    out = jnp.einsum('bdghvk,hi->bdghvik', s, eye)
    return out.reshape(b, two, h // RW_GROUP, RW_GROUP * n, RW_GROUP * n)


def _diag_blocks(s):
    b, two, g, l, _ = s.shape
    n = l // RW_GROUP
    s = s.reshape(b, two, g, RW_GROUP, n, RW_GROUP, n)
    s = jnp.moveaxis(jnp.diagonal(s, axis1=3, axis2=5), -1, 3)
    return s.reshape(b, two, g * RW_GROUP, n, n)


def kernel(x_prompt, x_sample, state_rwkv, state_hgrn, cache_ckv, cache_krope, c, c_ctx, ada_w, ada_b, norm1_w, norm2_w, ffn_w_in, ffn_w_out, final_norm_w, rw_mu, rw_wr, rw_wk, rw_wv, rw_wo, rw_w0, rw_w1, rw_w2, rw_a0, rw_a1, rw_a2, rw_g1, rw_g2, rw_kk, rw_ka, rw_rk, rw_lnx_w, rw_lnx_b, hg_w_in, hg_lb, hg_norm_w, hg_wo, ml_w_down, ml_qnorm_w, ml_kvnorm_w, ml_w_uq, ml_w_ukv, ml_wo):
    n_ctx, ctx_len, d = x_prompt.shape
    n_lat, lat_len, _ = x_sample.shape
    depth = ada_w.shape[0]
    lay = Layout(n_ctx, ctx_len, n_lat, lat_len)
    d_ff = ffn_w_out.shape[1]
    x = jnp.concatenate([x_prompt.reshape(lay.nc, d), x_sample.reshape(n_lat * lat_len, d)], axis=0)

    n_cond = -(-(1 + n_lat) // SUBLANES_V7X) * SUBLANES_V7X
    cond = jnp.zeros((n_cond, d), F32).at[0].set(c_ctx).at[1:1 + n_lat].set(c)
    mod_all = adaln(cond, ada_w, ada_b)
    mod_all = mod_all.reshape(depth, n_cond, 6, 1, d).transpose(0, 2, 1, 3, 4)

    lb_table = jnp.cumsum(jax.nn.softmax(hg_lb.astype(F32), axis=0), axis=0)
    lb_table = lb_table - lb_table[0]
    cos, sin = _rope_tables(lay)
    cos2, sin2 = jnp.tile(cos, (1, 2)), jnp.tile(sin, (1, 2))
    bf = lambda t: t.astype(BF16)

    new_rwkv, new_hgrn, new_ckv, new_krope = [], [], [], []
    for l in range(depth):
        kind, j = l % 3, l // 3
        mods = mod_all[l]
        if kind == 0:
            pad1 = lambda w: jnp.pad(w, ((0, 0), (0, 0), (0, LORA_PAD - w.shape[2])))
            pad2 = lambda w: jnp.pad(w, ((0, 0), (0, LORA_PAD - w.shape[1]), (0, 0)))
            p = {'mu': rw_mu[j], 'wr': bf(rw_wr[j]), 'wk': bf(rw_wk[j]), 'wv': bf(rw_wv[j]),
                 'wo': bf(rw_wo[j]),
                 'w0': rw_w0[j].reshape(2, 1, d), 'w1': bf(pad1(rw_w1[j])), 'w2': bf(pad2(rw_w2[j])),
                 'a0': rw_a0[j].reshape(2, 1, d), 'a1': bf(pad1(rw_a1[j])), 'a2': bf(pad2(rw_a2[j])),
                 'g1': bf(rw_g1[j]), 'g2': bf(rw_g2[j]),
                 'kk': rw_kk[j].reshape(1, d), 'ka': rw_ka[j].reshape(1, d),
                 'rk': rw_rk[j].reshape(2, 1, d),
                 'lnx_w': rw_lnx_w[j].reshape(1, d), 'lnx_b': rw_lnx_b[j].reshape(1, d)}
            s_lat = _block_diag_states(state_rwkv[:, j].astype(F32))
            s0 = jnp.concatenate([jnp.zeros((1,) + s_lat.shape[1:], F32), s_lat], axis=0)
            x, sfin = rwkv_layer(lay, x, mods, norm1_w[l], p, s0)
            new_rwkv.append(_diag_blocks(sfin[:n_ctx]))
        elif kind == 1:
            hk = d
            w_in = hg_w_in[j]
            p = {'w_in': [bf(w_in[:, i * hk:(i + 1) * hk]) for i in range(5)],
                 'norm_w': hg_norm_w[j].reshape(1, HG_K), 'wo': bf(hg_wo[j])}
            s_lat = jnp.swapaxes(state_hgrn[:, j].astype(F32), -1, -2)
            s0t = jnp.concatenate([jnp.zeros((1,) + s_lat.shape[1:], F32), s_lat], axis=0)
            x, sfin = hgrn_layer(lay, x, mods, norm1_w[l], p, lb_table[l].reshape(1, d), s0t)
            new_hgrn.append(jnp.swapaxes(sfin[:n_ctx], -1, -2))
        else:
            wd = ml_w_down[j]
            kr_w = wd[:, ML_KR_OFF:]
            zpad = jnp.zeros((d, LANES_V7X - ML_ROPE), wd.dtype)
            w_down = jnp.concatenate([wd, zpad, _swap_cols(kr_w), zpad], axis=1)
            wq = ml_w_uq[j].reshape(ML_Q_LORA, ML_H, ML_NOPE + ML_ROPE)
            wq_n = wq[:, :, :ML_NOPE].reshape(ML_Q_LORA, ML_H * ML_NOPE)
            wq_r = wq[:, :, ML_NOPE:].reshape(ML_Q_LORA, ML_H * ML_ROPE)
            wkv = ml_w_ukv[j].reshape(ML_KV_LORA, ML_H, ML_NOPE + ML_V)
            p = {'w_down': bf(w_down), 'qnorm_w': ml_qnorm_w[j].reshape(1, -1),
                 'kvnorm_w': ml_kvnorm_w[j].reshape(1, -1),
                 'w_uq_nope': bf(wq_n), 'w_uq_rope': bf(wq_r), 'w_uq_rope_sw': bf(_swap_cols(wq_r)),
                 'w_ukn': bf(wkv[:, :, :ML_NOPE].reshape(ML_KV_LORA, ML_H * ML_NOPE)),
                 'w_uv': bf(wkv[:, :, ML_NOPE:].reshape(ML_KV_LORA, ML_H * ML_V)),
                 'wo': bf(ml_wo[j])}
            x, ckv_c, kr_c = mla_layer(lay, x, mods, norm1_w[l], p, cache_ckv[:, j], cache_krope[:, j],
                                       cos, sin, cos2, sin2)
            new_ckv.append(ckv_c.reshape(n_ctx, ctx_len, ML_KV_LORA))
            new_krope.append(kr_c.reshape(n_ctx, ctx_len, ML_ROPE))
        w_in = ffn_w_in[l]
        x = ffn(lay, x, mods, norm2_w[l], bf(w_in[:, :d_ff]), bf(w_in[:, d_ff:]), bf(ffn_w_out[l]))

    y_prompt = rmsnorm_rows(lay, x, final_norm_w, 0, lay.nc).reshape(n_ctx, ctx_len, d)
    y_sample = rmsnorm_rows(lay, x, final_norm_w, lay.nc, lay.n - lay.nc).reshape(n_lat, lat_len, d)
    return (y_prompt, y_sample, jnp.stack(new_rwkv, axis=1), jnp.stack(new_hgrn, axis=1),
            jnp.stack(new_ckv, axis=1), jnp.stack(new_krope, axis=1))
```

```python
import functools
import math

import numpy as np
import jax
import jax.numpy as jnp
from jax import lax
from jax.experimental import pallas as pl
from jax.experimental.pallas import tpu as pltpu

F32 = jnp.float32
BF16 = jnp.bfloat16

LANES_V7X = 128
SUBLANES_V7X = 8
VMEM_BYTES_V7X = 64 * 1024 * 1024
VMEM_LIMIT = 56 * 1024 * 1024

NORM_EPS = 1e-6
RW_HEAD = 64
RW_LN_EPS = 64e-5
RW_GROUP = 4
RW_LANES = RW_GROUP * RW_HEAD
RW_CHUNK = 64
RW_GSTEP = 2
HG_K = 128
HG_CHUNK_TOKENS = 64
HG_HSTEP = 2
ML_H = 16
ML_NOPE = 128
ML_ROPE = 64
ML_V = 128
ML_Q_LORA = 512
ML_KV_LORA = 512
GRID_W = 64
ROPE_BASE = 10000.0
LORA_PAD = 128


class Layout:
    def __init__(self, n_ctx, ctx_len, n_lat, lat_len):
        self.n_ctx, self.ctx_len, self.n_lat, self.lat_len = n_ctx, ctx_len, n_lat, lat_len
        self.nc = n_ctx * ctx_len
        self.n = self.nc + n_lat * lat_len
        self.tb = min(256, ctx_len)
        assert ctx_len % self.tb == 0 and lat_len % self.tb == 0 and self.tb % RW_CHUNK == 0
        self.nb = self.n // self.tb
        self.nb_ctx = self.nc // self.tb
        self.bps_ctx = ctx_len // self.tb
        self.bps_lat = lat_len // self.tb
        self.n_seq = n_ctx + n_lat

    def tile(self, want):
        t = want
        while self.nc % t or self.lat_len % t:
            t //= 2
        return t

    def cond_of_tile(self, i, tm):
        row = i * tm
        return jnp.where(row < self.nc, 0, 1 + (row - self.nc) // self.lat_len)

    def seq_info(self, blk):
        is_ctx = blk < self.nb_ctx
        lat = blk - self.nb_ctx
        seq = jnp.where(is_ctx, blk // self.bps_ctx, self.n_ctx + lat // self.bps_lat)
        pos = jnp.where(is_ctx, blk % self.bps_ctx, lat % self.bps_lat)
        cnt = jnp.where(is_ctx, self.bps_ctx, self.bps_lat)
        return seq, pos, cnt


def _cparams(n_axes):
    return pltpu.CompilerParams(dimension_semantics=("arbitrary",) * n_axes, vmem_limit_bytes=VMEM_LIMIT)


def _bdot(a, b):
    return jnp.dot(a.astype(BF16), b.astype(BF16), preferred_element_type=F32)


def _bdot_nt(a, b):
    return lax.dot_general(a.astype(BF16), b.astype(BF16), (((1,), (1,)), ((), ())),
                           preferred_element_type=F32)


def _silu(x):
    return x * jax.nn.sigmoid(x)


def _adaln_kernel(c_ref, w_ref, b_ref, o_ref):
    a = _silu(c_ref[...]).astype(BF16)
    o_ref[...] = jnp.dot(a, w_ref[...].astype(BF16), preferred_element_type=F32) + b_ref[...]


def adaln(cond, ada_w, ada_b):
    depth, d, d6 = ada_w.shape
    r = cond.shape[0]
    tn = 1024
    return pl.pallas_call(
        _adaln_kernel,
        grid=(depth, d6 // tn),
        in_specs=[pl.BlockSpec((r, d), lambda l, j: (0, 0)),
                  pl.BlockSpec((None, d, tn), lambda l, j: (l, 0, j)),
                  pl.BlockSpec((None, 1, tn), lambda l, j: (l, 0, j))],
        out_specs=pl.BlockSpec((None, r, tn), lambda l, j: (l, 0, j)),
        out_shape=jax.ShapeDtypeStruct((depth, r, d6), F32),
        compiler_params=_cparams(2), name="adaln",
    )(cond, ada_w, ada_b.reshape(depth, 1, d6))


def _norm_mod(x, nw, sc, sh):
    y = x * lax.rsqrt(jnp.mean(x * x, axis=-1, keepdims=True) + NORM_EPS)
    return (y * nw) * (1.0 + sc) + sh


def _norm_mod_kernel(x_ref, nw_ref, sc_ref, sh_ref, o_ref):
    o_ref[...] = _norm_mod(x_ref[...], nw_ref[...], sc_ref[...], sh_ref[...]).astype(o_ref.dtype)


def norm_mod(lay, x, nw, sc, sh, out_dtype):
    n, d = x.shape
    tm = lay.tile(512)
    cmap = lambda i: (lay.cond_of_tile(i, tm), 0, 0)
    return pl.pallas_call(
        _norm_mod_kernel,
        grid=(n // tm,),
        in_specs=[pl.BlockSpec((tm, d), lambda i: (i, 0)),
                  pl.BlockSpec((1, d), lambda i: (0, 0)),
                  pl.BlockSpec((None, 1, d), cmap),
                  pl.BlockSpec((None, 1, d), cmap)],
        out_specs=pl.BlockSpec((tm, d), lambda i: (i, 0)),
        out_shape=jax.ShapeDtypeStruct((n, d), out_dtype),
        compiler_params=_cparams(1), name="norm_mod",
    )(x, nw.reshape(1, d), sc, sh)


def _rmsnorm_kernel(x_ref, w_ref, o_ref):
    x = x_ref[...]
    o_ref[...] = x * lax.rsqrt(jnp.mean(x * x, axis=-1, keepdims=True) + NORM_EPS) * w_ref[...]


def rmsnorm_rows(lay, x, w, row0, rows):
    d = x.shape[1]
    tm = lay.tile(512)
    b0 = row0 // tm
    return pl.pallas_call(
        _rmsnorm_kernel,
        grid=(rows // tm,),
        in_specs=[pl.BlockSpec((tm, d), lambda i: (b0 + i, 0)), pl.BlockSpec((1, d), lambda i: (0, 0))],
        out_specs=pl.BlockSpec((tm, d), lambda i: (i, 0)),
        out_shape=jax.ShapeDtypeStruct((rows, d), x.dtype),
        compiler_params=_cparams(1), name="final_norm",
    )(x, w.reshape(1, d))


def _mm_kernel(*refs, n_w, n_e, epi):
    a = refs[0][...]
    accs = [jnp.dot(a, refs[1 + i][...], preferred_element_type=F32) for i in range(n_w)]
    extras = [refs[1 + n_w + i][...] for i in range(n_e)]
    outs = epi(accs, extras)
    o_refs = refs[1 + n_w + n_e:]
    for o_ref, val in zip(o_refs, outs):
        o_ref[...] = val.astype(o_ref.dtype)


def matmul(a, ws, epi, out_dtypes, *, tm, tn, extras=(), name):
    m, k = a.shape
    nw = ws[0].shape[1]
    assert m % tm == 0 and nw % tn == 0
    in_specs = [pl.BlockSpec((tm, k), lambda i, j: (i, 0))]
    in_specs += [pl.BlockSpec((k, tn), lambda i, j: (0, j)) for _ in ws]
    in_specs += [spec for _, spec in extras]
    outs = pl.pallas_call(
        functools.partial(_mm_kernel, n_w=len(ws), n_e=len(extras), epi=epi),
        grid=(m // tm, nw // tn),
        in_specs=in_specs,
        out_specs=[pl.BlockSpec((tm, tn), lambda i, j: (i, j)) for _ in out_dtypes],
        out_shape=[jax.ShapeDtypeStruct((m, nw), dt) for dt in out_dtypes],
        compiler_params=_cparams(2), name=name,
    )(a, *ws, *[arr for arr, _ in extras])
    return outs


def _epi_plain(accs, extras):
    return accs


def _epi_gated_residual(accs, extras):
    x, g = extras
    return [x + g * accs[0]]


def matmul_gated_residual(lay, a, w, x, gate, *, tm, tn, name):
    extras = ((x, pl.BlockSpec((tm, tn), lambda i, j: (i, j))),
              (gate, pl.BlockSpec((None, 1, tn), lambda i, j: (lay.cond_of_tile(i, tm), 0, j))))
    return matmul(a, [w], _epi_gated_residual, [F32], tm=tm, tn=tn, extras=extras, name=name)[0]


def _epi_swiglu(accs, extras):
    return [_silu(accs[0]) * accs[1]]


RWKV_PREP_SLAB = LANES_V7X


def _rwkv_prep_kernel(x_ref, xp_ref, xn_ref, nw_ref, sc_ref, sh_ref, mu_ref, *o_refs, lay):
    i = pl.program_id(0)
    _, pos, cnt = lay.seq_info(i)
    def inv_rms(x):
        return lax.rsqrt(jnp.mean(x * x, axis=-1, keepdims=True) + NORM_EPS)

    tb, d = x_ref.shape
    inv = inv_rms(x_ref[...])
    inv_p = inv_rms(xp_ref[SUBLANES_V7X - 1:SUBLANES_V7X, :])
    inv_n = inv_rms(xn_ref[0:1, :])
    keep_p = jnp.where(pos == 0, 0.0, 1.0)
    keep_n = jnp.where(pos == cnt - 1, 0.0, 1.0)
    row = lax.broadcasted_iota(jnp.int32, (tb, RWKV_PREP_SLAB), 0)
    for c0 in range(0, d, RWKV_PREP_SLAB):
        cs = slice(c0, c0 + RWKV_PREP_SLAB)
        nw, sc, sh = nw_ref[:, cs], sc_ref[:, cs], sh_ref[:, cs]
        h = ((x_ref[:, cs] * inv) * nw) * (1.0 + sc) + sh
        hp = (((xp_ref[SUBLANES_V7X - 1:SUBLANES_V7X, cs] * inv_p) * nw) * (1.0 + sc) + sh) * keep_p
        hn = (((xn_ref[0:1, cs] * inv_n) * nw) * (1.0 + sc) + sh) * keep_n
        prev = jnp.where(row == 0, hp, pltpu.roll(h, 1, axis=0))
        nxt = jnp.where(row == tb - 1, hn, pltpu.roll(h, tb - 1, axis=0))
        xx = 0.5 * (prev + nxt) - h
        for idx, o_ref in enumerate(o_refs):
            o_ref[:, cs] = (h + xx * mu_ref[idx:idx + 1, cs]).astype(o_ref.dtype)


def rwkv_prep(lay, x, nw, sc, sh, mu):
    n, d = x.shape
    tb = lay.tb
    r8 = tb // SUBLANES_V7X
    last8 = n // SUBLANES_V7X - 1
    cmap = lambda i: (lay.cond_of_tile(i, tb), 0, 0)
    return pl.pallas_call(
        functools.partial(_rwkv_prep_kernel, lay=lay),
        grid=(n // tb,),
        in_specs=[pl.BlockSpec((tb, d), lambda i: (i, 0)),
                  pl.BlockSpec((SUBLANES_V7X, d), lambda i: (jnp.maximum(i * r8 - 1, 0), 0)),
                  pl.BlockSpec((SUBLANES_V7X, d), lambda i: (jnp.minimum((i + 1) * r8, last8), 0)),
                  pl.BlockSpec((1, d), lambda i: (0, 0)),
                  pl.BlockSpec((None, 1, d), cmap),
                  pl.BlockSpec((None, 1, d), cmap),
                  pl.BlockSpec((6, d), lambda i: (0, 0))],
        out_specs=[pl.BlockSpec((tb, d), lambda i: (i, 0))] * 6,
        out_shape=[jax.ShapeDtypeStruct((n, d), BF16)] * 6,
        compiler_params=_cparams(1), name="rwkv_prep",
    )(x, x, x, nw.reshape(1, d), sc, sh, mu)


RW_LOG_DECAY_SCALE = -math.exp(-0.5)


def _rwkv_lora_kernel(xw_ref, xa_ref, xg_ref, w1_ref, w2_ref, w0_ref, a1_ref, a2_ref, a0_ref,
                      g1_ref, g2_ref, lw_ref, a_ref, g_ref):
    xw, xa, xg = xw_ref[...], xa_ref[...], xg_ref[...]
    for d in range(2):
        t = jnp.tanh(jnp.dot(xw, w1_ref[d], preferred_element_type=F32))
        wl = w0_ref[d] + _bdot(t, w2_ref[d])
        lw_ref[d] = RW_LOG_DECAY_SCALE * jax.nn.sigmoid(wl)
        t = jnp.dot(xa, a1_ref[d], preferred_element_type=F32)
        a_ref[d] = jax.nn.sigmoid(a0_ref[d] + _bdot(t, a2_ref[d]))
    t = jax.nn.sigmoid(jnp.dot(xg, g1_ref[...], preferred_element_type=F32))
    g_ref[...] = _bdot(t, g2_ref[...])


def rwkv_lora(lay, xw, xa, xg, w1, w2, w0, a1, a2, a0, g1, g2):
    n, d = xw.shape
    tm = lay.tile(256)
    full = lambda arr: pl.BlockSpec(arr.shape, lambda i: (0,) * arr.ndim)
    row = pl.BlockSpec((tm, d), lambda i: (i, 0))
    return pl.pallas_call(
        _rwkv_lora_kernel,
        grid=(n // tm,),
        in_specs=[row, row, row] + [full(t) for t in (w1, w2, w0, a1, a2, a0, g1, g2)],
        out_specs=[pl.BlockSpec((2, tm, d), lambda i: (0, i, 0)),
                   pl.BlockSpec((2, tm, d), lambda i: (0, i, 0)),
                   row],
        out_shape=[jax.ShapeDtypeStruct((2, n, d), F32), jax.ShapeDtypeStruct((2, n, d), F32),
                   jax.ShapeDtypeStruct((n, d), F32)],
        compiler_params=_cparams(1), name="rwkv_lora",
    )(xw, xa, xg, w1, w2, w0, a1, a2, a0, g1, g2)


def _wkv_constants():
    c, g, hd = RW_CHUNK, RW_GROUP, RW_HEAD
    gc, lanes = g * c, g * hd
    t = np.arange(c)
    cum = np.stack([(t[None, :] <= t[:, None]), (t[None, :] >= t[:, None])])
    tr = np.arange(c)[:, None]
    tc = np.arange(gc)[None, :] % c
    strict = np.stack([tc < tr, tc > tr])
    incl = np.stack([tc <= tr, tc >= tr])
    head_rows = np.arange(gc)[:, None] // c == np.arange(lanes)[None, :] // hd
    blk_rows = np.arange(gc)[:, None] // c == np.arange(gc)[None, :] // c
    bd = np.arange(lanes)[:, None] // hd == np.arange(lanes)[None, :] // hd
    eye_w = tr == tc
    return dict(cum=jnp.asarray(cum, BF16), strict=jnp.asarray(strict, F32), incl=jnp.asarray(incl, F32),
                head_rows=jnp.asarray(head_rows, BF16), blk_rows=jnp.asarray(blk_rows, BF16),
                bd=jnp.asarray(bd, F32), bd_b=jnp.asarray(bd, BF16), eye_w=jnp.asarray(eye_w, F32))


def _split_dot(m01, x, passes):
    acc, rem = None, x
    for _ in range(passes):
        part = rem.astype(BF16)
        term = jnp.dot(m01, part, preferred_element_type=F32)
        acc = term if acc is None else acc + term
        rem = rem - part.astype(F32)
    return acc


def _split_dot_r(x, m01, passes):
    acc, rem = None, x
    for _ in range(passes):
        part = rem.astype(BF16)
        term = jnp.dot(part, m01, preferred_element_type=F32)
        acc = term if acc is None else acc + term
        rem = rem - part.astype(F32)
    return acc


def _tile_rows(x, mask_b):
    return jnp.concatenate([x.astype(BF16)] * RW_GROUP, axis=0) * mask_b


def _wkv_prep1(raw):
    return [_split_dot(u[6], u[5], 3) for u in raw]


def _wkv_prep2(raw, cums, head_rows):
    c = RW_CHUNK
    mids = []
    for (r, v, kk, kd, a, lw, _, _, _), cum in zip(raw, cums):
        tot = jnp.sum(lw, axis=0, keepdims=True)
        kka = kk * a
        e_inv, e_rest = jnp.exp(-cum), jnp.exp(tot - cum)
        q2 = jnp.concatenate([kk * jnp.exp(cum - lw), r * jnp.exp(cum)], axis=0).astype(BF16)
        mids.append(dict(q2=q2, kdh=_tile_rows(kd * e_inv, head_rows), kkah=_tile_rows(kka * e_inv, head_rows),
                         vbd=_tile_rows(v, head_rows), v=v, decay=jnp.exp(tot),
                         kw=jnp.concatenate([kd * e_rest, -(kka * e_rest)], axis=0).astype(BF16)))
    s1s = [_bdot_nt(m['q2'], m['kdh']) for m in mids]
    s2s = [_bdot_nt(m['q2'][c:], m['kkah']) for m in mids]
    for m, u, s1, s2 in zip(mids, raw, s1s, s2s):
        strict_w, incl_w = u[7], u[8]
        m['lad'] = jnp.concatenate([jnp.where(strict_w > 0, s1[:c], 0.0),
                                    jnp.where(incl_w > 0, s1[c:], 0.0)], axis=0).astype(BF16)
        m['a_a'] = jnp.where(incl_w > 0, s2, 0.0).astype(BF16)
    return mids


def _wkv_prep3(mids):
    for m, lav in zip(mids, [_bdot(m['lad'], m['vbd']) for m in mids]):
        m['lav'] = lav
    return mids


def _wkv_adv1(states, preps):
    return [_bdot_nt(p['q2'], s) for p, s in zip(preps, states)]


def _wkv_adv2(p0s, preps, t_ws, head_rows):
    c = RW_CHUNK
    return [_bdot(t_w, _tile_rows(p0[:c] + p['lav'][:c], head_rows))
            for p0, p, t_w in zip(p0s, preps, t_ws)]


def _wkv_adv3(states, p0s, us, preps, head_rows, bd):
    c = RW_CHUNK
    upds = [_bdot(jnp.concatenate([p['v'].astype(F32), u], axis=0).T, p['kw']) for p, u in zip(preps, us)]
    aus = [_bdot(p['a_a'], _tile_rows(u, head_rows)) for p, u in zip(preps, us)]
    new_states = [s * p['decay'] + jnp.where(bd > 0, upd, 0.0) for s, p, upd in zip(states, preps, upds)]
    ys = [p0[c:] + p['lav'][c:] - au for p0, p, au in zip(p0s, preps, aus)]
    return new_states, ys


RKV_TN = RW_LANES


def _rwkv_rkv_kernel(xr_ref, xk_ref, xv_ref, wr_ref, wk_ref, wv_ref, a_ref, kkw_ref, kaw_ref, rk_ref, bd_ref,
                     r_ref, v_ref, kk_ref, kd_ref, b_ref):
    r = jnp.dot(xr_ref[...], wr_ref[...], preferred_element_type=F32)
    k = jnp.dot(xk_ref[...], wk_ref[...], preferred_element_type=F32)
    v = jnp.dot(xv_ref[...], wv_ref[...], preferred_element_type=F32)
    r_ref[...] = r
    v_ref[...] = v.astype(v_ref.dtype)
    bd = bd_ref[...]
    kk = k * kkw_ref[...]
    mix = None
    for d in range(2):
        kd = k * (1.0 + (a_ref[d] - 1.0) * kaw_ref[...])
        kd_ref[d] = kd
        term = kd * rk_ref[d]
        mix = term if mix is None else mix + term
    rm = r * mix
    for h in range(RKV_TN // RW_LANES):
        cs = slice(h * RW_LANES, (h + 1) * RW_LANES)
        kkh = kk[:, cs]
        kk_ref[:, cs] = kkh * lax.rsqrt(_split_dot_r(kkh * kkh, bd, 2) + 1e-12)
        b_ref[:, cs] = _split_dot_r(rm[:, cs], bd, 2) * v[:, cs]


def rwkv_rkv(lay, xr, xk, xv, wr, wk, wv, a, kkw, kaw, rk):
    n, kdim = xr.shape
    dm = wr.shape[1]
    tm, tn = lay.tile(1024), RKV_TN
    bd_b = _wkv_constants()['bd_b']
    lhs = pl.BlockSpec((tm, kdim), lambda i, j: (i, 0))
    rhs = pl.BlockSpec((kdim, tn), lambda i, j: (0, j))
    tok = pl.BlockSpec((tm, tn), lambda i, j: (i, j))
    two = pl.BlockSpec((2, tm, tn), lambda i, j: (0, i, j))
    par = pl.BlockSpec((1, tn), lambda i, j: (0, j))
    return pl.pallas_call(
        _rwkv_rkv_kernel,
        grid=(n // tm, dm // tn),
        in_specs=[lhs, lhs, lhs, rhs, rhs, rhs, two, par, par,
                  pl.BlockSpec((2, 1, tn), lambda i, j: (0, 0, j)),
                  pl.BlockSpec(bd_b.shape, lambda i, j: (0, 0))],
        out_specs=[tok, tok, tok, two, tok],
        out_shape=[jax.ShapeDtypeStruct((n, dm), F32), jax.ShapeDtypeStruct((n, dm), BF16),
                   jax.ShapeDtypeStruct((n, dm), F32), jax.ShapeDtypeStruct((2, n, dm), F32),
                   jax.ShapeDtypeStruct((n, dm), F32)],
        compiler_params=_cparams(2), name="rwkv_rkv",
    )(xr, xk, xv, wr, wk, wv, a, kkw, kaw, rk, bd_b)


def _tinv_begin(items, head_rows, blk_rows, eye_w):
    cums = [_split_dot(it[3], it[2], 3) for it in items]
    ops = [(kk * jnp.exp(cum - lw), _tile_rows(kk * a * jnp.exp(-cum), head_rows))
           for (kk, a, lw, _, _), cum in zip(items, cums)]
    n_ws = [-jnp.where(it[4] > 0, _bdot_nt(q, w), 0.0) for it, (q, w) in zip(items, ops)]
    n_pows = [_bdot(n_w, _tile_rows(n_w, blk_rows)) for n_w in n_ws]
    return [eye_w + n_w for n_w in n_ws], n_pows


def _tinv_level(t_ws, n_pows, last, blk_rows):
    c = RW_CHUNK
    ws = [_tile_rows(n_pow, blk_rows) for n_pow in n_pows]
    if last:
        return [t_w + _bdot(t_w, w) for t_w, w in zip(t_ws, ws)], None
    boths = [_bdot(jnp.concatenate([t_w, n_pow], axis=0), w) for t_w, n_pow, w in zip(t_ws, n_pows, ws)]
    return [t_w + both[:c] for t_w, both in zip(t_ws, boths)], [both[c:] for both in boths]


RW_TINV_LEVELS = int(math.log2(RW_CHUNK))


def _wkv_fused_kernel(*refs, lay):
    (rf, vf, kkf, kdf, af, lwf, s0f, kkfn, afn, lwfn,
     rb, vb, kkb, kdb, ab, lwb, s0b, kkbn, abn, lwbn,
     cum_ref, strict_ref, incl_ref, hr_ref, br_ref, eye_ref, bd_ref,
     yf_ref, yb_ref, sff_ref, sfb_ref, sf_scr, sb_scr, t_scr) = refs
    c = RW_CHUNK
    n_chunks = lay.tb // c
    j = pl.program_id(1)
    cur = lax.rem(j, 2)
    nxt = 1 - cur
    _, pos_f, cnt_f = lay.seq_info(j)
    _, pos_b, cnt_b = lay.seq_info(lay.nb - 1 - j)

    @pl.when(pos_f == 0)
    def _():
        sf_scr[...] = s0f[...]

    @pl.when(pos_b == cnt_b - 1)
    def _():
        sb_scr[...] = s0b[...]

    hr, br, eye_w, bd = hr_ref[...], br_ref[...], eye_ref[...], bd_ref[...]
    sl_f = [slice(ci * c, (ci + 1) * c) for ci in range(n_chunks)]
    sl_b = sl_f[::-1]
    cols = [slice(gi * RW_LANES, (gi + 1) * RW_LANES) for gi in range(RW_GSTEP)]

    def inv_items(kk_f, a_f, lw_f, kk_b, a_b, lw_b, sl):
        out = []
        for cs in cols:
            out.append((kk_f[sl, cs], a_f[sl, cs], lw_f[sl, cs], cum_ref[0], strict_ref[0]))
            out.append((kk_b[sl, cs], a_b[sl, cs], lw_b[sl, cs], cum_ref[1], strict_ref[1]))
        return out

    def inv_store(slot, sl, t_ws):
        for gi, cs in enumerate(cols):
            t_scr[slot, 0, sl, cs] = t_ws[2 * gi].astype(t_scr.dtype)
            t_scr[slot, 1, sl, cs] = t_ws[2 * gi + 1].astype(t_scr.dtype)

    @pl.when(j == 0)
    def _():
        def body(ci, carry):
            sl = pl.ds(pl.multiple_of(ci * c, c), c)
            t_ws, n_pows = _tinv_begin(inv_items(kkf, af, lwf, kkb, ab, lwb, sl), hr, br, eye_w)
            for lv in range(1, RW_TINV_LEVELS):
                t_ws, n_pows = _tinv_level(t_ws, n_pows, lv == RW_TINV_LEVELS - 1, br)
            inv_store(cur, sl, t_ws)
            return carry
        lax.fori_loop(0, n_chunks, body, 0)

    def raw(ci):
        out = []
        for cs in cols:
            sf, sb = sl_f[ci], sl_b[ci]
            out.append((rf[sf, cs], vf[sf, cs], kkf[sf, cs], kdf[sf, cs], af[sf, cs], lwf[sf, cs],
                        cum_ref[0], strict_ref[0], incl_ref[0]))
            out.append((rb[sb, cs], vb[sb, cs], kkb[sb, cs], kdb[sb, cs], ab[sb, cs], lwb[sb, cs],
                        cum_ref[1], strict_ref[1], incl_ref[1]))
        return out

    states = []
    for gi in range(RW_GSTEP):
        states += [sf_scr[gi], sb_scr[gi]]
    raw_n = raw(0)
    preps = _wkv_prep3(_wkv_prep2(raw_n, _wkv_prep1(raw_n), hr))
    inv = {'lv': 0, 't': None, 'n': None}

    def inv_step():
        lv = inv['lv']
        if lv == 0:
            items = []
            for sl in sl_f:
                items += inv_items(kkfn, afn, lwfn, kkbn, abn, lwbn, sl)
            inv['t'], inv['n'] = _tinv_begin(items, hr, br, eye_w)
        elif lv < RW_TINV_LEVELS:
            inv['t'], inv['n'] = _tinv_level(inv['t'], inv['n'], lv == RW_TINV_LEVELS - 1, br)
            if lv == RW_TINV_LEVELS - 1:
                per = 2 * RW_GSTEP
                for ci2, sl in enumerate(sl_f):
                    inv_store(nxt, sl, inv['t'][ci2 * per:(ci2 + 1) * per])
        inv['lv'] = lv + 1

    for ci in range(n_chunks):
        more = ci + 1 < n_chunks
        t_ws = []
        for cs in cols:
            t_ws += [t_scr[cur, 0, sl_f[ci], cs], t_scr[cur, 1, sl_b[ci], cs]]
        p0s = _wkv_adv1(states, preps)
        if more:
            raw_n = raw(ci + 1)
            cums_n = _wkv_prep1(raw_n)
        inv_step()
        us = _wkv_adv2(p0s, preps, t_ws, hr)
        if more:
            mids_n = _wkv_prep2(raw_n, cums_n, hr)
        inv_step()
        states, ys = _wkv_adv3(states, p0s, us, preps, hr, bd)
        if more:
            preps = _wkv_prep3(mids_n)
        for gi, cs in enumerate(cols):
            yf_ref[sl_f[ci], cs] = ys[2 * gi]
            yb_ref[sl_b[ci], cs] = ys[2 * gi + 1]
    while inv['lv'] < RW_TINV_LEVELS:
        inv_step()
    for gi in range(RW_GSTEP):
        sf_scr[gi] = states[2 * gi]
        sb_scr[gi] = states[2 * gi + 1]

    @pl.when(pos_f == cnt_f - 1)
    def _():
        sff_ref[...] = sf_scr[...]

    @pl.when(pos_b == 0)
    def _():
        sfb_ref[...] = sb_scr[...]


def wkv_fused(lay, r, v, kk, kd, a, lw, s0):
    n, dm = r.shape
    tb, lanes = lay.tb, RW_LANES
    ng = dm // lanes
    width = RW_GSTEP * lanes
    k = _wkv_constants()
    full = lambda arr: pl.BlockSpec(arr.shape, lambda g, j: (0,) * arr.ndim)

    def views(d, blk, blk_next):
        tok = pl.BlockSpec((tb, width), lambda g, j: (blk(j), g))
        tok2 = pl.BlockSpec((None, tb, width), lambda g, j: (d, blk(j), g))
        tok_n = pl.BlockSpec((tb, width), lambda g, j: (blk_next(j), g))
        tok2_n = pl.BlockSpec((None, tb, width), lambda g, j: (d, blk_next(j), g))

        def s0_map(g, j):
            seq, _, _ = lay.seq_info(blk(j))
            return (jnp.maximum(seq - lay.n_ctx + 1, 0), d, g, 0, 0)

        def sfin_map(g, j):
            seq, _, _ = lay.seq_info(blk(j))
            return (seq, g, 0, 0)

        ins = [tok, tok, tok, tok2, tok2, tok2, pl.BlockSpec((None, None, RW_GSTEP, lanes, lanes), s0_map),
               tok_n, tok2_n, tok2_n]
        return ins, tok, pl.BlockSpec((None, RW_GSTEP, lanes, lanes), sfin_map)

    in_f, y_f, sf_f = views(0, lambda j: j, lambda j: jnp.minimum(j + 1, lay.nb - 1))
    in_b, y_b, sf_b = views(1, lambda j: lay.nb - 1 - j, lambda j: jnp.maximum(lay.nb - 2 - j, 0))
    consts = [k['cum'], k['strict'], k['incl'], k['head_rows'], k['blk_rows'], k['eye_w'], k['bd']]
    args = [r, v, kk, kd, a, lw, s0, kk, a, lw]
    return pl.pallas_call(
        functools.partial(_wkv_fused_kernel, lay=lay),
        grid=(ng // RW_GSTEP, lay.nb),
        in_specs=in_f + in_b + [full(x) for x in consts],
        out_specs=[y_f, y_b, sf_f, sf_b],
        out_shape=[jax.ShapeDtypeStruct((n, dm), F32), jax.ShapeDtypeStruct((n, dm), F32),
                   jax.ShapeDtypeStruct((lay.n_seq, ng, lanes, lanes), F32),
                   jax.ShapeDtypeStruct((lay.n_seq, ng, lanes, lanes), F32)],
        scratch_shapes=[pltpu.VMEM((RW_GSTEP, lanes, lanes), F32), pltpu.VMEM((RW_GSTEP, lanes, lanes), F32),
                        pltpu.VMEM((2, 2, tb, width), BF16)],
        compiler_params=_cparams(2), name="wkv_fused",
    )(*args, *args, *consts)


def _rwkv_post_kernel(yf_ref, yb_ref, b_ref, g_ref, lnw_ref, lnb_ref, bd_ref, o_ref):
    y = yf_ref[...] + yb_ref[...]
    bd = bd_ref[...]
    inv = 1.0 / RW_HEAD
    mu = _split_dot_r(y, bd, 2) * inv
    yc = y - mu
    var = _split_dot_r(yc * yc, bd, 2) * inv
    yn = yc * lax.rsqrt(var + RW_LN_EPS)
    out = yn * lnw_ref[...] + lnb_ref[...] + b_ref[...]
    o_ref[...] = (out * g_ref[...]).astype(o_ref.dtype)


def rwkv_post(lay, y_f, y_b, bonus, g, lnw, lnb):
    n, dm = y_f.shape
    lanes = RW_LANES
    tm = lay.tile(1024)
    bd_b = _wkv_constants()['bd_b']
    tok = pl.BlockSpec((tm, lanes), lambda i, c: (i, c))
    par = pl.BlockSpec((1, lanes), lambda i, c: (0, c))
    return pl.pallas_call(
        _rwkv_post_kernel,
        grid=(n // tm, dm // lanes),
        in_specs=[tok, tok, tok, tok, par, par, pl.BlockSpec(bd_b.shape, lambda i, c: (0, 0))],
        out_specs=tok,
        out_shape=jax.ShapeDtypeStruct((n, dm), BF16),
        compiler_params=_cparams(2), name="rwkv_post",
    )(y_f, y_b, bonus, g, lnw, lnb, bd_b)


def rwkv_layer(lay, x, mods, nw, p, s0):
    n, d = x.shape
    sh1, sc1, g1 = mods[0], mods[1], mods[2]
    xr, xw, xk, xv, xa, xg = rwkv_prep(lay, x, nw, sc1, sh1, p['mu'])
    tm = lay.tile(1024)
    lw, a, g = rwkv_lora(lay, xw, xa, xg, p['w1'], p['w2'], p['w0'], p['a1'], p['a2'], p['a0'],
                         p['g1'], p['g2'])
    r, v, kk, kd, bonus = rwkv_rkv(lay, xr, xk, xv, p['wr'], p['wk'], p['wv'], a, p['kk'], p['ka'], p['rk'])
    y_f, y_b, sfin_f, sfin_b = wkv_fused(lay, r, v, kk, kd, a, lw, s0)
    z = rwkv_post(lay, y_f, y_b, bonus, g, p['lnx_w'], p['lnx_b'])
    x = matmul_gated_residual(lay, z, p['wo'], x, g1, tm=tm, tn=512, name="rwkv_o")
    return x, jnp.stack([sfin_f, sfin_b], axis=1)


def _hgrn_constants():
    c = HG_CHUNK_TOKENS
    t = np.arange(c)[:, None]
    j = np.arange(c)[None, :]
    cums, masks_all = [], []
    for rev in (False, True):
        masks = []
        h = 1
        while h < c:
            upper = (t % (2 * h)) >= h
            same = (t // (2 * h)) == (j // (2 * h))
            if not rev:
                mask = same & upper & ((j % (2 * h)) < h)
            else:
                mask = same & (~upper) & ((j % (2 * h)) >= h)
            masks.append(mask)
            h *= 2
        masks.append(t == j)
        cums.append((j >= t) if rev else (j <= t))
        masks_all.append(np.stack(masks, 0))
    return jnp.asarray(np.stack(cums), BF16), jnp.asarray(np.stack(masks_all).astype(np.float32))


def _hgrn_level_exponents(g, gcum, rev):
    c, kdim = g.shape
    row = lax.broadcasted_iota(jnp.int32, g.shape, 0)
    nxt = pltpu.roll(g, c - 1, axis=0)
    prv = pltpu.roll(g, 1, axis=0)
    r2, r4 = row & 1, row & 3
    if not rev:
        x1 = jnp.where(r2 == 1, g, 0.0)
        x2 = jnp.where(r4 == 0, nxt, jnp.where(r4 == 2, g, jnp.where(r4 == 3, prv + g, 0.0)))
    else:
        x1 = jnp.where(r2 == 0, g, 0.0)
        x2 = jnp.where(r4 == 0, g + nxt, jnp.where(r4 == 1, g, jnp.where(r4 == 3, prv, 0.0)))
    xs = [x1, x2]
    h = 4
    while h < c:
        gr = gcum.reshape(c // (2 * h), 2 * h, kdim)
        ref = gr[:, h:h + 1, :] if rev else gr[:, h - 1:h, :]
        upper = lax.broadcasted_iota(jnp.int32, gr.shape, 1) >= h
        diff = gr - ref
        x = jnp.where(upper, -diff, diff) if rev else jnp.where(upper, diff, -diff)
        xs.append(x.reshape(c, kdim))
        h *= 2
    return xs


def _hgrn_units(units, masks_by_dir, cum_by_dir):
    c = HG_CHUNK_TOKENS
    gs = [jnp.log(f) for _, f, _, _ in units]
    gcums = [_split_dot(cum_by_dir[rev], g, 3) for g, (_, _, _, rev) in zip(gs, units)]
    outs = []
    pend = []
    for (q, f, iv, rev), g, gcum in zip(units, gs, gcums):
        k = 1.0 - f
        tot = gcum[0:1] if rev else gcum[c - 1:c]
        es = [jnp.exp(x) for x in _hgrn_level_exponents(g, gcum, rev)]
        pend.append((q, k, iv, rev, es, jnp.exp(gcum), jnp.exp(tot - gcum), jnp.exp(tot)))
    for q, k, iv, rev, es, eg, erest, etot in pend:
        masks = masks_by_dir[rev]
        a = masks[len(es)] * _bdot_nt(q, k)
        for lv, el in enumerate(es):
            a = a + masks[lv] * _bdot_nt(q * el, k * el)
        outs.append(dict(a=a, iv=iv, qe=(q * eg).astype(BF16), kdec=(k * erest).astype(BF16), decay=etot))
    for u in outs:
        u['av'] = _bdot(u['a'], u['iv'])
    for u in outs:
        u['upd'] = _bdot(u['iv'].T, u['kdec'])
    return outs


def _hgrn_kernel(*refs, lay):
    (qf, ff, if_, s0f, qb, fb, ib, s0b, cum_ref, mask_ref, of_ref, ob_ref, sff_ref, sfb_ref,
     sf_scr, sb_scr) = refs
    c = HG_CHUNK_TOKENS
    n_chunks = lay.tb // c
    j = pl.program_id(1)
    _, pos_f, cnt_f = lay.seq_info(j)
    _, pos_b, cnt_b = lay.seq_info(lay.nb - 1 - j)

    @pl.when(pos_f == 0)
    def _():
        sf_scr[...] = s0f[...]

    @pl.when(pos_b == cnt_b - 1)
    def _():
        sb_scr[...] = s0b[...]

    masks_by_dir = [mask_ref[0], mask_ref[1]]
    cum_by_dir = [cum_ref[0], cum_ref[1]]
    sl_f = [slice(ci * c, (ci + 1) * c) for ci in range(n_chunks)]
    sl_b = sl_f[::-1]
    cols = [slice(hi * HG_K, (hi + 1) * HG_K) for hi in range(HG_HSTEP)]
    chains = []
    for hi, cs in enumerate(cols):
        chains.append((sf_scr, hi, of_ref, [(sl, cs) for sl in sl_f],
                       [(qf[sl, cs], ff[sl, cs], if_[sl, cs], False) for sl in sl_f]))
        chains.append((sb_scr, hi, ob_ref, [(sl, cs) for sl in sl_b],
                       [(qb[sl, cs], fb[sl, cs], ib[sl, cs], True) for sl in sl_b]))
    done = _hgrn_units([u for ch in chains for u in ch[4]], masks_by_dir, cum_by_dir)
    pend = []
    for n_ch, (scr, hi, o_ref, where, _) in enumerate(chains):
        s = scr[hi]
        for ci in range(n_chunks):
            u = done[n_ch * n_chunks + ci]
            pend.append((o_ref, where[ci], u, s))
            s = s * u['decay'] + u['upd']
        scr[hi] = s
    for o_ref, (sl, cs), u, s_prev in pend:
        o_ref[sl, cs] = u['av'] + _bdot_nt(u['qe'], s_prev)

    @pl.when(pos_f == cnt_f - 1)
    def _():
        sff_ref[...] = sf_scr[...]

    @pl.when(pos_b == 0)
    def _():
        sfb_ref[...] = sb_scr[...]


def hgrn_scan(lay, q, f_fwd, f_bwd, iv, s0t):
    n, dm = q.shape
    tb = lay.tb
    nh = dm // HG_K
    width = HG_HSTEP * HG_K
    cum_m, masks = _hgrn_constants()
    full = lambda arr: pl.BlockSpec(arr.shape, lambda h, j: (0,) * arr.ndim)

    def views(d, blk):
        tok = pl.BlockSpec((tb, width), lambda h, j: (blk(j), h))

        def s0_map(h, j):
            seq, _, _ = lay.seq_info(blk(j))
            return (jnp.maximum(seq - lay.n_ctx + 1, 0), d, h, 0, 0)

        def sfin_map(h, j):
            seq, _, _ = lay.seq_info(blk(j))
            return (seq, h, 0, 0)

        ins = [tok, tok, tok, pl.BlockSpec((None, None, HG_HSTEP, HG_K, HG_K), s0_map)]
        return ins, tok, pl.BlockSpec((None, HG_HSTEP, HG_K, HG_K), sfin_map)

    in_f, o_f, sf_f = views(0, lambda j: j)
    in_b, o_b, sf_b = views(1, lambda j: lay.nb - 1 - j)
    return pl.pallas_call(
        functools.partial(_hgrn_kernel, lay=lay),
        grid=(nh // HG_HSTEP, lay.nb),
        in_specs=in_f + in_b + [full(cum_m), full(masks)],
        out_specs=[o_f, o_b, sf_f, sf_b],
        out_shape=[jax.ShapeDtypeStruct((n, dm), F32), jax.ShapeDtypeStruct((n, dm), F32),
                   jax.ShapeDtypeStruct((lay.n_seq, nh, HG_K, HG_K), F32),
                   jax.ShapeDtypeStruct((lay.n_seq, nh, HG_K, HG_K), F32)],
        scratch_shapes=[pltpu.VMEM((HG_HSTEP, HG_K, HG_K), F32), pltpu.VMEM((HG_HSTEP, HG_K, HG_K), F32)],
        compiler_params=_cparams(2), name="hgrn_scan",
    )(q, f_fwd, iv, s0t, q, f_bwd, iv, s0t, cum_m, masks)


HG_POST_HEADS = 4


def _hgrn_post_kernel(of_ref, ob_ref, g_ref, nw_ref, z_ref):
    for h in range(HG_POST_HEADS):
        cs = slice(h * HG_K, (h + 1) * HG_K)
        o = of_ref[:, cs] + ob_ref[:, cs]
        o = o * lax.rsqrt(jnp.mean(o * o, axis=-1, keepdims=True) + NORM_EPS) * nw_ref[...] * g_ref[:, cs]
        z_ref[:, cs] = o.astype(z_ref.dtype)


def hgrn_post(lay, o_f, o_b, gs, nw):
    n, dm = o_f.shape
    tm = lay.tile(1024)
    width = HG_POST_HEADS * HG_K
    tok = pl.BlockSpec((tm, width), lambda i, h: (i, h))
    return pl.pallas_call(
        _hgrn_post_kernel,
        grid=(n // tm, dm // width),
        in_specs=[tok, tok, tok, pl.BlockSpec((1, HG_K), lambda i, h: (0, 0))],
        out_specs=tok,
        out_shape=jax.ShapeDtypeStruct((n, dm), BF16),
        compiler_params=_cparams(2), name="hgrn_post",
    )(o_f, o_b, gs, nw)


def _epi_hgrn_in(accs, extras):
    lb = extras[0]
    q = _silu(accs[0])
    f0 = lb + (1.0 - lb) * jax.nn.sigmoid(accs[1])
    f1 = lb + (1.0 - lb) * jax.nn.sigmoid(accs[2])
    return [q, f0, f1, accs[3], _silu(accs[4])]


def hgrn_layer(lay, x, mods, nw, p, lb, s0t):
    n, d = x.shape
    sh1, sc1, g1 = mods[0], mods[1], mods[2]
    h = norm_mod(lay, x, nw, sc1, sh1, BF16)
    tm, tn = lay.tile(1024), 256
    extras = ((lb, pl.BlockSpec((1, tn), lambda i, j: (0, j))),)
    q, f0, f1, iv, gs = matmul(h, p['w_in'], _epi_hgrn_in, [F32] * 5, tm=tm, tn=tn, extras=extras,
                               name="hgrn_in")
    o_f, o_b, sfin_f, sfin_b = hgrn_scan(lay, q, f0, f1, iv, s0t)
    z = hgrn_post(lay, o_f, o_b, gs, p['norm_w'])
    x = matmul_gated_residual(lay, z, p['wo'], x, g1, tm=lay.tile(1024), tn=512, name="hgrn_o")
    return x, jnp.stack([sfin_f, sfin_b], axis=1)


ML_DOWN_COLS = 1280
ML_KR_OFF = ML_Q_LORA + ML_KV_LORA
ML_KRS_OFF = ML_KR_OFF + LANES_V7X


def _rms(x, w):
    return x * lax.rsqrt(jnp.mean(x * x, axis=-1, keepdims=True) + NORM_EPS) * w


def _mla_mid_kernel(dn_ref, qw_ref, kvw_ref, cos_ref, sin_ref, qn_ref, ckv_ref, kr_ref):
    dn = dn_ref[...]
    qn_ref[...] = _rms(dn[:, :ML_Q_LORA], qw_ref[...]).astype(qn_ref.dtype)
    ckv_ref[...] = _rms(dn[:, ML_Q_LORA:ML_KR_OFF], kvw_ref[...])
    kr = dn[:, ML_KR_OFF:ML_KR_OFF + ML_ROPE]
    krs = dn[:, ML_KRS_OFF:ML_KRS_OFF + ML_ROPE]
    kr_ref[...] = kr * cos_ref[...] + krs * sin_ref[...]


def mla_mid(lay, dn, qw, kvw, cos, sin):
    n = dn.shape[0]
    tm = lay.tile(512)
    return pl.pallas_call(
        _mla_mid_kernel,
        grid=(n // tm,),
        in_specs=[pl.BlockSpec((tm, ML_DOWN_COLS), lambda i: (i, 0)),
                  pl.BlockSpec((1, ML_Q_LORA), lambda i: (0, 0)),
                  pl.BlockSpec((1, ML_KV_LORA), lambda i: (0, 0)),
                  pl.BlockSpec((tm, ML_ROPE), lambda i: (i, 0)),
                  pl.BlockSpec((tm, ML_ROPE), lambda i: (i, 0))],
        out_specs=[pl.BlockSpec((tm, ML_Q_LORA), lambda i: (i, 0)),
                   pl.BlockSpec((tm, ML_KV_LORA), lambda i: (i, 0)),
                   pl.BlockSpec((tm, ML_ROPE), lambda i: (i, 0))],
        out_shape=[jax.ShapeDtypeStruct((n, ML_Q_LORA), BF16),
                   jax.ShapeDtypeStruct((n, ML_KV_LORA), F32),
                   jax.ShapeDtypeStruct((n, ML_ROPE), F32)],
        compiler_params=_cparams(1), name="mla_mid",
    )(dn, qw, kvw, cos, sin)


ML_QSCALE = math.log2(math.e) / math.sqrt(ML_NOPE + ML_ROPE)


def _epi_qscale(accs, extras):
    return [accs[0] * ML_QSCALE]


def _epi_rope(accs, extras):
    cos, sin = extras
    return [(accs[0] * cos + accs[1] * sin) * ML_QSCALE]


def _attn_kernel(qn_ref, qr_ref, kn_ref, kr_ref, v_ref, o_ref, kc_scr):
    @pl.when(pl.program_id(2) == 0)
    def _():
        for h in range(2):
            kc_scr[h, :, :ML_NOPE] = kn_ref[:, h * ML_NOPE:(h + 1) * ML_NOPE]
            kc_scr[h, :, ML_NOPE:] = kr_ref[...]

    scores = []
    for h in range(2):
        q = jnp.concatenate([qn_ref[:, h * ML_NOPE:(h + 1) * ML_NOPE],
                             qr_ref[:, h * ML_ROPE:(h + 1) * ML_ROPE]], axis=1)
        scores.append(lax.dot_general(q, kc_scr[h], (((1,), (1,)), ((), ())), preferred_element_type=F32))
    outs = []
    for h, s in enumerate(scores):
        m = jnp.max(s, axis=-1, keepdims=True)
        p = jnp.exp2(s - m)
        l = jnp.sum(p, axis=-1, keepdims=True)
        o = jnp.dot(p.astype(BF16), v_ref[:, h * ML_V:(h + 1) * ML_V], preferred_element_type=F32)
        outs.append(o / l)
    o_ref[...] = jnp.concatenate(outs, axis=1).astype(o_ref.dtype)


def attention(qn, qr, kn, kr, v, *, n_seq, q_len, k_len, row0, tq):
    heads2 = qn.shape[1] // (2 * ML_NOPE)
    qb = q_len // tq
    rb0 = row0 // tq
    return pl.pallas_call(
        _attn_kernel,
        grid=(n_seq, heads2, qb),
        in_specs=[pl.BlockSpec((tq, 2 * ML_NOPE), lambda s, h, i: (rb0 + s * qb + i, h)),
                  pl.BlockSpec((tq, 2 * ML_ROPE), lambda s, h, i: (rb0 + s * qb + i, h)),
                  pl.BlockSpec((k_len, 2 * ML_NOPE), lambda s, h, i: (s, h)),
                  pl.BlockSpec((k_len, ML_ROPE), lambda s, h, i: (s, 0)),
                  pl.BlockSpec((k_len, 2 * ML_V), lambda s, h, i: (s, h))],
        out_specs=pl.BlockSpec((tq, 2 * ML_V), lambda s, h, i: (s * qb + i, h)),
        out_shape=jax.ShapeDtypeStruct((n_seq * q_len, heads2 * 2 * ML_V), BF16),
        scratch_shapes=[pltpu.VMEM((2, k_len, ML_NOPE + ML_ROPE), BF16)],
        compiler_params=_cparams(3), name="mla_attn",
    )(qn, qr, kn, kr, v)


def mla_layer(lay, x, mods, nw, p, cache_ckv, cache_kr, cos, sin, cos2, sin2):
    n, d = x.shape
    sh1, sc1, g1 = mods[0], mods[1], mods[2]
    h = norm_mod(lay, x, nw, sc1, sh1, BF16)
    tm = lay.tile(512)
    dn = matmul(h, [p['w_down']], _epi_plain, [F32], tm=tm, tn=ML_DOWN_COLS, name="mla_down")[0]
    qlat, ckv, kr = mla_mid(lay, dn, p['qnorm_w'], p['kvnorm_w'], cos, sin)
    qn = matmul(qlat, [p['w_uq_nope']], _epi_qscale, [BF16], tm=tm, tn=512, name="mla_qn")[0]
    tn = 2 * ML_ROPE
    extras = ((cos2, pl.BlockSpec((tm, tn), lambda i, j: (i, 0))),
              (sin2, pl.BlockSpec((tm, tn), lambda i, j: (i, 0))))
    qr = matmul(qlat, [p['w_uq_rope'], p['w_uq_rope_sw']], _epi_rope, [BF16], tm=tm, tn=tn,
                extras=extras, name="mla_qr")[0]
    nc, past = lay.nc, cache_ckv.shape[1]
    ckv_b, kr_b = ckv.astype(BF16), kr.astype(BF16)
    kn_c, v_c = matmul(ckv_b[:nc], [p['w_ukn'], p['w_uv']], _epi_plain, [BF16, BF16],
                       tm=lay.tile(512), tn=512, name="mla_kv_ctx")
    o_c = attention(qn, qr, kn_c, kr_b[:nc], v_c, n_seq=lay.n_ctx, q_len=lay.ctx_len,
                    k_len=lay.ctx_len, row0=0, tq=min(256, lay.ctx_len))
    k_len = lay.lat_len + past
    ckv_l = jnp.concatenate([ckv_b[nc:].reshape(lay.n_lat, lay.lat_len, -1), cache_ckv.astype(BF16)],
                            axis=1).reshape(lay.n_lat * k_len, -1)
    kr_l = jnp.concatenate([kr_b[nc:].reshape(lay.n_lat, lay.lat_len, -1), cache_kr.astype(BF16)],
                           axis=1).reshape(lay.n_lat * k_len, -1)
    tk = math.gcd(k_len, 512)
    kn_l, v_l = matmul(ckv_l, [p['w_ukn'], p['w_uv']], _epi_plain, [BF16, BF16], tm=tk, tn=512,
                       name="mla_kv_lat")
    o_l = attention(qn, qr, kn_l, kr_l, v_l, n_seq=lay.n_lat, q_len=lay.lat_len, k_len=k_len,
                    row0=nc, tq=min(256, lay.lat_len))
    o = jnp.concatenate([o_c, o_l], axis=0)
    x = matmul_gated_residual(lay, o, p['wo'], x, g1, tm=lay.tile(1024), tn=512, name="mla_o")
    return x, ckv[:nc], kr[:nc]


def _rope_tables(lay):
    t = lay.lat_len
    rows = t // GRID_W
    rr = jnp.broadcast_to(jnp.arange(rows, dtype=F32)[:, None], (rows, GRID_W)).reshape(-1)
    cc = jnp.broadcast_to(jnp.arange(GRID_W, dtype=F32)[None, :], (rows, GRID_W)).reshape(-1)
    nf = ML_ROPE // 4
    inv = ROPE_BASE ** (-jnp.arange(nf, dtype=F32) / nf)
    ar, ac = rr[:, None] * inv, cc[:, None] * inv
    cos = jnp.concatenate([jnp.cos(ar), jnp.cos(ar), jnp.cos(ac), jnp.cos(ac)], axis=-1)
    sin = jnp.concatenate([-jnp.sin(ar), jnp.sin(ar), -jnp.sin(ac), jnp.sin(ac)], axis=-1)
    cos = jnp.concatenate([jnp.ones((lay.nc, ML_ROPE), F32), jnp.tile(cos, (lay.n_lat, 1))], axis=0)
    sin = jnp.concatenate([jnp.zeros((lay.nc, ML_ROPE), F32), jnp.tile(sin, (lay.n_lat, 1))], axis=0)
    return cos, sin


def _swap_cols(w):
    k, c = w.shape
    w4 = w.reshape(k, c // 32, 2, 16)
    return w4[:, :, ::-1, :].reshape(k, c)


def ffn(lay, x, mods, nw, w_a, w_b, w_out):
    sh2, sc2, g2 = mods[3], mods[4], mods[5]
    h = norm_mod(lay, x, nw, sc2, sh2, BF16)
    act = matmul(h, [w_a, w_b], _epi_swiglu, [BF16], tm=lay.tile(1024), tn=512, name="ffn_in")[0]
    return matmul_gated_residual(lay, act, w_out, x, g2, tm=lay.tile(1024), tn=512, name="ffn_out")


def _block_diag_states(s):
    b, two, h, n, _ = s.shape
    s = s.reshape(b, two, h // RW_GROUP, RW_GROUP, n, n)
    eye = jnp.eye(RW_GROUP, dtype=s.dtype)
    out = jnp.einsum('bdghvk,hi->bdghvik', s, eye)
    return out.reshape(b, two, h // RW_GROUP, RW_GROUP * n, RW_GROUP * n)


def _diag_blocks(s):
    b, two, g, l, _ = s.shape
    n = l // RW_GROUP
    s = s.reshape(b, two, g, RW_GROUP, n, RW_GROUP, n)
    s = jnp.moveaxis(jnp.diagonal(s, axis1=3, axis2=5), -1, 3)
    return s.reshape(b, two, g * RW_GROUP, n, n)


def kernel(x_prompt, x_sample, state_rwkv, state_hgrn, cache_ckv, cache_krope, c, c_ctx, ada_w, ada_b, norm1_w, norm2_w, ffn_w_in, ffn_w_out, final_norm_w, rw_mu, rw_wr, rw_wk, rw_wv, rw_wo, rw_w0, rw_w1, rw_w2, rw_a0, rw_a1, rw_a2, rw_g1, rw_g2, rw_kk, rw_ka, rw_rk, rw_lnx_w, rw_lnx_b, hg_w_in, hg_lb, hg_norm_w, hg_wo, ml_w_down, ml_qnorm_w, ml_kvnorm_w, ml_w_uq, ml_w_ukv, ml_wo):
    n_ctx, ctx_len, d = x_prompt.shape
    n_lat, lat_len, _ = x_sample.shape
    depth = ada_w.shape[0]
    lay = Layout(n_ctx, ctx_len, n_lat, lat_len)
    d_ff = ffn_w_out.shape[1]
    x = jnp.concatenate([x_prompt.reshape(lay.nc, d), x_sample.reshape(n_lat * lat_len, d)], axis=0)

    n_cond = -(-(1 + n_lat) // SUBLANES_V7X) * SUBLANES_V7X
    cond = jnp.zeros((n_cond, d), F32).at[0].set(c_ctx).at[1:1 + n_lat].set(c)
    mod_all = adaln(cond, ada_w, ada_b)
    mod_all = mod_all.reshape(depth, n_cond, 6, 1, d).transpose(0, 2, 1, 3, 4)

    lb_table = jnp.cumsum(jax.nn.softmax(hg_lb.astype(F32), axis=0), axis=0)
    lb_table = lb_table - lb_table[0]
    cos, sin = _rope_tables(lay)
    cos2, sin2 = jnp.tile(cos, (1, 2)), jnp.tile(sin, (1, 2))
    bf = lambda t: t.astype(BF16)

    new_rwkv, new_hgrn, new_ckv, new_krope = [], [], [], []
    for l in range(depth):
        kind, j = l % 3, l // 3
        mods = mod_all[l]
        if kind == 0:
            pad1 = lambda w: jnp.pad(w, ((0, 0), (0, 0), (0, LORA_PAD - w.shape[2])))
            pad2 = lambda w: jnp.pad(w, ((0, 0), (0, LORA_PAD - w.shape[1]), (0, 0)))
            p = {'mu': rw_mu[j], 'wr': bf(rw_wr[j]), 'wk': bf(rw_wk[j]), 'wv': bf(rw_wv[j]),
                 'wo': bf(rw_wo[j]),
                 'w0': rw_w0[j].reshape(2, 1, d), 'w1': bf(pad1(rw_w1[j])), 'w2': bf(pad2(rw_w2[j])),
                 'a0': rw_a0[j].reshape(2, 1, d), 'a1': bf(pad1(rw_a1[j])), 'a2': bf(pad2(rw_a2[j])),
                 'g1': bf(rw_g1[j]), 'g2': bf(rw_g2[j]),
                 'kk': rw_kk[j].reshape(1, d), 'ka': rw_ka[j].reshape(1, d),
                 'rk': rw_rk[j].reshape(2, 1, d),
                 'lnx_w': rw_lnx_w[j].reshape(1, d), 'lnx_b': rw_lnx_b[j].reshape(1, d)}
            s_lat = _block_diag_states(state_rwkv[:, j].astype(F32))
            s0 = jnp.concatenate([jnp.zeros((1,) + s_lat.shape[1:], F32), s_lat], axis=0)
            x, sfin = rwkv_layer(lay, x, mods, norm1_w[l], p, s0)
            new_rwkv.append(_diag_blocks(sfin[:n_ctx]))
        elif kind == 1:
            hk = d
            w_in = hg_w_in[j]
            p = {'w_in': [bf(w_in[:, i * hk:(i + 1) * hk]) for i in range(5)],
                 'norm_w': hg_norm_w[j].reshape(1, HG_K), 'wo': bf(hg_wo[j])}
            s_lat = jnp.swapaxes(state_hgrn[:, j].astype(F32), -1, -2)
            s0t = jnp.concatenate([jnp.zeros((1,) + s_lat.shape[1:], F32), s_lat], axis=0)
            x, sfin = hgrn_layer(lay, x, mods, norm1_w[l], p, lb_table[l].reshape(1, d), s0t)
            new_hgrn.append(jnp.swapaxes(sfin[:n_ctx], -1, -2))
        else:
            wd = ml_w_down[j]
            kr_w = wd[:, ML_KR_OFF:]
            zpad = jnp.zeros((d, LANES_V7X - ML_ROPE), wd.dtype)
            w_down = jnp.concatenate([wd, zpad, _swap_cols(kr_w), zpad], axis=1)
            wq = ml_w_uq[j].reshape(ML_Q_LORA, ML_H, ML_NOPE + ML_ROPE)
            wq_n = wq[:, :, :ML_NOPE].reshape(ML_Q_LORA, ML_H * ML_NOPE)
            wq_r = wq[:, :, ML_NOPE:].reshape(ML_Q_LORA, ML_H * ML_ROPE)
            wkv = ml_w_ukv[j].reshape(ML_KV_LORA, ML_H, ML_NOPE + ML_V)
            p = {'w_down': bf(w_down), 'qnorm_w': ml_qnorm_w[j].reshape(1, -1),
                 'kvnorm_w': ml_kvnorm_w[j].reshape(1, -1),
                 'w_uq_nope': bf(wq_n), 'w_uq_rope': bf(wq_r), 'w_uq_rope_sw': bf(_swap_cols(wq_r)),
                 'w_ukn': bf(wkv[:, :, :ML_NOPE].reshape(ML_KV_LORA, ML_H * ML_NOPE)),
                 'w_uv': bf(wkv[:, :, ML_NOPE:].reshape(ML_KV_LORA, ML_H * ML_V)),
                 'wo': bf(ml_wo[j])}
            x, ckv_c, kr_c = mla_layer(lay, x, mods, norm1_w[l], p, cache_ckv[:, j], cache_krope[:, j],
                                       cos, sin, cos2, sin2)
            new_ckv.append(ckv_c.reshape(n_ctx, ctx_len, ML_KV_LORA))
            new_krope.append(kr_c.reshape(n_ctx, ctx_len, ML_ROPE))
        w_in = ffn_w_in[l]
        x = ffn(lay, x, mods, norm2_w[l], bf(w_in[:, :d_ff]), bf(w_in[:, d_ff:]), bf(ffn_w_out[l]))

    y_prompt = rmsnorm_rows(lay, x, final_norm_w, 0, lay.nc).reshape(n_ctx, ctx_len, d)
    y_sample = rmsnorm_rows(lay, x, final_norm_w, lay.nc, lay.n - lay.nc).reshape(n_lat, lat_len, d)
    return (y_prompt, y_sample, jnp.stack(new_rwkv, axis=1), jnp.stack(new_hgrn, axis=1),
            jnp.stack(new_ckv, axis=1), jnp.stack(new_krope, axis=1))
```

```python
import functools
import math

import numpy as np
import jax
import jax.numpy as jnp
from jax import lax
from jax.experimental import pallas as pl
from jax.experimental.pallas import tpu as pltpu

F32 = jnp.float32
BF16 = jnp.bfloat16

LANES_V7X = 128
SUBLANES_V7X = 8
VMEM_BYTES_V7X = 64 * 1024 * 1024
VMEM_LIMIT = 56 * 1024 * 1024

NORM_EPS = 1e-6
RW_HEAD = 64
RW_LN_EPS = 64e-5
RW_GROUP = 4
RW_LANES = RW_GROUP * RW_HEAD
RW_CHUNK = 64
RW_GSTEP = 2
HG_K = 128
HG_CHUNK_TOKENS = 64
HG_HSTEP = 2
ML_H = 16
ML_NOPE = 128
ML_ROPE = 64
ML_V = 128
ML_Q_LORA = 512
ML_KV_LORA = 512
GRID_W = 64
ROPE_BASE = 10000.0
LORA_PAD = 128


class Layout:
    def __init__(self, n_ctx, ctx_len, n_lat, lat_len):
        self.n_ctx, self.ctx_len, self.n_lat, self.lat_len = n_ctx, ctx_len, n_lat, lat_len
        self.nc = n_ctx * ctx_len
        self.n = self.nc + n_lat * lat_len
        self.tb = min(256, ctx_len)
        assert ctx_len % self.tb == 0 and lat_len % self.tb == 0 and self.tb % RW_CHUNK == 0
        self.nb = self.n // self.tb
        self.nb_ctx = self.nc // self.tb
        self.bps_ctx = ctx_len // self.tb
        self.bps_lat = lat_len // self.tb
        self.n_seq = n_ctx + n_lat

    def tile(self, want):
        t = want
        while self.nc % t or self.lat_len % t:
            t //= 2
        return t

    def cond_of_tile(self, i, tm):
        row = i * tm
        return jnp.where(row < self.nc, 0, 1 + (row - self.nc) // self.lat_len)

    def seq_info(self, blk):
        is_ctx = blk < self.nb_ctx
        lat = blk - self.nb_ctx
        seq = jnp.where(is_ctx, blk // self.bps_ctx, self.n_ctx + lat // self.bps_lat)
        pos = jnp.where(is_ctx, blk % self.bps_ctx, lat % self.bps_lat)
        cnt = jnp.where(is_ctx, self.bps_ctx, self.bps_lat)
        return seq, pos, cnt


def _cparams(n_axes):
    return pltpu.CompilerParams(dimension_semantics=("arbitrary",) * n_axes, vmem_limit_bytes=VMEM_LIMIT)


def _bdot(a, b):
    return jnp.dot(a.astype(BF16), b.astype(BF16), preferred_element_type=F32)


def _bdot_nt(a, b):
    return lax.dot_general(a.astype(BF16), b.astype(BF16), (((1,), (1,)), ((), ())),
                           preferred_element_type=F32)


def _silu(x):
    return x * jax.nn.sigmoid(x)


def _adaln_kernel(c_ref, w_ref, b_ref, o_ref):
    a = _silu(c_ref[...]).astype(BF16)
    o_ref[...] = jnp.dot(a, w_ref[...].astype(BF16), preferred_element_type=F32) + b_ref[...]


def adaln(cond, ada_w, ada_b):
    depth, d, d6 = ada_w.shape
    r = cond.shape[0]
    tn = 1024
    return pl.pallas_call(
        _adaln_kernel,
        grid=(depth, d6 // tn),
        in_specs=[pl.BlockSpec((r, d), lambda l, j: (0, 0)),
                  pl.BlockSpec((None, d, tn), lambda l, j: (l, 0, j)),
                  pl.BlockSpec((None, 1, tn), lambda l, j: (l, 0, j))],
        out_specs=pl.BlockSpec((None, r, tn), lambda l, j: (l, 0, j)),
        out_shape=jax.ShapeDtypeStruct((depth, r, d6), F32),
        compiler_params=_cparams(2), name="adaln",
    )(cond, ada_w, ada_b.reshape(depth, 1, d6))


def _norm_mod(x, nw, sc, sh):
    y = x * lax.rsqrt(jnp.mean(x * x, axis=-1, keepdims=True) + NORM_EPS)
    return (y * nw) * (1.0 + sc) + sh


def _norm_mod_kernel(x_ref, nw_ref, sc_ref, sh_ref, o_ref):
    o_ref[...] = _norm_mod(x_ref[...], nw_ref[...], sc_ref[...], sh_ref[...]).astype(o_ref.dtype)


def norm_mod(lay, x, nw, sc, sh, out_dtype):
    n, d = x.shape
    tm = lay.tile(512)
    cmap = lambda i: (lay.cond_of_tile(i, tm), 0, 0)
    return pl.pallas_call(
        _norm_mod_kernel,
        grid=(n // tm,),
        in_specs=[pl.BlockSpec((tm, d), lambda i: (i, 0)),
                  pl.BlockSpec((1, d), lambda i: (0, 0)),
                  pl.BlockSpec((None, 1, d), cmap),
                  pl.BlockSpec((None, 1, d), cmap)],
        out_specs=pl.BlockSpec((tm, d), lambda i: (i, 0)),
        out_shape=jax.ShapeDtypeStruct((n, d), out_dtype),
        compiler_params=_cparams(1), name="norm_mod",
    )(x, nw.reshape(1, d), sc, sh)


def _rmsnorm_kernel(x_ref, w_ref, o_ref):
    x = x_ref[...]
    o_ref[...] = x * lax.rsqrt(jnp.mean(x * x, axis=-1, keepdims=True) + NORM_EPS) * w_ref[...]


def rmsnorm_rows(lay, x, w, row0, rows):
    d = x.shape[1]
    tm = lay.tile(512)
    b0 = row0 // tm
    return pl.pallas_call(
        _rmsnorm_kernel,
        grid=(rows // tm,),
        in_specs=[pl.BlockSpec((tm, d), lambda i: (b0 + i, 0)), pl.BlockSpec((1, d), lambda i: (0, 0))],
        out_specs=pl.BlockSpec((tm, d), lambda i: (i, 0)),
        out_shape=jax.ShapeDtypeStruct((rows, d), x.dtype),
        compiler_params=_cparams(1), name="final_norm",
    )(x, w.reshape(1, d))


def _mm_kernel(*refs, n_w, n_e, epi):
    a = refs[0][...]
    accs = [jnp.dot(a, refs[1 + i][...], preferred_element_type=F32) for i in range(n_w)]
    extras = [refs[1 + n_w + i][...] for i in range(n_e)]
    outs = epi(accs, extras)
    o_refs = refs[1 + n_w + n_e:]
    for o_ref, val in zip(o_refs, outs):
        o_ref[...] = val.astype(o_ref.dtype)


def matmul(a, ws, epi, out_dtypes, *, tm, tn, extras=(), name):
    m, k = a.shape
    nw = ws[0].shape[1]
    assert m % tm == 0 and nw % tn == 0
    in_specs = [pl.BlockSpec((tm, k), lambda i, j: (i, 0))]
    in_specs += [pl.BlockSpec((k, tn), lambda i, j: (0, j)) for _ in ws]
    in_specs += [spec for _, spec in extras]
    outs = pl.pallas_call(
        functools.partial(_mm_kernel, n_w=len(ws), n_e=len(extras), epi=epi),
        grid=(m // tm, nw // tn),
        in_specs=in_specs,
        out_specs=[pl.BlockSpec((tm, tn), lambda i, j: (i, j)) for _ in out_dtypes],
        out_shape=[jax.ShapeDtypeStruct((m, nw), dt) for dt in out_dtypes],
        compiler_params=_cparams(2), name=name,
    )(a, *ws, *[arr for arr, _ in extras])
    return outs


def _epi_plain(accs, extras):
    return accs


def _epi_gated_residual(accs, extras):
    x, g = extras
    return [x + g * accs[0]]


def matmul_gated_residual(lay, a, w, x, gate, *, tm, tn, name):
    extras = ((x, pl.BlockSpec((tm, tn), lambda i, j: (i, j))),
              (gate, pl.BlockSpec((None, 1, tn), lambda i, j: (lay.cond_of_tile(i, tm), 0, j))))
    return matmul(a, [w], _epi_gated_residual, [F32], tm=tm, tn=tn, extras=extras, name=name)[0]


def _epi_swiglu(accs, extras):
    return [_silu(accs[0]) * accs[1]]


RWKV_PREP_SLAB = LANES_V7X


def _rwkv_prep_kernel(x_ref, xp_ref, xn_ref, nw_ref, sc_ref, sh_ref, mu_ref, *o_refs, lay):
    i = pl.program_id(0)
    _, pos, cnt = lay.seq_info(i)
    def inv_rms(x):
        return lax.rsqrt(jnp.mean(x * x, axis=-1, keepdims=True) + NORM_EPS)

    tb, d = x_ref.shape
    inv = inv_rms(x_ref[...])
    inv_p = inv_rms(xp_ref[SUBLANES_V7X - 1:SUBLANES_V7X, :])
    inv_n = inv_rms(xn_ref[0:1, :])
    keep_p = jnp.where(pos == 0, 0.0, 1.0)
    keep_n = jnp.where(pos == cnt - 1, 0.0, 1.0)
    row = lax.broadcasted_iota(jnp.int32, (tb, RWKV_PREP_SLAB), 0)
    for c0 in range(0, d, RWKV_PREP_SLAB):
        cs = slice(c0, c0 + RWKV_PREP_SLAB)
        nw, sc, sh = nw_ref[:, cs], sc_ref[:, cs], sh_ref[:, cs]
        h = ((x_ref[:, cs] * inv) * nw) * (1.0 + sc) + sh
        hp = (((xp_ref[SUBLANES_V7X - 1:SUBLANES_V7X, cs] * inv_p) * nw) * (1.0 + sc) + sh) * keep_p
        hn = (((xn_ref[0:1, cs] * inv_n) * nw) * (1.0 + sc) + sh) * keep_n
        prev = jnp.where(row == 0, hp, pltpu.roll(h, 1, axis=0))
        nxt = jnp.where(row == tb - 1, hn, pltpu.roll(h, tb - 1, axis=0))
        xx = 0.5 * (prev + nxt) - h
        for idx, o_ref in enumerate(o_refs):
            o_ref[:, cs] = (h + xx * mu_ref[idx:idx + 1, cs]).astype(o_ref.dtype)


def rwkv_prep(lay, x, nw, sc, sh, mu):
    n, d = x.shape
    tb = lay.tb
    r8 = tb // SUBLANES_V7X
    last8 = n // SUBLANES_V7X - 1
    cmap = lambda i: (lay.cond_of_tile(i, tb), 0, 0)
    return pl.pallas_call(
        functools.partial(_rwkv_prep_kernel, lay=lay),
        grid=(n // tb,),
        in_specs=[pl.BlockSpec((tb, d), lambda i: (i, 0)),
                  pl.BlockSpec((SUBLANES_V7X, d), lambda i: (jnp.maximum(i * r8 - 1, 0), 0)),
                  pl.BlockSpec((SUBLANES_V7X, d), lambda i: (jnp.minimum((i + 1) * r8, last8), 0)),
                  pl.BlockSpec((1, d), lambda i: (0, 0)),
                  pl.BlockSpec((None, 1, d), cmap),
                  pl.BlockSpec((None, 1, d), cmap),
                  pl.BlockSpec((6, d), lambda i: (0, 0))],
        out_specs=[pl.BlockSpec((tb, d), lambda i: (i, 0))] * 6,
        out_shape=[jax.ShapeDtypeStruct((n, d), BF16)] * 6,
        compiler_params=_cparams(1), name="rwkv_prep",
    )(x, x, x, nw.reshape(1, d), sc, sh, mu)


RW_LOG_DECAY_SCALE = -math.exp(-0.5)


def _rwkv_lora_kernel(xw_ref, xa_ref, xg_ref, w1_ref, w2_ref, w0_ref, a1_ref, a2_ref, a0_ref,
                      g1_ref, g2_ref, lw_ref, a_ref, g_ref):
    xw, xa, xg = xw_ref[...], xa_ref[...], xg_ref[...]
    for d in range(2):
        t = jnp.tanh(jnp.dot(xw, w1_ref[d], preferred_element_type=F32))
        wl = w0_ref[d] + _bdot(t, w2_ref[d])
        lw_ref[d] = RW_LOG_DECAY_SCALE * jax.nn.sigmoid(wl)
        t = jnp.dot(xa, a1_ref[d], preferred_element_type=F32)
        a_ref[d] = jax.nn.sigmoid(a0_ref[d] + _bdot(t, a2_ref[d]))
    t = jax.nn.sigmoid(jnp.dot(xg, g1_ref[...], preferred_element_type=F32))
    g_ref[...] = _bdot(t, g2_ref[...])


def rwkv_lora(lay, xw, xa, xg, w1, w2, w0, a1, a2, a0, g1, g2):
    n, d = xw.shape
    tm = lay.tile(256)
    full = lambda arr: pl.BlockSpec(arr.shape, lambda i: (0,) * arr.ndim)
    row = pl.BlockSpec((tm, d), lambda i: (i, 0))
    return pl.pallas_call(
        _rwkv_lora_kernel,
        grid=(n // tm,),
        in_specs=[row, row, row] + [full(t) for t in (w1, w2, w0, a1, a2, a0, g1, g2)],
        out_specs=[pl.BlockSpec((2, tm, d), lambda i: (0, i, 0)),
                   pl.BlockSpec((2, tm, d), lambda i: (0, i, 0)),
                   row],
        out_shape=[jax.ShapeDtypeStruct((2, n, d), F32), jax.ShapeDtypeStruct((2, n, d), F32),
                   jax.ShapeDtypeStruct((n, d), F32)],
        compiler_params=_cparams(1), name="rwkv_lora",
    )(xw, xa, xg, w1, w2, w0, a1, a2, a0, g1, g2)


def _wkv_constants():
    c, g, hd = RW_CHUNK, RW_GROUP, RW_HEAD
    gc, lanes = g * c, g * hd
    t = np.arange(c)
    cum = np.stack([(t[None, :] <= t[:, None]), (t[None, :] >= t[:, None])])
    tr = np.arange(c)[:, None]
    tc = np.arange(gc)[None, :] % c
    strict = np.stack([tc < tr, tc > tr])
    incl = np.stack([tc <= tr, tc >= tr])
    head_rows = np.arange(gc)[:, None] // c == np.arange(lanes)[None, :] // hd
    blk_rows = np.arange(gc)[:, None] // c == np.arange(gc)[None, :] // c
    bd = np.arange(lanes)[:, None] // hd == np.arange(lanes)[None, :] // hd
    eye_w = tr == tc
    return dict(cum=jnp.asarray(cum, BF16), strict=jnp.asarray(strict, F32), incl=jnp.asarray(incl, F32),
                head_rows=jnp.asarray(head_rows, BF16), blk_rows=jnp.asarray(blk_rows, BF16),
                bd=jnp.asarray(bd, F32), bd_b=jnp.asarray(bd, BF16), eye_w=jnp.asarray(eye_w, F32))


def _split_dot(m01, x, passes):
    acc, rem = None, x
    for _ in range(passes):
        part = rem.astype(BF16)
        term = jnp.dot(m01, part, preferred_element_type=F32)
        acc = term if acc is None else acc + term
        rem = rem - part.astype(F32)
    return acc


def _split_dot_r(x, m01, passes):
    acc, rem = None, x
    for _ in range(passes):
        part = rem.astype(BF16)
        term = jnp.dot(part, m01, preferred_element_type=F32)
        acc = term if acc is None else acc + term
        rem = rem - part.astype(F32)
    return acc


def _tile_rows(x, mask_b):
    return jnp.concatenate([x.astype(BF16)] * RW_GROUP, axis=0) * mask_b


def _wkv_prep1(raw):
    return [_split_dot(u[6], u[5], 3) for u in raw]


def _wkv_prep2(raw, cums, head_rows):
    c = RW_CHUNK
    mids = []
    for (r, v, kk, kd, a, lw, _, _, _), cum in zip(raw, cums):
        tot = jnp.sum(lw, axis=0, keepdims=True)
        kka = kk * a
        e_inv, e_rest = jnp.exp(-cum), jnp.exp(tot - cum)
        q2 = jnp.concatenate([kk * jnp.exp(cum - lw), r * jnp.exp(cum)], axis=0).astype(BF16)
        mids.append(dict(q2=q2, kdh=_tile_rows(kd * e_inv, head_rows), kkah=_tile_rows(kka * e_inv, head_rows),
                         vbd=_tile_rows(v, head_rows), v=v, decay=jnp.exp(tot),
                         kw=jnp.concatenate([kd * e_rest, -(kka * e_rest)], axis=0).astype(BF16)))
    s1s = [_bdot_nt(m['q2'], m['kdh']) for m in mids]
    s2s = [_bdot_nt(m['q2'][c:], m['kkah']) for m in mids]
    for m, u, s1, s2 in zip(mids, raw, s1s, s2s):
        strict_w, incl_w = u[7], u[8]
        m['lad'] = jnp.concatenate([jnp.where(strict_w > 0, s1[:c], 0.0),
                                    jnp.where(incl_w > 0, s1[c:], 0.0)], axis=0).astype(BF16)
        m['a_a'] = jnp.where(incl_w > 0, s2, 0.0).astype(BF16)
    return mids


def _wkv_prep3(mids):
    for m, lav in zip(mids, [_bdot(m['lad'], m['vbd']) for m in mids]):
        m['lav'] = lav
    return mids


def _wkv_adv1(states, preps):
    return [_bdot_nt(p['q2'], s) for p, s in zip(preps, states)]


def _wkv_adv2(p0s, preps, t_ws, head_rows):
    c = RW_CHUNK
    return [_bdot(t_w, _tile_rows(p0[:c] + p['lav'][:c], head_rows))
            for p0, p, t_w in zip(p0s, preps, t_ws)]


def _wkv_adv3(states, p0s, us, preps, head_rows, bd):
    c = RW_CHUNK
    upds = [_bdot(jnp.concatenate([p['v'].astype(F32), u], axis=0).T, p['kw']) for p, u in zip(preps, us)]
    aus = [_bdot(p['a_a'], _tile_rows(u, head_rows)) for p, u in zip(preps, us)]
    new_states = [s * p['decay'] + jnp.where(bd > 0, upd, 0.0) for s, p, upd in zip(states, preps, upds)]
    ys = [p0[c:] + p['lav'][c:] - au for p0, p, au in zip(p0s, preps, aus)]
    return new_states, ys


RKV_TN = RW_LANES


def _rwkv_rkv_kernel(xr_ref, xk_ref, xv_ref, wr_ref, wk_ref, wv_ref, a_ref, kkw_ref, kaw_ref, rk_ref, bd_ref,
                     r_ref, v_ref, kk_ref, kd_ref, b_ref):
    r = jnp.dot(xr_ref[...], wr_ref[...], preferred_element_type=F32)
    k = jnp.dot(xk_ref[...], wk_ref[...], preferred_element_type=F32)
    v = jnp.dot(xv_ref[...], wv_ref[...], preferred_element_type=F32)
    r_ref[...] = r
    v_ref[...] = v.astype(v_ref.dtype)
    bd = bd_ref[...]
    kk = k * kkw_ref[...]
    mix = None
    for d in range(2):
        kd = k * (1.0 + (a_ref[d] - 1.0) * kaw_ref[...])
        kd_ref[d] = kd
        term = kd * rk_ref[d]
        mix = term if mix is None else mix + term
    rm = r * mix
    for h in range(RKV_TN // RW_LANES):
        cs = slice(h * RW_LANES, (h + 1) * RW_LANES)
        kkh = kk[:, cs]
        kk_ref[:, cs] = kkh * lax.rsqrt(_split_dot_r(kkh * kkh, bd, 2) + 1e-12)
        b_ref[:, cs] = _split_dot_r(rm[:, cs], bd, 2) * v[:, cs]


def rwkv_rkv(lay, xr, xk, xv, wr, wk, wv, a, kkw, kaw, rk):
    n, kdim = xr.shape
    dm = wr.shape[1]
    tm, tn = lay.tile(1024), RKV_TN
    bd_b = _wkv_constants()['bd_b']
    lhs = pl.BlockSpec((tm, kdim), lambda i, j: (i, 0))
    rhs = pl.BlockSpec((kdim, tn), lambda i, j: (0, j))
    tok = pl.BlockSpec((tm, tn), lambda i, j: (i, j))
    two = pl.BlockSpec((2, tm, tn), lambda i, j: (0, i, j))
    par = pl.BlockSpec((1, tn), lambda i, j: (0, j))
    return pl.pallas_call(
        _rwkv_rkv_kernel,
        grid=(n // tm, dm // tn),
        in_specs=[lhs, lhs, lhs, rhs, rhs, rhs, two, par, par,
                  pl.BlockSpec((2, 1, tn), lambda i, j: (0, 0, j)),
                  pl.BlockSpec(bd_b.shape, lambda i, j: (0, 0))],
        out_specs=[tok, tok, tok, two, tok],
        out_shape=[jax.ShapeDtypeStruct((n, dm), F32), jax.ShapeDtypeStruct((n, dm), BF16),
                   jax.ShapeDtypeStruct((n, dm), F32), jax.ShapeDtypeStruct((2, n, dm), F32),
                   jax.ShapeDtypeStruct((n, dm), F32)],
        compiler_params=_cparams(2), name="rwkv_rkv",
    )(xr, xk, xv, wr, wk, wv, a, kkw, kaw, rk, bd_b)


def _tinv_begin(items, head_rows, blk_rows, eye_w):
    cums = [_split_dot(it[3], it[2], 3) for it in items]
    ops = [(kk * jnp.exp(cum - lw), _tile_rows(kk * a * jnp.exp(-cum), head_rows))
           for (kk, a, lw, _, _), cum in zip(items, cums)]
    n_ws = [-jnp.where(it[4] > 0, _bdot_nt(q, w), 0.0) for it, (q, w) in zip(items, ops)]
    n_pows = [_bdot(n_w, _tile_rows(n_w, blk_rows)) for n_w in n_ws]
    return [eye_w + n_w for n_w in n_ws], n_pows


def _tinv_level(t_ws, n_pows, last, blk_rows):
    c = RW_CHUNK
    ws = [_tile_rows(n_pow, blk_rows) for n_pow in n_pows]
    if last:
        return [t_w + _bdot(t_w, w) for t_w, w in zip(t_ws, ws)], None
    boths = [_bdot(jnp.concatenate([t_w, n_pow], axis=0), w) for t_w, n_pow, w in zip(t_ws, n_pows, ws)]
    return [t_w + both[:c] for t_w, both in zip(t_ws, boths)], [both[c:] for both in boths]


RW_TINV_LEVELS = int(math.log2(RW_CHUNK))


def _wkv_fused_kernel(*refs, lay):
    (rf, vf, kkf, kdf, af, lwf, s0f, kkfn, afn, lwfn,
     rb, vb, kkb, kdb, ab, lwb, s0b, kkbn, abn, lwbn,
     cum_ref, strict_ref, incl_ref, hr_ref, br_ref, eye_ref, bd_ref,
     yf_ref, yb_ref, sff_ref, sfb_ref, sf_scr, sb_scr, t_scr) = refs
    c = RW_CHUNK
    n_chunks = lay.tb // c
    j = pl.program_id(1)
    cur = lax.rem(j, 2)
    nxt = 1 - cur
    _, pos_f, cnt_f = lay.seq_info(j)
    _, pos_b, cnt_b = lay.seq_info(lay.nb - 1 - j)

    @pl.when(pos_f == 0)
    def _():
        sf_scr[...] = s0f[...]

    @pl.when(pos_b == cnt_b - 1)
    def _():
        sb_scr[...] = s0b[...]

    hr, br, eye_w, bd = hr_ref[...], br_ref[...], eye_ref[...], bd_ref[...]
    sl_f = [slice(ci * c, (ci + 1) * c) for ci in range(n_chunks)]
    sl_b = sl_f[::-1]
    cols = [slice(gi * RW_LANES, (gi + 1) * RW_LANES) for gi in range(RW_GSTEP)]

    def inv_items(kk_f, a_f, lw_f, kk_b, a_b, lw_b, sl):
        out = []
        for cs in cols:
            out.append((kk_f[sl, cs], a_f[sl, cs], lw_f[sl, cs], cum_ref[0], strict_ref[0]))
            out.append((kk_b[sl, cs], a_b[sl, cs], lw_b[sl, cs], cum_ref[1], strict_ref[1]))
        return out

    def inv_store(slot, sl, t_ws):
        for gi, cs in enumerate(cols):
            t_scr[slot, 0, sl, cs] = t_ws[2 * gi].astype(t_scr.dtype)
            t_scr[slot, 1, sl, cs] = t_ws[2 * gi + 1].astype(t_scr.dtype)

    @pl.when(j == 0)
    def _():
        def body(ci, carry):
            sl = pl.ds(pl.multiple_of(ci * c, c), c)
            t_ws, n_pows = _tinv_begin(inv_items(kkf, af, lwf, kkb, ab, lwb, sl), hr, br, eye_w)
            for lv in range(1, RW_TINV_LEVELS):
                t_ws, n_pows = _tinv_level(t_ws, n_pows, lv == RW_TINV_LEVELS - 1, br)
            inv_store(cur, sl, t_ws)
            return carry
        lax.fori_loop(0, n_chunks, body, 0)

    def raw(ci):
        out = []
        for cs in cols:
            sf, sb = sl_f[ci], sl_b[ci]
            out.append((rf[sf, cs], vf[sf, cs], kkf[sf, cs], kdf[sf, cs], af[sf, cs], lwf[sf, cs],
                        cum_ref[0], strict_ref[0], incl_ref[0]))
            out.append((rb[sb, cs], vb[sb, cs], kkb[sb, cs], kdb[sb, cs], ab[sb, cs], lwb[sb, cs],
                        cum_ref[1], strict_ref[1], incl_ref[1]))
        return out

    states = []
    for gi in range(RW_GSTEP):
        states += [sf_scr[gi], sb_scr[gi]]
    raw_n = raw(0)
    preps = _wkv_prep3(_wkv_prep2(raw_n, _wkv_prep1(raw_n), hr))
    inv = {'lv': 0, 't': None, 'n': None}

    def inv_step():
        lv = inv['lv']
        if lv == 0:
            items = []
            for sl in sl_f:
                items += inv_items(kkfn, afn, lwfn, kkbn, abn, lwbn, sl)
            inv['t'], inv['n'] = _tinv_begin(items, hr, br, eye_w)
        elif lv < RW_TINV_LEVELS:
            inv['t'], inv['n'] = _tinv_level(inv['t'], inv['n'], lv == RW_TINV_LEVELS - 1, br)
            if lv == RW_TINV_LEVELS - 1:
                per = 2 * RW_GSTEP
                for ci2, sl in enumerate(sl_f):
                    inv_store(nxt, sl, inv['t'][ci2 * per:(ci2 + 1) * per])
        inv['lv'] = lv + 1

    for ci in range(n_chunks):
        more = ci + 1 < n_chunks
        t_ws = []
        for cs in cols:
            t_ws += [t_scr[cur, 0, sl_f[ci], cs], t_scr[cur, 1, sl_b[ci], cs]]
        p0s = _wkv_adv1(states, preps)
        if more:
            raw_n = raw(ci + 1)
            cums_n = _wkv_prep1(raw_n)
        inv_step()
        us = _wkv_adv2(p0s, preps, t_ws, hr)
        if more:
            mids_n = _wkv_prep2(raw_n, cums_n, hr)
        inv_step()
        states, ys = _wkv_adv3(states, p0s, us, preps, hr, bd)
        if more:
            preps = _wkv_prep3(mids_n)
        for gi, cs in enumerate(cols):
            yf_ref[sl_f[ci], cs] = ys[2 * gi]
            yb_ref[sl_b[ci], cs] = ys[2 * gi + 1]
    while inv['lv'] < RW_TINV_LEVELS:
        inv_step()
    for gi in range(RW_GSTEP):
        sf_scr[gi] = states[2 * gi]
        sb_scr[gi] = states[2 * gi + 1]

    @pl.when(pos_f == cnt_f - 1)
    def _():
        sff_ref[...] = sf_scr[...]

    @pl.when(pos_b == 0)
    def _():
        sfb_ref[...] = sb_scr[...]


def wkv_fused(lay, r, v, kk, kd, a, lw, s0):
    n, dm = r.shape
    tb, lanes = lay.tb, RW_LANES
    ng = dm // lanes
    width = RW_GSTEP * lanes
    k = _wkv_constants()
    full = lambda arr: pl.BlockSpec(arr.shape, lambda g, j: (0,) * arr.ndim)

    def views(d, blk, blk_next):
        tok = pl.BlockSpec((tb, width), lambda g, j: (blk(j), g))
        tok2 = pl.BlockSpec((None, tb, width), lambda g, j: (d, blk(j), g))
        tok_n = pl.BlockSpec((tb, width), lambda g, j: (blk_next(j), g))
        tok2_n = pl.BlockSpec((None, tb, width), lambda g, j: (d, blk_next(j), g))

        def s0_map(g, j):
            seq, _, _ = lay.seq_info(blk(j))
            return (jnp.maximum(seq - lay.n_ctx + 1, 0), d, g, 0, 0)

        def sfin_map(g, j):
            seq, _, _ = lay.seq_info(blk(j))
            return (seq, g, 0, 0)

        ins = [tok, tok, tok, tok2, tok2, tok2, pl.BlockSpec((None, None, RW_GSTEP, lanes, lanes), s0_map),
               tok_n, tok2_n, tok2_n]
        return ins, tok, pl.BlockSpec((None, RW_GSTEP, lanes, lanes), sfin_map)

    in_f, y_f, sf_f = views(0, lambda j: j, lambda j: jnp.minimum(j + 1, lay.nb - 1))
    in_b, y_b, sf_b = views(1, lambda j: lay.nb - 1 - j, lambda j: jnp.maximum(lay.nb - 2 - j, 0))
    consts = [k['cum'], k['strict'], k['incl'], k['head_rows'], k['blk_rows'], k['eye_w'], k['bd']]
    args = [r, v, kk, kd, a, lw, s0, kk, a, lw]
    return pl.pallas_call(
        functools.partial(_wkv_fused_kernel, lay=lay),
        grid=(ng // RW_GSTEP, lay.nb),
        in_specs=in_f + in_b + [full(x) for x in consts],
        out_specs=[y_f, y_b, sf_f, sf_b],
        out_shape=[jax.ShapeDtypeStruct((n, dm), F32), jax.ShapeDtypeStruct((n, dm), F32),
                   jax.ShapeDtypeStruct((lay.n_seq, ng, lanes, lanes), F32),
                   jax.ShapeDtypeStruct((lay.n_seq, ng, lanes, lanes), F32)],
        scratch_shapes=[pltpu.VMEM((RW_GSTEP, lanes, lanes), F32), pltpu.VMEM((RW_GSTEP, lanes, lanes), F32),
                        pltpu.VMEM((2, 2, tb, width), BF16)],
        compiler_params=_cparams(2), name="wkv_fused",
    )(*args, *args, *consts)


def _rwkv_post_kernel(yf_ref, yb_ref, b_ref, g_ref, lnw_ref, lnb_ref, bd_ref, o_ref):
    y = yf_ref[...] + yb_ref[...]
    bd = bd_ref[...]
    inv = 1.0 / RW_HEAD
    mu = _split_dot_r(y, bd, 2) * inv
    yc = y - mu
    var = _split_dot_r(yc * yc, bd, 2) * inv
    yn = yc * lax.rsqrt(var + RW_LN_EPS)
    out = yn * lnw_ref[...] + lnb_ref[...] + b_ref[...]
    o_ref[...] = (out * g_ref[...]).astype(o_ref.dtype)


def rwkv_post(lay, y_f, y_b, bonus, g, lnw, lnb):
    n, dm = y_f.shape
    lanes = RW_LANES
    tm = lay.tile(1024)
    bd_b = _wkv_constants()['bd_b']
    tok = pl.BlockSpec((tm, lanes), lambda i, c: (i, c))
    par = pl.BlockSpec((1, lanes), lambda i, c: (0, c))
    return pl.pallas_call(
        _rwkv_post_kernel,
        grid=(n // tm, dm // lanes),
        in_specs=[tok, tok, tok, tok, par, par, pl.BlockSpec(bd_b.shape, lambda i, c: (0, 0))],
        out_specs=tok,
        out_shape=jax.ShapeDtypeStruct((n, dm), BF16),
        compiler_params=_cparams(2), name="rwkv_post",
    )(y_f, y_b, bonus, g, lnw, lnb, bd_b)


def rwkv_layer(lay, x, mods, nw, p, s0):
    n, d = x.shape
    sh1, sc1, g1 = mods[0], mods[1], mods[2]
    xr, xw, xk, xv, xa, xg = rwkv_prep(lay, x, nw, sc1, sh1, p['mu'])
    tm = lay.tile(1024)
    lw, a, g = rwkv_lora(lay, xw, xa, xg, p['w1'], p['w2'], p['w0'], p['a1'], p['a2'], p['a0'],
                         p['g1'], p['g2'])
    r, v, kk, kd, bonus = rwkv_rkv(lay, xr, xk, xv, p['wr'], p['wk'], p['wv'], a, p['kk'], p['ka'], p['rk'])
    y_f, y_b, sfin_f, sfin_b = wkv_fused(lay, r, v, kk, kd, a, lw, s0)
    z = rwkv_post(lay, y_f, y_b, bonus, g, p['lnx_w'], p['lnx_b'])
    x = matmul_gated_residual(lay, z, p['wo'], x, g1, tm=tm, tn=512, name="rwkv_o")
    return x, jnp.stack([sfin_f, sfin_b], axis=1)


def _hgrn_constants():
    c = HG_CHUNK_TOKENS
    t = np.arange(c)[:, None]
    j = np.arange(c)[None, :]
    cums, masks_all = [], []
    for rev in (False, True):
        masks = []
        h = 1
        while h < c:
            upper = (t % (2 * h)) >= h
            same = (t // (2 * h)) == (j // (2 * h))
            if not rev:
                mask = same & upper & ((j % (2 * h)) < h)
            else:
                mask = same & (~upper) & ((j % (2 * h)) >= h)
            masks.append(mask)
            h *= 2
        masks.append(t == j)
        cums.append((j >= t) if rev else (j <= t))
        masks_all.append(np.stack(masks, 0))
    return jnp.asarray(np.stack(cums), BF16), jnp.asarray(np.stack(masks_all).astype(np.float32))


def _hgrn_level_exponents(g, gcum, rev):
    c, kdim = g.shape
    row = lax.broadcasted_iota(jnp.int32, g.shape, 0)
    nxt = pltpu.roll(g, c - 1, axis=0)
    prv = pltpu.roll(g, 1, axis=0)
    r2, r4 = row & 1, row & 3
    if not rev:
        x1 = jnp.where(r2 == 1, g, 0.0)
        x2 = jnp.where(r4 == 0, nxt, jnp.where(r4 == 2, g, jnp.where(r4 == 3, prv + g, 0.0)))
    else:
        x1 = jnp.where(r2 == 0, g, 0.0)
        x2 = jnp.where(r4 == 0, g + nxt, jnp.where(r4 == 1, g, jnp.where(r4 == 3, prv, 0.0)))
    xs = [x1, x2]
    h = 4
    while h < c:
        gr = gcum.reshape(c // (2 * h), 2 * h, kdim)
        ref = gr[:, h:h + 1, :] if rev else gr[:, h - 1:h, :]
        upper = lax.broadcasted_iota(jnp.int32, gr.shape, 1) >= h
        diff = gr - ref
        x = jnp.where(upper, -diff, diff) if rev else jnp.where(upper, diff, -diff)
        xs.append(x.reshape(c, kdim))
        h *= 2
    return xs


def _hgrn_units(units, masks_by_dir, cum_by_dir):
    c = HG_CHUNK_TOKENS
    gs = [jnp.log(f) for _, f, _, _ in units]
    gcums = [_split_dot(cum_by_dir[rev], g, 3) for g, (_, _, _, rev) in zip(gs, units)]
    outs = []
    pend = []
    for (q, f, iv, rev), g, gcum in zip(units, gs, gcums):
        k = 1.0 - f
        tot = gcum[0:1] if rev else gcum[c - 1:c]
        es = [jnp.exp(x) for x in _hgrn_level_exponents(g, gcum, rev)]
        pend.append((q, k, iv, rev, es, jnp.exp(gcum), jnp.exp(tot - gcum), jnp.exp(tot)))
    for q, k, iv, rev, es, eg, erest, etot in pend:
        masks = masks_by_dir[rev]
        a = masks[len(es)] * _bdot_nt(q, k)
        for lv, el in enumerate(es):
            a = a + masks[lv] * _bdot_nt(q * el, k * el)
        outs.append(dict(a=a, iv=iv, qe=(q * eg).astype(BF16), kdec=(k * erest).astype(BF16), decay=etot))
    for u in outs:
        u['av'] = _bdot(u['a'], u['iv'])
    for u in outs:
        u['upd'] = _bdot(u['iv'].T, u['kdec'])
    return outs


def _hgrn_kernel(*refs, lay):
    (qf, ff, if_, s0f, qb, fb, ib, s0b, cum_ref, mask_ref, of_ref, ob_ref, sff_ref, sfb_ref,
     sf_scr, sb_scr) = refs
    c = HG_CHUNK_TOKENS
    n_chunks = lay.tb // c
    j = pl.program_id(1)
    _, pos_f, cnt_f = lay.seq_info(j)
    _, pos_b, cnt_b = lay.seq_info(lay.nb - 1 - j)

    @pl.when(pos_f == 0)
    def _():
        sf_scr[...] = s0f[...]

    @pl.when(pos_b == cnt_b - 1)
    def _():
        sb_scr[...] = s0b[...]

    masks_by_dir = [mask_ref[0], mask_ref[1]]
    cum_by_dir = [cum_ref[0], cum_ref[1]]
    sl_f = [slice(ci * c, (ci + 1) * c) for ci in range(n_chunks)]
    sl_b = sl_f[::-1]
    cols = [slice(hi * HG_K, (hi + 1) * HG_K) for hi in range(HG_HSTEP)]
    chains = []
    for hi, cs in enumerate(cols):
        chains.append((sf_scr, hi, of_ref, [(sl, cs) for sl in sl_f],
                       [(qf[sl, cs], ff[sl, cs], if_[sl, cs], False) for sl in sl_f]))
        chains.append((sb_scr, hi, ob_ref, [(sl, cs) for sl in sl_b],
                       [(qb[sl, cs], fb[sl, cs], ib[sl, cs], True) for sl in sl_b]))
    done = _hgrn_units([u for ch in chains for u in ch[4]], masks_by_dir, cum_by_dir)
    pend = []
    for n_ch, (scr, hi, o_ref, where, _) in enumerate(chains):
        s = scr[hi]
        for ci in range(n_chunks):
            u = done[n_ch * n_chunks + ci]
            pend.append((o_ref, where[ci], u, s))
            s = s * u['decay'] + u['upd']
        scr[hi] = s
    for o_ref, (sl, cs), u, s_prev in pend:
        o_ref[sl, cs] = u['av'] + _bdot_nt(u['qe'], s_prev)

    @pl.when(pos_f == cnt_f - 1)
    def _():
        sff_ref[...] = sf_scr[...]

    @pl.when(pos_b == 0)
    def _():
        sfb_ref[...] = sb_scr[...]


def hgrn_scan(lay, q, f_fwd, f_bwd, iv, s0t):
    n, dm = q.shape
    tb = lay.tb
    nh = dm // HG_K
    width = HG_HSTEP * HG_K
    cum_m, masks = _hgrn_constants()
    full = lambda arr: pl.BlockSpec(arr.shape, lambda h, j: (0,) * arr.ndim)

    def views(d, blk):
        tok = pl.BlockSpec((tb, width), lambda h, j: (blk(j), h))

        def s0_map(h, j):
            seq, _, _ = lay.seq_info(blk(j))
            return (jnp.maximum(seq - lay.n_ctx + 1, 0), d, h, 0, 0)

        def sfin_map(h, j):
            seq, _, _ = lay.seq_info(blk(j))
            return (seq, h, 0, 0)

        ins = [tok, tok, tok, pl.BlockSpec((None, None, HG_HSTEP, HG_K, HG_K), s0_map)]
        return ins, tok, pl.BlockSpec((None, HG_HSTEP, HG_K, HG_K), sfin_map)

    in_f, o_f, sf_f = views(0, lambda j: j)
    in_b, o_b, sf_b = views(1, lambda j: lay.nb - 1 - j)
    return pl.pallas_call(
        functools.partial(_hgrn_kernel, lay=lay),
        grid=(nh // HG_HSTEP, lay.nb),
        in_specs=in_f + in_b + [full(cum_m), full(masks)],
        out_specs=[o_f, o_b, sf_f, sf_b],
        out_shape=[jax.ShapeDtypeStruct((n, dm), F32), jax.ShapeDtypeStruct((n, dm), F32),
                   jax.ShapeDtypeStruct((lay.n_seq, nh, HG_K, HG_K), F32),
                   jax.ShapeDtypeStruct((lay.n_seq, nh, HG_K, HG_K), F32)],
        scratch_shapes=[pltpu.VMEM((HG_HSTEP, HG_K, HG_K), F32), pltpu.VMEM((HG_HSTEP, HG_K, HG_K), F32)],
        compiler_params=_cparams(2), name="hgrn_scan",
    )(q, f_fwd, iv, s0t, q, f_bwd, iv, s0t, cum_m, masks)


HG_POST_HEADS = 4


def _hgrn_post_kernel(of_ref, ob_ref, g_ref, nw_ref, z_ref):
    for h in range(HG_POST_HEADS):
        cs = slice(h * HG_K, (h + 1) * HG_K)
        o = of_ref[:, cs] + ob_ref[:, cs]
        o = o * lax.rsqrt(jnp.mean(o * o, axis=-1, keepdims=True) + NORM_EPS) * nw_ref[...] * g_ref[:, cs]
        z_ref[:, cs] = o.astype(z_ref.dtype)


def hgrn_post(lay, o_f, o_b, gs, nw):
    n, dm = o_f.shape
    tm = lay.tile(1024)
    width = HG_POST_HEADS * HG_K
    tok = pl.BlockSpec((tm, width), lambda i, h: (i, h))
    return pl.pallas_call(
        _hgrn_post_kernel,
        grid=(n // tm, dm // width),
        in_specs=[tok, tok, tok, pl.BlockSpec((1, HG_K), lambda i, h: (0, 0))],
        out_specs=tok,
        out_shape=jax.ShapeDtypeStruct((n, dm), BF16),
        compiler_params=_cparams(2), name="hgrn_post",
    )(o_f, o_b, gs, nw)


def _epi_hgrn_in(accs, extras):
    lb = extras[0]
    q = _silu(accs[0])
    f0 = lb + (1.0 - lb) * jax.nn.sigmoid(accs[1])
    f1 = lb + (1.0 - lb) * jax.nn.sigmoid(accs[2])
    return [q, f0, f1, accs[3], _silu(accs[4])]


def hgrn_layer(lay, x, mods, nw, p, lb, s0t):
    n, d = x.shape
    sh1, sc1, g1 = mods[0], mods[1], mods[2]
    h = norm_mod(lay, x, nw, sc1, sh1, BF16)
    tm, tn = lay.tile(1024), 256
    extras = ((lb, pl.BlockSpec((1, tn), lambda i, j: (0, j))),)
    q, f0, f1, iv, gs = matmul(h, p['w_in'], _epi_hgrn_in, [F32] * 5, tm=tm, tn=tn, extras=extras,
                               name="hgrn_in")
    o_f, o_b, sfin_f, sfin_b = hgrn_scan(lay, q, f0, f1, iv, s0t)
    z = hgrn_post(lay, o_f, o_b, gs, p['norm_w'])
    x = matmul_gated_residual(lay, z, p['wo'], x, g1, tm=lay.tile(1024), tn=512, name="hgrn_o")
    return x, jnp.stack([sfin_f, sfin_b], axis=1)


ML_DOWN_COLS = 1280
ML_KR_OFF = ML_Q_LORA + ML_KV_LORA
ML_KRS_OFF = ML_KR_OFF + LANES_V7X


def _rms(x, w):
    return x * lax.rsqrt(jnp.mean(x * x, axis=-1, keepdims=True) + NORM_EPS) * w


def _mla_mid_kernel(dn_ref, qw_ref, kvw_ref, cos_ref, sin_ref, qn_ref, ckv_ref, kr_ref):
    dn = dn_ref[...]
    qn_ref[...] = _rms(dn[:, :ML_Q_LORA], qw_ref[...]).astype(qn_ref.dtype)
    ckv_ref[...] = _rms(dn[:, ML_Q_LORA:ML_KR_OFF], kvw_ref[...])
    kr = dn[:, ML_KR_OFF:ML_KR_OFF + ML_ROPE]
    krs = dn[:, ML_KRS_OFF:ML_KRS_OFF + ML_ROPE]
    kr_ref[...] = kr * cos_ref[...] + krs * sin_ref[...]


def mla_mid(lay, dn, qw, kvw, cos, sin):
    n = dn.shape[0]
    tm = lay.tile(512)
    return pl.pallas_call(
        _mla_mid_kernel,
        grid=(n // tm,),
        in_specs=[pl.BlockSpec((tm, ML_DOWN_COLS), lambda i: (i, 0)),
                  pl.BlockSpec((1, ML_Q_LORA), lambda i: (0, 0)),
                  pl.BlockSpec((1, ML_KV_LORA), lambda i: (0, 0)),
                  pl.BlockSpec((tm, ML_ROPE), lambda i: (i, 0)),
                  pl.BlockSpec((tm, ML_ROPE), lambda i: (i, 0))],
        out_specs=[pl.BlockSpec((tm, ML_Q_LORA), lambda i: (i, 0)),
                   pl.BlockSpec((tm, ML_KV_LORA), lambda i: (i, 0)),
                   pl.BlockSpec((tm, ML_ROPE), lambda i: (i, 0))],
        out_shape=[jax.ShapeDtypeStruct((n, ML_Q_LORA), BF16),
                   jax.ShapeDtypeStruct((n, ML_KV_LORA), F32),
                   jax.ShapeDtypeStruct((n, ML_ROPE), F32)],
        compiler_params=_cparams(1), name="mla_mid",
    )(dn, qw, kvw, cos, sin)


ML_QSCALE = math.log2(math.e) / math.sqrt(ML_NOPE + ML_ROPE)


def _epi_qscale(accs, extras):
    return [accs[0] * ML_QSCALE]


def _epi_rope(accs, extras):
    cos, sin = extras
    return [(accs[0] * cos + accs[1] * sin) * ML_QSCALE]


ML_KEY_SPLITS = 2


def _attn_kernel(qn_ref, qr_ref, kn_ref, kr_ref, v_ref, o_ref, kc_scr, vt_scr):
    @pl.when(pl.program_id(2) == 0)
    def _():
        for h in range(2):
            kc_scr[h, :, :ML_NOPE] = kn_ref[:, h * ML_NOPE:(h + 1) * ML_NOPE]
            kc_scr[h, :, ML_NOPE:] = kr_ref[...]
            vt_scr[h] = v_ref[:, h * ML_V:(h + 1) * ML_V].astype(F32).T.astype(vt_scr.dtype)

    k_len = kc_scr.shape[1]
    kh = k_len // ML_KEY_SPLITS
    scores = []
    for h in range(2):
        q = jnp.concatenate([qn_ref[:, h * ML_NOPE:(h + 1) * ML_NOPE],
                             qr_ref[:, h * ML_ROPE:(h + 1) * ML_ROPE]], axis=1)
        scores.append([lax.dot_general(kc_scr[h, kb * kh:(kb + 1) * kh, :], q, (((1,), (1,)), ((), ())),
                                       preferred_element_type=F32) for kb in range(ML_KEY_SPLITS)])
    for h in range(2):
        m_acc = l_acc = o_acc = None
        for kb, st in enumerate(scores[h]):
            m = jnp.max(st, axis=0, keepdims=True)
            p = jnp.exp2(st - m)
            l = jnp.sum(p, axis=0, keepdims=True)
            ot = jnp.dot(vt_scr[h, :, kb * kh:(kb + 1) * kh], p.astype(BF16),
                         preferred_element_type=F32)
            if m_acc is None:
                m_acc, l_acc, o_acc = m, l, ot
            else:
                m_new = jnp.maximum(m_acc, m)
                c_old, c_new = jnp.exp2(m_acc - m_new), jnp.exp2(m - m_new)
                l_acc = l_acc * c_old + l * c_new
                o_acc = o_acc * c_old + ot * c_new
                m_acc = m_new
        o_ref[:, h * ML_V:(h + 1) * ML_V] = (o_acc / l_acc).T.astype(o_ref.dtype)


def attention(qn, qr, kn, kr, v, *, n_seq, q_len, k_len, row0, tq):
    heads2 = qn.shape[1] // (2 * ML_NOPE)
    qb = q_len // tq
    rb0 = row0 // tq
    return pl.pallas_call(
        _attn_kernel,
        grid=(n_seq, heads2, qb),
        in_specs=[pl.BlockSpec((tq, 2 * ML_NOPE), lambda s, h, i: (rb0 + s * qb + i, h)),
                  pl.BlockSpec((tq, 2 * ML_ROPE), lambda s, h, i: (rb0 + s * qb + i, h)),
                  pl.BlockSpec((k_len, 2 * ML_NOPE), lambda s, h, i: (s, h)),
                  pl.BlockSpec((k_len, ML_ROPE), lambda s, h, i: (s, 0)),
                  pl.BlockSpec((k_len, 2 * ML_V), lambda s, h, i: (s, h))],
        out_specs=pl.BlockSpec((tq, 2 * ML_V), lambda s, h, i: (s * qb + i, h)),
        out_shape=jax.ShapeDtypeStruct((n_seq * q_len, heads2 * 2 * ML_V), BF16),
        scratch_shapes=[pltpu.VMEM((2, k_len, ML_NOPE + ML_ROPE), BF16), pltpu.VMEM((2, ML_V, k_len), BF16)],
        compiler_params=_cparams(3), name="mla_attn",
    )(qn, qr, kn, kr, v)


def mla_layer(lay, x, mods, nw, p, cache_ckv, cache_kr, cos, sin, cos2, sin2):
    n, d = x.shape
    sh1, sc1, g1 = mods[0], mods[1], mods[2]
    h = norm_mod(lay, x, nw, sc1, sh1, BF16)
    tm = lay.tile(512)
    dn = matmul(h, [p['w_down']], _epi_plain, [F32], tm=tm, tn=ML_DOWN_COLS, name="mla_down")[0]
    qlat, ckv, kr = mla_mid(lay, dn, p['qnorm_w'], p['kvnorm_w'], cos, sin)
    qn = matmul(qlat, [p['w_uq_nope']], _epi_qscale, [BF16], tm=tm, tn=512, name="mla_qn")[0]
    tn = 2 * ML_ROPE
    extras = ((cos2, pl.BlockSpec((tm, tn), lambda i, j: (i, 0))),
              (sin2, pl.BlockSpec((tm, tn), lambda i, j: (i, 0))))
    qr = matmul(qlat, [p['w_uq_rope'], p['w_uq_rope_sw']], _epi_rope, [BF16], tm=tm, tn=tn,
                extras=extras, name="mla_qr")[0]
    nc, past = lay.nc, cache_ckv.shape[1]
    ckv_b, kr_b = ckv.astype(BF16), kr.astype(BF16)
    kn_c, v_c = matmul(ckv_b[:nc], [p['w_ukn'], p['w_uv']], _epi_plain, [BF16, BF16],
                       tm=lay.tile(512), tn=512, name="mla_kv_ctx")
    o_c = attention(qn, qr, kn_c, kr_b[:nc], v_c, n_seq=lay.n_ctx, q_len=lay.ctx_len,
                    k_len=lay.ctx_len, row0=0, tq=min(256, lay.ctx_len))
    k_len = lay.lat_len + past
    ckv_l = jnp.concatenate([ckv_b[nc:].reshape(lay.n_lat, lay.lat_len, -1), cache_ckv.astype(BF16)],
                            axis=1).reshape(lay.n_lat * k_len, -1)
    kr_l = jnp.concatenate([kr_b[nc:].reshape(lay.n_lat, lay.lat_len, -1), cache_kr.astype(BF16)],
                           axis=1).reshape(lay.n_lat * k_len, -1)
    tk = math.gcd(k_len, 512)
    kn_l, v_l = matmul(ckv_l, [p['w_ukn'], p['w_uv']], _epi_plain, [BF16, BF16], tm=tk, tn=512,
                       name="mla_kv_lat")
    o_l = attention(qn, qr, kn_l, kr_l, v_l, n_seq=lay.n_lat, q_len=lay.lat_len, k_len=k_len,
                    row0=nc, tq=min(256, lay.lat_len))
    o = jnp.concatenate([o_c, o_l], axis=0)
    x = matmul_gated_residual(lay, o, p['wo'], x, g1, tm=lay.tile(1024), tn=512, name="mla_o")
    return x, ckv[:nc], kr[:nc]


def _rope_tables(lay):
    t = lay.lat_len
    rows = t // GRID_W
    rr = jnp.broadcast_to(jnp.arange(rows, dtype=F32)[:, None], (rows, GRID_W)).reshape(-1)
    cc = jnp.broadcast_to(jnp.arange(GRID_W, dtype=F32)[None, :], (rows, GRID_W)).reshape(-1)
    nf = ML_ROPE // 4
    inv = ROPE_BASE ** (-jnp.arange(nf, dtype=F32) / nf)
    ar, ac = rr[:, None] * inv, cc[:, None] * inv
    cos = jnp.concatenate([jnp.cos(ar), jnp.cos(ar), jnp.cos(ac), jnp.cos(ac)], axis=-1)
    sin = jnp.concatenate([-jnp.sin(ar), jnp.sin(ar), -jnp.sin(ac), jnp.sin(ac)], axis=-1)
    cos = jnp.concatenate([jnp.ones((lay.nc, ML_ROPE), F32), jnp.tile(cos, (lay.n_lat, 1))], axis=0)
    sin = jnp.concatenate([jnp.zeros((lay.nc, ML_ROPE), F32), jnp.tile(sin, (lay.n_lat, 1))], axis=0)
    return cos, sin


def _swap_cols(w):
    k, c = w.shape
    w4 = w.reshape(k, c // 32, 2, 16)
    return w4[:, :, ::-1, :].reshape(k, c)


def ffn(lay, x, mods, nw, w_a, w_b, w_out):
    sh2, sc2, g2 = mods[3], mods[4], mods[5]
    h = norm_mod(lay, x, nw, sc2, sh2, BF16)
    act = matmul(h, [w_a, w_b], _epi_swiglu, [BF16], tm=lay.tile(1024), tn=512, name="ffn_in")[0]
    return matmul_gated_residual(lay, act, w_out, x, g2, tm=lay.tile(1024), tn=512, name="ffn_out")


def _block_diag_states(s):
    b, two, h, n, _ = s.shape
    s = s.reshape(b, two, h // RW_GROUP, RW_GROUP, n, n)
    eye = jnp.eye(RW_GROUP, dtype=s.dtype)
    out = jnp.einsum('bdghvk,hi->bdghvik', s, eye)
    return out.reshape(b, two, h // RW_GROUP, RW_GROUP * n, RW_GROUP * n)


def _diag_blocks(s):
    b, two, g, l, _ = s.shape
    n = l // RW_GROUP
    s = s.reshape(b, two, g, RW_GROUP, n, RW_GROUP, n)
    s = jnp.moveaxis(jnp.diagonal(s, axis1=3, axis2=5), -1, 3)
    return s.reshape(b, two, g * RW_GROUP, n, n)


def kernel(x_prompt, x_sample, state_rwkv, state_hgrn, cache_ckv, cache_krope, c, c_ctx, ada_w, ada_b, norm1_w, norm2_w, ffn_w_in, ffn_w_out, final_norm_w, rw_mu, rw_wr, rw_wk, rw_wv, rw_wo, rw_w0, rw_w1, rw_w2, rw_a0, rw_a1, rw_a2, rw_g1, rw_g2, rw_kk, rw_ka, rw_rk, rw_lnx_w, rw_lnx_b, hg_w_in, hg_lb, hg_norm_w, hg_wo, ml_w_down, ml_qnorm_w, ml_kvnorm_w, ml_w_uq, ml_w_ukv, ml_wo):
    n_ctx, ctx_len, d = x_prompt.shape
    n_lat, lat_len, _ = x_sample.shape
    depth = ada_w.shape[0]
    lay = Layout(n_ctx, ctx_len, n_lat, lat_len)
    d_ff = ffn_w_out.shape[1]
    x = jnp.concatenate([x_prompt.reshape(lay.nc, d), x_sample.reshape(n_lat * lat_len, d)], axis=0)

    n_cond = -(-(1 + n_lat) // SUBLANES_V7X) * SUBLANES_V7X
    cond = jnp.zeros((n_cond, d), F32).at[0].set(c_ctx).at[1:1 + n_lat].set(c)
    mod_all = adaln(cond, ada_w, ada_b)
    mod_all = mod_all.reshape(depth, n_cond, 6, 1, d).transpose(0, 2, 1, 3, 4)

    lb_table = jnp.cumsum(jax.nn.softmax(hg_lb.astype(F32), axis=0), axis=0)
    lb_table = lb_table - lb_table[0]
    cos, sin = _rope_tables(lay)
    cos2, sin2 = jnp.tile(cos, (1, 2)), jnp.tile(sin, (1, 2))
    bf = lambda t: t.astype(BF16)

    new_rwkv, new_hgrn, new_ckv, new_krope = [], [], [], []
    for l in range(depth):
        kind, j = l % 3, l // 3
        mods = mod_all[l]
        if kind == 0:
            pad1 = lambda w: jnp.pad(w, ((0, 0), (0, 0), (0, LORA_PAD - w.shape[2])))
            pad2 = lambda w: jnp.pad(w, ((0, 0), (0, LORA_PAD - w.shape[1]), (0, 0)))
            p = {'mu': rw_mu[j], 'wr': bf(rw_wr[j]), 'wk': bf(rw_wk[j]), 'wv': bf(rw_wv[j]),
                 'wo': bf(rw_wo[j]),
                 'w0': rw_w0[j].reshape(2, 1, d), 'w1': bf(pad1(rw_w1[j])), 'w2': bf(pad2(rw_w2[j])),
                 'a0': rw_a0[j].reshape(2, 1, d), 'a1': bf(pad1(rw_a1[j])), 'a2': bf(pad2(rw_a2[j])),
                 'g1': bf(rw_g1[j]), 'g2': bf(rw_g2[j]),
                 'kk': rw_kk[j].reshape(1, d), 'ka': rw_ka[j].reshape(1, d),
                 'rk': rw_rk[j].reshape(2, 1, d),
                 'lnx_w': rw_lnx_w[j].reshape(1, d), 'lnx_b': rw_lnx_b[j].reshape(1, d)}
            s_lat = _block_diag_states(state_rwkv[:, j].astype(F32))
            s0 = jnp.concatenate([jnp.zeros((1,) + s_lat.shape[1:], F32), s_lat], axis=0)
            x, sfin = rwkv_layer(lay, x, mods, norm1_w[l], p, s0)
            new_rwkv.append(_diag_blocks(sfin[:n_ctx]))
        elif kind == 1:
            hk = d
            w_in = hg_w_in[j]
            p = {'w_in': [bf(w_in[:, i * hk:(i + 1) * hk]) for i in range(5)],
                 'norm_w': hg_norm_w[j].reshape(1, HG_K), 'wo': bf(hg_wo[j])}
            s_lat = jnp.swapaxes(state_hgrn[:, j].astype(F32), -1, -2)
            s0t = jnp.concatenate([jnp.zeros((1,) + s_lat.shape[1:], F32), s_lat], axis=0)
            x, sfin = hgrn_layer(lay, x, mods, norm1_w[l], p, lb_table[l].reshape(1, d), s0t)
            new_hgrn.append(jnp.swapaxes(sfin[:n_ctx], -1, -2))
        else:
            wd = ml_w_down[j]
            kr_w = wd[:, ML_KR_OFF:]
            zpad = jnp.zeros((d, LANES_V7X - ML_ROPE), wd.dtype)
            w_down = jnp.concatenate([wd, zpad, _swap_cols(kr_w), zpad], axis=1)
            wq = ml_w_uq[j].reshape(ML_Q_LORA, ML_H, ML_NOPE + ML_ROPE)
            wq_n = wq[:, :, :ML_NOPE].reshape(ML_Q_LORA, ML_H * ML_NOPE)
            wq_r = wq[:, :, ML_NOPE:].reshape(ML_Q_LORA, ML_H * ML_ROPE)
            wkv = ml_w_ukv[j].reshape(ML_KV_LORA, ML_H, ML_NOPE + ML_V)
            p = {'w_down': bf(w_down), 'qnorm_w': ml_qnorm_w[j].reshape(1, -1),
                 'kvnorm_w': ml_kvnorm_w[j].reshape(1, -1),
                 'w_uq_nope': bf(wq_n), 'w_uq_rope': bf(wq_r), 'w_uq_rope_sw': bf(_swap_cols(wq_r)),
                 'w_ukn': bf(wkv[:, :, :ML_NOPE].reshape(ML_KV_LORA, ML_H * ML_NOPE)),
                 'w_uv': bf(wkv[:, :, ML_NOPE:].reshape(ML_KV_LORA, ML_H * ML_V)),
                 'wo': bf(ml_wo[j])}
            x, ckv_c, kr_c = mla_layer(lay, x, mods, norm1_w[l], p, cache_ckv[:, j], cache_krope[:, j],
                                       cos, sin, cos2, sin2)
            new_ckv.append(ckv_c.reshape(n_ctx, ctx_len, ML_KV_LORA))
            new_krope.append(kr_c.reshape(n_ctx, ctx_len, ML_ROPE))
        w_in = ffn_w_in[l]
        x = ffn(lay, x, mods, norm2_w[l], bf(w_in[:, :d_ff]), bf(w_in[:, d_ff:]), bf(ffn_w_out[l]))

    y_prompt = rmsnorm_rows(lay, x, final_norm_w, 0, lay.nc).reshape(n_ctx, ctx_len, d)
    y_sample = rmsnorm_rows(lay, x, final_norm_w, lay.nc, lay.n - lay.nc).reshape(n_lat, lat_len, d)
    return (y_prompt, y_sample, jnp.stack(new_rwkv, axis=1), jnp.stack(new_hgrn, axis=1),
            jnp.stack(new_ckv, axis=1), jnp.stack(new_krope, axis=1))
```

```python
import functools
import math

import numpy as np
import jax
import jax.numpy as jnp
from jax import lax
from jax.experimental import pallas as pl
from jax.experimental.pallas import tpu as pltpu

F32 = jnp.float32
BF16 = jnp.bfloat16

LANES_V7X = 128
SUBLANES_V7X = 8
VMEM_BYTES_V7X = 64 * 1024 * 1024
VMEM_LIMIT = VMEM_BYTES_V7X - 8 * 1024 * 1024

NORM_EPS = 1e-6
RW_HEAD = 64
RW_LN_EPS = 64e-5
RW_GROUP = 4
RW_LANES = RW_GROUP * RW_HEAD
RW_CHUNK = 64
RW_GSTEP = 2
HG_K = 128
HG_CHUNK_TOKENS = 64
HG_HSTEP = 2
ML_H = 16
ML_NOPE = 128
ML_ROPE = 64
ML_V = 128
ML_Q_LORA = 512
ML_KV_LORA = 512
GRID_W = 64
ROPE_BASE = 10000.0
LORA_PAD = 128


class Layout:
    def __init__(self, n_ctx, ctx_len, n_lat, lat_len):
        self.n_ctx, self.ctx_len, self.n_lat, self.lat_len = n_ctx, ctx_len, n_lat, lat_len
        self.nc = n_ctx * ctx_len
        self.n = self.nc + n_lat * lat_len
        self.tb = min(256, ctx_len)
        assert ctx_len % self.tb == 0 and lat_len % self.tb == 0 and self.tb % RW_CHUNK == 0
        self.nb = self.n // self.tb
        self.nb_ctx = self.nc // self.tb
        self.bps_ctx = ctx_len // self.tb
        self.bps_lat = lat_len // self.tb
        self.n_seq = n_ctx + n_lat

    def tile(self, want):
        t = want
        while self.nc % t or self.lat_len % t:
            t //= 2
        return t

    def cond_of_tile(self, i, tm):
        row = i * tm
        return jnp.where(row < self.nc, 0, 1 + (row - self.nc) // self.lat_len)

    def seq_info(self, blk):
        is_ctx = blk < self.nb_ctx
        lat = blk - self.nb_ctx
        seq = jnp.where(is_ctx, blk // self.bps_ctx, self.n_ctx + lat // self.bps_lat)
        pos = jnp.where(is_ctx, blk % self.bps_ctx, lat % self.bps_lat)
        cnt = jnp.where(is_ctx, self.bps_ctx, self.bps_lat)
        return seq, pos, cnt


def _cparams(n_axes):
    return pltpu.CompilerParams(dimension_semantics=("arbitrary",) * n_axes, vmem_limit_bytes=VMEM_LIMIT)


def _bdot(a, b):
    return jnp.dot(a.astype(BF16), b.astype(BF16), preferred_element_type=F32)


def _bdot_nt(a, b):
    return lax.dot_general(a.astype(BF16), b.astype(BF16), (((1,), (1,)), ((), ())),
                           preferred_element_type=F32)


def _silu(x):
    return x * jax.nn.sigmoid(x)


def _adaln_kernel(c_ref, w_ref, b_ref, o_ref):
    a = _silu(c_ref[...]).astype(BF16)
    o_ref[...] = jnp.dot(a, w_ref[...].astype(BF16), preferred_element_type=F32) + b_ref[...]


def adaln(cond, ada_w, ada_b):
    depth, d, d6 = ada_w.shape
    r = cond.shape[0]
    tn = 1024
    return pl.pallas_call(
        _adaln_kernel,
        grid=(depth, d6 // tn),
        in_specs=[pl.BlockSpec((r, d), lambda l, j: (0, 0)),
                  pl.BlockSpec((None, d, tn), lambda l, j: (l, 0, j)),
                  pl.BlockSpec((None, 1, tn), lambda l, j: (l, 0, j))],
        out_specs=pl.BlockSpec((None, r, tn), lambda l, j: (l, 0, j)),
        out_shape=jax.ShapeDtypeStruct((depth, r, d6), F32),
        compiler_params=_cparams(2), name="adaln",
    )(cond, ada_w, ada_b.reshape(depth, 1, d6))


def _norm_mod(x, nw, sc, sh):
    y = x * lax.rsqrt(jnp.mean(x * x, axis=-1, keepdims=True) + NORM_EPS)
    return (y * nw) * (1.0 + sc) + sh


def _norm_mod_kernel(x_ref, nw_ref, sc_ref, sh_ref, o_ref):
    o_ref[...] = _norm_mod(x_ref[...], nw_ref[...], sc_ref[...], sh_ref[...]).astype(o_ref.dtype)


def norm_mod(lay, x, nw, sc, sh, out_dtype):
    n, d = x.shape
    tm = lay.tile(512)
    cmap = lambda i: (lay.cond_of_tile(i, tm), 0, 0)
    return pl.pallas_call(
        _norm_mod_kernel,
        grid=(n // tm,),
        in_specs=[pl.BlockSpec((tm, d), lambda i: (i, 0)),
                  pl.BlockSpec((1, d), lambda i: (0, 0)),
                  pl.BlockSpec((None, 1, d), cmap),
                  pl.BlockSpec((None, 1, d), cmap)],
        out_specs=pl.BlockSpec((tm, d), lambda i: (i, 0)),
        out_shape=jax.ShapeDtypeStruct((n, d), out_dtype),
        compiler_params=_cparams(1), name="norm_mod",
    )(x, nw.reshape(1, d), sc, sh)


def _rmsnorm_kernel(x_ref, w_ref, o_ref):
    x = x_ref[...]
    o_ref[...] = x * lax.rsqrt(jnp.mean(x * x, axis=-1, keepdims=True) + NORM_EPS) * w_ref[...]


def rmsnorm_rows(lay, x, w, row0, rows):
    d = x.shape[1]
    tm = lay.tile(512)
    b0 = row0 // tm
    return pl.pallas_call(
        _rmsnorm_kernel,
        grid=(rows // tm,),
        in_specs=[pl.BlockSpec((tm, d), lambda i: (b0 + i, 0)), pl.BlockSpec((1, d), lambda i: (0, 0))],
        out_specs=pl.BlockSpec((tm, d), lambda i: (i, 0)),
        out_shape=jax.ShapeDtypeStruct((rows, d), x.dtype),
        compiler_params=_cparams(1), name="final_norm",
    )(x, w.reshape(1, d))


def _mm_kernel(*refs, n_w, n_e, epi):
    a = refs[0][...]
    accs = [jnp.dot(a, refs[1 + i][...], preferred_element_type=F32) for i in range(n_w)]
    extras = [refs[1 + n_w + i][...] for i in range(n_e)]
    outs = epi(accs, extras)
    o_refs = refs[1 + n_w + n_e:]
    for o_ref, val in zip(o_refs, outs):
        o_ref[...] = val.astype(o_ref.dtype)


def matmul(a, ws, epi, out_dtypes, *, tm, tn, extras=(), name):
    m, k = a.shape
    nw = ws[0].shape[1]
    assert m % tm == 0 and nw % tn == 0
    in_specs = [pl.BlockSpec((tm, k), lambda i, j: (i, 0))]
    in_specs += [pl.BlockSpec((k, tn), lambda i, j: (0, j)) for _ in ws]
    in_specs += [spec for _, spec in extras]
    outs = pl.pallas_call(
        functools.partial(_mm_kernel, n_w=len(ws), n_e=len(extras), epi=epi),
        grid=(m // tm, nw // tn),
        in_specs=in_specs,
        out_specs=[pl.BlockSpec((tm, tn), lambda i, j: (i, j)) for _ in out_dtypes],
        out_shape=[jax.ShapeDtypeStruct((m, nw), dt) for dt in out_dtypes],
        compiler_params=_cparams(2), name=name,
    )(a, *ws, *[arr for arr, _ in extras])
    return outs


def _epi_plain(accs, extras):
    return accs


def _epi_gated_residual(accs, extras):
    x, g = extras
    return [x + g * accs[0]]


def matmul_gated_residual(lay, a, w, x, gate, *, tm, tn, name):
    extras = ((x, pl.BlockSpec((tm, tn), lambda i, j: (i, j))),
              (gate, pl.BlockSpec((None, 1, tn), lambda i, j: (lay.cond_of_tile(i, tm), 0, j))))
    return matmul(a, [w], _epi_gated_residual, [F32], tm=tm, tn=tn, extras=extras, name=name)[0]


def _epi_swiglu(accs, extras):
    return [_silu(accs[0]) * accs[1]]


RWKV_PREP_SLAB = LANES_V7X


def _rwkv_prep_kernel(x_ref, xp_ref, xn_ref, nw_ref, sc_ref, sh_ref, mu_ref, *o_refs, lay):
    i = pl.program_id(0)
    _, pos, cnt = lay.seq_info(i)
    def inv_rms(x):
        return lax.rsqrt(jnp.mean(x * x, axis=-1, keepdims=True) + NORM_EPS)

    tb, d = x_ref.shape
    inv = inv_rms(x_ref[...])
    inv_p = inv_rms(xp_ref[SUBLANES_V7X - 1:SUBLANES_V7X, :])
    inv_n = inv_rms(xn_ref[0:1, :])
    keep_p = jnp.where(pos == 0, 0.0, 1.0)
    keep_n = jnp.where(pos == cnt - 1, 0.0, 1.0)
    row = lax.broadcasted_iota(jnp.int32, (tb, RWKV_PREP_SLAB), 0)
    for c0 in range(0, d, RWKV_PREP_SLAB):
        cs = slice(c0, c0 + RWKV_PREP_SLAB)
        nw, sc, sh = nw_ref[:, cs], sc_ref[:, cs], sh_ref[:, cs]
        h = ((x_ref[:, cs] * inv) * nw) * (1.0 + sc) + sh
        hp = (((xp_ref[SUBLANES_V7X - 1:SUBLANES_V7X, cs] * inv_p) * nw) * (1.0 + sc) + sh) * keep_p
        hn = (((xn_ref[0:1, cs] * inv_n) * nw) * (1.0 + sc) + sh) * keep_n
        prev = jnp.where(row == 0, hp, pltpu.roll(h, 1, axis=0))
        nxt = jnp.where(row == tb - 1, hn, pltpu.roll(h, tb - 1, axis=0))
        xx = 0.5 * (prev + nxt) - h
        for idx, o_ref in enumerate(o_refs):
            o_ref[:, cs] = (h + xx * mu_ref[idx:idx + 1, cs]).astype(o_ref.dtype)


def rwkv_prep(lay, x, nw, sc, sh, mu):
    n, d = x.shape
    tb = lay.tb
    r8 = tb // SUBLANES_V7X
    last8 = n // SUBLANES_V7X - 1
    cmap = lambda i: (lay.cond_of_tile(i, tb), 0, 0)
    return pl.pallas_call(
        functools.partial(_rwkv_prep_kernel, lay=lay),
        grid=(n // tb,),
        in_specs=[pl.BlockSpec((tb, d), lambda i: (i, 0)),
                  pl.BlockSpec((SUBLANES_V7X, d), lambda i: (jnp.maximum(i * r8 - 1, 0), 0)),
                  pl.BlockSpec((SUBLANES_V7X, d), lambda i: (jnp.minimum((i + 1) * r8, last8), 0)),
                  pl.BlockSpec((1, d), lambda i: (0, 0)),
                  pl.BlockSpec((None, 1, d), cmap),
                  pl.BlockSpec((None, 1, d), cmap),
                  pl.BlockSpec((6, d), lambda i: (0, 0))],
        out_specs=[pl.BlockSpec((tb, d), lambda i: (i, 0))] * 6,
        out_shape=[jax.ShapeDtypeStruct((n, d), BF16)] * 6,
        compiler_params=_cparams(1), name="rwkv_prep",
    )(x, x, x, nw.reshape(1, d), sc, sh, mu)


RW_LOG_DECAY_SCALE = -math.exp(-0.5)


def _rwkv_lora_kernel(xw_ref, xa_ref, xg_ref, w1_ref, w2_ref, w0_ref, a1_ref, a2_ref, a0_ref,
                      g1_ref, g2_ref, lw_ref, a_ref, g_ref):
    xw, xa, xg = xw_ref[...], xa_ref[...], xg_ref[...]
    for d in range(2):
        t = jnp.tanh(jnp.dot(xw, w1_ref[d], preferred_element_type=F32))
        wl = w0_ref[d] + _bdot(t, w2_ref[d])
        lw_ref[d] = RW_LOG_DECAY_SCALE * jax.nn.sigmoid(wl)
        t = jnp.dot(xa, a1_ref[d], preferred_element_type=F32)
        a_ref[d] = jax.nn.sigmoid(a0_ref[d] + _bdot(t, a2_ref[d]))
    t = jax.nn.sigmoid(jnp.dot(xg, g1_ref[...], preferred_element_type=F32))
    g_ref[...] = _bdot(t, g2_ref[...])


def rwkv_lora(lay, xw, xa, xg, w1, w2, w0, a1, a2, a0, g1, g2):
    n, d = xw.shape
    tm = lay.tile(256)
    full = lambda arr: pl.BlockSpec(arr.shape, lambda i: (0,) * arr.ndim)
    row = pl.BlockSpec((tm, d), lambda i: (i, 0))
    return pl.pallas_call(
        _rwkv_lora_kernel,
        grid=(n // tm,),
        in_specs=[row, row, row] + [full(t) for t in (w1, w2, w0, a1, a2, a0, g1, g2)],
        out_specs=[pl.BlockSpec((2, tm, d), lambda i: (0, i, 0)),
                   pl.BlockSpec((2, tm, d), lambda i: (0, i, 0)),
                   row],
        out_shape=[jax.ShapeDtypeStruct((2, n, d), F32), jax.ShapeDtypeStruct((2, n, d), F32),
                   jax.ShapeDtypeStruct((n, d), F32)],
        compiler_params=_cparams(1), name="rwkv_lora",
    )(xw, xa, xg, w1, w2, w0, a1, a2, a0, g1, g2)


def _wkv_constants():
    c, g, hd = RW_CHUNK, RW_GROUP, RW_HEAD
    gc, lanes = g * c, g * hd
    t = np.arange(c)
    cum = np.stack([(t[None, :] <= t[:, None]), (t[None, :] >= t[:, None])])
    tr = np.arange(c)[:, None]
    tc = np.arange(gc)[None, :] % c
    strict = np.stack([tc < tr, tc > tr])
    incl = np.stack([tc <= tr, tc >= tr])
    head_rows = np.arange(gc)[:, None] // c == np.arange(lanes)[None, :] // hd
    blk_rows = np.arange(gc)[:, None] // c == np.arange(gc)[None, :] // c
    bd = np.arange(lanes)[:, None] // hd == np.arange(lanes)[None, :] // hd
    eye_w = tr == tc
    return dict(cum=jnp.asarray(cum, BF16), strict=jnp.asarray(strict, F32), incl=jnp.asarray(incl, F32),
                head_rows=jnp.asarray(head_rows, BF16), blk_rows=jnp.asarray(blk_rows, BF16),
                bd=jnp.asarray(bd, F32), bd_b=jnp.asarray(bd, BF16), eye_w=jnp.asarray(eye_w, F32))


def _split_dot(m01, x, passes):
    acc, rem = None, x
    for _ in range(passes):
        part = rem.astype(BF16)
        term = jnp.dot(m01, part, preferred_element_type=F32)
        acc = term if acc is None else acc + term
        rem = rem - part.astype(F32)
    return acc


def _split_dot_r(x, m01, passes):
    acc, rem = None, x
    for _ in range(passes):
        part = rem.astype(BF16)
        term = jnp.dot(part, m01, preferred_element_type=F32)
        acc = term if acc is None else acc + term
        rem = rem - part.astype(F32)
    return acc


def _tile_rows(x, mask_b):
    return jnp.concatenate([x.astype(BF16)] * RW_GROUP, axis=0) * mask_b


def _wkv_prep1(raw):
    return [_split_dot(u[6], u[5], 3) for u in raw]


def _wkv_prep2(raw, cums, head_rows):
    c = RW_CHUNK
    mids = []
    for (r, v, kk, kd, a, lw, _, _, _), cum in zip(raw, cums):
        tot = jnp.sum(lw, axis=0, keepdims=True)
        kka = kk * a
        e_inv, e_rest = jnp.exp(-cum), jnp.exp(tot - cum)
        q2 = jnp.concatenate([kk * jnp.exp(cum - lw), r * jnp.exp(cum)], axis=0).astype(BF16)
        mids.append(dict(q2=q2, kdh=_tile_rows(kd * e_inv, head_rows), kkah=_tile_rows(kka * e_inv, head_rows),
                         vbd=_tile_rows(v, head_rows), v=v, decay=jnp.exp(tot),
                         kw=jnp.concatenate([kd * e_rest, -(kka * e_rest)], axis=0).astype(BF16)))
    s1s = [_bdot_nt(m['q2'], m['kdh']) for m in mids]
    s2s = [_bdot_nt(m['q2'][c:], m['kkah']) for m in mids]
    for m, u, s1, s2 in zip(mids, raw, s1s, s2s):
        strict_w, incl_w = u[7], u[8]
        m['lad'] = jnp.concatenate([jnp.where(strict_w > 0, s1[:c], 0.0),
                                    jnp.where(incl_w > 0, s1[c:], 0.0)], axis=0).astype(BF16)
        m['a_a'] = jnp.where(incl_w > 0, s2, 0.0).astype(BF16)
    return mids


def _wkv_prep3(mids):
    for m, lav in zip(mids, [_bdot(m['lad'], m['vbd']) for m in mids]):
        m['lav'] = lav
    return mids


def _wkv_adv1(states, preps):
    return [_bdot_nt(p['q2'], s) for p, s in zip(preps, states)]


def _wkv_adv2(p0s, preps, t_ws, head_rows):
    c = RW_CHUNK
    return [_bdot(t_w, _tile_rows(p0[:c] + p['lav'][:c], head_rows))
            for p0, p, t_w in zip(p0s, preps, t_ws)]


def _wkv_adv3(states, p0s, us, preps, head_rows, bd):
    c = RW_CHUNK
    upds = [_bdot(jnp.concatenate([p['v'].astype(F32), u], axis=0).T, p['kw']) for p, u in zip(preps, us)]
    aus = [_bdot(p['a_a'], _tile_rows(u, head_rows)) for p, u in zip(preps, us)]
    new_states = [s * p['decay'] + jnp.where(bd > 0, upd, 0.0) for s, p, upd in zip(states, preps, upds)]
    ys = [p0[c:] + p['lav'][c:] - au for p0, p, au in zip(p0s, preps, aus)]
    return new_states, ys


RKV_TN = RW_LANES


def _rwkv_rkv_kernel(xr_ref, xk_ref, xv_ref, wr_ref, wk_ref, wv_ref, a_ref, kkw_ref, kaw_ref, rk_ref, bd_ref,
                     r_ref, v_ref, kk_ref, kd_ref, b_ref):
    r = jnp.dot(xr_ref[...], wr_ref[...], preferred_element_type=F32)
    k = jnp.dot(xk_ref[...], wk_ref[...], preferred_element_type=F32)
    v = jnp.dot(xv_ref[...], wv_ref[...], preferred_element_type=F32)
    r_ref[...] = r
    v_ref[...] = v.astype(v_ref.dtype)
    bd = bd_ref[...]
    kk = k * kkw_ref[...]
    mix = None
    for d in range(2):
        kd = k * (1.0 + (a_ref[d] - 1.0) * kaw_ref[...])
        kd_ref[d] = kd
        term = kd * rk_ref[d]
        mix = term if mix is None else mix + term
    rm = r * mix
    for h in range(RKV_TN // RW_LANES):
        cs = slice(h * RW_LANES, (h + 1) * RW_LANES)
        kkh = kk[:, cs]
        kk_ref[:, cs] = kkh * lax.rsqrt(_split_dot_r(kkh * kkh, bd, 2) + 1e-12)
        b_ref[:, cs] = _split_dot_r(rm[:, cs], bd, 2) * v[:, cs]


def rwkv_rkv(lay, xr, xk, xv, wr, wk, wv, a, kkw, kaw, rk):
    n, kdim = xr.shape
    dm = wr.shape[1]
    tm, tn = lay.tile(1024), RKV_TN
    bd_b = _wkv_constants()['bd_b']
    lhs = pl.BlockSpec((tm, kdim), lambda i, j: (i, 0))
    rhs = pl.BlockSpec((kdim, tn), lambda i, j: (0, j))
    tok = pl.BlockSpec((tm, tn), lambda i, j: (i, j))
    two = pl.BlockSpec((2, tm, tn), lambda i, j: (0, i, j))
    par = pl.BlockSpec((1, tn), lambda i, j: (0, j))
    return pl.pallas_call(
        _rwkv_rkv_kernel,
        grid=(n // tm, dm // tn),
        in_specs=[lhs, lhs, lhs, rhs, rhs, rhs, two, par, par,
                  pl.BlockSpec((2, 1, tn), lambda i, j: (0, 0, j)),
                  pl.BlockSpec(bd_b.shape, lambda i, j: (0, 0))],
        out_specs=[tok, tok, tok, two, tok],
        out_shape=[jax.ShapeDtypeStruct((n, dm), F32), jax.ShapeDtypeStruct((n, dm), BF16),
                   jax.ShapeDtypeStruct((n, dm), F32), jax.ShapeDtypeStruct((2, n, dm), F32),
                   jax.ShapeDtypeStruct((n, dm), F32)],
        compiler_params=_cparams(2), name="rwkv_rkv",
    )(xr, xk, xv, wr, wk, wv, a, kkw, kaw, rk, bd_b)


def _tinv_begin(items, head_rows, blk_rows, eye_w):
    cums = [_split_dot(it[3], it[2], 3) for it in items]
    ops = [(kk * jnp.exp(cum - lw), _tile_rows(kk * a * jnp.exp(-cum), head_rows))
           for (kk, a, lw, _, _), cum in zip(items, cums)]
    n_ws = [-jnp.where(it[4] > 0, _bdot_nt(q, w), 0.0) for it, (q, w) in zip(items, ops)]
    n_pows = [_bdot(n_w, _tile_rows(n_w, blk_rows)) for n_w in n_ws]
    return [eye_w + n_w for n_w in n_ws], n_pows


def _tinv_level(t_ws, n_pows, last, blk_rows):
    c = RW_CHUNK
    ws = [_tile_rows(n_pow, blk_rows) for n_pow in n_pows]
    if last:
        return [t_w + _bdot(t_w, w) for t_w, w in zip(t_ws, ws)], None
    boths = [_bdot(jnp.concatenate([t_w, n_pow], axis=0), w) for t_w, n_pow, w in zip(t_ws, n_pows, ws)]
    return [t_w + both[:c] for t_w, both in zip(t_ws, boths)], [both[c:] for both in boths]


RW_TINV_LEVELS = int(math.log2(RW_CHUNK))


def _wkv_fused_kernel(*refs, lay):
    (rf, vf, kkf, kdf, af, lwf, s0f, kkfn, afn, lwfn,
     rb, vb, kkb, kdb, ab, lwb, s0b, kkbn, abn, lwbn,
     cum_ref, strict_ref, incl_ref, hr_ref, br_ref, eye_ref, bd_ref,
     yf_ref, yb_ref, sff_ref, sfb_ref, sf_scr, sb_scr, t_scr) = refs
    c = RW_CHUNK
    n_chunks = lay.tb // c
    j = pl.program_id(1)
    cur = lax.rem(j, 2)
    nxt = 1 - cur
    _, pos_f, cnt_f = lay.seq_info(j)
    _, pos_b, cnt_b = lay.seq_info(lay.nb - 1 - j)

    @pl.when(pos_f == 0)
    def _():
        sf_scr[...] = s0f[...]

    @pl.when(pos_b == cnt_b - 1)
    def _():
        sb_scr[...] = s0b[...]

    hr, br, eye_w, bd = hr_ref[...], br_ref[...], eye_ref[...], bd_ref[...]
    sl_f = [slice(ci * c, (ci + 1) * c) for ci in range(n_chunks)]
    sl_b = sl_f[::-1]
    cols = [slice(gi * RW_LANES, (gi + 1) * RW_LANES) for gi in range(RW_GSTEP)]

    def inv_items(kk_f, a_f, lw_f, kk_b, a_b, lw_b, sl):
        out = []
        for cs in cols:
            out.append((kk_f[sl, cs], a_f[sl, cs], lw_f[sl, cs], cum_ref[0], strict_ref[0]))
            out.append((kk_b[sl, cs], a_b[sl, cs], lw_b[sl, cs], cum_ref[1], strict_ref[1]))
        return out

    def inv_store(slot, sl, t_ws):
        for gi, cs in enumerate(cols):
            t_scr[slot, 0, sl, cs] = t_ws[2 * gi].astype(t_scr.dtype)
            t_scr[slot, 1, sl, cs] = t_ws[2 * gi + 1].astype(t_scr.dtype)

    @pl.when(j == 0)
    def _():
        def body(ci, carry):
            sl = pl.ds(pl.multiple_of(ci * c, c), c)
            t_ws, n_pows = _tinv_begin(inv_items(kkf, af, lwf, kkb, ab, lwb, sl), hr, br, eye_w)
            for lv in range(1, RW_TINV_LEVELS):
                t_ws, n_pows = _tinv_level(t_ws, n_pows, lv == RW_TINV_LEVELS - 1, br)
            inv_store(cur, sl, t_ws)
            return carry
        lax.fori_loop(0, n_chunks, body, 0)

    def raw(ci):
        out = []
        for cs in cols:
            sf, sb = sl_f[ci], sl_b[ci]
            out.append((rf[sf, cs], vf[sf, cs], kkf[sf, cs], kdf[sf, cs], af[sf, cs], lwf[sf, cs],
                        cum_ref[0], strict_ref[0], incl_ref[0]))
            out.append((rb[sb, cs], vb[sb, cs], kkb[sb, cs], kdb[sb, cs], ab[sb, cs], lwb[sb, cs],
                        cum_ref[1], strict_ref[1], incl_ref[1]))
        return out

    states = []
    for gi in range(RW_GSTEP):
        states += [sf_scr[gi], sb_scr[gi]]
    raw_n = raw(0)
    preps = _wkv_prep3(_wkv_prep2(raw_n, _wkv_prep1(raw_n), hr))
    inv = {'lv': 0, 't': None, 'n': None}

    def inv_step():
        lv = inv['lv']
        if lv == 0:
            items = []
            for sl in sl_f:
                items += inv_items(kkfn, afn, lwfn, kkbn, abn, lwbn, sl)
            inv['t'], inv['n'] = _tinv_begin(items, hr, br, eye_w)
        elif lv < RW_TINV_LEVELS:
            inv['t'], inv['n'] = _tinv_level(inv['t'], inv['n'], lv == RW_TINV_LEVELS - 1, br)
            if lv == RW_TINV_LEVELS - 1:
                per = 2 * RW_GSTEP
                for ci2, sl in enumerate(sl_f):
                    inv_store(nxt, sl, inv['t'][ci2 * per:(ci2 + 1) * per])
        inv['lv'] = lv + 1

    for ci in range(n_chunks):
        more = ci + 1 < n_chunks
        t_ws = []
        for cs in cols:
            t_ws += [t_scr[cur, 0, sl_f[ci], cs], t_scr[cur, 1, sl_b[ci], cs]]
        p0s = _wkv_adv1(states, preps)
        if more:
            raw_n = raw(ci + 1)
            cums_n = _wkv_prep1(raw_n)
        inv_step()
        us = _wkv_adv2(p0s, preps, t_ws, hr)
        if more:
            mids_n = _wkv_prep2(raw_n, cums_n, hr)
        inv_step()
        states, ys = _wkv_adv3(states, p0s, us, preps, hr, bd)
        if more:
            preps = _wkv_prep3(mids_n)
        for gi, cs in enumerate(cols):
            yf_ref[sl_f[ci], cs] = ys[2 * gi]
            yb_ref[sl_b[ci], cs] = ys[2 * gi + 1]
    while inv['lv'] < RW_TINV_LEVELS:
        inv_step()
    for gi in range(RW_GSTEP):
        sf_scr[gi] = states[2 * gi]
        sb_scr[gi] = states[2 * gi + 1]

    @pl.when(pos_f == cnt_f - 1)
    def _():
        sff_ref[...] = sf_scr[...]

    @pl.when(pos_b == 0)
    def _():
        sfb_ref[...] = sb_scr[...]


def wkv_fused(lay, r, v, kk, kd, a, lw, s0):
    n, dm = r.shape
    tb, lanes = lay.tb, RW_LANES
    ng = dm // lanes
    width = RW_GSTEP * lanes
    k = _wkv_constants()
    full = lambda arr: pl.BlockSpec(arr.shape, lambda g, j: (0,) * arr.ndim)

    def views(d, blk, blk_next):
        tok = pl.BlockSpec((tb, width), lambda g, j: (blk(j), g))
        tok2 = pl.BlockSpec((None, tb, width), lambda g, j: (d, blk(j), g))
        tok_n = pl.BlockSpec((tb, width), lambda g, j: (blk_next(j), g))
        tok2_n = pl.BlockSpec((None, tb, width), lambda g, j: (d, blk_next(j), g))

        def s0_map(g, j):
            seq, _, _ = lay.seq_info(blk(j))
            return (jnp.maximum(seq - lay.n_ctx + 1, 0), d, g, 0, 0)

        def sfin_map(g, j):
            seq, _, _ = lay.seq_info(blk(j))
            return (seq, g, 0, 0)

        ins = [tok, tok, tok, tok2, tok2, tok2, pl.BlockSpec((None, None, RW_GSTEP, lanes, lanes), s0_map),
               tok_n, tok2_n, tok2_n]
        return ins, tok, pl.BlockSpec((None, RW_GSTEP, lanes, lanes), sfin_map)

    in_f, y_f, sf_f = views(0, lambda j: j, lambda j: jnp.minimum(j + 1, lay.nb - 1))
    in_b, y_b, sf_b = views(1, lambda j: lay.nb - 1 - j, lambda j: jnp.maximum(lay.nb - 2 - j, 0))
    consts = [k['cum'], k['strict'], k['incl'], k['head_rows'], k['blk_rows'], k['eye_w'], k['bd']]
    args = [r, v, kk, kd, a, lw, s0, kk, a, lw]
    return pl.pallas_call(
        functools.partial(_wkv_fused_kernel, lay=lay),
        grid=(ng // RW_GSTEP, lay.nb),
        in_specs=in_f + in_b + [full(x) for x in consts],
        out_specs=[y_f, y_b, sf_f, sf_b],
        out_shape=[jax.ShapeDtypeStruct((n, dm), F32), jax.ShapeDtypeStruct((n, dm), F32),
                   jax.ShapeDtypeStruct((lay.n_seq, ng, lanes, lanes), F32),
                   jax.ShapeDtypeStruct((lay.n_seq, ng, lanes, lanes), F32)],
        scratch_shapes=[pltpu.VMEM((RW_GSTEP, lanes, lanes), F32), pltpu.VMEM((RW_GSTEP, lanes, lanes), F32),
                        pltpu.VMEM((2, 2, tb, width), BF16)],
        compiler_params=_cparams(2), name="wkv_fused",
    )(*args, *args, *consts)


def _rwkv_post_kernel(yf_ref, yb_ref, b_ref, g_ref, lnw_ref, lnb_ref, bd_ref, o_ref):
    y = yf_ref[...] + yb_ref[...]
    bd = bd_ref[...]
    inv = 1.0 / RW_HEAD
    mu = _split_dot_r(y, bd, 2) * inv
    yc = y - mu
    var = _split_dot_r(yc * yc, bd, 2) * inv
    yn = yc * lax.rsqrt(var + RW_LN_EPS)
    out = yn * lnw_ref[...] + lnb_ref[...] + b_ref[...]
    o_ref[...] = (out * g_ref[...]).astype(o_ref.dtype)


def rwkv_post(lay, y_f, y_b, bonus, g, lnw, lnb):
    n, dm = y_f.shape
    lanes = RW_LANES
    tm = lay.tile(1024)
    bd_b = _wkv_constants()['bd_b']
    tok = pl.BlockSpec((tm, lanes), lambda i, c: (i, c))
    par = pl.BlockSpec((1, lanes), lambda i, c: (0, c))
    return pl.pallas_call(
        _rwkv_post_kernel,
        grid=(n // tm, dm // lanes),
        in_specs=[tok, tok, tok, tok, par, par, pl.BlockSpec(bd_b.shape, lambda i, c: (0, 0))],
        out_specs=tok,
        out_shape=jax.ShapeDtypeStruct((n, dm), BF16),
        compiler_params=_cparams(2), name="rwkv_post",
    )(y_f, y_b, bonus, g, lnw, lnb, bd_b)


def rwkv_layer(lay, x, mods, nw, p, s0):
    n, d = x.shape
    sh1, sc1, g1 = mods[0], mods[1], mods[2]
    xr, xw, xk, xv, xa, xg = rwkv_prep(lay, x, nw, sc1, sh1, p['mu'])
    tm = lay.tile(1024)
    lw, a, g = rwkv_lora(lay, xw, xa, xg, p['w1'], p['w2'], p['w0'], p['a1'], p['a2'], p['a0'],
                         p['g1'], p['g2'])
    r, v, kk, kd, bonus = rwkv_rkv(lay, xr, xk, xv, p['wr'], p['wk'], p['wv'], a, p['kk'], p['ka'], p['rk'])
    y_f, y_b, sfin_f, sfin_b = wkv_fused(lay, r, v, kk, kd, a, lw, s0)
    z = rwkv_post(lay, y_f, y_b, bonus, g, p['lnx_w'], p['lnx_b'])
    x = matmul_gated_residual(lay, z, p['wo'], x, g1, tm=tm, tn=512, name="rwkv_o")
    return x, jnp.stack([sfin_f, sfin_b], axis=1)


def _hgrn_constants():
    c = HG_CHUNK_TOKENS
    t = np.arange(c)[:, None]
    j = np.arange(c)[None, :]
    cums, masks_all = [], []
    for rev in (False, True):
        masks = []
        h = 1
        while h < c:
            upper = (t % (2 * h)) >= h
            same = (t // (2 * h)) == (j // (2 * h))
            if not rev:
                mask = same & upper & ((j % (2 * h)) < h)
            else:
                mask = same & (~upper) & ((j % (2 * h)) >= h)
            masks.append(mask)
            h *= 2
        masks.append(t == j)
        cums.append((j >= t) if rev else (j <= t))
        masks_all.append(np.stack(masks, 0))
    return jnp.asarray(np.stack(cums), BF16), jnp.asarray(np.stack(masks_all).astype(np.float32))


def _hgrn_level_exponents(g, gcum, rev):
    c, kdim = g.shape
    row = lax.broadcasted_iota(jnp.int32, g.shape, 0)
    nxt = pltpu.roll(g, c - 1, axis=0)
    prv = pltpu.roll(g, 1, axis=0)
    r2, r4 = row & 1, row & 3
    if not rev:
        x1 = jnp.where(r2 == 1, g, 0.0)
        x2 = jnp.where(r4 == 0, nxt, jnp.where(r4 == 2, g, jnp.where(r4 == 3, prv + g, 0.0)))
    else:
        x1 = jnp.where(r2 == 0, g, 0.0)
        x2 = jnp.where(r4 == 0, g + nxt, jnp.where(r4 == 1, g, jnp.where(r4 == 3, prv, 0.0)))
    xs = [x1, x2]
    h = 4
    while h < c:
        gr = gcum.reshape(c // (2 * h), 2 * h, kdim)
        ref = gr[:, h:h + 1, :] if rev else gr[:, h - 1:h, :]
        upper = lax.broadcasted_iota(jnp.int32, gr.shape, 1) >= h
        diff = gr - ref
        x = jnp.where(upper, -diff, diff) if rev else jnp.where(upper, diff, -diff)
        xs.append(x.reshape(c, kdim))
        h *= 2
    return xs


def _hgrn_units(units, masks_by_dir, cum_by_dir):
    c = HG_CHUNK_TOKENS
    gs = [jnp.log(f) for _, f, _, _ in units]
    gcums = [_split_dot(cum_by_dir[rev], g, 3) for g, (_, _, _, rev) in zip(gs, units)]
    outs = []
    pend = []
    for (q, f, iv, rev), g, gcum in zip(units, gs, gcums):
        k = 1.0 - f
        tot = gcum[0:1] if rev else gcum[c - 1:c]
        es = [jnp.exp(x) for x in _hgrn_level_exponents(g, gcum, rev)]
        pend.append((q, k, iv, rev, es, jnp.exp(gcum), jnp.exp(tot - gcum), jnp.exp(tot)))
    for q, k, iv, rev, es, eg, erest, etot in pend:
        masks = masks_by_dir[rev]
        a = masks[len(es)] * _bdot_nt(q, k)
        for lv, el in enumerate(es):
            a = a + masks[lv] * _bdot_nt(q * el, k * el)
        outs.append(dict(a=a, iv=iv, qe=(q * eg).astype(BF16), kdec=(k * erest).astype(BF16), decay=etot))
    for u in outs:
        u['av'] = _bdot(u['a'], u['iv'])
    for u in outs:
        u['upd'] = _bdot(u['iv'].T, u['kdec'])
    return outs


def _hgrn_kernel(*refs, lay):
    (qf, ff, if_, s0f, qb, fb, ib, s0b, cum_ref, mask_ref, of_ref, ob_ref, sff_ref, sfb_ref,
     sf_scr, sb_scr) = refs
    c = HG_CHUNK_TOKENS
    n_chunks = lay.tb // c
    j = pl.program_id(1)
    _, pos_f, cnt_f = lay.seq_info(j)
    _, pos_b, cnt_b = lay.seq_info(lay.nb - 1 - j)

    @pl.when(pos_f == 0)
    def _():
        sf_scr[...] = s0f[...]

    @pl.when(pos_b == cnt_b - 1)
    def _():
        sb_scr[...] = s0b[...]

    masks_by_dir = [mask_ref[0], mask_ref[1]]
    cum_by_dir = [cum_ref[0], cum_ref[1]]
    sl_f = [slice(ci * c, (ci + 1) * c) for ci in range(n_chunks)]
    sl_b = sl_f[::-1]
    cols = [slice(hi * HG_K, (hi + 1) * HG_K) for hi in range(HG_HSTEP)]
    chains = []
    for hi, cs in enumerate(cols):
        chains.append((sf_scr, hi, of_ref, [(sl, cs) for sl in sl_f],
                       [(qf[sl, cs], ff[sl, cs], if_[sl, cs], False) for sl in sl_f]))
        chains.append((sb_scr, hi, ob_ref, [(sl, cs) for sl in sl_b],
                       [(qb[sl, cs], fb[sl, cs], ib[sl, cs], True) for sl in sl_b]))
    done = _hgrn_units([u for ch in chains for u in ch[4]], masks_by_dir, cum_by_dir)
    pend = []
    for n_ch, (scr, hi, o_ref, where, _) in enumerate(chains):
        s = scr[hi]
        for ci in range(n_chunks):
            u = done[n_ch * n_chunks + ci]
            pend.append((o_ref, where[ci], u, s))
            s = s * u['decay'] + u['upd']
        scr[hi] = s
    for o_ref, (sl, cs), u, s_prev in pend:
        o_ref[sl, cs] = u['av'] + _bdot_nt(u['qe'], s_prev)

    @pl.when(pos_f == cnt_f - 1)
    def _():
        sff_ref[...] = sf_scr[...]

    @pl.when(pos_b == 0)
    def _():
        sfb_ref[...] = sb_scr[...]


def hgrn_scan(lay, q, f_fwd, f_bwd, iv, s0t):
    n, dm = q.shape
    tb = lay.tb
    nh = dm // HG_K
    width = HG_HSTEP * HG_K
    cum_m, masks = _hgrn_constants()
    full = lambda arr: pl.BlockSpec(arr.shape, lambda h, j: (0,) * arr.ndim)

    def views(d, blk):
        tok = pl.BlockSpec((tb, width), lambda h, j: (blk(j), h))

        def s0_map(h, j):
            seq, _, _ = lay.seq_info(blk(j))
            return (jnp.maximum(seq - lay.n_ctx + 1, 0), d, h, 0, 0)

        def sfin_map(h, j):
            seq, _, _ = lay.seq_info(blk(j))
            return (seq, h, 0, 0)

        ins = [tok, tok, tok, pl.BlockSpec((None, None, HG_HSTEP, HG_K, HG_K), s0_map)]
        return ins, tok, pl.BlockSpec((None, HG_HSTEP, HG_K, HG_K), sfin_map)

    in_f, o_f, sf_f = views(0, lambda j: j)
    in_b, o_b, sf_b = views(1, lambda j: lay.nb - 1 - j)
    return pl.pallas_call(
        functools.partial(_hgrn_kernel, lay=lay),
        grid=(nh // HG_HSTEP, lay.nb),
        in_specs=in_f + in_b + [full(cum_m), full(masks)],
        out_specs=[o_f, o_b, sf_f, sf_b],
        out_shape=[jax.ShapeDtypeStruct((n, dm), F32), jax.ShapeDtypeStruct((n, dm), F32),
                   jax.ShapeDtypeStruct((lay.n_seq, nh, HG_K, HG_K), F32),
                   jax.ShapeDtypeStruct((lay.n_seq, nh, HG_K, HG_K), F32)],
        scratch_shapes=[pltpu.VMEM((HG_HSTEP, HG_K, HG_K), F32), pltpu.VMEM((HG_HSTEP, HG_K, HG_K), F32)],
        compiler_params=_cparams(2), name="hgrn_scan",
    )(q, f_fwd, iv, s0t, q, f_bwd, iv, s0t, cum_m, masks)


HG_POST_HEADS = 4


def _hgrn_post_kernel(of_ref, ob_ref, g_ref, nw_ref, z_ref):
    for h in range(HG_POST_HEADS):
        cs = slice(h * HG_K, (h + 1) * HG_K)
        o = of_ref[:, cs] + ob_ref[:, cs]
        o = o * lax.rsqrt(jnp.mean(o * o, axis=-1, keepdims=True) + NORM_EPS) * nw_ref[...] * g_ref[:, cs]
        z_ref[:, cs] = o.astype(z_ref.dtype)


def hgrn_post(lay, o_f, o_b, gs, nw):
    n, dm = o_f.shape
    tm = lay.tile(1024)
    width = HG_POST_HEADS * HG_K
    tok = pl.BlockSpec((tm, width), lambda i, h: (i, h))
    return pl.pallas_call(
        _hgrn_post_kernel,
        grid=(n // tm, dm // width),
        in_specs=[tok, tok, tok, pl.BlockSpec((1, HG_K), lambda i, h: (0, 0))],
        out_specs=tok,
        out_shape=jax.ShapeDtypeStruct((n, dm), BF16),
        compiler_params=_cparams(2), name="hgrn_post",
    )(o_f, o_b, gs, nw)


def _epi_hgrn_in(accs, extras):
    lb = extras[0]
    q = _silu(accs[0])
    f0 = lb + (1.0 - lb) * jax.nn.sigmoid(accs[1])
    f1 = lb + (1.0 - lb) * jax.nn.sigmoid(accs[2])
    return [q, f0, f1, accs[3], _silu(accs[4])]


def hgrn_layer(lay, x, mods, nw, p, lb, s0t):
    n, d = x.shape
    sh1, sc1, g1 = mods[0], mods[1], mods[2]
    h = norm_mod(lay, x, nw, sc1, sh1, BF16)
    tm, tn = lay.tile(1024), 256
    extras = ((lb, pl.BlockSpec((1, tn), lambda i, j: (0, j))),)
    q, f0, f1, iv, gs = matmul(h, p['w_in'], _epi_hgrn_in, [F32] * 5, tm=tm, tn=tn, extras=extras,
                               name="hgrn_in")
    o_f, o_b, sfin_f, sfin_b = hgrn_scan(lay, q, f0, f1, iv, s0t)
    z = hgrn_post(lay, o_f, o_b, gs, p['norm_w'])
    x = matmul_gated_residual(lay, z, p['wo'], x, g1, tm=lay.tile(1024), tn=512, name="hgrn_o")
    return x, jnp.stack([sfin_f, sfin_b], axis=1)


ML_DOWN_COLS = 1280
ML_KR_OFF = ML_Q_LORA + ML_KV_LORA
ML_KRS_OFF = ML_KR_OFF + LANES_V7X


def _rms(x, w):
    return x * lax.rsqrt(jnp.mean(x * x, axis=-1, keepdims=True) + NORM_EPS) * w


def _mla_mid_kernel(dn_ref, qw_ref, kvw_ref, cos_ref, sin_ref, qn_ref, ckv_ref, kr_ref):
    dn = dn_ref[...]
    qn_ref[...] = _rms(dn[:, :ML_Q_LORA], qw_ref[...]).astype(qn_ref.dtype)
    ckv_ref[...] = _rms(dn[:, ML_Q_LORA:ML_KR_OFF], kvw_ref[...])
    kr = dn[:, ML_KR_OFF:ML_KR_OFF + ML_ROPE]
    krs = dn[:, ML_KRS_OFF:ML_KRS_OFF + ML_ROPE]
    kr_ref[...] = kr * cos_ref[...] + krs * sin_ref[...]


def mla_mid(lay, dn, qw, kvw, cos, sin):
    n = dn.shape[0]
    tm = lay.tile(512)
    return pl.pallas_call(
        _mla_mid_kernel,
        grid=(n // tm,),
        in_specs=[pl.BlockSpec((tm, ML_DOWN_COLS), lambda i: (i, 0)),
                  pl.BlockSpec((1, ML_Q_LORA), lambda i: (0, 0)),
                  pl.BlockSpec((1, ML_KV_LORA), lambda i: (0, 0)),
                  pl.BlockSpec((tm, ML_ROPE), lambda i: (i, 0)),
                  pl.BlockSpec((tm, ML_ROPE), lambda i: (i, 0))],
        out_specs=[pl.BlockSpec((tm, ML_Q_LORA), lambda i: (i, 0)),
                   pl.BlockSpec((tm, ML_KV_LORA), lambda i: (i, 0)),
                   pl.BlockSpec((tm, ML_ROPE), lambda i: (i, 0))],
        out_shape=[jax.ShapeDtypeStruct((n, ML_Q_LORA), BF16),
                   jax.ShapeDtypeStruct((n, ML_KV_LORA), F32),
                   jax.ShapeDtypeStruct((n, ML_ROPE), F32)],
        compiler_params=_cparams(1), name="mla_mid",
    )(dn, qw, kvw, cos, sin)


ML_QSCALE = math.log2(math.e) / math.sqrt(ML_NOPE + ML_ROPE)


def _epi_qscale(accs, extras):
    return [accs[0] * ML_QSCALE]


def _epi_rope(accs, extras):
    cos, sin = extras
    reps = accs[0].shape[1] // cos.shape[1]
    cos, sin = jnp.concatenate([cos] * reps, axis=1), jnp.concatenate([sin] * reps, axis=1)
    return [(accs[0] * cos + accs[1] * sin) * ML_QSCALE]


ML_KEY_SPLITS = 2


def _attn_kernel(qn_ref, qr_ref, kn_ref, kr_ref, v_ref, o_ref, kc_scr, vt_scr):
    @pl.when(pl.program_id(2) == 0)
    def _():
        for h in range(2):
            kc_scr[h, :, :ML_NOPE] = kn_ref[:, h * ML_NOPE:(h + 1) * ML_NOPE]
            kc_scr[h, :, ML_NOPE:] = kr_ref[...]
            vt_scr[h] = v_ref[:, h * ML_V:(h + 1) * ML_V].astype(F32).T.astype(vt_scr.dtype)

    k_len = kc_scr.shape[1]
    kh = k_len // ML_KEY_SPLITS
    scores = []
    for h in range(2):
        q = jnp.concatenate([qn_ref[:, h * ML_NOPE:(h + 1) * ML_NOPE],
                             qr_ref[:, h * ML_ROPE:(h + 1) * ML_ROPE]], axis=1)
        scores.append([lax.dot_general(kc_scr[h, kb * kh:(kb + 1) * kh, :], q, (((1,), (1,)), ((), ())),
                                       preferred_element_type=F32) for kb in range(ML_KEY_SPLITS)])
    for h in range(2):
        m_acc = l_acc = o_acc = None
        for kb, st in enumerate(scores[h]):
            m = jnp.max(st, axis=0, keepdims=True)
            p = jnp.exp2(st - m)
            l = jnp.sum(p, axis=0, keepdims=True)
            ot = jnp.dot(vt_scr[h, :, kb * kh:(kb + 1) * kh], p.astype(BF16),
                         preferred_element_type=F32)
            if m_acc is None:
                m_acc, l_acc, o_acc = m, l, ot
            else:
                m_new = jnp.maximum(m_acc, m)
                c_old, c_new = jnp.exp2(m_acc - m_new), jnp.exp2(m - m_new)
                l_acc = l_acc * c_old + l * c_new
                o_acc = o_acc * c_old + ot * c_new
                m_acc = m_new
        o_ref[:, h * ML_V:(h + 1) * ML_V] = (o_acc / l_acc).T.astype(o_ref.dtype)


def attention(qn, qr, kn, kr, v, *, n_seq, q_len, k_len, row0, tq):
    heads2 = qn.shape[1] // (2 * ML_NOPE)
    qb = q_len // tq
    rb0 = row0 // tq
    return pl.pallas_call(
        _attn_kernel,
        grid=(n_seq, heads2, qb),
        in_specs=[pl.BlockSpec((tq, 2 * ML_NOPE), lambda s, h, i: (rb0 + s * qb + i, h)),
                  pl.BlockSpec((tq, 2 * ML_ROPE), lambda s, h, i: (rb0 + s * qb + i, h)),
                  pl.BlockSpec((k_len, 2 * ML_NOPE), lambda s, h, i: (s, h)),
                  pl.BlockSpec((k_len, ML_ROPE), lambda s, h, i: (s, 0)),
                  pl.BlockSpec((k_len, 2 * ML_V), lambda s, h, i: (s, h))],
        out_specs=pl.BlockSpec((tq, 2 * ML_V), lambda s, h, i: (s * qb + i, h)),
        out_shape=jax.ShapeDtypeStruct((n_seq * q_len, heads2 * 2 * ML_V), BF16),
        scratch_shapes=[pltpu.VMEM((2, k_len, ML_NOPE + ML_ROPE), BF16), pltpu.VMEM((2, ML_V, k_len), BF16)],
        compiler_params=_cparams(3), name="mla_attn",
    )(qn, qr, kn, kr, v)


def mla_layer(lay, x, mods, nw, p, cache_ckv, cache_kr, cos, sin, cos2, sin2):
    n, d = x.shape
    sh1, sc1, g1 = mods[0], mods[1], mods[2]
    h = norm_mod(lay, x, nw, sc1, sh1, BF16)
    tm = lay.tile(512)
    dn = matmul(h, [p['w_down']], _epi_plain, [F32], tm=tm, tn=ML_DOWN_COLS, name="mla_down")[0]
    qlat, ckv, kr = mla_mid(lay, dn, p['qnorm_w'], p['kvnorm_w'], cos, sin)
    tm_q = lay.tile(1024)
    qn = matmul(qlat, [p['w_uq_nope']], _epi_qscale, [BF16], tm=tm_q, tn=1024, name="mla_qn")[0]
    tw = cos2.shape[1]
    extras = ((cos2, pl.BlockSpec((tm_q, tw), lambda i, j: (i, 0))),
              (sin2, pl.BlockSpec((tm_q, tw), lambda i, j: (i, 0))))
    qr = matmul(qlat, [p['w_uq_rope'], p['w_uq_rope_sw']], _epi_rope, [BF16], tm=tm_q,
                tn=p['w_uq_rope'].shape[1], extras=extras, name="mla_qr")[0]
    nc, past = lay.nc, cache_ckv.shape[1]
    ckv_b, kr_b = ckv.astype(BF16), kr.astype(BF16)
    kn_c, v_c = matmul(ckv_b[:nc], [p['w_ukn'], p['w_uv']], _epi_plain, [BF16, BF16],
                       tm=lay.tile(512), tn=512, name="mla_kv_ctx")
    o_c = attention(qn, qr, kn_c, kr_b[:nc], v_c, n_seq=lay.n_ctx, q_len=lay.ctx_len,
                    k_len=lay.ctx_len, row0=0, tq=min(256, lay.ctx_len))
    k_len = lay.lat_len + past
    ckv_l = jnp.concatenate([ckv_b[nc:].reshape(lay.n_lat, lay.lat_len, -1), cache_ckv.astype(BF16)],
                            axis=1).reshape(lay.n_lat * k_len, -1)
    kr_l = jnp.concatenate([kr_b[nc:].reshape(lay.n_lat, lay.lat_len, -1), cache_kr.astype(BF16)],
                           axis=1).reshape(lay.n_lat * k_len, -1)
    bf16_rows = 2 * SUBLANES_V7X
    tk = k_len // 2 if (k_len // 2) % bf16_rows == 0 else math.gcd(k_len, 512)
    kn_l, v_l = matmul(ckv_l, [p['w_ukn'], p['w_uv']], _epi_plain, [BF16, BF16], tm=tk, tn=1024,
                       name="mla_kv_lat")
    o_l = attention(qn, qr, kn_l, kr_l, v_l, n_seq=lay.n_lat, q_len=lay.lat_len, k_len=k_len,
                    row0=nc, tq=min(256, lay.lat_len))
    o = jnp.concatenate([o_c, o_l], axis=0)
    x = matmul_gated_residual(lay, o, p['wo'], x, g1, tm=lay.tile(1024), tn=512, name="mla_o")
    return x, ckv[:nc], kr[:nc]


def _rope_tables(lay):
    t = lay.lat_len
    rows = t // GRID_W
    rr = jnp.broadcast_to(jnp.arange(rows, dtype=F32)[:, None], (rows, GRID_W)).reshape(-1)
    cc = jnp.broadcast_to(jnp.arange(GRID_W, dtype=F32)[None, :], (rows, GRID_W)).reshape(-1)
    nf = ML_ROPE // 4
    inv = ROPE_BASE ** (-jnp.arange(nf, dtype=F32) / nf)
    ar, ac = rr[:, None] * inv, cc[:, None] * inv
    cos = jnp.concatenate([jnp.cos(ar), jnp.cos(ar), jnp.cos(ac), jnp.cos(ac)], axis=-1)
    sin = jnp.concatenate([-jnp.sin(ar), jnp.sin(ar), -jnp.sin(ac), jnp.sin(ac)], axis=-1)
    cos = jnp.concatenate([jnp.ones((lay.nc, ML_ROPE), F32), jnp.tile(cos, (lay.n_lat, 1))], axis=0)
    sin = jnp.concatenate([jnp.zeros((lay.nc, ML_ROPE), F32), jnp.tile(sin, (lay.n_lat, 1))], axis=0)
    return cos, sin


def _swap_cols(w):
    k, c = w.shape
    w4 = w.reshape(k, c // 32, 2, 16)
    return w4[:, :, ::-1, :].reshape(k, c)


def ffn(lay, x, mods, nw, w_a, w_b, w_out):
    sh2, sc2, g2 = mods[3], mods[4], mods[5]
    h = norm_mod(lay, x, nw, sc2, sh2, BF16)
    act = matmul(h, [w_a, w_b], _epi_swiglu, [BF16], tm=lay.tile(1024), tn=512, name="ffn_in")[0]
    return matmul_gated_residual(lay, act, w_out, x, g2, tm=lay.tile(1024), tn=512, name="ffn_out")


def _block_diag_states(s):
    b, two, h, n, _ = s.shape
    s = s.reshape(b, two, h // RW_GROUP, RW_GROUP, n, n)
    eye = jnp.eye(RW_GROUP, dtype=s.dtype)
    out = jnp.einsum('bdghvk,hi->bdghvik', s, eye)
    return out.reshape(b, two, h // RW_GROUP, RW_GROUP * n, RW_GROUP * n)


def _diag_blocks(s):
    b, two, g, l, _ = s.shape
    n = l // RW_GROUP
    s = s.reshape(b, two, g, RW_GROUP, n, RW_GROUP, n)
    s = jnp.moveaxis(jnp.diagonal(s, axis1=3, axis2=5), -1, 3)
    return s.reshape(b, two, g * RW_GROUP, n, n)


def kernel(x_prompt, x_sample, state_rwkv, state_hgrn, cache_ckv, cache_krope, c, c_ctx, ada_w, ada_b, norm1_w, norm2_w, ffn_w_in, ffn_w_out, final_norm_w, rw_mu, rw_wr, rw_wk, rw_wv, rw_wo, rw_w0, rw_w1, rw_w2, rw_a0, rw_a1, rw_a2, rw_g1, rw_g2, rw_kk, rw_ka, rw_rk, rw_lnx_w, rw_lnx_b, hg_w_in, hg_lb, hg_norm_w, hg_wo, ml_w_down, ml_qnorm_w, ml_kvnorm_w, ml_w_uq, ml_w_ukv, ml_wo):
    n_ctx, ctx_len, d = x_prompt.shape
    n_lat, lat_len, _ = x_sample.shape
    depth = ada_w.shape[0]
    lay = Layout(n_ctx, ctx_len, n_lat, lat_len)
    d_ff = ffn_w_out.shape[1]
    x = jnp.concatenate([x_prompt.reshape(lay.nc, d), x_sample.reshape(n_lat * lat_len, d)], axis=0)

    n_cond = -(-(1 + n_lat) // SUBLANES_V7X) * SUBLANES_V7X
    cond = jnp.zeros((n_cond, d), F32).at[0].set(c_ctx).at[1:1 + n_lat].set(c)
    mod_all = adaln(cond, ada_w, ada_b)
    mod_all = mod_all.reshape(depth, n_cond, 6, 1, d).transpose(0, 2, 1, 3, 4)

    lb_table = jnp.cumsum(jax.nn.softmax(hg_lb.astype(F32), axis=0), axis=0)
    lb_table = lb_table - lb_table[0]
    cos, sin = _rope_tables(lay)
    cos2, sin2 = jnp.tile(cos, (1, 2)), jnp.tile(sin, (1, 2))
    bf = lambda t: t.astype(BF16)

    new_rwkv, new_hgrn, new_ckv, new_krope = [], [], [], []
    for l in range(depth):
        kind, j = l % 3, l // 3
        mods = mod_all[l]
        if kind == 0:
            pad1 = lambda w: jnp.pad(w, ((0, 0), (0, 0), (0, LORA_PAD - w.shape[2])))
            pad2 = lambda w: jnp.pad(w, ((0, 0), (0, LORA_PAD - w.shape[1]), (0, 0)))
            p = {'mu': rw_mu[j], 'wr': bf(rw_wr[j]), 'wk': bf(rw_wk[j]), 'wv': bf(rw_wv[j]),
                 'wo': bf(rw_wo[j]),
                 'w0': rw_w0[j].reshape(2, 1, d), 'w1': bf(pad1(rw_w1[j])), 'w2': bf(pad2(rw_w2[j])),
                 'a0': rw_a0[j].reshape(2, 1, d), 'a1': bf(pad1(rw_a1[j])), 'a2': bf(pad2(rw_a2[j])),
                 'g1': bf(rw_g1[j]), 'g2': bf(rw_g2[j]),
                 'kk': rw_kk[j].reshape(1, d), 'ka': rw_ka[j].reshape(1, d),
                 'rk': rw_rk[j].reshape(2, 1, d),
                 'lnx_w': rw_lnx_w[j].reshape(1, d), 'lnx_b': rw_lnx_b[j].reshape(1, d)}
            s_lat = _block_diag_states(state_rwkv[:, j].astype(F32))
            s0 = jnp.concatenate([jnp.zeros((1,) + s_lat.shape[1:], F32), s_lat], axis=0)
            x, sfin = rwkv_layer(lay, x, mods, norm1_w[l], p, s0)
            new_rwkv.append(_diag_blocks(sfin[:n_ctx]))
        elif kind == 1:
            hk = d
            w_in = hg_w_in[j]
            p = {'w_in': [bf(w_in[:, i * hk:(i + 1) * hk]) for i in range(5)],
                 'norm_w': hg_norm_w[j].reshape(1, HG_K), 'wo': bf(hg_wo[j])}
            s_lat = jnp.swapaxes(state_hgrn[:, j].astype(F32), -1, -2)
            s0t = jnp.concatenate([jnp.zeros((1,) + s_lat.shape[1:], F32), s_lat], axis=0)
            x, sfin = hgrn_layer(lay, x, mods, norm1_w[l], p, lb_table[l].reshape(1, d), s0t)
            new_hgrn.append(jnp.swapaxes(sfin[:n_ctx], -1, -2))
        else:
            wd = ml_w_down[j]
            kr_w = wd[:, ML_KR_OFF:]
            zpad = jnp.zeros((d, LANES_V7X - ML_ROPE), wd.dtype)
            w_down = jnp.concatenate([wd, zpad, _swap_cols(kr_w), zpad], axis=1)
            wq = ml_w_uq[j].reshape(ML_Q_LORA, ML_H, ML_NOPE + ML_ROPE)
            wq_n = wq[:, :, :ML_NOPE].reshape(ML_Q_LORA, ML_H * ML_NOPE)
            wq_r = wq[:, :, ML_NOPE:].reshape(ML_Q_LORA, ML_H * ML_ROPE)
            wkv = ml_w_ukv[j].reshape(ML_KV_LORA, ML_H, ML_NOPE + ML_V)
            p = {'w_down': bf(w_down), 'qnorm_w': ml_qnorm_w[j].reshape(1, -1),
                 'kvnorm_w': ml_kvnorm_w[j].reshape(1, -1),
                 'w_uq_nope': bf(wq_n), 'w_uq_rope': bf(wq_r), 'w_uq_rope_sw': bf(_swap_cols(wq_r)),
                 'w_ukn': bf(wkv[:, :, :ML_NOPE].reshape(ML_KV_LORA, ML_H * ML_NOPE)),
                 'w_uv': bf(wkv[:, :, ML_NOPE:].reshape(ML_KV_LORA, ML_H * ML_V)),
                 'wo': bf(ml_wo[j])}
            x, ckv_c, kr_c = mla_layer(lay, x, mods, norm1_w[l], p, cache_ckv[:, j], cache_krope[:, j],
                                       cos, sin, cos2, sin2)
            new_ckv.append(ckv_c.reshape(n_ctx, ctx_len, ML_KV_LORA))
            new_krope.append(kr_c.reshape(n_ctx, ctx_len, ML_ROPE))
        w_in = ffn_w_in[l]
        x = ffn(lay, x, mods, norm2_w[l], bf(w_in[:, :d_ff]), bf(w_in[:, d_ff:]), bf(ffn_w_out[l]))

    y_prompt = rmsnorm_rows(lay, x, final_norm_w, 0, lay.nc).reshape(n_ctx, ctx_len, d)
    y_sample = rmsnorm_rows(lay, x, final_norm_w, lay.nc, lay.n - lay.nc).reshape(n_lat, lat_len, d)
    return (y_prompt, y_sample, jnp.stack(new_rwkv, axis=1), jnp.stack(new_hgrn, axis=1),
            jnp.stack(new_ckv, axis=1), jnp.stack(new_krope, axis=1))
```

```python
import functools
import math

import numpy as np
import jax
import jax.numpy as jnp
from jax import lax
from jax.experimental import pallas as pl
from jax.experimental.pallas import tpu as pltpu

F32 = jnp.float32
BF16 = jnp.bfloat16

LANES_V7X = 128
SUBLANES_V7X = 8
VMEM_BYTES_V7X = 64 * 1024 * 1024
VMEM_LIMIT = VMEM_BYTES_V7X - 8 * 1024 * 1024

NORM_EPS = 1e-6
RW_HEAD = 64
RW_LN_EPS = 64e-5
RW_GROUP = 4
RW_LANES = RW_GROUP * RW_HEAD
RW_CHUNK = 64
RW_GSTEP = 2
HG_K = 128
HG_CHUNK_TOKENS = 64
HG_HSTEP = 2
ML_H = 16
ML_NOPE = 128
ML_ROPE = 64
ML_V = 128
ML_Q_LORA = 512
ML_KV_LORA = 512
GRID_W = 64
ROPE_BASE = 10000.0
LORA_PAD = 128


class Layout:
    def __init__(self, n_ctx, ctx_len, n_lat, lat_len):
        self.n_ctx, self.ctx_len, self.n_lat, self.lat_len = n_ctx, ctx_len, n_lat, lat_len
        self.nc = n_ctx * ctx_len
        self.n = self.nc + n_lat * lat_len
        self.tb = min(256, ctx_len)
        assert ctx_len % self.tb == 0 and lat_len % self.tb == 0 and self.tb % RW_CHUNK == 0
        self.nb = self.n // self.tb
        self.nb_ctx = self.nc // self.tb
        self.bps_ctx = ctx_len // self.tb
        self.bps_lat = lat_len // self.tb
        self.n_seq = n_ctx + n_lat

    def tile(self, want):
        t = want
        while self.nc % t or self.lat_len % t:
            t //= 2
        return t

    def cond_of_tile(self, i, tm):
        row = i * tm
        return jnp.where(row < self.nc, 0, 1 + (row - self.nc) // self.lat_len)

    def seq_info(self, blk):
        is_ctx = blk < self.nb_ctx
        lat = blk - self.nb_ctx
        seq = jnp.where(is_ctx, blk // self.bps_ctx, self.n_ctx + lat // self.bps_lat)
        pos = jnp.where(is_ctx, blk % self.bps_ctx, lat % self.bps_lat)
        cnt = jnp.where(is_ctx, self.bps_ctx, self.bps_lat)
        return seq, pos, cnt


def _cparams(n_axes):
    return pltpu.CompilerParams(dimension_semantics=("arbitrary",) * n_axes, vmem_limit_bytes=VMEM_LIMIT)


def _bdot(a, b):
    return jnp.dot(a.astype(BF16), b.astype(BF16), preferred_element_type=F32)


def _bdot_nt(a, b):
    return lax.dot_general(a.astype(BF16), b.astype(BF16), (((1,), (1,)), ((), ())),
                           preferred_element_type=F32)


def _silu(x):
    return x * jax.nn.sigmoid(x)


def _adaln_kernel(c_ref, w_ref, b_ref, o_ref):
    a = _silu(c_ref[...]).astype(BF16)
    o_ref[...] = jnp.dot(a, w_ref[...].astype(BF16), preferred_element_type=F32) + b_ref[...]


def adaln(cond, ada_w, ada_b):
    depth, d, d6 = ada_w.shape
    r = cond.shape[0]
    tn = 1024
    return pl.pallas_call(
        _adaln_kernel,
        grid=(depth, d6 // tn),
        in_specs=[pl.BlockSpec((r, d), lambda l, j: (0, 0)),
                  pl.BlockSpec((None, d, tn), lambda l, j: (l, 0, j)),
                  pl.BlockSpec((None, 1, tn), lambda l, j: (l, 0, j))],
        out_specs=pl.BlockSpec((None, r, tn), lambda l, j: (l, 0, j)),
        out_shape=jax.ShapeDtypeStruct((depth, r, d6), F32),
        compiler_params=_cparams(2), name="adaln",
    )(cond, ada_w, ada_b.reshape(depth, 1, d6))


def _norm_mod(x, nw, sc, sh):
    y = x * lax.rsqrt(jnp.mean(x * x, axis=-1, keepdims=True) + NORM_EPS)
    return (y * nw) * (1.0 + sc) + sh


def _norm_mod_kernel(x_ref, nw_ref, sc_ref, sh_ref, o_ref):
    o_ref[...] = _norm_mod(x_ref[...], nw_ref[...], sc_ref[...], sh_ref[...]).astype(o_ref.dtype)


def norm_mod(lay, x, nw, sc, sh, out_dtype):
    n, d = x.shape
    tm = lay.tile(512)
    cmap = lambda i: (lay.cond_of_tile(i, tm), 0, 0)
    return pl.pallas_call(
        _norm_mod_kernel,
        grid=(n // tm,),
        in_specs=[pl.BlockSpec((tm, d), lambda i: (i, 0)),
                  pl.BlockSpec((1, d), lambda i: (0, 0)),
                  pl.BlockSpec((None, 1, d), cmap),
                  pl.BlockSpec((None, 1, d), cmap)],
        out_specs=pl.BlockSpec((tm, d), lambda i: (i, 0)),
        out_shape=jax.ShapeDtypeStruct((n, d), out_dtype),
        compiler_params=_cparams(1), name="norm_mod",
    )(x, nw.reshape(1, d), sc, sh)


def _rmsnorm_kernel(x_ref, w_ref, o_ref):
    x = x_ref[...]
    o_ref[...] = x * lax.rsqrt(jnp.mean(x * x, axis=-1, keepdims=True) + NORM_EPS) * w_ref[...]


def rmsnorm_rows(lay, x, w, row0, rows):
    d = x.shape[1]
    tm = lay.tile(512)
    b0 = row0 // tm
    return pl.pallas_call(
        _rmsnorm_kernel,
        grid=(rows // tm,),
        in_specs=[pl.BlockSpec((tm, d), lambda i: (b0 + i, 0)), pl.BlockSpec((1, d), lambda i: (0, 0))],
        out_specs=pl.BlockSpec((tm, d), lambda i: (i, 0)),
        out_shape=jax.ShapeDtypeStruct((rows, d), x.dtype),
        compiler_params=_cparams(1), name="final_norm",
    )(x, w.reshape(1, d))


def _mm_kernel(*refs, n_w, n_e, epi):
    a = refs[0][...]
    accs = [jnp.dot(a, refs[1 + i][...], preferred_element_type=F32) for i in range(n_w)]
    extras = [refs[1 + n_w + i][...] for i in range(n_e)]
    outs = epi(accs, extras)
    o_refs = refs[1 + n_w + n_e:]
    for o_ref, val in zip(o_refs, outs):
        o_ref[...] = val.astype(o_ref.dtype)


def matmul(a, ws, epi, out_dtypes, *, tm, tn, extras=(), name):
    m, k = a.shape
    nw = ws[0].shape[1]
    assert m % tm == 0 and nw % tn == 0
    in_specs = [pl.BlockSpec((tm, k), lambda i, j: (i, 0))]
    in_specs += [pl.BlockSpec((k, tn), lambda i, j: (0, j)) for _ in ws]
    in_specs += [spec for _, spec in extras]
    outs = pl.pallas_call(
        functools.partial(_mm_kernel, n_w=len(ws), n_e=len(extras), epi=epi),
        grid=(m // tm, nw // tn),
        in_specs=in_specs,
        out_specs=[pl.BlockSpec((tm, tn), lambda i, j: (i, j)) for _ in out_dtypes],
        out_shape=[jax.ShapeDtypeStruct((m, nw), dt) for dt in out_dtypes],
        compiler_params=_cparams(2), name=name,
    )(a, *ws, *[arr for arr, _ in extras])
    return outs


def _epi_plain(accs, extras):
    return accs


def _epi_gated_residual(accs, extras):
    x, g = extras
    return [x + g * accs[0]]


def matmul_gated_residual(lay, a, w, x, gate, *, tm, tn, name):
    extras = ((x, pl.BlockSpec((tm, tn), lambda i, j: (i, j))),
              (gate, pl.BlockSpec((None, 1, tn), lambda i, j: (lay.cond_of_tile(i, tm), 0, j))))
    return matmul(a, [w], _epi_gated_residual, [F32], tm=tm, tn=tn, extras=extras, name=name)[0]


def _epi_swiglu(accs, extras):
    return [_silu(accs[0]) * accs[1]]


RWKV_PREP_SLAB = LANES_V7X


def _rwkv_prep_kernel(x_ref, xp_ref, xn_ref, nw_ref, sc_ref, sh_ref, mu_ref, *o_refs, lay):
    i = pl.program_id(0)
    _, pos, cnt = lay.seq_info(i)
    def inv_rms(x):
        return lax.rsqrt(jnp.mean(x * x, axis=-1, keepdims=True) + NORM_EPS)

    tb, d = x_ref.shape
    inv = inv_rms(x_ref[...])
    inv_p = inv_rms(xp_ref[SUBLANES_V7X - 1:SUBLANES_V7X, :])
    inv_n = inv_rms(xn_ref[0:1, :])
    keep_p = jnp.where(pos == 0, 0.0, 1.0)
    keep_n = jnp.where(pos == cnt - 1, 0.0, 1.0)
    row = lax.broadcasted_iota(jnp.int32, (tb, RWKV_PREP_SLAB), 0)
    for c0 in range(0, d, RWKV_PREP_SLAB):
        cs = slice(c0, c0 + RWKV_PREP_SLAB)
        nw, sc, sh = nw_ref[:, cs], sc_ref[:, cs], sh_ref[:, cs]
        h = ((x_ref[:, cs] * inv) * nw) * (1.0 + sc) + sh
        hp = (((xp_ref[SUBLANES_V7X - 1:SUBLANES_V7X, cs] * inv_p) * nw) * (1.0 + sc) + sh) * keep_p
        hn = (((xn_ref[0:1, cs] * inv_n) * nw) * (1.0 + sc) + sh) * keep_n
        prev = jnp.where(row == 0, hp, pltpu.roll(h, 1, axis=0))
        nxt = jnp.where(row == tb - 1, hn, pltpu.roll(h, tb - 1, axis=0))
        xx = 0.5 * (prev + nxt) - h
        for idx, o_ref in enumerate(o_refs):
            o_ref[:, cs] = (h + xx * mu_ref[idx:idx + 1, cs]).astype(o_ref.dtype)


def rwkv_prep(lay, x, nw, sc, sh, mu):
    n, d = x.shape
    tb = lay.tb
    r8 = tb // SUBLANES_V7X
    last8 = n // SUBLANES_V7X - 1
    cmap = lambda i: (lay.cond_of_tile(i, tb), 0, 0)
    return pl.pallas_call(
        functools.partial(_rwkv_prep_kernel, lay=lay),
        grid=(n // tb,),
        in_specs=[pl.BlockSpec((tb, d), lambda i: (i, 0)),
                  pl.BlockSpec((SUBLANES_V7X, d), lambda i: (jnp.maximum(i * r8 - 1, 0), 0)),
                  pl.BlockSpec((SUBLANES_V7X, d), lambda i: (jnp.minimum((i + 1) * r8, last8), 0)),
                  pl.BlockSpec((1, d), lambda i: (0, 0)),
                  pl.BlockSpec((None, 1, d), cmap),
                  pl.BlockSpec((None, 1, d), cmap),
                  pl.BlockSpec((6, d), lambda i: (0, 0))],
        out_specs=[pl.BlockSpec((tb, d), lambda i: (i, 0))] * 6,
        out_shape=[jax.ShapeDtypeStruct((n, d), BF16)] * 6,
        compiler_params=_cparams(1), name="rwkv_prep",
    )(x, x, x, nw.reshape(1, d), sc, sh, mu)


RW_LOG_DECAY_SCALE = -math.exp(-0.5)


def _rwkv_lora_kernel(xw_ref, xa_ref, xg_ref, w1_ref, w2_ref, w0_ref, a1_ref, a2_ref, a0_ref,
                      g1_ref, g2_ref, lw_ref, a_ref, g_ref):
    xw, xa, xg = xw_ref[...], xa_ref[...], xg_ref[...]
    for d in range(2):
        t = jnp.tanh(jnp.dot(xw, w1_ref[d], preferred_element_type=F32))
        wl = w0_ref[d] + _bdot(t, w2_ref[d])
        lw_ref[d] = RW_LOG_DECAY_SCALE * jax.nn.sigmoid(wl)
        t = jnp.dot(xa, a1_ref[d], preferred_element_type=F32)
        a_ref[d] = jax.nn.sigmoid(a0_ref[d] + _bdot(t, a2_ref[d]))
    t = jax.nn.sigmoid(jnp.dot(xg, g1_ref[...], preferred_element_type=F32))
    g_ref[...] = _bdot(t, g2_ref[...])


def rwkv_lora(lay, xw, xa, xg, w1, w2, w0, a1, a2, a0, g1, g2):
    n, d = xw.shape
    tm = lay.tile(256)
    full = lambda arr: pl.BlockSpec(arr.shape, lambda i: (0,) * arr.ndim)
    row = pl.BlockSpec((tm, d), lambda i: (i, 0))
    return pl.pallas_call(
        _rwkv_lora_kernel,
        grid=(n // tm,),
        in_specs=[row, row, row] + [full(t) for t in (w1, w2, w0, a1, a2, a0, g1, g2)],
        out_specs=[pl.BlockSpec((2, tm, d), lambda i: (0, i, 0)),
                   pl.BlockSpec((2, tm, d), lambda i: (0, i, 0)),
                   row],
        out_shape=[jax.ShapeDtypeStruct((2, n, d), F32), jax.ShapeDtypeStruct((2, n, d), F32),
                   jax.ShapeDtypeStruct((n, d), F32)],
        compiler_params=_cparams(1), name="rwkv_lora",
    )(xw, xa, xg, w1, w2, w0, a1, a2, a0, g1, g2)


def _wkv_constants():
    c, g, hd = RW_CHUNK, RW_GROUP, RW_HEAD
    gc, lanes = g * c, g * hd
    t = np.arange(c)
    cum = np.stack([(t[None, :] <= t[:, None]), (t[None, :] >= t[:, None])])
    tr = np.arange(c)[:, None]
    tc = np.arange(gc)[None, :] % c
    strict = np.stack([tc < tr, tc > tr])
    incl = np.stack([tc <= tr, tc >= tr])
    head_rows = np.arange(gc)[:, None] // c == np.arange(lanes)[None, :] // hd
    blk_rows = np.arange(gc)[:, None] // c == np.arange(gc)[None, :] // c
    bd = np.arange(lanes)[:, None] // hd == np.arange(lanes)[None, :] // hd
    eye_w = tr == tc
    return dict(cum=jnp.asarray(cum, BF16), strict=jnp.asarray(strict, F32), incl=jnp.asarray(incl, F32),
                head_rows=jnp.asarray(head_rows, BF16), blk_rows=jnp.asarray(blk_rows, BF16),
                bd=jnp.asarray(bd, F32), bd_b=jnp.asarray(bd, BF16), eye_w=jnp.asarray(eye_w, F32))


def _split_dot(m01, x, passes):
    acc, rem = None, x
    for _ in range(passes):
        part = rem.astype(BF16)
        term = jnp.dot(m01, part, preferred_element_type=F32)
        acc = term if acc is None else acc + term
        rem = rem - part.astype(F32)
    return acc


def _split_dot_r(x, m01, passes):
    acc, rem = None, x
    for _ in range(passes):
        part = rem.astype(BF16)
        term = jnp.dot(part, m01, preferred_element_type=F32)
        acc = term if acc is None else acc + term
        rem = rem - part.astype(F32)
    return acc


def _tile_rows(x, mask_b):
    return jnp.concatenate([x.astype(BF16)] * RW_GROUP, axis=0) * mask_b


def _wkv_prep1(raw):
    return [_split_dot(u[6], u[5], 3) for u in raw]


def _wkv_prep2(raw, cums, head_rows):
    c = RW_CHUNK
    mids = []
    for (r, v, kk, kd, a, lw, _, _, _), cum in zip(raw, cums):
        tot = jnp.sum(lw, axis=0, keepdims=True)
        kka = kk * a
        e_inv, e_rest = jnp.exp(-cum), jnp.exp(tot - cum)
        q2 = jnp.concatenate([kk * jnp.exp(cum - lw), r * jnp.exp(cum)], axis=0).astype(BF16)
        mids.append(dict(q2=q2, kdh=_tile_rows(kd * e_inv, head_rows), kkah=_tile_rows(kka * e_inv, head_rows),
                         vbd=_tile_rows(v, head_rows), v=v, decay=jnp.exp(tot),
                         kw=jnp.concatenate([kd * e_rest, -(kka * e_rest)], axis=0).astype(BF16)))
    s1s = [_bdot_nt(m['q2'], m['kdh']) for m in mids]
    s2s = [_bdot_nt(m['q2'][c:], m['kkah']) for m in mids]
    for m, u, s1, s2 in zip(mids, raw, s1s, s2s):
        strict_w, incl_w = u[7], u[8]
        m['lad'] = jnp.concatenate([jnp.where(strict_w > 0, s1[:c], 0.0),
                                    jnp.where(incl_w > 0, s1[c:], 0.0)], axis=0).astype(BF16)
        m['a_a'] = jnp.where(incl_w > 0, s2, 0.0).astype(BF16)
    return mids


def _wkv_prep3(mids):
    for m, lav in zip(mids, [_bdot(m['lad'], m['vbd']) for m in mids]):
        m['lav'] = lav
    return mids


def _wkv_adv1(states, preps):
    return [_bdot_nt(p['q2'], s) for p, s in zip(preps, states)]


def _wkv_adv2(p0s, preps, t_ws, head_rows):
    c = RW_CHUNK
    return [_bdot(t_w, _tile_rows(p0[:c] + p['lav'][:c], head_rows))
            for p0, p, t_w in zip(p0s, preps, t_ws)]


def _wkv_adv3(states, p0s, us, preps, head_rows, bd):
    c = RW_CHUNK
    upds = [_bdot(jnp.concatenate([p['v'].astype(F32), u], axis=0).T, p['kw']) for p, u in zip(preps, us)]
    aus = [_bdot(p['a_a'], _tile_rows(u, head_rows)) for p, u in zip(preps, us)]
    new_states = [s * p['decay'] + jnp.where(bd > 0, upd, 0.0) for s, p, upd in zip(states, preps, upds)]
    ys = [p0[c:] + p['lav'][c:] - au for p0, p, au in zip(p0s, preps, aus)]
    return new_states, ys


RKV_TN = RW_LANES


def _rwkv_rkv_kernel(xr_ref, xk_ref, xv_ref, wr_ref, wk_ref, wv_ref, a_ref, kkw_ref, kaw_ref, rk_ref, bd_ref,
                     r_ref, v_ref, kk_ref, kd_ref, b_ref):
    r = jnp.dot(xr_ref[...], wr_ref[...], preferred_element_type=F32)
    k = jnp.dot(xk_ref[...], wk_ref[...], preferred_element_type=F32)
    v = jnp.dot(xv_ref[...], wv_ref[...], preferred_element_type=F32)
    r_ref[...] = r
    v_ref[...] = v.astype(v_ref.dtype)
    bd = bd_ref[...]
    kk = k * kkw_ref[...]
    mix = None
    for d in range(2):
        kd = k * (1.0 + (a_ref[d] - 1.0) * kaw_ref[...])
        kd_ref[d] = kd
        term = kd * rk_ref[d]
        mix = term if mix is None else mix + term
    rm = r * mix
    for h in range(RKV_TN // RW_LANES):
        cs = slice(h * RW_LANES, (h + 1) * RW_LANES)
        kkh = kk[:, cs]
        kk_ref[:, cs] = kkh * lax.rsqrt(_split_dot_r(kkh * kkh, bd, 2) + 1e-12)
        b_ref[:, cs] = _split_dot_r(rm[:, cs], bd, 2) * v[:, cs]


def rwkv_rkv(lay, xr, xk, xv, wr, wk, wv, a, kkw, kaw, rk):
    n, kdim = xr.shape
    dm = wr.shape[1]
    tm, tn = lay.tile(1024), RKV_TN
    bd_b = _wkv_constants()['bd_b']
    lhs = pl.BlockSpec((tm, kdim), lambda i, j: (i, 0))
    rhs = pl.BlockSpec((kdim, tn), lambda i, j: (0, j))
    tok = pl.BlockSpec((tm, tn), lambda i, j: (i, j))
    two = pl.BlockSpec((2, tm, tn), lambda i, j: (0, i, j))
    par = pl.BlockSpec((1, tn), lambda i, j: (0, j))
    return pl.pallas_call(
        _rwkv_rkv_kernel,
        grid=(n // tm, dm // tn),
        in_specs=[lhs, lhs, lhs, rhs, rhs, rhs, two, par, par,
                  pl.BlockSpec((2, 1, tn), lambda i, j: (0, 0, j)),
                  pl.BlockSpec(bd_b.shape, lambda i, j: (0, 0))],
        out_specs=[tok, tok, tok, two, tok],
        out_shape=[jax.ShapeDtypeStruct((n, dm), F32), jax.ShapeDtypeStruct((n, dm), BF16),
                   jax.ShapeDtypeStruct((n, dm), F32), jax.ShapeDtypeStruct((2, n, dm), F32),
                   jax.ShapeDtypeStruct((n, dm), F32)],
        compiler_params=_cparams(2), name="rwkv_rkv",
    )(xr, xk, xv, wr, wk, wv, a, kkw, kaw, rk, bd_b)


def _tinv_begin(items, head_rows, blk_rows, eye_w):
    cums = [_split_dot(it[3], it[2], 3) for it in items]
    ops = [(kk * jnp.exp(cum - lw), _tile_rows(kk * a * jnp.exp(-cum), head_rows))
           for (kk, a, lw, _, _), cum in zip(items, cums)]
    n_ws = [-jnp.where(it[4] > 0, _bdot_nt(q, w), 0.0) for it, (q, w) in zip(items, ops)]
    n_pows = [_bdot(n_w, _tile_rows(n_w, blk_rows)) for n_w in n_ws]
    return [eye_w + n_w for n_w in n_ws], n_pows


def _tinv_level(t_ws, n_pows, last, blk_rows):
    c = RW_CHUNK
    ws = [_tile_rows(n_pow, blk_rows) for n_pow in n_pows]
    if last:
        return [t_w + _bdot(t_w, w) for t_w, w in zip(t_ws, ws)], None
    boths = [_bdot(jnp.concatenate([t_w, n_pow], axis=0), w) for t_w, n_pow, w in zip(t_ws, n_pows, ws)]
    return [t_w + both[:c] for t_w, both in zip(t_ws, boths)], [both[c:] for both in boths]


RW_TINV_LEVELS = int(math.log2(RW_CHUNK))


def _wkv_fused_kernel(*refs, lay):
    (rf, vf, kkf, kdf, af, lwf, s0f, kkfn, afn, lwfn,
     rb, vb, kkb, kdb, ab, lwb, s0b, kkbn, abn, lwbn,
     cum_ref, strict_ref, incl_ref, hr_ref, br_ref, eye_ref, bd_ref,
     yf_ref, yb_ref, sff_ref, sfb_ref, sf_scr, sb_scr, t_scr) = refs
    c = RW_CHUNK
    n_chunks = lay.tb // c
    j = pl.program_id(1)
    cur = lax.rem(j, 2)
    nxt = 1 - cur
    _, pos_f, cnt_f = lay.seq_info(j)
    _, pos_b, cnt_b = lay.seq_info(lay.nb - 1 - j)

    heads = [(gi, h, slice(h * RW_HEAD, (h + 1) * RW_HEAD)) for gi in range(RW_GSTEP) for h in range(RW_GROUP)]

    def load_states(scr, s0):
        scr[...] = jnp.zeros(scr.shape, scr.dtype)
        for gi, h, hs in heads:
            scr[gi, hs, hs] = s0[gi, h]

    def store_states(out, scr):
        for gi, h, hs in heads:
            out[gi, h] = scr[gi, hs, hs]

    @pl.when(pos_f == 0)
    def _():
        load_states(sf_scr, s0f)

    @pl.when(pos_b == cnt_b - 1)
    def _():
        load_states(sb_scr, s0b)

    hr, br, eye_w, bd = hr_ref[...], br_ref[...], eye_ref[...], bd_ref[...]
    sl_f = [slice(ci * c, (ci + 1) * c) for ci in range(n_chunks)]
    sl_b = sl_f[::-1]
    cols = [slice(gi * RW_LANES, (gi + 1) * RW_LANES) for gi in range(RW_GSTEP)]

    def inv_items(kk_f, a_f, lw_f, kk_b, a_b, lw_b, sl):
        out = []
        for cs in cols:
            out.append((kk_f[sl, cs], a_f[sl, cs], lw_f[sl, cs], cum_ref[0], strict_ref[0]))
            out.append((kk_b[sl, cs], a_b[sl, cs], lw_b[sl, cs], cum_ref[1], strict_ref[1]))
        return out

    def inv_store(slot, sl, t_ws):
        for gi, cs in enumerate(cols):
            t_scr[slot, 0, sl, cs] = t_ws[2 * gi].astype(t_scr.dtype)
            t_scr[slot, 1, sl, cs] = t_ws[2 * gi + 1].astype(t_scr.dtype)

    @pl.when(j == 0)
    def _():
        def body(ci, carry):
            sl = pl.ds(pl.multiple_of(ci * c, c), c)
            t_ws, n_pows = _tinv_begin(inv_items(kkf, af, lwf, kkb, ab, lwb, sl), hr, br, eye_w)
            for lv in range(1, RW_TINV_LEVELS):
                t_ws, n_pows = _tinv_level(t_ws, n_pows, lv == RW_TINV_LEVELS - 1, br)
            inv_store(cur, sl, t_ws)
            return carry
        lax.fori_loop(0, n_chunks, body, 0)

    def raw(ci):
        out = []
        for cs in cols:
            sf, sb = sl_f[ci], sl_b[ci]
            out.append((rf[sf, cs], vf[sf, cs], kkf[sf, cs], kdf[sf, cs], af[sf, cs], lwf[sf, cs],
                        cum_ref[0], strict_ref[0], incl_ref[0]))
            out.append((rb[sb, cs], vb[sb, cs], kkb[sb, cs], kdb[sb, cs], ab[sb, cs], lwb[sb, cs],
                        cum_ref[1], strict_ref[1], incl_ref[1]))
        return out

    states = []
    for gi in range(RW_GSTEP):
        states += [sf_scr[gi], sb_scr[gi]]
    raw_n = raw(0)
    preps = _wkv_prep3(_wkv_prep2(raw_n, _wkv_prep1(raw_n), hr))
    inv = {'lv': 0, 't': None, 'n': None}

    def inv_step():
        lv = inv['lv']
        if lv == 0:
            items = []
            for sl in sl_f:
                items += inv_items(kkfn, afn, lwfn, kkbn, abn, lwbn, sl)
            inv['t'], inv['n'] = _tinv_begin(items, hr, br, eye_w)
        elif lv < RW_TINV_LEVELS:
            inv['t'], inv['n'] = _tinv_level(inv['t'], inv['n'], lv == RW_TINV_LEVELS - 1, br)
            if lv == RW_TINV_LEVELS - 1:
                per = 2 * RW_GSTEP
                for ci2, sl in enumerate(sl_f):
                    inv_store(nxt, sl, inv['t'][ci2 * per:(ci2 + 1) * per])
        inv['lv'] = lv + 1

    for ci in range(n_chunks):
        more = ci + 1 < n_chunks
        t_ws = []
        for cs in cols:
            t_ws += [t_scr[cur, 0, sl_f[ci], cs], t_scr[cur, 1, sl_b[ci], cs]]
        p0s = _wkv_adv1(states, preps)
        if more:
            raw_n = raw(ci + 1)
            cums_n = _wkv_prep1(raw_n)
        inv_step()
        us = _wkv_adv2(p0s, preps, t_ws, hr)
        if more:
            mids_n = _wkv_prep2(raw_n, cums_n, hr)
        inv_step()
        states, ys = _wkv_adv3(states, p0s, us, preps, hr, bd)
        if more:
            preps = _wkv_prep3(mids_n)
        for gi, cs in enumerate(cols):
            yf_ref[sl_f[ci], cs] = ys[2 * gi]
            yb_ref[sl_b[ci], cs] = ys[2 * gi + 1]
    while inv['lv'] < RW_TINV_LEVELS:
        inv_step()
    for gi in range(RW_GSTEP):
        sf_scr[gi] = states[2 * gi]
        sb_scr[gi] = states[2 * gi + 1]

    @pl.when(pos_f == cnt_f - 1)
    def _():
        store_states(sff_ref, sf_scr)

    @pl.when(pos_b == 0)
    def _():
        store_states(sfb_ref, sb_scr)


def wkv_fused(lay, r, v, kk, kd, a, lw, s0):
    n, dm = r.shape
    tb, lanes = lay.tb, RW_LANES
    ng = dm // lanes
    width = RW_GSTEP * lanes
    k = _wkv_constants()
    full = lambda arr: pl.BlockSpec(arr.shape, lambda g, j: (0,) * arr.ndim)

    def views(d, blk, blk_next):
        tok = pl.BlockSpec((tb, width), lambda g, j: (blk(j), g))
        tok2 = pl.BlockSpec((None, tb, width), lambda g, j: (d, blk(j), g))
        tok_n = pl.BlockSpec((tb, width), lambda g, j: (blk_next(j), g))
        tok2_n = pl.BlockSpec((None, tb, width), lambda g, j: (d, blk_next(j), g))

        def s0_map(g, j):
            seq, _, _ = lay.seq_info(blk(j))
            return (jnp.maximum(seq - lay.n_ctx + 1, 0), d, g, 0, 0, 0)

        def sfin_map(g, j):
            seq, _, _ = lay.seq_info(blk(j))
            return (seq, g, 0, 0, 0)

        per_head = (RW_GSTEP, RW_GROUP, RW_HEAD, RW_HEAD)
        ins = [tok, tok, tok, tok2, tok2, tok2, pl.BlockSpec((None, None) + per_head, s0_map),
               tok_n, tok2_n, tok2_n]
        return ins, tok, pl.BlockSpec((None,) + per_head, sfin_map)

    in_f, y_f, sf_f = views(0, lambda j: j, lambda j: jnp.minimum(j + 1, lay.nb - 1))
    in_b, y_b, sf_b = views(1, lambda j: lay.nb - 1 - j, lambda j: jnp.maximum(lay.nb - 2 - j, 0))
    consts = [k['cum'], k['strict'], k['incl'], k['head_rows'], k['blk_rows'], k['eye_w'], k['bd']]
    args = [r, v, kk, kd, a, lw, s0, kk, a, lw]
    return pl.pallas_call(
        functools.partial(_wkv_fused_kernel, lay=lay),
        grid=(ng // RW_GSTEP, lay.nb),
        in_specs=in_f + in_b + [full(x) for x in consts],
        out_specs=[y_f, y_b, sf_f, sf_b],
        out_shape=[jax.ShapeDtypeStruct((n, dm), F32), jax.ShapeDtypeStruct((n, dm), F32),
                   jax.ShapeDtypeStruct((lay.n_seq, ng, RW_GROUP, RW_HEAD, RW_HEAD), F32),
                   jax.ShapeDtypeStruct((lay.n_seq, ng, RW_GROUP, RW_HEAD, RW_HEAD), F32)],
        scratch_shapes=[pltpu.VMEM((RW_GSTEP, lanes, lanes), F32), pltpu.VMEM((RW_GSTEP, lanes, lanes), F32),
                        pltpu.VMEM((2, 2, tb, width), BF16)],
        compiler_params=_cparams(2), name="wkv_fused",
    )(*args, *args, *consts)


def _rwkv_post_kernel(yf_ref, yb_ref, b_ref, g_ref, lnw_ref, lnb_ref, bd_ref, o_ref):
    y = yf_ref[...] + yb_ref[...]
    bd = bd_ref[...]
    inv = 1.0 / RW_HEAD
    mu = _split_dot_r(y, bd, 2) * inv
    yc = y - mu
    var = _split_dot_r(yc * yc, bd, 2) * inv
    yn = yc * lax.rsqrt(var + RW_LN_EPS)
    out = yn * lnw_ref[...] + lnb_ref[...] + b_ref[...]
    o_ref[...] = (out * g_ref[...]).astype(o_ref.dtype)


def rwkv_post(lay, y_f, y_b, bonus, g, lnw, lnb):
    n, dm = y_f.shape
    lanes = RW_LANES
    tm = lay.tile(1024)
    bd_b = _wkv_constants()['bd_b']
    tok = pl.BlockSpec((tm, lanes), lambda i, c: (i, c))
    par = pl.BlockSpec((1, lanes), lambda i, c: (0, c))
    return pl.pallas_call(
        _rwkv_post_kernel,
        grid=(n // tm, dm // lanes),
        in_specs=[tok, tok, tok, tok, par, par, pl.BlockSpec(bd_b.shape, lambda i, c: (0, 0))],
        out_specs=tok,
        out_shape=jax.ShapeDtypeStruct((n, dm), BF16),
        compiler_params=_cparams(2), name="rwkv_post",
    )(y_f, y_b, bonus, g, lnw, lnb, bd_b)


def rwkv_layer(lay, x, mods, nw, p, s0):
    n, d = x.shape
    sh1, sc1, g1 = mods[0], mods[1], mods[2]
    xr, xw, xk, xv, xa, xg = rwkv_prep(lay, x, nw, sc1, sh1, p['mu'])
    tm = lay.tile(1024)
    lw, a, g = rwkv_lora(lay, xw, xa, xg, p['w1'], p['w2'], p['w0'], p['a1'], p['a2'], p['a0'],
                         p['g1'], p['g2'])
    r, v, kk, kd, bonus = rwkv_rkv(lay, xr, xk, xv, p['wr'], p['wk'], p['wv'], a, p['kk'], p['ka'], p['rk'])
    y_f, y_b, sfin_f, sfin_b = wkv_fused(lay, r, v, kk, kd, a, lw, s0)
    z = rwkv_post(lay, y_f, y_b, bonus, g, p['lnx_w'], p['lnx_b'])
    x = matmul_gated_residual(lay, z, p['wo'], x, g1, tm=tm, tn=512, name="rwkv_o")
    return x, jnp.stack([sfin_f, sfin_b], axis=1)


def _hgrn_constants():
    c = HG_CHUNK_TOKENS
    t = np.arange(c)[:, None]
    j = np.arange(c)[None, :]
    cums, masks_all = [], []
    for rev in (False, True):
        masks = []
        h = 1
        while h < c:
            upper = (t % (2 * h)) >= h
            same = (t // (2 * h)) == (j // (2 * h))
            if not rev:
                mask = same & upper & ((j % (2 * h)) < h)
            else:
                mask = same & (~upper) & ((j % (2 * h)) >= h)
            masks.append(mask)
            h *= 2
        masks.append(t == j)
        cums.append((j >= t) if rev else (j <= t))
        masks_all.append(np.stack(masks, 0))
    return jnp.asarray(np.stack(cums), BF16), jnp.asarray(np.stack(masks_all).astype(np.float32))


def _hgrn_level_exponents(g, gcum, rev):
    c, kdim = g.shape
    row = lax.broadcasted_iota(jnp.int32, g.shape, 0)
    nxt = pltpu.roll(g, c - 1, axis=0)
    prv = pltpu.roll(g, 1, axis=0)
    r2, r4 = row & 1, row & 3
    if not rev:
        x1 = jnp.where(r2 == 1, g, 0.0)
        x2 = jnp.where(r4 == 0, nxt, jnp.where(r4 == 2, g, jnp.where(r4 == 3, prv + g, 0.0)))
    else:
        x1 = jnp.where(r2 == 0, g, 0.0)
        x2 = jnp.where(r4 == 0, g + nxt, jnp.where(r4 == 1, g, jnp.where(r4 == 3, prv, 0.0)))
    xs = [x1, x2]
    h = 4
    while h < c:
        gr = gcum.reshape(c // (2 * h), 2 * h, kdim)
        ref = gr[:, h:h + 1, :] if rev else gr[:, h - 1:h, :]
        upper = lax.broadcasted_iota(jnp.int32, gr.shape, 1) >= h
        diff = gr - ref
        x = jnp.where(upper, -diff, diff) if rev else jnp.where(upper, diff, -diff)
        xs.append(x.reshape(c, kdim))
        h *= 2
    return xs


def _hgrn_units(units, masks_by_dir, cum_by_dir):
    c = HG_CHUNK_TOKENS
    gs = [jnp.log(f) for _, f, _, _ in units]
    gcums = [_split_dot(cum_by_dir[rev], g, 3) for g, (_, _, _, rev) in zip(gs, units)]
    outs = []
    pend = []
    for (q, f, iv, rev), g, gcum in zip(units, gs, gcums):
        k = 1.0 - f
        tot = gcum[0:1] if rev else gcum[c - 1:c]
        es = [jnp.exp(x) for x in _hgrn_level_exponents(g, gcum, rev)]
        pend.append((q, k, iv, rev, es, jnp.exp(gcum), jnp.exp(tot - gcum), jnp.exp(tot)))
    for q, k, iv, rev, es, eg, erest, etot in pend:
        masks = masks_by_dir[rev]
        a = masks[len(es)] * _bdot_nt(q, k)
        for lv, el in enumerate(es):
            a = a + masks[lv] * _bdot_nt(q * el, k * el)
        outs.append(dict(a=a, iv=iv, qe=(q * eg).astype(BF16), kdec=(k * erest).astype(BF16), decay=etot))
    for u in outs:
        u['av'] = _bdot(u['a'], u['iv'])
    for u in outs:
        u['upd'] = _bdot(u['iv'].astype(F32).T, u['kdec'])
    return outs


def _hgrn_kernel(*refs, lay):
    (qf, ff, if_, s0f, qb, fb, ib, s0b, cum_ref, mask_ref, of_ref, ob_ref, sff_ref, sfb_ref,
     sf_scr, sb_scr) = refs
    c = HG_CHUNK_TOKENS
    n_chunks = lay.tb // c
    j = pl.program_id(1)
    _, pos_f, cnt_f = lay.seq_info(j)
    _, pos_b, cnt_b = lay.seq_info(lay.nb - 1 - j)

    @pl.when(pos_f == 0)
    def _():
        sf_scr[...] = s0f[...]

    @pl.when(pos_b == cnt_b - 1)
    def _():
        sb_scr[...] = s0b[...]

    masks_by_dir = [mask_ref[0], mask_ref[1]]
    cum_by_dir = [cum_ref[0], cum_ref[1]]
    sl_f = [slice(ci * c, (ci + 1) * c) for ci in range(n_chunks)]
    sl_b = sl_f[::-1]
    cols = [slice(hi * HG_K, (hi + 1) * HG_K) for hi in range(HG_HSTEP)]
    chains = []
    for hi, cs in enumerate(cols):
        chains.append((sf_scr, hi, of_ref, [(sl, cs) for sl in sl_f],
                       [(qf[sl, cs], ff[sl, cs], if_[sl, cs], False) for sl in sl_f]))
        chains.append((sb_scr, hi, ob_ref, [(sl, cs) for sl in sl_b],
                       [(qb[sl, cs], fb[sl, cs], ib[sl, cs], True) for sl in sl_b]))
    done = _hgrn_units([u for ch in chains for u in ch[4]], masks_by_dir, cum_by_dir)
    pend = []
    for n_ch, (scr, hi, o_ref, where, _) in enumerate(chains):
        s = scr[hi]
        for ci in range(n_chunks):
            u = done[n_ch * n_chunks + ci]
            pend.append((o_ref, where[ci], u, s))
            s = s * u['decay'] + u['upd']
        scr[hi] = s
    for o_ref, (sl, cs), u, s_prev in pend:
        o_ref[sl, cs] = u['av'] + _bdot_nt(u['qe'], s_prev)

    @pl.when(pos_f == cnt_f - 1)
    def _():
        sff_ref[...] = sf_scr[...]

    @pl.when(pos_b == 0)
    def _():
        sfb_ref[...] = sb_scr[...]


def hgrn_scan(lay, q, f_fwd, f_bwd, iv, s0t):
    n, dm = q.shape
    tb = lay.tb
    nh = dm // HG_K
    width = HG_HSTEP * HG_K
    cum_m, masks = _hgrn_constants()
    full = lambda arr: pl.BlockSpec(arr.shape, lambda h, j: (0,) * arr.ndim)

    def views(d, blk):
        tok = pl.BlockSpec((tb, width), lambda h, j: (blk(j), h))

        def s0_map(h, j):
            seq, _, _ = lay.seq_info(blk(j))
            return (jnp.maximum(seq - lay.n_ctx + 1, 0), d, h, 0, 0)

        def sfin_map(h, j):
            seq, _, _ = lay.seq_info(blk(j))
            return (seq, h, 0, 0)

        ins = [tok, tok, tok, pl.BlockSpec((None, None, HG_HSTEP, HG_K, HG_K), s0_map)]
        return ins, tok, pl.BlockSpec((None, HG_HSTEP, HG_K, HG_K), sfin_map)

    in_f, o_f, sf_f = views(0, lambda j: j)
    in_b, o_b, sf_b = views(1, lambda j: lay.nb - 1 - j)
    return pl.pallas_call(
        functools.partial(_hgrn_kernel, lay=lay),
        grid=(nh // HG_HSTEP, lay.nb),
        in_specs=in_f + in_b + [full(cum_m), full(masks)],
        out_specs=[o_f, o_b, sf_f, sf_b],
        out_shape=[jax.ShapeDtypeStruct((n, dm), F32), jax.ShapeDtypeStruct((n, dm), F32),
                   jax.ShapeDtypeStruct((lay.n_seq, nh, HG_K, HG_K), F32),
                   jax.ShapeDtypeStruct((lay.n_seq, nh, HG_K, HG_K), F32)],
        scratch_shapes=[pltpu.VMEM((HG_HSTEP, HG_K, HG_K), F32), pltpu.VMEM((HG_HSTEP, HG_K, HG_K), F32)],
        compiler_params=_cparams(2), name="hgrn_scan",
    )(q, f_fwd, iv, s0t, q, f_bwd, iv, s0t, cum_m, masks)


HG_POST_HEADS = 4


def _hgrn_post_kernel(of_ref, ob_ref, g_ref, nw_ref, z_ref):
    for h in range(HG_POST_HEADS):
        cs = slice(h * HG_K, (h + 1) * HG_K)
        o = of_ref[:, cs] + ob_ref[:, cs]
        o = o * lax.rsqrt(jnp.mean(o * o, axis=-1, keepdims=True) + NORM_EPS) * nw_ref[...] * g_ref[:, cs]
        z_ref[:, cs] = o.astype(z_ref.dtype)


def hgrn_post(lay, o_f, o_b, gs, nw):
    n, dm = o_f.shape
    tm = lay.tile(1024)
    width = HG_POST_HEADS * HG_K
    tok = pl.BlockSpec((tm, width), lambda i, h: (i, h))
    return pl.pallas_call(
        _hgrn_post_kernel,
        grid=(n // tm, dm // width),
        in_specs=[tok, tok, tok, pl.BlockSpec((1, HG_K), lambda i, h: (0, 0))],
        out_specs=tok,
        out_shape=jax.ShapeDtypeStruct((n, dm), BF16),
        compiler_params=_cparams(2), name="hgrn_post",
    )(o_f, o_b, gs, nw)


def _epi_hgrn_in(accs, extras):
    lb = extras[0]
    q = _silu(accs[0])
    f0 = lb + (1.0 - lb) * jax.nn.sigmoid(accs[1])
    f1 = lb + (1.0 - lb) * jax.nn.sigmoid(accs[2])
    return [q, f0, f1, accs[3], _silu(accs[4])]


def hgrn_layer(lay, x, mods, nw, p, lb, s0t):
    n, d = x.shape
    sh1, sc1, g1 = mods[0], mods[1], mods[2]
    h = norm_mod(lay, x, nw, sc1, sh1, BF16)
    tm, tn = lay.tile(1024), 256
    extras = ((lb, pl.BlockSpec((1, tn), lambda i, j: (0, j))),)
    q, f0, f1, iv, gs = matmul(h, p['w_in'], _epi_hgrn_in, [F32, F32, F32, BF16, F32], tm=tm, tn=tn, extras=extras,
                               name="hgrn_in")
    o_f, o_b, sfin_f, sfin_b = hgrn_scan(lay, q, f0, f1, iv, s0t)
    z = hgrn_post(lay, o_f, o_b, gs, p['norm_w'])
    x = matmul_gated_residual(lay, z, p['wo'], x, g1, tm=lay.tile(1024), tn=512, name="hgrn_o")
    return x, jnp.stack([sfin_f, sfin_b], axis=1)


ML_DOWN_COLS = 1280
ML_KR_OFF = ML_Q_LORA + ML_KV_LORA
ML_KRS_OFF = ML_KR_OFF + LANES_V7X


def _rms(x, w):
    return x * lax.rsqrt(jnp.mean(x * x, axis=-1, keepdims=True) + NORM_EPS) * w


def _mla_mid_kernel(dn_ref, qw_ref, kvw_ref, cos_ref, sin_ref, qn_ref, ckv_ref, kr_ref):
    dn = dn_ref[...]
    qn_ref[...] = _rms(dn[:, :ML_Q_LORA], qw_ref[...]).astype(qn_ref.dtype)
    ckv_ref[...] = _rms(dn[:, ML_Q_LORA:ML_KR_OFF], kvw_ref[...])
    kr = dn[:, ML_KR_OFF:ML_KR_OFF + ML_ROPE]
    krs = dn[:, ML_KRS_OFF:ML_KRS_OFF + ML_ROPE]
    kr_ref[...] = kr * cos_ref[...] + krs * sin_ref[...]


def mla_mid(lay, dn, qw, kvw, cos, sin):
    n = dn.shape[0]
    tm = lay.tile(512)
    return pl.pallas_call(
        _mla_mid_kernel,
        grid=(n // tm,),
        in_specs=[pl.BlockSpec((tm, ML_DOWN_COLS), lambda i: (i, 0)),
                  pl.BlockSpec((1, ML_Q_LORA), lambda i: (0, 0)),
                  pl.BlockSpec((1, ML_KV_LORA), lambda i: (0, 0)),
                  pl.BlockSpec((tm, ML_ROPE), lambda i: (i, 0)),
                  pl.BlockSpec((tm, ML_ROPE), lambda i: (i, 0))],
        out_specs=[pl.BlockSpec((tm, ML_Q_LORA), lambda i: (i, 0)),
                   pl.BlockSpec((tm, ML_KV_LORA), lambda i: (i, 0)),
                   pl.BlockSpec((tm, ML_ROPE), lambda i: (i, 0))],
        out_shape=[jax.ShapeDtypeStruct((n, ML_Q_LORA), BF16),
                   jax.ShapeDtypeStruct((n, ML_KV_LORA), F32),
                   jax.ShapeDtypeStruct((n, ML_ROPE), F32)],
        compiler_params=_cparams(1), name="mla_mid",
    )(dn, qw, kvw, cos, sin)


ML_QSCALE = math.log2(math.e) / math.sqrt(ML_NOPE + ML_ROPE)


def _epi_qscale(accs, extras):
    return [accs[0] * ML_QSCALE]


def _epi_rope(accs, extras):
    cos, sin = extras
    reps = accs[0].shape[1] // cos.shape[1]
    cos, sin = jnp.concatenate([cos] * reps, axis=1), jnp.concatenate([sin] * reps, axis=1)
    return [(accs[0] * cos + accs[1] * sin) * ML_QSCALE]


ML_KEY_SPLITS = 2


def _attn_kernel(qn_ref, qr_ref, kn_ref, kr_ref, v_ref, o_ref, kc_scr, vt_scr):
    @pl.when(pl.program_id(2) == 0)
    def _():
        for h in range(2):
            kc_scr[h, :, :ML_NOPE] = kn_ref[:, h * ML_NOPE:(h + 1) * ML_NOPE]
            kc_scr[h, :, ML_NOPE:] = kr_ref[...]
            vt_scr[h] = v_ref[:, h * ML_V:(h + 1) * ML_V].astype(F32).T.astype(vt_scr.dtype)

    k_len = kc_scr.shape[1]
    kh = k_len // ML_KEY_SPLITS
    scores = []
    for h in range(2):
        q = jnp.concatenate([qn_ref[:, h * ML_NOPE:(h + 1) * ML_NOPE],
                             qr_ref[:, h * ML_ROPE:(h + 1) * ML_ROPE]], axis=1)
        scores.append([lax.dot_general(kc_scr[h, kb * kh:(kb + 1) * kh, :], q, (((1,), (1,)), ((), ())),
                                       preferred_element_type=F32) for kb in range(ML_KEY_SPLITS)])
    for h in range(2):
        m_acc = l_acc = o_acc = None
        for kb, st in enumerate(scores[h]):
            m = jnp.max(st, axis=0, keepdims=True)
            p = jnp.exp2(st - m)
            l = jnp.sum(p, axis=0, keepdims=True)
            ot = jnp.dot(vt_scr[h, :, kb * kh:(kb + 1) * kh], p.astype(BF16),
                         preferred_element_type=F32)
            if m_acc is None:
                m_acc, l_acc, o_acc = m, l, ot
            else:
                m_new = jnp.maximum(m_acc, m)
                c_old, c_new = jnp.exp2(m_acc - m_new), jnp.exp2(m - m_new)
                l_acc = l_acc * c_old + l * c_new
                o_acc = o_acc * c_old + ot * c_new
                m_acc = m_new
        o_ref[:, h * ML_V:(h + 1) * ML_V] = (o_acc / l_acc).T.astype(o_ref.dtype)


def attention(qn, qr, kn, kr, v, *, n_seq, q_len, k_len, row0, tq):
    heads2 = qn.shape[1] // (2 * ML_NOPE)
    qb = q_len // tq
    rb0 = row0 // tq
    return pl.pallas_call(
        _attn_kernel,
        grid=(n_seq, heads2, qb),
        in_specs=[pl.BlockSpec((tq, 2 * ML_NOPE), lambda s, h, i: (rb0 + s * qb + i, h)),
                  pl.BlockSpec((tq, 2 * ML_ROPE), lambda s, h, i: (rb0 + s * qb + i, h)),
                  pl.BlockSpec((k_len, 2 * ML_NOPE), lambda s, h, i: (s, h)),
                  pl.BlockSpec((k_len, ML_ROPE), lambda s, h, i: (s, 0)),
                  pl.BlockSpec((k_len, 2 * ML_V), lambda s, h, i: (s, h))],
        out_specs=pl.BlockSpec((tq, 2 * ML_V), lambda s, h, i: (s * qb + i, h)),
        out_shape=jax.ShapeDtypeStruct((n_seq * q_len, heads2 * 2 * ML_V), BF16),
        scratch_shapes=[pltpu.VMEM((2, k_len, ML_NOPE + ML_ROPE), BF16), pltpu.VMEM((2, ML_V, k_len), BF16)],
        compiler_params=_cparams(3), name="mla_attn",
    )(qn, qr, kn, kr, v)


def mla_layer(lay, x, mods, nw, p, cache_ckv, cache_kr, cos, sin, cos2, sin2):
    n, d = x.shape
    sh1, sc1, g1 = mods[0], mods[1], mods[2]
    h = norm_mod(lay, x, nw, sc1, sh1, BF16)
    tm = lay.tile(512)
    dn = matmul(h, [p['w_down']], _epi_plain, [F32], tm=tm, tn=ML_DOWN_COLS, name="mla_down")[0]
    qlat, ckv, kr = mla_mid(lay, dn, p['qnorm_w'], p['kvnorm_w'], cos, sin)
    tm_q = lay.tile(1024)
    qn = matmul(qlat, [p['w_uq_nope']], _epi_qscale, [BF16], tm=tm_q, tn=1024, name="mla_qn")[0]
    tw = cos2.shape[1]
    extras = ((cos2, pl.BlockSpec((tm_q, tw), lambda i, j: (i, 0))),
              (sin2, pl.BlockSpec((tm_q, tw), lambda i, j: (i, 0))))
    qr = matmul(qlat, [p['w_uq_rope'], p['w_uq_rope_sw']], _epi_rope, [BF16], tm=tm_q,
                tn=p['w_uq_rope'].shape[1], extras=extras, name="mla_qr")[0]
    nc, past = lay.nc, cache_ckv.shape[1]
    ckv_b, kr_b = ckv.astype(BF16), kr.astype(BF16)
    kn_c, v_c = matmul(ckv_b[:nc], [p['w_ukn'], p['w_uv']], _epi_plain, [BF16, BF16],
                       tm=lay.tile(512), tn=512, name="mla_kv_ctx")
    o_c = attention(qn, qr, kn_c, kr_b[:nc], v_c, n_seq=lay.n_ctx, q_len=lay.ctx_len,
                    k_len=lay.ctx_len, row0=0, tq=min(256, lay.ctx_len))
    k_len = lay.lat_len + past
    ckv_l = jnp.concatenate([ckv_b[nc:].reshape(lay.n_lat, lay.lat_len, -1), cache_ckv.astype(BF16)],
                            axis=1).reshape(lay.n_lat * k_len, -1)
    kr_l = jnp.concatenate([kr_b[nc:].reshape(lay.n_lat, lay.lat_len, -1), cache_kr.astype(BF16)],
                           axis=1).reshape(lay.n_lat * k_len, -1)
    bf16_rows = 2 * SUBLANES_V7X
    tk = k_len // 2 if (k_len // 2) % bf16_rows == 0 else math.gcd(k_len, 512)
    kn_l, v_l = matmul(ckv_l, [p['w_ukn'], p['w_uv']], _epi_plain, [BF16, BF16], tm=tk, tn=1024,
                       name="mla_kv_lat")
    o_l = attention(qn, qr, kn_l, kr_l, v_l, n_seq=lay.n_lat, q_len=lay.lat_len, k_len=k_len,
                    row0=nc, tq=min(256, lay.lat_len))
    o = jnp.concatenate([o_c, o_l], axis=0)
    x = matmul_gated_residual(lay, o, p['wo'], x, g1, tm=lay.tile(1024), tn=512, name="mla_o")
    return x, ckv[:nc], kr[:nc]


def _rope_tables(lay):
    t = lay.lat_len
    rows = t // GRID_W
    rr = jnp.broadcast_to(jnp.arange(rows, dtype=F32)[:, None], (rows, GRID_W)).reshape(-1)
    cc = jnp.broadcast_to(jnp.arange(GRID_W, dtype=F32)[None, :], (rows, GRID_W)).reshape(-1)
    nf = ML_ROPE // 4
    inv = ROPE_BASE ** (-jnp.arange(nf, dtype=F32) / nf)
    ar, ac = rr[:, None] * inv, cc[:, None] * inv
    cos = jnp.concatenate([jnp.cos(ar), jnp.cos(ar), jnp.cos(ac), jnp.cos(ac)], axis=-1)
    sin = jnp.concatenate([-jnp.sin(ar), jnp.sin(ar), -jnp.sin(ac), jnp.sin(ac)], axis=-1)
    cos = jnp.concatenate([jnp.ones((lay.nc, ML_ROPE), F32), jnp.tile(cos, (lay.n_lat, 1))], axis=0)
    sin = jnp.concatenate([jnp.zeros((lay.nc, ML_ROPE), F32), jnp.tile(sin, (lay.n_lat, 1))], axis=0)
    return cos, sin


def _swap_cols(w):
    k, c = w.shape
    w4 = w.reshape(k, c // 32, 2, 16)
    return w4[:, :, ::-1, :].reshape(k, c)


def ffn(lay, x, mods, nw, w_a, w_b, w_out):
    sh2, sc2, g2 = mods[3], mods[4], mods[5]
    h = norm_mod(lay, x, nw, sc2, sh2, BF16)
    act = matmul(h, [w_a, w_b], _epi_swiglu, [BF16], tm=lay.tile(1024), tn=512, name="ffn_in")[0]
    return matmul_gated_residual(lay, act, w_out, x, g2, tm=lay.tile(1024), tn=512, name="ffn_out")


def kernel(x_prompt, x_sample, state_rwkv, state_hgrn, cache_ckv, cache_krope, c, c_ctx, ada_w, ada_b, norm1_w, norm2_w, ffn_w_in, ffn_w_out, final_norm_w, rw_mu, rw_wr, rw_wk, rw_wv, rw_wo, rw_w0, rw_w1, rw_w2, rw_a0, rw_a1, rw_a2, rw_g1, rw_g2, rw_kk, rw_ka, rw_rk, rw_lnx_w, rw_lnx_b, hg_w_in, hg_lb, hg_norm_w, hg_wo, ml_w_down, ml_qnorm_w, ml_kvnorm_w, ml_w_uq, ml_w_ukv, ml_wo):
    n_ctx, ctx_len, d = x_prompt.shape
    n_lat, lat_len, _ = x_sample.shape
    depth = ada_w.shape[0]
    lay = Layout(n_ctx, ctx_len, n_lat, lat_len)
    d_ff = ffn_w_out.shape[1]
    x = jnp.concatenate([x_prompt.reshape(lay.nc, d), x_sample.reshape(n_lat * lat_len, d)], axis=0)

    n_cond = -(-(1 + n_lat) // SUBLANES_V7X) * SUBLANES_V7X
    cond = jnp.zeros((n_cond, d), F32).at[0].set(c_ctx).at[1:1 + n_lat].set(c)
    mod_all = adaln(cond, ada_w, ada_b)
    mod_all = mod_all.reshape(depth, n_cond, 6, 1, d).transpose(0, 2, 1, 3, 4)

    lb_table = jnp.cumsum(jax.nn.softmax(hg_lb.astype(F32), axis=0), axis=0)
    lb_table = lb_table - lb_table[0]
    cos, sin = _rope_tables(lay)
    cos2, sin2 = jnp.tile(cos, (1, 2)), jnp.tile(sin, (1, 2))
    bf = lambda t: t.astype(BF16)

    new_rwkv, new_hgrn, new_ckv, new_krope = [], [], [], []
    for l in range(depth):
        kind, j = l % 3, l // 3
        mods = mod_all[l]
        if kind == 0:
            pad1 = lambda w: jnp.pad(w, ((0, 0), (0, 0), (0, LORA_PAD - w.shape[2])))
            pad2 = lambda w: jnp.pad(w, ((0, 0), (0, LORA_PAD - w.shape[1]), (0, 0)))
            p = {'mu': rw_mu[j], 'wr': bf(rw_wr[j]), 'wk': bf(rw_wk[j]), 'wv': bf(rw_wv[j]),
                 'wo': bf(rw_wo[j]),
                 'w0': rw_w0[j].reshape(2, 1, d), 'w1': bf(pad1(rw_w1[j])), 'w2': bf(pad2(rw_w2[j])),
                 'a0': rw_a0[j].reshape(2, 1, d), 'a1': bf(pad1(rw_a1[j])), 'a2': bf(pad2(rw_a2[j])),
                 'g1': bf(rw_g1[j]), 'g2': bf(rw_g2[j]),
                 'kk': rw_kk[j].reshape(1, d), 'ka': rw_ka[j].reshape(1, d),
                 'rk': rw_rk[j].reshape(2, 1, d),
                 'lnx_w': rw_lnx_w[j].reshape(1, d), 'lnx_b': rw_lnx_b[j].reshape(1, d)}
            rw_h = state_rwkv.shape[3]
            s_lat = state_rwkv[:, j].astype(F32).reshape(n_lat, 2, rw_h // RW_GROUP, RW_GROUP, RW_HEAD, RW_HEAD)
            s0 = jnp.concatenate([jnp.zeros((1,) + s_lat.shape[1:], F32), s_lat], axis=0)
            x, sfin = rwkv_layer(lay, x, mods, norm1_w[l], p, s0)
            new_rwkv.append(sfin[:n_ctx].reshape(n_ctx, 2, rw_h, RW_HEAD, RW_HEAD))
        elif kind == 1:
            hk = d
            w_in = hg_w_in[j]
            p = {'w_in': [bf(w_in[:, i * hk:(i + 1) * hk]) for i in range(5)],
                 'norm_w': hg_norm_w[j].reshape(1, HG_K), 'wo': bf(hg_wo[j])}
            s_lat = jnp.swapaxes(state_hgrn[:, j].astype(F32), -1, -2)
            s0t = jnp.concatenate([jnp.zeros((1,) + s_lat.shape[1:], F32), s_lat], axis=0)
            x, sfin = hgrn_layer(lay, x, mods, norm1_w[l], p, lb_table[l].reshape(1, d), s0t)
            new_hgrn.append(jnp.swapaxes(sfin[:n_ctx], -1, -2))
        else:
            wd = ml_w_down[j]
            kr_w = wd[:, ML_KR_OFF:]
            zpad = jnp.zeros((d, LANES_V7X - ML_ROPE), wd.dtype)
            w_down = jnp.concatenate([wd, zpad, _swap_cols(kr_w), zpad], axis=1)
            wq = ml_w_uq[j].reshape(ML_Q_LORA, ML_H, ML_NOPE + ML_ROPE)
            wq_n = wq[:, :, :ML_NOPE].reshape(ML_Q_LORA, ML_H * ML_NOPE)
            wq_r = wq[:, :, ML_NOPE:].reshape(ML_Q_LORA, ML_H * ML_ROPE)
            wkv = ml_w_ukv[j].reshape(ML_KV_LORA, ML_H, ML_NOPE + ML_V)
            p = {'w_down': bf(w_down), 'qnorm_w': ml_qnorm_w[j].reshape(1, -1),
                 'kvnorm_w': ml_kvnorm_w[j].reshape(1, -1),
                 'w_uq_nope': bf(wq_n), 'w_uq_rope': bf(wq_r), 'w_uq_rope_sw': bf(_swap_cols(wq_r)),
                 'w_ukn': bf(wkv[:, :, :ML_NOPE].reshape(ML_KV_LORA, ML_H * ML_NOPE)),
                 'w_uv': bf(wkv[:, :, ML_NOPE:].reshape(ML_KV_LORA, ML_H * ML_V)),
                 'wo': bf(ml_wo[j])}
            x, ckv_c, kr_c = mla_layer(lay, x, mods, norm1_w[l], p, cache_ckv[:, j], cache_krope[:, j],
                                       cos, sin, cos2, sin2)
            new_ckv.append(ckv_c.reshape(n_ctx, ctx_len, ML_KV_LORA))
            new_krope.append(kr_c.reshape(n_ctx, ctx_len, ML_ROPE))
        w_in = ffn_w_in[l]
        x = ffn(lay, x, mods, norm2_w[l], bf(w_in[:, :d_ff]), bf(w_in[:, d_ff:]), bf(ffn_w_out[l]))

    y_prompt = rmsnorm_rows(lay, x, final_norm_w, 0, lay.nc).reshape(n_ctx, ctx_len, d)
    y_sample = rmsnorm_rows(lay, x, final_norm_w, lay.nc, lay.n - lay.nc).reshape(n_lat, lat_len, d)
    return (y_prompt, y_sample, jnp.stack(new_rwkv, axis=1), jnp.stack(new_hgrn, axis=1),
            jnp.stack(new_ckv, axis=1), jnp.stack(new_krope, axis=1))
```

```python
import functools
import math

import numpy as np
import jax
import jax.numpy as jnp
from jax import lax
from jax.experimental import pallas as pl
from jax.experimental.pallas import tpu as pltpu

F32 = jnp.float32
BF16 = jnp.bfloat16

LANES_V7X = 128
SUBLANES_V7X = 8
VMEM_BYTES_V7X = 64 * 1024 * 1024
VMEM_LIMIT = VMEM_BYTES_V7X - 8 * 1024 * 1024

NORM_EPS = 1e-6
RW_HEAD = 64
RW_LN_EPS = 64e-5
RW_GROUP = 4
RW_LANES = RW_GROUP * RW_HEAD
RW_CHUNK = 64
RW_GSTEP = 2
HG_K = 128
HG_CHUNK_TOKENS = 64
HG_HSTEP = 4
ML_H = 16
ML_NOPE = 128
ML_ROPE = 64
ML_V = 128
ML_Q_LORA = 512
ML_KV_LORA = 512
GRID_W = 64
ROPE_BASE = 10000.0
LORA_PAD = 128


class Layout:
    def __init__(self, n_ctx, ctx_len, n_lat, lat_len):
        self.n_ctx, self.ctx_len, self.n_lat, self.lat_len = n_ctx, ctx_len, n_lat, lat_len
        self.nc = n_ctx * ctx_len
        self.n = self.nc + n_lat * lat_len
        self.tb = min(256, ctx_len)
        assert ctx_len % self.tb == 0 and lat_len % self.tb == 0 and self.tb % RW_CHUNK == 0
        self.nb = self.n // self.tb
        self.nb_ctx = self.nc // self.tb
        self.bps_ctx = ctx_len // self.tb
        self.bps_lat = lat_len // self.tb
        self.n_seq = n_ctx + n_lat

    def tile(self, want):
        t = want
        while self.nc % t or self.lat_len % t:
            t //= 2
        return t

    def cond_of_tile(self, i, tm):
        row = i * tm
        return jnp.where(row < self.nc, 0, 1 + (row - self.nc) // self.lat_len)

    def seq_info(self, blk):
        is_ctx = blk < self.nb_ctx
        lat = blk - self.nb_ctx
        seq = jnp.where(is_ctx, blk // self.bps_ctx, self.n_ctx + lat // self.bps_lat)
        pos = jnp.where(is_ctx, blk % self.bps_ctx, lat % self.bps_lat)
        cnt = jnp.where(is_ctx, self.bps_ctx, self.bps_lat)
        return seq, pos, cnt


def _cparams(n_axes):
    return pltpu.CompilerParams(dimension_semantics=("arbitrary",) * n_axes, vmem_limit_bytes=VMEM_LIMIT)


def _bdot(a, b):
    return jnp.dot(a.astype(BF16), b.astype(BF16), preferred_element_type=F32)


def _bdot_nt(a, b):
    return lax.dot_general(a.astype(BF16), b.astype(BF16), (((1,), (1,)), ((), ())),
                           preferred_element_type=F32)


def _silu(x):
    return x * jax.nn.sigmoid(x)


def _adaln_kernel(c_ref, w_ref, b_ref, o_ref):
    a = _silu(c_ref[...]).astype(BF16)
    o_ref[...] = jnp.dot(a, w_ref[...].astype(BF16), preferred_element_type=F32) + b_ref[...]


def adaln(cond, ada_w, ada_b):
    depth, d, d6 = ada_w.shape
    r = cond.shape[0]
    tn = 1024
    return pl.pallas_call(
        _adaln_kernel,
        grid=(depth, d6 // tn),
        in_specs=[pl.BlockSpec((r, d), lambda l, j: (0, 0)),
                  pl.BlockSpec((None, d, tn), lambda l, j: (l, 0, j)),
                  pl.BlockSpec((None, 1, tn), lambda l, j: (l, 0, j))],
        out_specs=pl.BlockSpec((None, r, tn), lambda l, j: (l, 0, j)),
        out_shape=jax.ShapeDtypeStruct((depth, r, d6), F32),
        compiler_params=_cparams(2), name="adaln",
    )(cond, ada_w, ada_b.reshape(depth, 1, d6))


def _norm_mod(x, nw, sc, sh):
    y = x * lax.rsqrt(jnp.mean(x * x, axis=-1, keepdims=True) + NORM_EPS)
    return (y * nw) * (1.0 + sc) + sh


def _norm_mod_kernel(x_ref, nw_ref, sc_ref, sh_ref, o_ref):
    o_ref[...] = _norm_mod(x_ref[...], nw_ref[...], sc_ref[...], sh_ref[...]).astype(o_ref.dtype)


def norm_mod(lay, x, nw, sc, sh, out_dtype):
    n, d = x.shape
    tm = lay.tile(512)
    cmap = lambda i: (lay.cond_of_tile(i, tm), 0, 0)
    return pl.pallas_call(
        _norm_mod_kernel,
        grid=(n // tm,),
        in_specs=[pl.BlockSpec((tm, d), lambda i: (i, 0)),
                  pl.BlockSpec((1, d), lambda i: (0, 0)),
                  pl.BlockSpec((None, 1, d), cmap),
                  pl.BlockSpec((None, 1, d), cmap)],
        out_specs=pl.BlockSpec((tm, d), lambda i: (i, 0)),
        out_shape=jax.ShapeDtypeStruct((n, d), out_dtype),
        compiler_params=_cparams(1), name="norm_mod",
    )(x, nw.reshape(1, d), sc, sh)


def _rmsnorm_kernel(x_ref, w_ref, o_ref):
    x = x_ref[...]
    o_ref[...] = x * lax.rsqrt(jnp.mean(x * x, axis=-1, keepdims=True) + NORM_EPS) * w_ref[...]


def rmsnorm_rows(lay, x, w, row0, rows):
    d = x.shape[1]
    tm = lay.tile(512)
    b0 = row0 // tm
    return pl.pallas_call(
        _rmsnorm_kernel,
        grid=(rows // tm,),
        in_specs=[pl.BlockSpec((tm, d), lambda i: (b0 + i, 0)), pl.BlockSpec((1, d), lambda i: (0, 0))],
        out_specs=pl.BlockSpec((tm, d), lambda i: (i, 0)),
        out_shape=jax.ShapeDtypeStruct((rows, d), x.dtype),
        compiler_params=_cparams(1), name="final_norm",
    )(x, w.reshape(1, d))


def _mm_kernel(*refs, n_w, n_e, epi):
    a = refs[0][...]
    accs = [jnp.dot(a, refs[1 + i][...], preferred_element_type=F32) for i in range(n_w)]
    extras = [refs[1 + n_w + i][...] for i in range(n_e)]
    outs = epi(accs, extras)
    o_refs = refs[1 + n_w + n_e:]
    for o_ref, val in zip(o_refs, outs):
        o_ref[...] = val.astype(o_ref.dtype)


def matmul(a, ws, epi, out_dtypes, *, tm, tn, extras=(), name):
    m, k = a.shape
    nw = ws[0].shape[1]
    assert m % tm == 0 and nw % tn == 0
    in_specs = [pl.BlockSpec((tm, k), lambda i, j: (i, 0))]
    in_specs += [pl.BlockSpec((k, tn), lambda i, j: (0, j)) for _ in ws]
    in_specs += [spec for _, spec in extras]
    outs = pl.pallas_call(
        functools.partial(_mm_kernel, n_w=len(ws), n_e=len(extras), epi=epi),
        grid=(m // tm, nw // tn),
        in_specs=in_specs,
        out_specs=[pl.BlockSpec((tm, tn), lambda i, j: (i, j)) for _ in out_dtypes],
        out_shape=[jax.ShapeDtypeStruct((m, nw), dt) for dt in out_dtypes],
        compiler_params=_cparams(2), name=name,
    )(a, *ws, *[arr for arr, _ in extras])
    return outs


def _epi_plain(accs, extras):
    return accs


def _epi_gated_residual(accs, extras):
    x, g = extras
    return [x + g * accs[0]]


def matmul_gated_residual(lay, a, w, x, gate, *, tm, tn, name):
    extras = ((x, pl.BlockSpec((tm, tn), lambda i, j: (i, j))),
              (gate, pl.BlockSpec((None, 1, tn), lambda i, j: (lay.cond_of_tile(i, tm), 0, j))))
    return matmul(a, [w], _epi_gated_residual, [F32], tm=tm, tn=tn, extras=extras, name=name)[0]


def _epi_swiglu(accs, extras):
    return [_silu(accs[0]) * accs[1]]


RWKV_PREP_SLAB = LANES_V7X


def _rwkv_prep_kernel(x_ref, xp_ref, xn_ref, nw_ref, sc_ref, sh_ref, mu_ref, *o_refs, lay):
    i = pl.program_id(0)
    _, pos, cnt = lay.seq_info(i)
    def inv_rms(x):
        return lax.rsqrt(jnp.mean(x * x, axis=-1, keepdims=True) + NORM_EPS)

    tb, d = x_ref.shape
    inv = inv_rms(x_ref[...])
    inv_p = inv_rms(xp_ref[SUBLANES_V7X - 1:SUBLANES_V7X, :])
    inv_n = inv_rms(xn_ref[0:1, :])
    keep_p = jnp.where(pos == 0, 0.0, 1.0)
    keep_n = jnp.where(pos == cnt - 1, 0.0, 1.0)
    row = lax.broadcasted_iota(jnp.int32, (tb, RWKV_PREP_SLAB), 0)
    for c0 in range(0, d, RWKV_PREP_SLAB):
        cs = slice(c0, c0 + RWKV_PREP_SLAB)
        nw, sc, sh = nw_ref[:, cs], sc_ref[:, cs], sh_ref[:, cs]
        h = ((x_ref[:, cs] * inv) * nw) * (1.0 + sc) + sh
        hp = (((xp_ref[SUBLANES_V7X - 1:SUBLANES_V7X, cs] * inv_p) * nw) * (1.0 + sc) + sh) * keep_p
        hn = (((xn_ref[0:1, cs] * inv_n) * nw) * (1.0 + sc) + sh) * keep_n
        prev = jnp.where(row == 0, hp, pltpu.roll(h, 1, axis=0))
        nxt = jnp.where(row == tb - 1, hn, pltpu.roll(h, tb - 1, axis=0))
        xx = 0.5 * (prev + nxt) - h
        for idx, o_ref in enumerate(o_refs):
            o_ref[:, cs] = (h + xx * mu_ref[idx:idx + 1, cs]).astype(o_ref.dtype)


def rwkv_prep(lay, x, nw, sc, sh, mu):
    n, d = x.shape
    tb = lay.tb
    r8 = tb // SUBLANES_V7X
    last8 = n // SUBLANES_V7X - 1
    cmap = lambda i: (lay.cond_of_tile(i, tb), 0, 0)
    return pl.pallas_call(
        functools.partial(_rwkv_prep_kernel, lay=lay),
        grid=(n // tb,),
        in_specs=[pl.BlockSpec((tb, d), lambda i: (i, 0)),
                  pl.BlockSpec((SUBLANES_V7X, d), lambda i: (jnp.maximum(i * r8 - 1, 0), 0)),
                  pl.BlockSpec((SUBLANES_V7X, d), lambda i: (jnp.minimum((i + 1) * r8, last8), 0)),
                  pl.BlockSpec((1, d), lambda i: (0, 0)),
                  pl.BlockSpec((None, 1, d), cmap),
                  pl.BlockSpec((None, 1, d), cmap),
                  pl.BlockSpec((6, d), lambda i: (0, 0))],
        out_specs=[pl.BlockSpec((tb, d), lambda i: (i, 0))] * 6,
        out_shape=[jax.ShapeDtypeStruct((n, d), BF16)] * 6,
        compiler_params=_cparams(1), name="rwkv_prep",
    )(x, x, x, nw.reshape(1, d), sc, sh, mu)


RW_LOG_DECAY_SCALE = -math.exp(-0.5)


def _rwkv_lora_kernel(xw_ref, xa_ref, xg_ref, w1_ref, w2_ref, w0_ref, a1_ref, a2_ref, a0_ref,
                      g1_ref, g2_ref, lw_ref, a_ref, g_ref):
    xw, xa, xg = xw_ref[...], xa_ref[...], xg_ref[...]
    for d in range(2):
        t = jnp.tanh(jnp.dot(xw, w1_ref[d], preferred_element_type=F32))
        wl = w0_ref[d] + _bdot(t, w2_ref[d])
        lw_ref[d] = RW_LOG_DECAY_SCALE * jax.nn.sigmoid(wl)
        t = jnp.dot(xa, a1_ref[d], preferred_element_type=F32)
        a_ref[d] = jax.nn.sigmoid(a0_ref[d] + _bdot(t, a2_ref[d]))
    t = jax.nn.sigmoid(jnp.dot(xg, g1_ref[...], preferred_element_type=F32))
    g_ref[...] = _bdot(t, g2_ref[...])


def rwkv_lora(lay, xw, xa, xg, w1, w2, w0, a1, a2, a0, g1, g2):
    n, d = xw.shape
    tm = lay.tile(256)
    full = lambda arr: pl.BlockSpec(arr.shape, lambda i: (0,) * arr.ndim)
    row = pl.BlockSpec((tm, d), lambda i: (i, 0))
    return pl.pallas_call(
        _rwkv_lora_kernel,
        grid=(n // tm,),
        in_specs=[row, row, row] + [full(t) for t in (w1, w2, w0, a1, a2, a0, g1, g2)],
        out_specs=[pl.BlockSpec((2, tm, d), lambda i: (0, i, 0)),
                   pl.BlockSpec((2, tm, d), lambda i: (0, i, 0)),
                   row],
        out_shape=[jax.ShapeDtypeStruct((2, n, d), F32), jax.ShapeDtypeStruct((2, n, d), F32),
                   jax.ShapeDtypeStruct((n, d), F32)],
        compiler_params=_cparams(1), name="rwkv_lora",
    )(xw, xa, xg, w1, w2, w0, a1, a2, a0, g1, g2)


def _wkv_constants():
    c, g, hd = RW_CHUNK, RW_GROUP, RW_HEAD
    gc, lanes = g * c, g * hd
    t = np.arange(c)
    cum = np.stack([(t[None, :] <= t[:, None]), (t[None, :] >= t[:, None])])
    tr = np.arange(c)[:, None]
    tc = np.arange(gc)[None, :] % c
    strict = np.stack([tc < tr, tc > tr])
    incl = np.stack([tc <= tr, tc >= tr])
    head_rows = np.arange(gc)[:, None] // c == np.arange(lanes)[None, :] // hd
    blk_rows = np.arange(gc)[:, None] // c == np.arange(gc)[None, :] // c
    bd = np.arange(lanes)[:, None] // hd == np.arange(lanes)[None, :] // hd
    eye_w = tr == tc
    return dict(cum=jnp.asarray(cum, BF16), strict=jnp.asarray(strict, F32), incl=jnp.asarray(incl, F32),
                head_rows=jnp.asarray(head_rows, BF16), blk_rows=jnp.asarray(blk_rows, BF16),
                bd=jnp.asarray(bd, F32), bd_b=jnp.asarray(bd, BF16), eye_w=jnp.asarray(eye_w, F32))


def _split_dot(m01, x, passes):
    acc, rem = None, x
    for _ in range(passes):
        part = rem.astype(BF16)
        term = jnp.dot(m01, part, preferred_element_type=F32)
        acc = term if acc is None else acc + term
        rem = rem - part.astype(F32)
    return acc


def _split_dot_r(x, m01, passes):
    acc, rem = None, x
    for _ in range(passes):
        part = rem.astype(BF16)
        term = jnp.dot(part, m01, preferred_element_type=F32)
        acc = term if acc is None else acc + term
        rem = rem - part.astype(F32)
    return acc


def _tile_rows(x, mask_b):
    return jnp.concatenate([x.astype(BF16)] * RW_GROUP, axis=0) * mask_b


def _wkv_prep2(raw, cums, head_rows):
    c = RW_CHUNK
    mids = []
    for (r, v, kk, kd, a, lw, _, _, _), cum in zip(raw, cums):
        tot = jnp.sum(lw, axis=0, keepdims=True)
        kka = kk * a
        e_inv, e_rest = jnp.exp(-cum), jnp.exp(tot - cum)
        q2 = jnp.concatenate([kk * jnp.exp(cum - lw), r * jnp.exp(cum)], axis=0).astype(BF16)
        mids.append(dict(q2=q2, kdh=_tile_rows(kd * e_inv, head_rows), kkah=_tile_rows(kka * e_inv, head_rows),
                         vbd=_tile_rows(v, head_rows), v=v, decay=jnp.exp(tot),
                         kw=jnp.concatenate([kd * e_rest, -(kka * e_rest)], axis=0).astype(BF16)))
    s1s = [_bdot_nt(m['q2'], m['kdh']) for m in mids]
    s2s = [_bdot_nt(m['q2'][c:], m['kkah']) for m in mids]
    for m, u, s1, s2 in zip(mids, raw, s1s, s2s):
        strict_w, incl_w = u[7], u[8]
        m['lad'] = jnp.concatenate([jnp.where(strict_w > 0, s1[:c], 0.0),
                                    jnp.where(incl_w > 0, s1[c:], 0.0)], axis=0).astype(BF16)
        m['a_a'] = jnp.where(incl_w > 0, s2, 0.0).astype(BF16)
    return mids


def _wkv_prep3(mids):
    for m, lav in zip(mids, [_bdot(m['lad'], m['vbd']) for m in mids]):
        m['lav'] = lav
    return mids


def _wkv_adv1(states, preps):
    return [_bdot_nt(p['q2'], s) for p, s in zip(preps, states)]


def _wkv_adv2(p0s, preps, t_ws, head_rows):
    c = RW_CHUNK
    return [_bdot(t_w, _tile_rows(p0[:c] + p['lav'][:c], head_rows))
            for p0, p, t_w in zip(p0s, preps, t_ws)]


def _wkv_adv3(states, p0s, us, preps, head_rows, bd):
    c = RW_CHUNK
    upds = [_bdot(jnp.concatenate([p['v'].astype(F32), u], axis=0).T, p['kw']) for p, u in zip(preps, us)]
    aus = [_bdot(p['a_a'], _tile_rows(u, head_rows)) for p, u in zip(preps, us)]
    new_states = [s * p['decay'] + jnp.where(bd > 0, upd, 0.0) for s, p, upd in zip(states, preps, upds)]
    ys = [p0[c:] + p['lav'][c:] - au for p0, p, au in zip(p0s, preps, aus)]
    return new_states, ys


RKV_TN = RW_LANES


def _rwkv_rkv_kernel(xr_ref, xk_ref, xv_ref, wr_ref, wk_ref, wv_ref, a_ref, kkw_ref, kaw_ref, rk_ref, bd_ref,
                     r_ref, v_ref, kk_ref, kd_ref, b_ref):
    r = jnp.dot(xr_ref[...], wr_ref[...], preferred_element_type=F32)
    k = jnp.dot(xk_ref[...], wk_ref[...], preferred_element_type=F32)
    v = jnp.dot(xv_ref[...], wv_ref[...], preferred_element_type=F32)
    r_ref[...] = r
    v_ref[...] = v.astype(v_ref.dtype)
    bd = bd_ref[...]
    kk = k * kkw_ref[...]
    mix = None
    for d in range(2):
        kd = k * (1.0 + (a_ref[d] - 1.0) * kaw_ref[...])
        kd_ref[d] = kd
        term = kd * rk_ref[d]
        mix = term if mix is None else mix + term
    rm = r * mix
    for h in range(RKV_TN // RW_LANES):
        cs = slice(h * RW_LANES, (h + 1) * RW_LANES)
        kkh = kk[:, cs]
        kk_ref[:, cs] = kkh * lax.rsqrt(_split_dot_r(kkh * kkh, bd, 2) + 1e-12)
        b_ref[:, cs] = _split_dot_r(rm[:, cs], bd, 2) * v[:, cs]


def rwkv_rkv(lay, xr, xk, xv, wr, wk, wv, a, kkw, kaw, rk):
    n, kdim = xr.shape
    dm = wr.shape[1]
    tm, tn = lay.tile(1024), RKV_TN
    bd_b = _wkv_constants()['bd_b']
    lhs = pl.BlockSpec((tm, kdim), lambda i, j: (i, 0))
    rhs = pl.BlockSpec((kdim, tn), lambda i, j: (0, j))
    tok = pl.BlockSpec((tm, tn), lambda i, j: (i, j))
    two = pl.BlockSpec((2, tm, tn), lambda i, j: (0, i, j))
    par = pl.BlockSpec((1, tn), lambda i, j: (0, j))
    return pl.pallas_call(
        _rwkv_rkv_kernel,
        grid=(n // tm, dm // tn),
        in_specs=[lhs, lhs, lhs, rhs, rhs, rhs, two, par, par,
                  pl.BlockSpec((2, 1, tn), lambda i, j: (0, 0, j)),
                  pl.BlockSpec(bd_b.shape, lambda i, j: (0, 0))],
        out_specs=[tok, tok, tok, two, tok],
        out_shape=[jax.ShapeDtypeStruct((n, dm), F32), jax.ShapeDtypeStruct((n, dm), BF16),
                   jax.ShapeDtypeStruct((n, dm), F32), jax.ShapeDtypeStruct((2, n, dm), F32),
                   jax.ShapeDtypeStruct((n, dm), F32)],
        compiler_params=_cparams(2), name="rwkv_rkv",
    )(xr, xk, xv, wr, wk, wv, a, kkw, kaw, rk, bd_b)


def _tinv_begin(items, head_rows, blk_rows, eye_w):
    cums = [_split_dot(it[3], it[2], 3) for it in items]
    ops = [(kk * jnp.exp(cum - lw), _tile_rows(kk * a * jnp.exp(-cum), head_rows))
           for (kk, a, lw, _, _), cum in zip(items, cums)]
    n_ws = [-jnp.where(it[4] > 0, _bdot_nt(q, w), 0.0) for it, (q, w) in zip(items, ops)]
    n_pows = [_bdot(n_w, _tile_rows(n_w, blk_rows)) for n_w in n_ws]
    return [eye_w + n_w for n_w in n_ws], n_pows, cums


def _tinv_level(t_ws, n_pows, last, blk_rows):
    c = RW_CHUNK
    ws = [_tile_rows(n_pow, blk_rows) for n_pow in n_pows]
    if last:
        return [t_w + _bdot(t_w, w) for t_w, w in zip(t_ws, ws)], None
    boths = [_bdot(jnp.concatenate([t_w, n_pow], axis=0), w) for t_w, n_pow, w in zip(t_ws, n_pows, ws)]
    return [t_w + both[:c] for t_w, both in zip(t_ws, boths)], [both[c:] for both in boths]


RW_TINV_LEVELS = int(math.log2(RW_CHUNK))


def _wkv_fused_kernel(*refs, lay):
    (rf, vf, kkf, kdf, af, lwf, s0f, kkfn, afn, lwfn,
     rb, vb, kkb, kdb, ab, lwb, s0b, kkbn, abn, lwbn,
     cum_ref, strict_ref, incl_ref, hr_ref, br_ref, eye_ref, bd_ref,
     yf_ref, yb_ref, sff_ref, sfb_ref, sf_scr, sb_scr, t_scr, c_scr) = refs
    c = RW_CHUNK
    n_chunks = lay.tb // c
    j = pl.program_id(1)
    cur = lax.rem(j, 2)
    nxt = 1 - cur
    _, pos_f, cnt_f = lay.seq_info(j)
    _, pos_b, cnt_b = lay.seq_info(lay.nb - 1 - j)

    heads = [(gi, h, slice(h * RW_HEAD, (h + 1) * RW_HEAD)) for gi in range(RW_GSTEP) for h in range(RW_GROUP)]

    def load_states(scr, s0):
        scr[...] = jnp.zeros(scr.shape, scr.dtype)
        for gi, h, hs in heads:
            scr[gi, hs, hs] = s0[gi, h]

    def store_states(out, scr):
        for gi, h, hs in heads:
            out[gi, h] = scr[gi, hs, hs]

    @pl.when(pos_f == 0)
    def _():
        load_states(sf_scr, s0f)

    @pl.when(pos_b == cnt_b - 1)
    def _():
        load_states(sb_scr, s0b)

    hr, br, eye_w, bd = hr_ref[...], br_ref[...], eye_ref[...], bd_ref[...]
    sl_f = [slice(ci * c, (ci + 1) * c) for ci in range(n_chunks)]
    sl_b = sl_f[::-1]
    cols = [slice(gi * RW_LANES, (gi + 1) * RW_LANES) for gi in range(RW_GSTEP)]

    def inv_items(kk_f, a_f, lw_f, kk_b, a_b, lw_b, sl):
        out = []
        for cs in cols:
            out.append((kk_f[sl, cs], a_f[sl, cs], lw_f[sl, cs], cum_ref[0], strict_ref[0]))
            out.append((kk_b[sl, cs], a_b[sl, cs], lw_b[sl, cs], cum_ref[1], strict_ref[1]))
        return out

    def inv_store(scr, slot, sl, vals):
        for gi, cs in enumerate(cols):
            scr[slot, 0, sl, cs] = vals[2 * gi].astype(scr.dtype)
            scr[slot, 1, sl, cs] = vals[2 * gi + 1].astype(scr.dtype)

    @pl.when(j == 0)
    def _():
        def body(ci, carry):
            sl = pl.ds(pl.multiple_of(ci * c, c), c)
            t_ws, n_pows, cums = _tinv_begin(inv_items(kkf, af, lwf, kkb, ab, lwb, sl), hr, br, eye_w)
            inv_store(c_scr, cur, sl, cums)
            for lv in range(1, RW_TINV_LEVELS):
                t_ws, n_pows = _tinv_level(t_ws, n_pows, lv == RW_TINV_LEVELS - 1, br)
            inv_store(t_scr, cur, sl, t_ws)
            return carry
        lax.fori_loop(0, n_chunks, body, 0)

    def raw(ci):
        out = []
        for cs in cols:
            sf, sb = sl_f[ci], sl_b[ci]
            out.append((rf[sf, cs], vf[sf, cs], kkf[sf, cs], kdf[sf, cs], af[sf, cs], lwf[sf, cs],
                        cum_ref[0], strict_ref[0], incl_ref[0]))
            out.append((rb[sb, cs], vb[sb, cs], kkb[sb, cs], kdb[sb, cs], ab[sb, cs], lwb[sb, cs],
                        cum_ref[1], strict_ref[1], incl_ref[1]))
        return out

    states = []
    for gi in range(RW_GSTEP):
        states += [sf_scr[gi], sb_scr[gi]]
    def cached_cums(ci):
        out = []
        for cs in cols:
            out += [c_scr[cur, 0, sl_f[ci], cs], c_scr[cur, 1, sl_b[ci], cs]]
        return out

    preps = _wkv_prep3(_wkv_prep2(raw(0), cached_cums(0), hr))
    inv = {'lv': 0, 't': None, 'n': None}
    per = 2 * RW_GSTEP

    def inv_step():
        lv = inv['lv']
        if lv == 0:
            items = []
            for sl in sl_f:
                items += inv_items(kkfn, afn, lwfn, kkbn, abn, lwbn, sl)
            inv['t'], inv['n'], cums = _tinv_begin(items, hr, br, eye_w)
            for ci2, sl in enumerate(sl_f):
                inv_store(c_scr, nxt, sl, cums[ci2 * per:(ci2 + 1) * per])
        elif lv < RW_TINV_LEVELS:
            inv['t'], inv['n'] = _tinv_level(inv['t'], inv['n'], lv == RW_TINV_LEVELS - 1, br)
            if lv == RW_TINV_LEVELS - 1:
                for ci2, sl in enumerate(sl_f):
                    inv_store(t_scr, nxt, sl, inv['t'][ci2 * per:(ci2 + 1) * per])
        inv['lv'] = lv + 1

    for ci in range(n_chunks):
        more = ci + 1 < n_chunks
        t_ws = []
        for cs in cols:
            t_ws += [t_scr[cur, 0, sl_f[ci], cs], t_scr[cur, 1, sl_b[ci], cs]]
        p0s = _wkv_adv1(states, preps)
        if more:
            raw_n = raw(ci + 1)
            cums_n = cached_cums(ci + 1)
        inv_step()
        us = _wkv_adv2(p0s, preps, t_ws, hr)
        if more:
            mids_n = _wkv_prep2(raw_n, cums_n, hr)
        inv_step()
        states, ys = _wkv_adv3(states, p0s, us, preps, hr, bd)
        if more:
            preps = _wkv_prep3(mids_n)
        for gi, cs in enumerate(cols):
            yf_ref[sl_f[ci], cs] = ys[2 * gi]
            yb_ref[sl_b[ci], cs] = ys[2 * gi + 1]
    while inv['lv'] < RW_TINV_LEVELS:
        inv_step()
    for gi in range(RW_GSTEP):
        sf_scr[gi] = states[2 * gi]
        sb_scr[gi] = states[2 * gi + 1]

    @pl.when(pos_f == cnt_f - 1)
    def _():
        store_states(sff_ref, sf_scr)

    @pl.when(pos_b == 0)
    def _():
        store_states(sfb_ref, sb_scr)


def wkv_fused(lay, r, v, kk, kd, a, lw, s0):
    n, dm = r.shape
    tb, lanes = lay.tb, RW_LANES
    ng = dm // lanes
    width = RW_GSTEP * lanes
    k = _wkv_constants()
    full = lambda arr: pl.BlockSpec(arr.shape, lambda g, j: (0,) * arr.ndim)

    def views(d, blk, blk_next):
        tok = pl.BlockSpec((tb, width), lambda g, j: (blk(j), g))
        tok2 = pl.BlockSpec((None, tb, width), lambda g, j: (d, blk(j), g))
        tok_n = pl.BlockSpec((tb, width), lambda g, j: (blk_next(j), g))
        tok2_n = pl.BlockSpec((None, tb, width), lambda g, j: (d, blk_next(j), g))

        def s0_map(g, j):
            seq, _, _ = lay.seq_info(blk(j))
            return (jnp.maximum(seq - lay.n_ctx + 1, 0), d, g, 0, 0, 0)

        def sfin_map(g, j):
            seq, _, _ = lay.seq_info(blk(j))
            return (seq, g, 0, 0, 0)

        per_head = (RW_GSTEP, RW_GROUP, RW_HEAD, RW_HEAD)
        ins = [tok, tok, tok, tok2, tok2, tok2, pl.BlockSpec((None, None) + per_head, s0_map),
               tok_n, tok2_n, tok2_n]
        return ins, tok, pl.BlockSpec((None,) + per_head, sfin_map)

    in_f, y_f, sf_f = views(0, lambda j: j, lambda j: jnp.minimum(j + 1, lay.nb - 1))
    in_b, y_b, sf_b = views(1, lambda j: lay.nb - 1 - j, lambda j: jnp.maximum(lay.nb - 2 - j, 0))
    consts = [k['cum'], k['strict'], k['incl'], k['head_rows'], k['blk_rows'], k['eye_w'], k['bd']]
    args = [r, v, kk, kd, a, lw, s0, kk, a, lw]
    return pl.pallas_call(
        functools.partial(_wkv_fused_kernel, lay=lay),
        grid=(ng // RW_GSTEP, lay.nb),
        in_specs=in_f + in_b + [full(x) for x in consts],
        out_specs=[y_f, y_b, sf_f, sf_b],
        out_shape=[jax.ShapeDtypeStruct((n, dm), F32), jax.ShapeDtypeStruct((n, dm), F32),
                   jax.ShapeDtypeStruct((lay.n_seq, ng, RW_GROUP, RW_HEAD, RW_HEAD), F32),
                   jax.ShapeDtypeStruct((lay.n_seq, ng, RW_GROUP, RW_HEAD, RW_HEAD), F32)],
        scratch_shapes=[pltpu.VMEM((RW_GSTEP, lanes, lanes), F32), pltpu.VMEM((RW_GSTEP, lanes, lanes), F32),
                        pltpu.VMEM((2, 2, tb, width), BF16), pltpu.VMEM((2, 2, tb, width), F32)],
        compiler_params=_cparams(2), name="wkv_fused",
    )(*args, *args, *consts)


def _rwkv_post_kernel(yf_ref, yb_ref, b_ref, g_ref, lnw_ref, lnb_ref, bd_ref, o_ref):
    y = yf_ref[...] + yb_ref[...]
    bd = bd_ref[...]
    inv = 1.0 / RW_HEAD
    mu = _split_dot_r(y, bd, 2) * inv
    yc = y - mu
    var = _split_dot_r(yc * yc, bd, 2) * inv
    yn = yc * lax.rsqrt(var + RW_LN_EPS)
    out = yn * lnw_ref[...] + lnb_ref[...] + b_ref[...]
    o_ref[...] = (out * g_ref[...]).astype(o_ref.dtype)


def rwkv_post(lay, y_f, y_b, bonus, g, lnw, lnb):
    n, dm = y_f.shape
    lanes = RW_LANES
    tm = lay.tile(1024)
    bd_b = _wkv_constants()['bd_b']
    tok = pl.BlockSpec((tm, lanes), lambda i, c: (i, c))
    par = pl.BlockSpec((1, lanes), lambda i, c: (0, c))
    return pl.pallas_call(
        _rwkv_post_kernel,
        grid=(n // tm, dm // lanes),
        in_specs=[tok, tok, tok, tok, par, par, pl.BlockSpec(bd_b.shape, lambda i, c: (0, 0))],
        out_specs=tok,
        out_shape=jax.ShapeDtypeStruct((n, dm), BF16),
        compiler_params=_cparams(2), name="rwkv_post",
    )(y_f, y_b, bonus, g, lnw, lnb, bd_b)


def rwkv_layer(lay, x, mods, nw, p, s0):
    n, d = x.shape
    sh1, sc1, g1 = mods[0], mods[1], mods[2]
    xr, xw, xk, xv, xa, xg = rwkv_prep(lay, x, nw, sc1, sh1, p['mu'])
    tm = lay.tile(1024)
    lw, a, g = rwkv_lora(lay, xw, xa, xg, p['w1'], p['w2'], p['w0'], p['a1'], p['a2'], p['a0'],
                         p['g1'], p['g2'])
    r, v, kk, kd, bonus = rwkv_rkv(lay, xr, xk, xv, p['wr'], p['wk'], p['wv'], a, p['kk'], p['ka'], p['rk'])
    y_f, y_b, sfin_f, sfin_b = wkv_fused(lay, r, v, kk, kd, a, lw, s0)
    z = rwkv_post(lay, y_f, y_b, bonus, g, p['lnx_w'], p['lnx_b'])
    x = matmul_gated_residual(lay, z, p['wo'], x, g1, tm=tm, tn=512, name="rwkv_o")
    return x, jnp.stack([sfin_f, sfin_b], axis=1)


def _hgrn_constants():
    c = HG_CHUNK_TOKENS
    t = np.arange(c)[:, None]
    j = np.arange(c)[None, :]
    cums, masks_all = [], []
    for rev in (False, True):
        masks = []
        h = 1
        while h < c:
            upper = (t % (2 * h)) >= h
            same = (t // (2 * h)) == (j // (2 * h))
            if not rev:
                mask = same & upper & ((j % (2 * h)) < h)
            else:
                mask = same & (~upper) & ((j % (2 * h)) >= h)
            masks.append(mask)
            h *= 2
        masks.append(t == j)
        cums.append((j >= t) if rev else (j <= t))
        masks_all.append(np.stack(masks, 0))
    return jnp.asarray(np.stack(cums), BF16), jnp.asarray(np.stack(masks_all).astype(np.float32))


def _hgrn_level_exponents(g, gcum, rev):
    c, kdim = g.shape
    row = lax.broadcasted_iota(jnp.int32, g.shape, 0)
    nxt = pltpu.roll(g, c - 1, axis=0)
    prv = pltpu.roll(g, 1, axis=0)
    r2, r4 = row & 1, row & 3
    if not rev:
        x1 = jnp.where(r2 == 1, g, 0.0)
        x2 = jnp.where(r4 == 0, nxt, jnp.where(r4 == 2, g, jnp.where(r4 == 3, prv + g, 0.0)))
    else:
        x1 = jnp.where(r2 == 0, g, 0.0)
        x2 = jnp.where(r4 == 0, g + nxt, jnp.where(r4 == 1, g, jnp.where(r4 == 3, prv, 0.0)))
    xs = [x1, x2]
    h = 4
    while h < c:
        gr = gcum.reshape(c // (2 * h), 2 * h, kdim)
        ref = gr[:, h:h + 1, :] if rev else gr[:, h - 1:h, :]
        upper = lax.broadcasted_iota(jnp.int32, gr.shape, 1) >= h
        diff = gr - ref
        x = jnp.where(upper, -diff, diff) if rev else jnp.where(upper, diff, -diff)
        xs.append(x.reshape(c, kdim))
        h *= 2
    return xs


def _hgrn_units(units, masks_by_dir, cum_by_dir):
    c = HG_CHUNK_TOKENS
    gs = [jnp.log(f) for _, f, _, _ in units]
    gcums = [_split_dot(cum_by_dir[rev], g, 3) for g, (_, _, _, rev) in zip(gs, units)]
    outs = []
    pend = []
    for (q, f, iv, rev), g, gcum in zip(units, gs, gcums):
        k = 1.0 - f
        tot = gcum[0:1] if rev else gcum[c - 1:c]
        es = [jnp.exp(x) for x in _hgrn_level_exponents(g, gcum, rev)]
        pend.append((q, k, iv, rev, es, jnp.exp(gcum), jnp.exp(tot - gcum), jnp.exp(tot)))
    for q, k, iv, rev, es, eg, erest, etot in pend:
        masks = masks_by_dir[rev]
        a = masks[len(es)] * _bdot_nt(q, k)
        for lv, el in enumerate(es):
            a = a + masks[lv] * _bdot_nt(q * el, k * el)
        outs.append(dict(a=a, iv=iv, qe=(q * eg).astype(BF16), kdec=(k * erest).astype(BF16), decay=etot))
    for u in outs:
        u['av'] = _bdot(u['a'], u['iv'])
    for u in outs:
        u['upd'] = _bdot(u['iv'].astype(F32).T, u['kdec'])
    return outs


def _hgrn_kernel(*refs, lay):
    (qf, ff, if_, s0f, qb, fb, ib, s0b, cum_ref, mask_ref, of_ref, ob_ref, sff_ref, sfb_ref,
     sf_scr, sb_scr) = refs
    c = HG_CHUNK_TOKENS
    n_chunks = lay.tb // c
    j = pl.program_id(1)
    _, pos_f, cnt_f = lay.seq_info(j)
    _, pos_b, cnt_b = lay.seq_info(lay.nb - 1 - j)

    @pl.when(pos_f == 0)
    def _():
        sf_scr[...] = s0f[...]

    @pl.when(pos_b == cnt_b - 1)
    def _():
        sb_scr[...] = s0b[...]

    masks_by_dir = [mask_ref[0], mask_ref[1]]
    cum_by_dir = [cum_ref[0], cum_ref[1]]
    sl_f = [slice(ci * c, (ci + 1) * c) for ci in range(n_chunks)]
    sl_b = sl_f[::-1]
    cols = [slice(hi * HG_K, (hi + 1) * HG_K) for hi in range(HG_HSTEP)]
    chains = []
    for hi, cs in enumerate(cols):
        chains.append((sf_scr, hi, of_ref, [(sl, cs) for sl in sl_f],
                       [(qf[sl, cs], ff[sl, cs], if_[sl, cs], False) for sl in sl_f]))
        chains.append((sb_scr, hi, ob_ref, [(sl, cs) for sl in sl_b],
                       [(qb[sl, cs], fb[sl, cs], ib[sl, cs], True) for sl in sl_b]))
    done = _hgrn_units([u for ch in chains for u in ch[4]], masks_by_dir, cum_by_dir)
    pend = []
    for n_ch, (scr, hi, o_ref, where, _) in enumerate(chains):
        s = scr[hi]
        for ci in range(n_chunks):
            u = done[n_ch * n_chunks + ci]
            pend.append((o_ref, where[ci], u, s))
            s = s * u['decay'] + u['upd']
        scr[hi] = s
    for o_ref, (sl, cs), u, s_prev in pend:
        o_ref[sl, cs] = u['av'] + _bdot_nt(u['qe'], s_prev)

    @pl.when(pos_f == cnt_f - 1)
    def _():
        sff_ref[...] = sf_scr[...]

    @pl.when(pos_b == 0)
    def _():
        sfb_ref[...] = sb_scr[...]


def hgrn_scan(lay, q, f_fwd, f_bwd, iv, s0t):
    n, dm = q.shape
    tb = lay.tb
    nh = dm // HG_K
    width = HG_HSTEP * HG_K
    cum_m, masks = _hgrn_constants()
    full = lambda arr: pl.BlockSpec(arr.shape, lambda h, j: (0,) * arr.ndim)

    def views(d, blk):
        tok = pl.BlockSpec((tb, width), lambda h, j: (blk(j), h))

        def s0_map(h, j):
            seq, _, _ = lay.seq_info(blk(j))
            return (jnp.maximum(seq - lay.n_ctx + 1, 0), d, h, 0, 0)

        def sfin_map(h, j):
            seq, _, _ = lay.seq_info(blk(j))
            return (seq, h, 0, 0)

        ins = [tok, tok, tok, pl.BlockSpec((None, None, HG_HSTEP, HG_K, HG_K), s0_map)]
        return ins, tok, pl.BlockSpec((None, HG_HSTEP, HG_K, HG_K), sfin_map)

    in_f, o_f, sf_f = views(0, lambda j: j)
    in_b, o_b, sf_b = views(1, lambda j: lay.nb - 1 - j)
    return pl.pallas_call(
        functools.partial(_hgrn_kernel, lay=lay),
        grid=(nh // HG_HSTEP, lay.nb),
        in_specs=in_f + in_b + [full(cum_m), full(masks)],
        out_specs=[o_f, o_b, sf_f, sf_b],
        out_shape=[jax.ShapeDtypeStruct((n, dm), F32), jax.ShapeDtypeStruct((n, dm), F32),
                   jax.ShapeDtypeStruct((lay.n_seq, nh, HG_K, HG_K), F32),
                   jax.ShapeDtypeStruct((lay.n_seq, nh, HG_K, HG_K), F32)],
        scratch_shapes=[pltpu.VMEM((HG_HSTEP, HG_K, HG_K), F32), pltpu.VMEM((HG_HSTEP, HG_K, HG_K), F32)],
        compiler_params=_cparams(2), name="hgrn_scan",
    )(q, f_fwd, iv, s0t, q, f_bwd, iv, s0t, cum_m, masks)


HG_POST_HEADS = 4


def _hgrn_post_kernel(of_ref, ob_ref, g_ref, nw_ref, z_ref):
    for h in range(HG_POST_HEADS):
        cs = slice(h * HG_K, (h + 1) * HG_K)
        o = of_ref[:, cs] + ob_ref[:, cs]
        o = o * lax.rsqrt(jnp.mean(o * o, axis=-1, keepdims=True) + NORM_EPS) * nw_ref[...] * g_ref[:, cs]
        z_ref[:, cs] = o.astype(z_ref.dtype)


def hgrn_post(lay, o_f, o_b, gs, nw):
    n, dm = o_f.shape
    tm = lay.tile(1024)
    width = HG_POST_HEADS * HG_K
    tok = pl.BlockSpec((tm, width), lambda i, h: (i, h))
    return pl.pallas_call(
        _hgrn_post_kernel,
        grid=(n // tm, dm // width),
        in_specs=[tok, tok, tok, pl.BlockSpec((1, HG_K), lambda i, h: (0, 0))],
        out_specs=tok,
        out_shape=jax.ShapeDtypeStruct((n, dm), BF16),
        compiler_params=_cparams(2), name="hgrn_post",
    )(o_f, o_b, gs, nw)


def _epi_hgrn_in(accs, extras):
    lb = extras[0]
    q = _silu(accs[0])
    f0 = lb + (1.0 - lb) * jax.nn.sigmoid(accs[1])
    f1 = lb + (1.0 - lb) * jax.nn.sigmoid(accs[2])
    return [q, f0, f1, accs[3], _silu(accs[4])]


def hgrn_layer(lay, x, mods, nw, p, lb, s0t):
    n, d = x.shape
    sh1, sc1, g1 = mods[0], mods[1], mods[2]
    h = norm_mod(lay, x, nw, sc1, sh1, BF16)
    tm, tn = lay.tile(1024), 256
    extras = ((lb, pl.BlockSpec((1, tn), lambda i, j: (0, j))),)
    q, f0, f1, iv, gs = matmul(h, p['w_in'], _epi_hgrn_in, [F32, F32, F32, BF16, F32], tm=tm, tn=tn, extras=extras,
                               name="hgrn_in")
    o_f, o_b, sfin_f, sfin_b = hgrn_scan(lay, q, f0, f1, iv, s0t)
    z = hgrn_post(lay, o_f, o_b, gs, p['norm_w'])
    x = matmul_gated_residual(lay, z, p['wo'], x, g1, tm=lay.tile(1024), tn=512, name="hgrn_o")
    return x, jnp.stack([sfin_f, sfin_b], axis=1)


ML_DOWN_COLS = 1280
ML_KR_OFF = ML_Q_LORA + ML_KV_LORA
ML_KRS_OFF = ML_KR_OFF + LANES_V7X


def _rms(x, w):
    return x * lax.rsqrt(jnp.mean(x * x, axis=-1, keepdims=True) + NORM_EPS) * w


def _mla_mid_kernel(dn_ref, qw_ref, kvw_ref, cos_ref, sin_ref, qn_ref, ckv_ref, kr_ref):
    dn = dn_ref[...]
    qn_ref[...] = _rms(dn[:, :ML_Q_LORA], qw_ref[...]).astype(qn_ref.dtype)
    ckv_ref[...] = _rms(dn[:, ML_Q_LORA:ML_KR_OFF], kvw_ref[...])
    kr = dn[:, ML_KR_OFF:ML_KR_OFF + ML_ROPE]
    krs = dn[:, ML_KRS_OFF:ML_KRS_OFF + ML_ROPE]
    kr_ref[...] = kr * cos_ref[...] + krs * sin_ref[...]


def mla_mid(lay, dn, qw, kvw, cos, sin):
    n = dn.shape[0]
    tm = lay.tile(512)
    return pl.pallas_call(
        _mla_mid_kernel,
        grid=(n // tm,),
        in_specs=[pl.BlockSpec((tm, ML_DOWN_COLS), lambda i: (i, 0)),
                  pl.BlockSpec((1, ML_Q_LORA), lambda i: (0, 0)),
                  pl.BlockSpec((1, ML_KV_LORA), lambda i: (0, 0)),
                  pl.BlockSpec((tm, ML_ROPE), lambda i: (i, 0)),
                  pl.BlockSpec((tm, ML_ROPE), lambda i: (i, 0))],
        out_specs=[pl.BlockSpec((tm, ML_Q_LORA), lambda i: (i, 0)),
                   pl.BlockSpec((tm, ML_KV_LORA), lambda i: (i, 0)),
                   pl.BlockSpec((tm, ML_ROPE), lambda i: (i, 0))],
        out_shape=[jax.ShapeDtypeStruct((n, ML_Q_LORA), BF16),
                   jax.ShapeDtypeStruct((n, ML_KV_LORA), F32),
                   jax.ShapeDtypeStruct((n, ML_ROPE), F32)],
        compiler_params=_cparams(1), name="mla_mid",
    )(dn, qw, kvw, cos, sin)


ML_QSCALE = math.log2(math.e) / math.sqrt(ML_NOPE + ML_ROPE)


def _epi_qscale(accs, extras):
    return [accs[0] * ML_QSCALE]


def _epi_rope(accs, extras):
    cos, sin = extras
    reps = accs[0].shape[1] // cos.shape[1]
    cos, sin = jnp.concatenate([cos] * reps, axis=1), jnp.concatenate([sin] * reps, axis=1)
    return [(accs[0] * cos + accs[1] * sin) * ML_QSCALE]


ML_KEY_SPLITS = 2


def _attn_kernel(qn_ref, qr_ref, kn_ref, kr_ref, v_ref, o_ref, kc_scr, vt_scr):
    @pl.when(pl.program_id(2) == 0)
    def _():
        for h in range(2):
            kc_scr[h, :, :ML_NOPE] = kn_ref[:, h * ML_NOPE:(h + 1) * ML_NOPE]
            kc_scr[h, :, ML_NOPE:] = kr_ref[...]
            vt_scr[h] = v_ref[:, h * ML_V:(h + 1) * ML_V].astype(F32).T.astype(vt_scr.dtype)

    k_len = kc_scr.shape[1]
    kh = k_len // ML_KEY_SPLITS
    scores = []
    for h in range(2):
        q = jnp.concatenate([qn_ref[:, h * ML_NOPE:(h + 1) * ML_NOPE],
                             qr_ref[:, h * ML_ROPE:(h + 1) * ML_ROPE]], axis=1)
        scores.append([lax.dot_general(kc_scr[h, kb * kh:(kb + 1) * kh, :], q, (((1,), (1,)), ((), ())),
                                       preferred_element_type=F32) for kb in range(ML_KEY_SPLITS)])
    for h in range(2):
        m_acc = l_acc = o_acc = None
        for kb, st in enumerate(scores[h]):
            m = jnp.max(st, axis=0, keepdims=True)
            p = jnp.exp2(st - m)
            l = jnp.sum(p, axis=0, keepdims=True)
            ot = jnp.dot(vt_scr[h, :, kb * kh:(kb + 1) * kh], p.astype(BF16),
                         preferred_element_type=F32)
            if m_acc is None:
                m_acc, l_acc, o_acc = m, l, ot
            else:
                m_new = jnp.maximum(m_acc, m)
                c_old, c_new = jnp.exp2(m_acc - m_new), jnp.exp2(m - m_new)
                l_acc = l_acc * c_old + l * c_new
                o_acc = o_acc * c_old + ot * c_new
                m_acc = m_new
        o_ref[:, h * ML_V:(h + 1) * ML_V] = (o_acc / l_acc).T.astype(o_ref.dtype)


def attention(qn, qr, kn, kr, v, *, n_seq, q_len, k_len, row0, tq):
    heads2 = qn.shape[1] // (2 * ML_NOPE)
    qb = q_len // tq
    rb0 = row0 // tq
    return pl.pallas_call(
        _attn_kernel,
        grid=(n_seq, heads2, qb),
        in_specs=[pl.BlockSpec((tq, 2 * ML_NOPE), lambda s, h, i: (rb0 + s * qb + i, h)),
                  pl.BlockSpec((tq, 2 * ML_ROPE), lambda s, h, i: (rb0 + s * qb + i, h)),
                  pl.BlockSpec((k_len, 2 * ML_NOPE), lambda s, h, i: (s, h)),
                  pl.BlockSpec((k_len, ML_ROPE), lambda s, h, i: (s, 0)),
                  pl.BlockSpec((k_len, 2 * ML_V), lambda s, h, i: (s, h))],
        out_specs=pl.BlockSpec((tq, 2 * ML_V), lambda s, h, i: (s * qb + i, h)),
        out_shape=jax.ShapeDtypeStruct((n_seq * q_len, heads2 * 2 * ML_V), BF16),
        scratch_shapes=[pltpu.VMEM((2, k_len, ML_NOPE + ML_ROPE), BF16), pltpu.VMEM((2, ML_V, k_len), BF16)],
        compiler_params=_cparams(3), name="mla_attn",
    )(qn, qr, kn, kr, v)


def mla_layer(lay, x, mods, nw, p, cache_ckv, cache_kr, cos, sin, cos2, sin2):
    n, d = x.shape
    sh1, sc1, g1 = mods[0], mods[1], mods[2]
    h = norm_mod(lay, x, nw, sc1, sh1, BF16)
    tm = lay.tile(512)
    dn = matmul(h, [p['w_down']], _epi_plain, [F32], tm=tm, tn=ML_DOWN_COLS, name="mla_down")[0]
    qlat, ckv, kr = mla_mid(lay, dn, p['qnorm_w'], p['kvnorm_w'], cos, sin)
    tm_q = lay.tile(1024)
    qn = matmul(qlat, [p['w_uq_nope']], _epi_qscale, [BF16], tm=tm_q, tn=1024, name="mla_qn")[0]
    tw = cos2.shape[1]
    extras = ((cos2, pl.BlockSpec((tm_q, tw), lambda i, j: (i, 0))),
              (sin2, pl.BlockSpec((tm_q, tw), lambda i, j: (i, 0))))
    qr = matmul(qlat, [p['w_uq_rope'], p['w_uq_rope_sw']], _epi_rope, [BF16], tm=tm_q,
                tn=p['w_uq_rope'].shape[1], extras=extras, name="mla_qr")[0]
    nc, past = lay.nc, cache_ckv.shape[1]
    ckv_b, kr_b = ckv.astype(BF16), kr.astype(BF16)
    kn_c, v_c = matmul(ckv_b[:nc], [p['w_ukn'], p['w_uv']], _epi_plain, [BF16, BF16],
                       tm=lay.tile(512), tn=512, name="mla_kv_ctx")
    o_c = attention(qn, qr, kn_c, kr_b[:nc], v_c, n_seq=lay.n_ctx, q_len=lay.ctx_len,
                    k_len=lay.ctx_len, row0=0, tq=min(256, lay.ctx_len))
    k_len = lay.lat_len + past
    ckv_l = jnp.concatenate([ckv_b[nc:].reshape(lay.n_lat, lay.lat_len, -1), cache_ckv.astype(BF16)],
                            axis=1).reshape(lay.n_lat * k_len, -1)
    kr_l = jnp.concatenate([kr_b[nc:].reshape(lay.n_lat, lay.lat_len, -1), cache_kr.astype(BF16)],
                           axis=1).reshape(lay.n_lat * k_len, -1)
    bf16_rows = 2 * SUBLANES_V7X
    tk = k_len // 2 if (k_len // 2) % bf16_rows == 0 else math.gcd(k_len, 512)
    kn_l, v_l = matmul(ckv_l, [p['w_ukn'], p['w_uv']], _epi_plain, [BF16, BF16], tm=tk, tn=1024,
                       name="mla_kv_lat")
    o_l = attention(qn, qr, kn_l, kr_l, v_l, n_seq=lay.n_lat, q_len=lay.lat_len, k_len=k_len,
                    row0=nc, tq=min(256, lay.lat_len))
    o = jnp.concatenate([o_c, o_l], axis=0)
    x = matmul_gated_residual(lay, o, p['wo'], x, g1, tm=lay.tile(1024), tn=512, name="mla_o")
    return x, ckv[:nc], kr[:nc]


def _rope_tables(lay):
    t = lay.lat_len
    rows = t // GRID_W
    rr = jnp.broadcast_to(jnp.arange(rows, dtype=F32)[:, None], (rows, GRID_W)).reshape(-1)
    cc = jnp.broadcast_to(jnp.arange(GRID_W, dtype=F32)[None, :], (rows, GRID_W)).reshape(-1)
    nf = ML_ROPE // 4
    inv = ROPE_BASE ** (-jnp.arange(nf, dtype=F32) / nf)
    ar, ac = rr[:, None] * inv, cc[:, None] * inv
    cos = jnp.concatenate([jnp.cos(ar), jnp.cos(ar), jnp.cos(ac), jnp.cos(ac)], axis=-1)
    sin = jnp.concatenate([-jnp.sin(ar), jnp.sin(ar), -jnp.sin(ac), jnp.sin(ac)], axis=-1)
    cos = jnp.concatenate([jnp.ones((lay.nc, ML_ROPE), F32), jnp.tile(cos, (lay.n_lat, 1))], axis=0)
    sin = jnp.concatenate([jnp.zeros((lay.nc, ML_ROPE), F32), jnp.tile(sin, (lay.n_lat, 1))], axis=0)
    return cos, sin


def _swap_cols(w):
    k, c = w.shape
    w4 = w.reshape(k, c // 32, 2, 16)
    return w4[:, :, ::-1, :].reshape(k, c)


def ffn(lay, x, mods, nw, w_a, w_b, w_out):
    sh2, sc2, g2 = mods[3], mods[4], mods[5]
    h = norm_mod(lay, x, nw, sc2, sh2, BF16)
    act = matmul(h, [w_a, w_b], _epi_swiglu, [BF16], tm=lay.tile(1024), tn=512, name="ffn_in")[0]
    return matmul_gated_residual(lay, act, w_out, x, g2, tm=lay.tile(1024), tn=512, name="ffn_out")


def kernel(x_prompt, x_sample, state_rwkv, state_hgrn, cache_ckv, cache_krope, c, c_ctx, ada_w, ada_b, norm1_w, norm2_w, ffn_w_in, ffn_w_out, final_norm_w, rw_mu, rw_wr, rw_wk, rw_wv, rw_wo, rw_w0, rw_w1, rw_w2, rw_a0, rw_a1, rw_a2, rw_g1, rw_g2, rw_kk, rw_ka, rw_rk, rw_lnx_w, rw_lnx_b, hg_w_in, hg_lb, hg_norm_w, hg_wo, ml_w_down, ml_qnorm_w, ml_kvnorm_w, ml_w_uq, ml_w_ukv, ml_wo):
    n_ctx, ctx_len, d = x_prompt.shape
    n_lat, lat_len, _ = x_sample.shape
    depth = ada_w.shape[0]
    lay = Layout(n_ctx, ctx_len, n_lat, lat_len)
    d_ff = ffn_w_out.shape[1]
    x = jnp.concatenate([x_prompt.reshape(lay.nc, d), x_sample.reshape(n_lat * lat_len, d)], axis=0)

    n_cond = -(-(1 + n_lat) // SUBLANES_V7X) * SUBLANES_V7X
    cond = jnp.zeros((n_cond, d), F32).at[0].set(c_ctx).at[1:1 + n_lat].set(c)
    mod_all = adaln(cond, ada_w, ada_b)
    mod_all = mod_all.reshape(depth, n_cond, 6, 1, d).transpose(0, 2, 1, 3, 4)

    lb_table = jnp.cumsum(jax.nn.softmax(hg_lb.astype(F32), axis=0), axis=0)
    lb_table = lb_table - lb_table[0]
    cos, sin = _rope_tables(lay)
    cos2, sin2 = jnp.tile(cos, (1, 2)), jnp.tile(sin, (1, 2))
    bf = lambda t: t.astype(BF16)

    new_rwkv, new_hgrn, new_ckv, new_krope = [], [], [], []
    for l in range(depth):
        kind, j = l % 3, l // 3
        mods = mod_all[l]
        if kind == 0:
            pad1 = lambda w: jnp.pad(w, ((0, 0), (0, 0), (0, LORA_PAD - w.shape[2])))
            pad2 = lambda w: jnp.pad(w, ((0, 0), (0, LORA_PAD - w.shape[1]), (0, 0)))
            p = {'mu': rw_mu[j], 'wr': bf(rw_wr[j]), 'wk': bf(rw_wk[j]), 'wv': bf(rw_wv[j]),
                 'wo': bf(rw_wo[j]),
                 'w0': rw_w0[j].reshape(2, 1, d), 'w1': bf(pad1(rw_w1[j])), 'w2': bf(pad2(rw_w2[j])),
                 'a0': rw_a0[j].reshape(2, 1, d), 'a1': bf(pad1(rw_a1[j])), 'a2': bf(pad2(rw_a2[j])),
                 'g1': bf(rw_g1[j]), 'g2': bf(rw_g2[j]),
                 'kk': rw_kk[j].reshape(1, d), 'ka': rw_ka[j].reshape(1, d),
                 'rk': rw_rk[j].reshape(2, 1, d),
                 'lnx_w': rw_lnx_w[j].reshape(1, d), 'lnx_b': rw_lnx_b[j].reshape(1, d)}
            rw_h = state_rwkv.shape[3]
            s_lat = state_rwkv[:, j].astype(F32).reshape(n_lat, 2, rw_h // RW_GROUP, RW_GROUP, RW_HEAD, RW_HEAD)
            s0 = jnp.concatenate([jnp.zeros((1,) + s_lat.shape[1:], F32), s_lat], axis=0)
            x, sfin = rwkv_layer(lay, x, mods, norm1_w[l], p, s0)
            new_rwkv.append(sfin[:n_ctx].reshape(n_ctx, 2, rw_h, RW_HEAD, RW_HEAD))
        elif kind == 1:
            hk = d
            w_in = hg_w_in[j]
            p = {'w_in': [bf(w_in[:, i * hk:(i + 1) * hk]) for i in range(5)],
                 'norm_w': hg_norm_w[j].reshape(1, HG_K), 'wo': bf(hg_wo[j])}
            s_lat = jnp.swapaxes(state_hgrn[:, j].astype(F32), -1, -2)
            s0t = jnp.concatenate([jnp.zeros((1,) + s_lat.shape[1:], F32), s_lat], axis=0)
            x, sfin = hgrn_layer(lay, x, mods, norm1_w[l], p, lb_table[l].reshape(1, d), s0t)
            new_hgrn.append(jnp.swapaxes(sfin[:n_ctx], -1, -2))
        else:
            wd = ml_w_down[j]
            kr_w = wd[:, ML_KR_OFF:]
            zpad = jnp.zeros((d, LANES_V7X - ML_ROPE), wd.dtype)
            w_down = jnp.concatenate([wd, zpad, _swap_cols(kr_w), zpad], axis=1)
            wq = ml_w_uq[j].reshape(ML_Q_LORA, ML_H, ML_NOPE + ML_ROPE)
            wq_n = wq[:, :, :ML_NOPE].reshape(ML_Q_LORA, ML_H * ML_NOPE)
            wq_r = wq[:, :, ML_NOPE:].reshape(ML_Q_LORA, ML_H * ML_ROPE)
            wkv = ml_w_ukv[j].reshape(ML_KV_LORA, ML_H, ML_NOPE + ML_V)
            p = {'w_down': bf(w_down), 'qnorm_w': ml_qnorm_w[j].reshape(1, -1),
                 'kvnorm_w': ml_kvnorm_w[j].reshape(1, -1),
                 'w_uq_nope': bf(wq_n), 'w_uq_rope': bf(wq_r), 'w_uq_rope_sw': bf(_swap_cols(wq_r)),
                 'w_ukn': bf(wkv[:, :, :ML_NOPE].reshape(ML_KV_LORA, ML_H * ML_NOPE)),
                 'w_uv': bf(wkv[:, :, ML_NOPE:].reshape(ML_KV_LORA, ML_H * ML_V)),
                 'wo': bf(ml_wo[j])}
            x, ckv_c, kr_c = mla_layer(lay, x, mods, norm1_w[l], p, cache_ckv[:, j], cache_krope[:, j],
                                       cos, sin, cos2, sin2)
            new_ckv.append(ckv_c.reshape(n_ctx, ctx_len, ML_KV_LORA))
            new_krope.append(kr_c.reshape(n_ctx, ctx_len, ML_ROPE))
        w_in = ffn_w_in[l]
        x = ffn(lay, x, mods, norm2_w[l], bf(w_in[:, :d_ff]), bf(w_in[:, d_ff:]), bf(ffn_w_out[l]))

    y_prompt = rmsnorm_rows(lay, x, final_norm_w, 0, lay.nc).reshape(n_ctx, ctx_len, d)
    y_sample = rmsnorm_rows(lay, x, final_norm_w, lay.nc, lay.n - lay.nc).reshape(n_lat, lat_len, d)
    return (y_prompt, y_sample, jnp.stack(new_rwkv, axis=1), jnp.stack(new_hgrn, axis=1),
            jnp.stack(new_ckv, axis=1), jnp.stack(new_krope, axis=1))
```

```python
import functools
import math

import numpy as np
import jax
import jax.numpy as jnp
from jax import lax
from jax.experimental import pallas as pl
from jax.experimental.pallas import tpu as pltpu

F32 = jnp.float32
BF16 = jnp.bfloat16

LANES_V7X = 128
SUBLANES_V7X = 8
VMEM_BYTES_V7X = 64 * 1024 * 1024
VMEM_LIMIT = VMEM_BYTES_V7X - 8 * 1024 * 1024

NORM_EPS = 1e-6
RW_HEAD = 64
RW_LN_EPS = 64e-5
RW_GROUP = 4
RW_LANES = RW_GROUP * RW_HEAD
RW_CHUNK = 64
RW_GSTEP = 2
HG_K = 128
HG_CHUNK_TOKENS = 64
HG_HSTEP = 8
ML_H = 16
ML_NOPE = 128
ML_ROPE = 64
ML_V = 128
ML_Q_LORA = 512
ML_KV_LORA = 512
GRID_W = 64
ROPE_BASE = 10000.0
LORA_PAD = 128


class Layout:
    def __init__(self, n_ctx, ctx_len, n_lat, lat_len):
        self.n_ctx, self.ctx_len, self.n_lat, self.lat_len = n_ctx, ctx_len, n_lat, lat_len
        self.nc = n_ctx * ctx_len
        self.n = self.nc + n_lat * lat_len
        self.tb = min(256, ctx_len)
        assert ctx_len % self.tb == 0 and lat_len % self.tb == 0 and self.tb % RW_CHUNK == 0
        self.nb = self.n // self.tb
        self.nb_ctx = self.nc // self.tb
        self.bps_ctx = ctx_len // self.tb
        self.bps_lat = lat_len // self.tb
        self.n_seq = n_ctx + n_lat

    def tile(self, want):
        t = want
        while self.nc % t or self.lat_len % t:
            t //= 2
        return t

    def cond_of_tile(self, i, tm):
        row = i * tm
        return jnp.where(row < self.nc, 0, 1 + (row - self.nc) // self.lat_len)

    def seq_info(self, blk):
        is_ctx = blk < self.nb_ctx
        lat = blk - self.nb_ctx
        seq = jnp.where(is_ctx, blk // self.bps_ctx, self.n_ctx + lat // self.bps_lat)
        pos = jnp.where(is_ctx, blk % self.bps_ctx, lat % self.bps_lat)
        cnt = jnp.where(is_ctx, self.bps_ctx, self.bps_lat)
        return seq, pos, cnt


def _cparams(n_axes):
    return pltpu.CompilerParams(dimension_semantics=("arbitrary",) * n_axes, vmem_limit_bytes=VMEM_LIMIT)


def _bdot(a, b):
    return jnp.dot(a.astype(BF16), b.astype(BF16), preferred_element_type=F32)


def _bdot_nt(a, b):
    return lax.dot_general(a.astype(BF16), b.astype(BF16), (((1,), (1,)), ((), ())),
                           preferred_element_type=F32)


def _silu(x):
    return x * jax.nn.sigmoid(x)


def _adaln_kernel(c_ref, w_ref, b_ref, o_ref):
    a = _silu(c_ref[...]).astype(BF16)
    o_ref[...] = jnp.dot(a, w_ref[...].astype(BF16), preferred_element_type=F32) + b_ref[...]


def adaln(cond, ada_w, ada_b):
    depth, d, d6 = ada_w.shape
    r = cond.shape[0]
    tn = 1024
    return pl.pallas_call(
        _adaln_kernel,
        grid=(depth, d6 // tn),
        in_specs=[pl.BlockSpec((r, d), lambda l, j: (0, 0)),
                  pl.BlockSpec((None, d, tn), lambda l, j: (l, 0, j)),
                  pl.BlockSpec((None, 1, tn), lambda l, j: (l, 0, j))],
        out_specs=pl.BlockSpec((None, r, tn), lambda l, j: (l, 0, j)),
        out_shape=jax.ShapeDtypeStruct((depth, r, d6), F32),
        compiler_params=_cparams(2), name="adaln",
    )(cond, ada_w, ada_b.reshape(depth, 1, d6))


def _norm_mod(x, nw, sc, sh):
    y = x * lax.rsqrt(jnp.mean(x * x, axis=-1, keepdims=True) + NORM_EPS)
    return (y * nw) * (1.0 + sc) + sh


def _norm_mod_kernel(x_ref, nw_ref, sc_ref, sh_ref, o_ref):
    o_ref[...] = _norm_mod(x_ref[...], nw_ref[...], sc_ref[...], sh_ref[...]).astype(o_ref.dtype)


def norm_mod(lay, x, nw, sc, sh, out_dtype):
    n, d = x.shape
    tm = lay.tile(512)
    cmap = lambda i: (lay.cond_of_tile(i, tm), 0, 0)
    return pl.pallas_call(
        _norm_mod_kernel,
        grid=(n // tm,),
        in_specs=[pl.BlockSpec((tm, d), lambda i: (i, 0)),
                  pl.BlockSpec((1, d), lambda i: (0, 0)),
                  pl.BlockSpec((None, 1, d), cmap),
                  pl.BlockSpec((None, 1, d), cmap)],
        out_specs=pl.BlockSpec((tm, d), lambda i: (i, 0)),
        out_shape=jax.ShapeDtypeStruct((n, d), out_dtype),
        compiler_params=_cparams(1), name="norm_mod",
    )(x, nw.reshape(1, d), sc, sh)


def _rmsnorm_kernel(x_ref, w_ref, o_ref):
    x = x_ref[...]
    o_ref[...] = x * lax.rsqrt(jnp.mean(x * x, axis=-1, keepdims=True) + NORM_EPS) * w_ref[...]


def rmsnorm_rows(lay, x, w, row0, rows):
    d = x.shape[1]
    tm = lay.tile(512)
    b0 = row0 // tm
    return pl.pallas_call(
        _rmsnorm_kernel,
        grid=(rows // tm,),
        in_specs=[pl.BlockSpec((tm, d), lambda i: (b0 + i, 0)), pl.BlockSpec((1, d), lambda i: (0, 0))],
        out_specs=pl.BlockSpec((tm, d), lambda i: (i, 0)),
        out_shape=jax.ShapeDtypeStruct((rows, d), x.dtype),
        compiler_params=_cparams(1), name="final_norm",
    )(x, w.reshape(1, d))


def _mm_kernel(*refs, n_w, n_e, epi):
    a = refs[0][...]
    accs = [jnp.dot(a, refs[1 + i][...], preferred_element_type=F32) for i in range(n_w)]
    extras = [refs[1 + n_w + i][...] for i in range(n_e)]
    outs = epi(accs, extras)
    o_refs = refs[1 + n_w + n_e:]
    for o_ref, val in zip(o_refs, outs):
        o_ref[...] = val.astype(o_ref.dtype)


def matmul(a, ws, epi, out_dtypes, *, tm, tn, extras=(), name):
    m, k = a.shape
    nw = ws[0].shape[1]
    assert m % tm == 0 and nw % tn == 0
    in_specs = [pl.BlockSpec((tm, k), lambda i, j: (i, 0))]
    in_specs += [pl.BlockSpec((k, tn), lambda i, j: (0, j)) for _ in ws]
    in_specs += [spec for _, spec in extras]
    outs = pl.pallas_call(
        functools.partial(_mm_kernel, n_w=len(ws), n_e=len(extras), epi=epi),
        grid=(m // tm, nw // tn),
        in_specs=in_specs,
        out_specs=[pl.BlockSpec((tm, tn), lambda i, j: (i, j)) for _ in out_dtypes],
        out_shape=[jax.ShapeDtypeStruct((m, nw), dt) for dt in out_dtypes],
        compiler_params=_cparams(2), name=name,
    )(a, *ws, *[arr for arr, _ in extras])
    return outs


def _epi_plain(accs, extras):
    return accs


def _epi_gated_residual(accs, extras):
    x, g = extras
    return [x + g * accs[0]]


def matmul_gated_residual(lay, a, w, x, gate, *, tm, tn, name):
    extras = ((x, pl.BlockSpec((tm, tn), lambda i, j: (i, j))),
              (gate, pl.BlockSpec((None, 1, tn), lambda i, j: (lay.cond_of_tile(i, tm), 0, j))))
    return matmul(a, [w], _epi_gated_residual, [F32], tm=tm, tn=tn, extras=extras, name=name)[0]


def _epi_swiglu(accs, extras):
    return [_silu(accs[0]) * accs[1]]


RWKV_PREP_SLAB = LANES_V7X


def _rwkv_prep_kernel(x_ref, xp_ref, xn_ref, nw_ref, sc_ref, sh_ref, mu_ref, *o_refs, lay):
    i = pl.program_id(0)
    _, pos, cnt = lay.seq_info(i)
    def inv_rms(x):
        return lax.rsqrt(jnp.mean(x * x, axis=-1, keepdims=True) + NORM_EPS)

    tb, d = x_ref.shape
    inv = inv_rms(x_ref[...])
    inv_p = inv_rms(xp_ref[SUBLANES_V7X - 1:SUBLANES_V7X, :])
    inv_n = inv_rms(xn_ref[0:1, :])
    keep_p = jnp.where(pos == 0, 0.0, 1.0)
    keep_n = jnp.where(pos == cnt - 1, 0.0, 1.0)
    row = lax.broadcasted_iota(jnp.int32, (tb, RWKV_PREP_SLAB), 0)
    for c0 in range(0, d, RWKV_PREP_SLAB):
        cs = slice(c0, c0 + RWKV_PREP_SLAB)
        nw, sc, sh = nw_ref[:, cs], sc_ref[:, cs], sh_ref[:, cs]
        h = ((x_ref[:, cs] * inv) * nw) * (1.0 + sc) + sh
        hp = (((xp_ref[SUBLANES_V7X - 1:SUBLANES_V7X, cs] * inv_p) * nw) * (1.0 + sc) + sh) * keep_p
        hn = (((xn_ref[0:1, cs] * inv_n) * nw) * (1.0 + sc) + sh) * keep_n
        prev = jnp.where(row == 0, hp, pltpu.roll(h, 1, axis=0))
        nxt = jnp.where(row == tb - 1, hn, pltpu.roll(h, tb - 1, axis=0))
        xx = 0.5 * (prev + nxt) - h
        for idx, o_ref in enumerate(o_refs):
            o_ref[:, cs] = (h + xx * mu_ref[idx:idx + 1, cs]).astype(o_ref.dtype)


def rwkv_prep(lay, x, nw, sc, sh, mu):
    n, d = x.shape
    tb = lay.tb
    r8 = tb // SUBLANES_V7X
    last8 = n // SUBLANES_V7X - 1
    cmap = lambda i: (lay.cond_of_tile(i, tb), 0, 0)
    return pl.pallas_call(
        functools.partial(_rwkv_prep_kernel, lay=lay),
        grid=(n // tb,),
        in_specs=[pl.BlockSpec((tb, d), lambda i: (i, 0)),
                  pl.BlockSpec((SUBLANES_V7X, d), lambda i: (jnp.maximum(i * r8 - 1, 0), 0)),
                  pl.BlockSpec((SUBLANES_V7X, d), lambda i: (jnp.minimum((i + 1) * r8, last8), 0)),
                  pl.BlockSpec((1, d), lambda i: (0, 0)),
                  pl.BlockSpec((None, 1, d), cmap),
                  pl.BlockSpec((None, 1, d), cmap),
                  pl.BlockSpec((6, d), lambda i: (0, 0))],
        out_specs=[pl.BlockSpec((tb, d), lambda i: (i, 0))] * 6,
        out_shape=[jax.ShapeDtypeStruct((n, d), BF16)] * 6,
        compiler_params=_cparams(1), name="rwkv_prep",
    )(x, x, x, nw.reshape(1, d), sc, sh, mu)


RW_LOG_DECAY_SCALE = -math.exp(-0.5)


def _rwkv_lora_kernel(xw_ref, xa_ref, xg_ref, w1_ref, w2_ref, w0_ref, a1_ref, a2_ref, a0_ref,
                      g1_ref, g2_ref, lw_ref, a_ref, g_ref):
    xw, xa, xg = xw_ref[...], xa_ref[...], xg_ref[...]
    for d in range(2):
        t = jnp.tanh(jnp.dot(xw, w1_ref[d], preferred_element_type=F32))
        wl = w0_ref[d] + _bdot(t, w2_ref[d])
        lw_ref[d] = RW_LOG_DECAY_SCALE * jax.nn.sigmoid(wl)
        t = jnp.dot(xa, a1_ref[d], preferred_element_type=F32)
        a_ref[d] = jax.nn.sigmoid(a0_ref[d] + _bdot(t, a2_ref[d]))
    t = jax.nn.sigmoid(jnp.dot(xg, g1_ref[...], preferred_element_type=F32))
    g_ref[...] = _bdot(t, g2_ref[...])


def rwkv_lora(lay, xw, xa, xg, w1, w2, w0, a1, a2, a0, g1, g2):
    n, d = xw.shape
    tm = lay.tile(256)
    full = lambda arr: pl.BlockSpec(arr.shape, lambda i: (0,) * arr.ndim)
    row = pl.BlockSpec((tm, d), lambda i: (i, 0))
    return pl.pallas_call(
        _rwkv_lora_kernel,
        grid=(n // tm,),
        in_specs=[row, row, row] + [full(t) for t in (w1, w2, w0, a1, a2, a0, g1, g2)],
        out_specs=[pl.BlockSpec((2, tm, d), lambda i: (0, i, 0)),
                   pl.BlockSpec((2, tm, d), lambda i: (0, i, 0)),
                   row],
        out_shape=[jax.ShapeDtypeStruct((2, n, d), F32), jax.ShapeDtypeStruct((2, n, d), F32),
                   jax.ShapeDtypeStruct((n, d), F32)],
        compiler_params=_cparams(1), name="rwkv_lora",
    )(xw, xa, xg, w1, w2, w0, a1, a2, a0, g1, g2)


def _wkv_constants():
    c, g, hd = RW_CHUNK, RW_GROUP, RW_HEAD
    gc, lanes = g * c, g * hd
    t = np.arange(c)
    cum = np.stack([(t[None, :] <= t[:, None]), (t[None, :] >= t[:, None])])
    tr = np.arange(c)[:, None]
    tc = np.arange(gc)[None, :] % c
    strict = np.stack([tc < tr, tc > tr])
    incl = np.stack([tc <= tr, tc >= tr])
    head_rows = np.arange(gc)[:, None] // c == np.arange(lanes)[None, :] // hd
    blk_rows = np.arange(gc)[:, None] // c == np.arange(gc)[None, :] // c
    bd = np.arange(lanes)[:, None] // hd == np.arange(lanes)[None, :] // hd
    eye_w = tr == tc
    return dict(cum=jnp.asarray(cum, BF16), strict=jnp.asarray(strict, F32), incl=jnp.asarray(incl, F32),
                head_rows=jnp.asarray(head_rows, BF16), blk_rows=jnp.asarray(blk_rows, BF16),
                bd=jnp.asarray(bd, F32), bd_b=jnp.asarray(bd, BF16), eye_w=jnp.asarray(eye_w, F32))


def _split_dot(m01, x, passes):
    acc, rem = None, x
    for _ in range(passes):
        part = rem.astype(BF16)
        term = jnp.dot(m01, part, preferred_element_type=F32)
        acc = term if acc is None else acc + term
        rem = rem - part.astype(F32)
    return acc


def _split_dot_r(x, m01, passes):
    acc, rem = None, x
    for _ in range(passes):
        part = rem.astype(BF16)
        term = jnp.dot(part, m01, preferred_element_type=F32)
        acc = term if acc is None else acc + term
        rem = rem - part.astype(F32)
    return acc


def _tile_rows(x, mask_b):
    return jnp.concatenate([x.astype(BF16)] * RW_GROUP, axis=0) * mask_b


def _wkv_prep2(raw, cums, head_rows):
    c = RW_CHUNK
    mids = []
    for (r, v, kk, kd, a, lw, _, _, _), cum in zip(raw, cums):
        tot = jnp.sum(lw, axis=0, keepdims=True)
        kka = kk * a
        e_inv, e_rest = jnp.exp(-cum), jnp.exp(tot - cum)
        q2 = jnp.concatenate([kk * jnp.exp(cum - lw), r * jnp.exp(cum)], axis=0).astype(BF16)
        mids.append(dict(q2=q2, kdh=_tile_rows(kd * e_inv, head_rows), kkah=_tile_rows(kka * e_inv, head_rows),
                         vbd=_tile_rows(v, head_rows), v=v, decay=jnp.exp(tot),
                         kw=jnp.concatenate([kd * e_rest, -(kka * e_rest)], axis=0).astype(BF16)))
    s1s = [_bdot_nt(m['q2'], m['kdh']) for m in mids]
    s2s = [_bdot_nt(m['q2'][c:], m['kkah']) for m in mids]
    for m, u, s1, s2 in zip(mids, raw, s1s, s2s):
        strict_w, incl_w = u[7], u[8]
        m['lad'] = jnp.concatenate([jnp.where(strict_w > 0, s1[:c], 0.0),
                                    jnp.where(incl_w > 0, s1[c:], 0.0)], axis=0).astype(BF16)
        m['a_a'] = jnp.where(incl_w > 0, s2, 0.0).astype(BF16)
    return mids


def _wkv_prep3(mids):
    for m, lav in zip(mids, [_bdot(m['lad'], m['vbd']) for m in mids]):
        m['lav'] = lav
    return mids


def _wkv_adv1(states, preps):
    return [_bdot_nt(p['q2'], s) for p, s in zip(preps, states)]


def _wkv_adv2(p0s, preps, t_ws, head_rows):
    c = RW_CHUNK
    return [_bdot(t_w, _tile_rows(p0[:c] + p['lav'][:c], head_rows))
            for p0, p, t_w in zip(p0s, preps, t_ws)]


def _wkv_adv3(states, p0s, us, preps, head_rows, bd):
    c = RW_CHUNK
    upds = [_bdot(jnp.concatenate([p['v'].astype(F32), u], axis=0).T, p['kw']) for p, u in zip(preps, us)]
    aus = [_bdot(p['a_a'], _tile_rows(u, head_rows)) for p, u in zip(preps, us)]
    new_states = [s * p['decay'] + jnp.where(bd > 0, upd, 0.0) for s, p, upd in zip(states, preps, upds)]
    ys = [p0[c:] + p['lav'][c:] - au for p0, p, au in zip(p0s, preps, aus)]
    return new_states, ys


RKV_TN = RW_LANES


def _rwkv_rkv_kernel(xr_ref, xk_ref, xv_ref, wr_ref, wk_ref, wv_ref, a_ref, kkw_ref, kaw_ref, rk_ref, bd_ref,
                     r_ref, v_ref, kk_ref, kd_ref, b_ref):
    r = jnp.dot(xr_ref[...], wr_ref[...], preferred_element_type=F32)
    k = jnp.dot(xk_ref[...], wk_ref[...], preferred_element_type=F32)
    v = jnp.dot(xv_ref[...], wv_ref[...], preferred_element_type=F32)
    r_ref[...] = r
    v_ref[...] = v.astype(v_ref.dtype)
    bd = bd_ref[...]
    kk = k * kkw_ref[...]
    mix = None
    for d in range(2):
        kd = k * (1.0 + (a_ref[d] - 1.0) * kaw_ref[...])
        kd_ref[d] = kd
        term = kd * rk_ref[d]
        mix = term if mix is None else mix + term
    rm = r * mix
    for h in range(RKV_TN // RW_LANES):
        cs = slice(h * RW_LANES, (h + 1) * RW_LANES)
        kkh = kk[:, cs]
        kk_ref[:, cs] = kkh * lax.rsqrt(_split_dot_r(kkh * kkh, bd, 2) + 1e-12)
        b_ref[:, cs] = _split_dot_r(rm[:, cs], bd, 2) * v[:, cs]


def rwkv_rkv(lay, xr, xk, xv, wr, wk, wv, a, kkw, kaw, rk):
    n, kdim = xr.shape
    dm = wr.shape[1]
    tm, tn = lay.tile(1024), RKV_TN
    bd_b = _wkv_constants()['bd_b']
    lhs = pl.BlockSpec((tm, kdim), lambda i, j: (i, 0))
    rhs = pl.BlockSpec((kdim, tn), lambda i, j: (0, j))
    tok = pl.BlockSpec((tm, tn), lambda i, j: (i, j))
    two = pl.BlockSpec((2, tm, tn), lambda i, j: (0, i, j))
    par = pl.BlockSpec((1, tn), lambda i, j: (0, j))
    return pl.pallas_call(
        _rwkv_rkv_kernel,
        grid=(n // tm, dm // tn),
        in_specs=[lhs, lhs, lhs, rhs, rhs, rhs, two, par, par,
                  pl.BlockSpec((2, 1, tn), lambda i, j: (0, 0, j)),
                  pl.BlockSpec(bd_b.shape, lambda i, j: (0, 0))],
        out_specs=[tok, tok, tok, two, tok],
        out_shape=[jax.ShapeDtypeStruct((n, dm), F32), jax.ShapeDtypeStruct((n, dm), BF16),
                   jax.ShapeDtypeStruct((n, dm), F32), jax.ShapeDtypeStruct((2, n, dm), F32),
                   jax.ShapeDtypeStruct((n, dm), F32)],
        compiler_params=_cparams(2), name="rwkv_rkv",
    )(xr, xk, xv, wr, wk, wv, a, kkw, kaw, rk, bd_b)


def _tinv_begin(items, head_rows, blk_rows, eye_w):
    cums = [_split_dot(it[3], it[2], 3) for it in items]
    ops = [(kk * jnp.exp(cum - lw), _tile_rows(kk * a * jnp.exp(-cum), head_rows))
           for (kk, a, lw, _, _), cum in zip(items, cums)]
    n_ws = [-jnp.where(it[4] > 0, _bdot_nt(q, w), 0.0) for it, (q, w) in zip(items, ops)]
    n_pows = [_bdot(n_w, _tile_rows(n_w, blk_rows)) for n_w in n_ws]
    return [eye_w + n_w for n_w in n_ws], n_pows, cums


def _tinv_level(t_ws, n_pows, last, blk_rows):
    c = RW_CHUNK
    ws = [_tile_rows(n_pow, blk_rows) for n_pow in n_pows]
    if last:
        return [t_w + _bdot(t_w, w) for t_w, w in zip(t_ws, ws)], None
    boths = [_bdot(jnp.concatenate([t_w, n_pow], axis=0), w) for t_w, n_pow, w in zip(t_ws, n_pows, ws)]
    return [t_w + both[:c] for t_w, both in zip(t_ws, boths)], [both[c:] for both in boths]


RW_TINV_LEVELS = int(math.log2(RW_CHUNK))


def _wkv_fused_kernel(*refs, lay):
    (rf, vf, kkf, kdf, af, lwf, s0f, kkfn, afn, lwfn,
     rb, vb, kkb, kdb, ab, lwb, s0b, kkbn, abn, lwbn,
     cum_ref, strict_ref, incl_ref, hr_ref, br_ref, eye_ref, bd_ref,
     yf_ref, yb_ref, sff_ref, sfb_ref, sf_scr, sb_scr, t_scr, c_scr) = refs
    c = RW_CHUNK
    n_chunks = lay.tb // c
    j = pl.program_id(1)
    cur = lax.rem(j, 2)
    nxt = 1 - cur
    _, pos_f, cnt_f = lay.seq_info(j)
    _, pos_b, cnt_b = lay.seq_info(lay.nb - 1 - j)

    heads = [(gi, h, slice(h * RW_HEAD, (h + 1) * RW_HEAD)) for gi in range(RW_GSTEP) for h in range(RW_GROUP)]

    def load_states(scr, s0):
        scr[...] = jnp.zeros(scr.shape, scr.dtype)
        for gi, h, hs in heads:
            scr[gi, hs, hs] = s0[gi, h]

    def store_states(out, scr):
        for gi, h, hs in heads:
            out[gi, h] = scr[gi, hs, hs]

    @pl.when(pos_f == 0)
    def _():
        load_states(sf_scr, s0f)

    @pl.when(pos_b == cnt_b - 1)
    def _():
        load_states(sb_scr, s0b)

    hr, br, eye_w, bd = hr_ref[...], br_ref[...], eye_ref[...], bd_ref[...]
    sl_f = [slice(ci * c, (ci + 1) * c) for ci in range(n_chunks)]
    sl_b = sl_f[::-1]
    cols = [slice(gi * RW_LANES, (gi + 1) * RW_LANES) for gi in range(RW_GSTEP)]

    def inv_items(kk_f, a_f, lw_f, kk_b, a_b, lw_b, sl):
        out = []
        for cs in cols:
            out.append((kk_f[sl, cs], a_f[sl, cs], lw_f[sl, cs], cum_ref[0], strict_ref[0]))
            out.append((kk_b[sl, cs], a_b[sl, cs], lw_b[sl, cs], cum_ref[1], strict_ref[1]))
        return out

    def inv_store(scr, slot, sl, vals):
        for gi, cs in enumerate(cols):
            scr[slot, 0, sl, cs] = vals[2 * gi].astype(scr.dtype)
            scr[slot, 1, sl, cs] = vals[2 * gi + 1].astype(scr.dtype)

    @pl.when(j == 0)
    def _():
        def body(ci, carry):
            sl = pl.ds(pl.multiple_of(ci * c, c), c)
            t_ws, n_pows, cums = _tinv_begin(inv_items(kkf, af, lwf, kkb, ab, lwb, sl), hr, br, eye_w)
            inv_store(c_scr, cur, sl, cums)
            for lv in range(1, RW_TINV_LEVELS):
                t_ws, n_pows = _tinv_level(t_ws, n_pows, lv == RW_TINV_LEVELS - 1, br)
            inv_store(t_scr, cur, sl, t_ws)
            return carry
        lax.fori_loop(0, n_chunks, body, 0)

    def raw(ci):
        out = []
        for cs in cols:
            sf, sb = sl_f[ci], sl_b[ci]
            out.append((rf[sf, cs], vf[sf, cs], kkf[sf, cs], kdf[sf, cs], af[sf, cs], lwf[sf, cs],
                        cum_ref[0], strict_ref[0], incl_ref[0]))
            out.append((rb[sb, cs], vb[sb, cs], kkb[sb, cs], kdb[sb, cs], ab[sb, cs], lwb[sb, cs],
                        cum_ref[1], strict_ref[1], incl_ref[1]))
        return out

    states = []
    for gi in range(RW_GSTEP):
        states += [sf_scr[gi], sb_scr[gi]]
    def cached_cums(ci):
        out = []
        for cs in cols:
            out += [c_scr[cur, 0, sl_f[ci], cs], c_scr[cur, 1, sl_b[ci], cs]]
        return out

    preps = _wkv_prep3(_wkv_prep2(raw(0), cached_cums(0), hr))
    inv = {'lv': 0, 't': None, 'n': None}
    per = 2 * RW_GSTEP

    def inv_step():
        lv = inv['lv']
        if lv == 0:
            items = []
            for sl in sl_f:
                items += inv_items(kkfn, afn, lwfn, kkbn, abn, lwbn, sl)
            inv['t'], inv['n'], cums = _tinv_begin(items, hr, br, eye_w)
            for ci2, sl in enumerate(sl_f):
                inv_store(c_scr, nxt, sl, cums[ci2 * per:(ci2 + 1) * per])
        elif lv < RW_TINV_LEVELS:
            inv['t'], inv['n'] = _tinv_level(inv['t'], inv['n'], lv == RW_TINV_LEVELS - 1, br)
            if lv == RW_TINV_LEVELS - 1:
                for ci2, sl in enumerate(sl_f):
                    inv_store(t_scr, nxt, sl, inv['t'][ci2 * per:(ci2 + 1) * per])
        inv['lv'] = lv + 1

    for ci in range(n_chunks):
        more = ci + 1 < n_chunks
        t_ws = []
        for cs in cols:
            t_ws += [t_scr[cur, 0, sl_f[ci], cs], t_scr[cur, 1, sl_b[ci], cs]]
        p0s = _wkv_adv1(states, preps)
        if more:
            raw_n = raw(ci + 1)
            cums_n = cached_cums(ci + 1)
        inv_step()
        us = _wkv_adv2(p0s, preps, t_ws, hr)
        if more:
            mids_n = _wkv_prep2(raw_n, cums_n, hr)
        inv_step()
        states, ys = _wkv_adv3(states, p0s, us, preps, hr, bd)
        if more:
            preps = _wkv_prep3(mids_n)
        for gi, cs in enumerate(cols):
            yf_ref[sl_f[ci], cs] = ys[2 * gi]
            yb_ref[sl_b[ci], cs] = ys[2 * gi + 1]
    while inv['lv'] < RW_TINV_LEVELS:
        inv_step()
    for gi in range(RW_GSTEP):
        sf_scr[gi] = states[2 * gi]
        sb_scr[gi] = states[2 * gi + 1]

    @pl.when(pos_f == cnt_f - 1)
    def _():
        store_states(sff_ref, sf_scr)

    @pl.when(pos_b == 0)
    def _():
        store_states(sfb_ref, sb_scr)


def wkv_fused(lay, r, v, kk, kd, a, lw, s0):
    n, dm = r.shape
    tb, lanes = lay.tb, RW_LANES
    ng = dm // lanes
    width = RW_GSTEP * lanes
    k = _wkv_constants()
    full = lambda arr: pl.BlockSpec(arr.shape, lambda g, j: (0,) * arr.ndim)

    def views(d, blk, blk_next):
        tok = pl.BlockSpec((tb, width), lambda g, j: (blk(j), g))
        tok2 = pl.BlockSpec((None, tb, width), lambda g, j: (d, blk(j), g))
        tok_n = pl.BlockSpec((tb, width), lambda g, j: (blk_next(j), g))
        tok2_n = pl.BlockSpec((None, tb, width), lambda g, j: (d, blk_next(j), g))

        def s0_map(g, j):
            seq, _, _ = lay.seq_info(blk(j))
            return (jnp.maximum(seq - lay.n_ctx + 1, 0), d, g, 0, 0, 0)

        def sfin_map(g, j):
            seq, _, _ = lay.seq_info(blk(j))
            return (seq, g, 0, 0, 0)

        per_head = (RW_GSTEP, RW_GROUP, RW_HEAD, RW_HEAD)
        ins = [tok, tok, tok, tok2, tok2, tok2, pl.BlockSpec((None, None) + per_head, s0_map),
               tok_n, tok2_n, tok2_n]
        return ins, tok, pl.BlockSpec((None,) + per_head, sfin_map)

    in_f, y_f, sf_f = views(0, lambda j: j, lambda j: jnp.minimum(j + 1, lay.nb - 1))
    in_b, y_b, sf_b = views(1, lambda j: lay.nb - 1 - j, lambda j: jnp.maximum(lay.nb - 2 - j, 0))
    consts = [k['cum'], k['strict'], k['incl'], k['head_rows'], k['blk_rows'], k['eye_w'], k['bd']]
    args = [r, v, kk, kd, a, lw, s0, kk, a, lw]
    return pl.pallas_call(
        functools.partial(_wkv_fused_kernel, lay=lay),
        grid=(ng // RW_GSTEP, lay.nb),
        in_specs=in_f + in_b + [full(x) for x in consts],
        out_specs=[y_f, y_b, sf_f, sf_b],
        out_shape=[jax.ShapeDtypeStruct((n, dm), F32), jax.ShapeDtypeStruct((n, dm), F32),
                   jax.ShapeDtypeStruct((lay.n_seq, ng, RW_GROUP, RW_HEAD, RW_HEAD), F32),
                   jax.ShapeDtypeStruct((lay.n_seq, ng, RW_GROUP, RW_HEAD, RW_HEAD), F32)],
        scratch_shapes=[pltpu.VMEM((RW_GSTEP, lanes, lanes), F32), pltpu.VMEM((RW_GSTEP, lanes, lanes), F32),
                        pltpu.VMEM((2, 2, tb, width), BF16), pltpu.VMEM((2, 2, tb, width), F32)],
        compiler_params=_cparams(2), name="wkv_fused",
    )(*args, *args, *consts)


def _rwkv_post_kernel(yf_ref, yb_ref, b_ref, g_ref, lnw_ref, lnb_ref, bd_ref, o_ref):
    y = yf_ref[...] + yb_ref[...]
    bd = bd_ref[...]
    inv = 1.0 / RW_HEAD
    mu = _split_dot_r(y, bd, 2) * inv
    yc = y - mu
    var = _split_dot_r(yc * yc, bd, 2) * inv
    yn = yc * lax.rsqrt(var + RW_LN_EPS)
    out = yn * lnw_ref[...] + lnb_ref[...] + b_ref[...]
    o_ref[...] = (out * g_ref[...]).astype(o_ref.dtype)


def rwkv_post(lay, y_f, y_b, bonus, g, lnw, lnb):
    n, dm = y_f.shape
    lanes = RW_LANES
    tm = lay.tile(1024)
    bd_b = _wkv_constants()['bd_b']
    tok = pl.BlockSpec((tm, lanes), lambda i, c: (i, c))
    par = pl.BlockSpec((1, lanes), lambda i, c: (0, c))
    return pl.pallas_call(
        _rwkv_post_kernel,
        grid=(n // tm, dm // lanes),
        in_specs=[tok, tok, tok, tok, par, par, pl.BlockSpec(bd_b.shape, lambda i, c: (0, 0))],
        out_specs=tok,
        out_shape=jax.ShapeDtypeStruct((n, dm), BF16),
        compiler_params=_cparams(2), name="rwkv_post",
    )(y_f, y_b, bonus, g, lnw, lnb, bd_b)


def rwkv_layer(lay, x, mods, nw, p, s0):
    n, d = x.shape
    sh1, sc1, g1 = mods[0], mods[1], mods[2]
    xr, xw, xk, xv, xa, xg = rwkv_prep(lay, x, nw, sc1, sh1, p['mu'])
    tm = lay.tile(1024)
    lw, a, g = rwkv_lora(lay, xw, xa, xg, p['w1'], p['w2'], p['w0'], p['a1'], p['a2'], p['a0'],
                         p['g1'], p['g2'])
    r, v, kk, kd, bonus = rwkv_rkv(lay, xr, xk, xv, p['wr'], p['wk'], p['wv'], a, p['kk'], p['ka'], p['rk'])
    y_f, y_b, sfin_f, sfin_b = wkv_fused(lay, r, v, kk, kd, a, lw, s0)
    z = rwkv_post(lay, y_f, y_b, bonus, g, p['lnx_w'], p['lnx_b'])
    x = matmul_gated_residual(lay, z, p['wo'], x, g1, tm=tm, tn=1024, name="rwkv_o")
    return x, jnp.stack([sfin_f, sfin_b], axis=1)


def _hgrn_constants():
    c = HG_CHUNK_TOKENS
    t = np.arange(c)[:, None]
    j = np.arange(c)[None, :]
    cums, masks_all = [], []
    for rev in (False, True):
        masks = []
        h = 1
        while h < c:
            upper = (t % (2 * h)) >= h
            same = (t // (2 * h)) == (j // (2 * h))
            if not rev:
                mask = same & upper & ((j % (2 * h)) < h)
            else:
                mask = same & (~upper) & ((j % (2 * h)) >= h)
            masks.append(mask)
            h *= 2
        masks.append(t == j)
        cums.append((j >= t) if rev else (j <= t))
        masks_all.append(np.stack(masks, 0))
    return jnp.asarray(np.stack(cums), BF16), jnp.asarray(np.stack(masks_all).astype(np.float32))


def _hgrn_level_exponents(g, gcum, rev):
    c, kdim = g.shape
    row = lax.broadcasted_iota(jnp.int32, g.shape, 0)
    nxt = pltpu.roll(g, c - 1, axis=0)
    prv = pltpu.roll(g, 1, axis=0)
    r2, r4 = row & 1, row & 3
    if not rev:
        x1 = jnp.where(r2 == 1, g, 0.0)
        x2 = jnp.where(r4 == 0, nxt, jnp.where(r4 == 2, g, jnp.where(r4 == 3, prv + g, 0.0)))
    else:
        x1 = jnp.where(r2 == 0, g, 0.0)
        x2 = jnp.where(r4 == 0, g + nxt, jnp.where(r4 == 1, g, jnp.where(r4 == 3, prv, 0.0)))
    xs = [x1, x2]
    h = 4
    while h < c:
        gr = gcum.reshape(c // (2 * h), 2 * h, kdim)
        ref = gr[:, h:h + 1, :] if rev else gr[:, h - 1:h, :]
        upper = lax.broadcasted_iota(jnp.int32, gr.shape, 1) >= h
        diff = gr - ref
        x = jnp.where(upper, -diff, diff) if rev else jnp.where(upper, diff, -diff)
        xs.append(x.reshape(c, kdim))
        h *= 2
    return xs


def _hgrn_units(units, masks_by_dir, cum_by_dir):
    c = HG_CHUNK_TOKENS
    gs = [jnp.log(f) for _, f, _, _ in units]
    gcums = [_split_dot(cum_by_dir[rev], g, 3) for g, (_, _, _, rev) in zip(gs, units)]
    outs = []
    pend = []
    for (q, f, iv, rev), g, gcum in zip(units, gs, gcums):
        k = 1.0 - f
        tot = gcum[0:1] if rev else gcum[c - 1:c]
        es = [jnp.exp(x) for x in _hgrn_level_exponents(g, gcum, rev)]
        pend.append((q, k, iv, rev, es, jnp.exp(gcum), jnp.exp(tot - gcum), jnp.exp(tot)))
    for q, k, iv, rev, es, eg, erest, etot in pend:
        masks = masks_by_dir[rev]
        a = masks[len(es)] * _bdot_nt(q, k)
        for lv, el in enumerate(es):
            a = a + masks[lv] * _bdot_nt(q * el, k * el)
        outs.append(dict(a=a, iv=iv, qe=(q * eg).astype(BF16), kdec=(k * erest).astype(BF16), decay=etot))
    for u in outs:
        u['av'] = _bdot(u['a'], u['iv'])
    for u in outs:
        u['upd'] = _bdot(u['iv'].astype(F32).T, u['kdec'])
    return outs


def _hgrn_kernel(*refs, lay):
    (qf, ff, if_, s0f, qb, fb, ib, s0b, cum_ref, mask_ref, of_ref, ob_ref, sff_ref, sfb_ref,
     sf_scr, sb_scr) = refs
    c = HG_CHUNK_TOKENS
    n_chunks = lay.tb // c
    j = pl.program_id(1)
    _, pos_f, cnt_f = lay.seq_info(j)
    _, pos_b, cnt_b = lay.seq_info(lay.nb - 1 - j)

    @pl.when(pos_f == 0)
    def _():
        sf_scr[...] = s0f[...]

    @pl.when(pos_b == cnt_b - 1)
    def _():
        sb_scr[...] = s0b[...]

    masks_by_dir = [mask_ref[0], mask_ref[1]]
    cum_by_dir = [cum_ref[0], cum_ref[1]]
    sl_f = [slice(ci * c, (ci + 1) * c) for ci in range(n_chunks)]
    sl_b = sl_f[::-1]
    cols = [slice(hi * HG_K, (hi + 1) * HG_K) for hi in range(HG_HSTEP)]
    chains = []
    for hi, cs in enumerate(cols):
        chains.append((sf_scr, hi, of_ref, [(sl, cs) for sl in sl_f],
                       [(qf[sl, cs], ff[sl, cs], if_[sl, cs], False) for sl in sl_f]))
        chains.append((sb_scr, hi, ob_ref, [(sl, cs) for sl in sl_b],
                       [(qb[sl, cs], fb[sl, cs], ib[sl, cs], True) for sl in sl_b]))
    done = _hgrn_units([u for ch in chains for u in ch[4]], masks_by_dir, cum_by_dir)
    pend = []
    for n_ch, (scr, hi, o_ref, where, _) in enumerate(chains):
        s = scr[hi]
        for ci in range(n_chunks):
            u = done[n_ch * n_chunks + ci]
            pend.append((o_ref, where[ci], u, s))
            s = s * u['decay'] + u['upd']
        scr[hi] = s
    for o_ref, (sl, cs), u, s_prev in pend:
        o_ref[sl, cs] = u['av'] + _bdot_nt(u['qe'], s_prev)

    @pl.when(pos_f == cnt_f - 1)
    def _():
        sff_ref[...] = sf_scr[...]

    @pl.when(pos_b == 0)
    def _():
        sfb_ref[...] = sb_scr[...]


def hgrn_scan(lay, q, f_fwd, f_bwd, iv, s0t):
    n, dm = q.shape
    tb = lay.tb
    nh = dm // HG_K
    width = HG_HSTEP * HG_K
    cum_m, masks = _hgrn_constants()
    full = lambda arr: pl.BlockSpec(arr.shape, lambda h, j: (0,) * arr.ndim)

    def views(d, blk):
        tok = pl.BlockSpec((tb, width), lambda h, j: (blk(j), h))

        def s0_map(h, j):
            seq, _, _ = lay.seq_info(blk(j))
            return (jnp.maximum(seq - lay.n_ctx + 1, 0), d, h, 0, 0)

        def sfin_map(h, j):
            seq, _, _ = lay.seq_info(blk(j))
            return (seq, h, 0, 0)

        ins = [tok, tok, tok, pl.BlockSpec((None, None, HG_HSTEP, HG_K, HG_K), s0_map)]
        return ins, tok, pl.BlockSpec((None, HG_HSTEP, HG_K, HG_K), sfin_map)

    in_f, o_f, sf_f = views(0, lambda j: j)
    in_b, o_b, sf_b = views(1, lambda j: lay.nb - 1 - j)
    return pl.pallas_call(
        functools.partial(_hgrn_kernel, lay=lay),
        grid=(nh // HG_HSTEP, lay.nb),
        in_specs=in_f + in_b + [full(cum_m), full(masks)],
        out_specs=[o_f, o_b, sf_f, sf_b],
        out_shape=[jax.ShapeDtypeStruct((n, dm), F32), jax.ShapeDtypeStruct((n, dm), F32),
                   jax.ShapeDtypeStruct((lay.n_seq, nh, HG_K, HG_K), F32),
                   jax.ShapeDtypeStruct((lay.n_seq, nh, HG_K, HG_K), F32)],
        scratch_shapes=[pltpu.VMEM((HG_HSTEP, HG_K, HG_K), F32), pltpu.VMEM((HG_HSTEP, HG_K, HG_K), F32)],
        compiler_params=_cparams(2), name="hgrn_scan",
    )(q, f_fwd, iv, s0t, q, f_bwd, iv, s0t, cum_m, masks)


HG_POST_HEADS = 4


def _hgrn_post_kernel(of_ref, ob_ref, g_ref, nw_ref, z_ref):
    for h in range(HG_POST_HEADS):
        cs = slice(h * HG_K, (h + 1) * HG_K)
        o = of_ref[:, cs] + ob_ref[:, cs]
        o = o * lax.rsqrt(jnp.mean(o * o, axis=-1, keepdims=True) + NORM_EPS) * nw_ref[...] * g_ref[:, cs]
        z_ref[:, cs] = o.astype(z_ref.dtype)


def hgrn_post(lay, o_f, o_b, gs, nw):
    n, dm = o_f.shape
    tm = lay.tile(1024)
    width = HG_POST_HEADS * HG_K
    tok = pl.BlockSpec((tm, width), lambda i, h: (i, h))
    return pl.pallas_call(
        _hgrn_post_kernel,
        grid=(n // tm, dm // width),
        in_specs=[tok, tok, tok, pl.BlockSpec((1, HG_K), lambda i, h: (0, 0))],
        out_specs=tok,
        out_shape=jax.ShapeDtypeStruct((n, dm), BF16),
        compiler_params=_cparams(2), name="hgrn_post",
    )(o_f, o_b, gs, nw)


def _epi_hgrn_in(accs, extras):
    lb = extras[0]
    q = _silu(accs[0])
    f0 = lb + (1.0 - lb) * jax.nn.sigmoid(accs[1])
    f1 = lb + (1.0 - lb) * jax.nn.sigmoid(accs[2])
    return [q, f0, f1, accs[3], _silu(accs[4])]


def hgrn_layer(lay, x, mods, nw, p, lb, s0t):
    n, d = x.shape
    sh1, sc1, g1 = mods[0], mods[1], mods[2]
    h = norm_mod(lay, x, nw, sc1, sh1, BF16)
    tm, tn = lay.tile(1024), 256
    extras = ((lb, pl.BlockSpec((1, tn), lambda i, j: (0, j))),)
    q, f0, f1, iv, gs = matmul(h, p['w_in'], _epi_hgrn_in, [F32, F32, F32, BF16, F32], tm=tm, tn=tn, extras=extras,
                               name="hgrn_in")
    o_f, o_b, sfin_f, sfin_b = hgrn_scan(lay, q, f0, f1, iv, s0t)
    z = hgrn_post(lay, o_f, o_b, gs, p['norm_w'])
    x = matmul_gated_residual(lay, z, p['wo'], x, g1, tm=lay.tile(1024), tn=1024, name="hgrn_o")
    return x, jnp.stack([sfin_f, sfin_b], axis=1)


ML_DOWN_COLS = 1280
ML_KR_OFF = ML_Q_LORA + ML_KV_LORA
ML_KRS_OFF = ML_KR_OFF + LANES_V7X


def _rms(x, w):
    return x * lax.rsqrt(jnp.mean(x * x, axis=-1, keepdims=True) + NORM_EPS) * w


def _mla_mid_kernel(dn_ref, qw_ref, kvw_ref, cos_ref, sin_ref, qn_ref, ckv_ref, kr_ref):
    dn = dn_ref[...]
    qn_ref[...] = _rms(dn[:, :ML_Q_LORA], qw_ref[...]).astype(qn_ref.dtype)
    ckv_ref[...] = _rms(dn[:, ML_Q_LORA:ML_KR_OFF], kvw_ref[...])
    kr = dn[:, ML_KR_OFF:ML_KR_OFF + ML_ROPE]
    krs = dn[:, ML_KRS_OFF:ML_KRS_OFF + ML_ROPE]
    kr_ref[...] = kr * cos_ref[...] + krs * sin_ref[...]


def mla_mid(lay, dn, qw, kvw, cos, sin):
    n = dn.shape[0]
    tm = lay.tile(512)
    return pl.pallas_call(
        _mla_mid_kernel,
        grid=(n // tm,),
        in_specs=[pl.BlockSpec((tm, ML_DOWN_COLS), lambda i: (i, 0)),
                  pl.BlockSpec((1, ML_Q_LORA), lambda i: (0, 0)),
                  pl.BlockSpec((1, ML_KV_LORA), lambda i: (0, 0)),
                  pl.BlockSpec((tm, ML_ROPE), lambda i: (i, 0)),
                  pl.BlockSpec((tm, ML_ROPE), lambda i: (i, 0))],
        out_specs=[pl.BlockSpec((tm, ML_Q_LORA), lambda i: (i, 0)),
                   pl.BlockSpec((tm, ML_KV_LORA), lambda i: (i, 0)),
                   pl.BlockSpec((tm, ML_ROPE), lambda i: (i, 0))],
        out_shape=[jax.ShapeDtypeStruct((n, ML_Q_LORA), BF16),
                   jax.ShapeDtypeStruct((n, ML_KV_LORA), F32),
                   jax.ShapeDtypeStruct((n, ML_ROPE), F32)],
        compiler_params=_cparams(1), name="mla_mid",
    )(dn, qw, kvw, cos, sin)


ML_QSCALE = math.log2(math.e) / math.sqrt(ML_NOPE + ML_ROPE)


def _epi_qscale(accs, extras):
    return [accs[0] * ML_QSCALE]


def _epi_rope(accs, extras):
    cos, sin = extras
    reps = accs[0].shape[1] // cos.shape[1]
    cos, sin = jnp.concatenate([cos] * reps, axis=1), jnp.concatenate([sin] * reps, axis=1)
    return [(accs[0] * cos + accs[1] * sin) * ML_QSCALE]


ML_KEY_SPLITS = 2


def _attn_kernel(qn_ref, qr_ref, kn_ref, kr_ref, v_ref, o_ref, kc_scr, vt_scr):
    @pl.when(pl.program_id(2) == 0)
    def _():
        for h in range(2):
            kc_scr[h, :, :ML_NOPE] = kn_ref[:, h * ML_NOPE:(h + 1) * ML_NOPE]
            kc_scr[h, :, ML_NOPE:] = kr_ref[...]
            vt_scr[h] = v_ref[:, h * ML_V:(h + 1) * ML_V].astype(F32).T.astype(vt_scr.dtype)

    k_len = kc_scr.shape[1]
    kh = k_len // ML_KEY_SPLITS
    scores = []
    for h in range(2):
        q = jnp.concatenate([qn_ref[:, h * ML_NOPE:(h + 1) * ML_NOPE],
                             qr_ref[:, h * ML_ROPE:(h + 1) * ML_ROPE]], axis=1)
        scores.append([lax.dot_general(kc_scr[h, kb * kh:(kb + 1) * kh, :], q, (((1,), (1,)), ((), ())),
                                       preferred_element_type=F32) for kb in range(ML_KEY_SPLITS)])
    for h in range(2):
        m_acc = l_acc = o_acc = None
        for kb, st in enumerate(scores[h]):
            m = jnp.max(st, axis=0, keepdims=True)
            p = jnp.exp2(st - m)
            l = jnp.sum(p, axis=0, keepdims=True)
            ot = jnp.dot(vt_scr[h, :, kb * kh:(kb + 1) * kh], p.astype(BF16),
                         preferred_element_type=F32)
            if m_acc is None:
                m_acc, l_acc, o_acc = m, l, ot
            else:
                m_new = jnp.maximum(m_acc, m)
                c_old, c_new = jnp.exp2(m_acc - m_new), jnp.exp2(m - m_new)
                l_acc = l_acc * c_old + l * c_new
                o_acc = o_acc * c_old + ot * c_new
                m_acc = m_new
        o_ref[:, h * ML_V:(h + 1) * ML_V] = (o_acc / l_acc).T.astype(o_ref.dtype)


def attention(qn, qr, kn, kr, v, *, n_seq, q_len, k_len, row0, tq):
    heads2 = qn.shape[1] // (2 * ML_NOPE)
    qb = q_len // tq
    rb0 = row0 // tq
    return pl.pallas_call(
        _attn_kernel,
        grid=(n_seq, heads2, qb),
        in_specs=[pl.BlockSpec((tq, 2 * ML_NOPE), lambda s, h, i: (rb0 + s * qb + i, h)),
                  pl.BlockSpec((tq, 2 * ML_ROPE), lambda s, h, i: (rb0 + s * qb + i, h)),
                  pl.BlockSpec((k_len, 2 * ML_NOPE), lambda s, h, i: (s, h)),
                  pl.BlockSpec((k_len, ML_ROPE), lambda s, h, i: (s, 0)),
                  pl.BlockSpec((k_len, 2 * ML_V), lambda s, h, i: (s, h))],
        out_specs=pl.BlockSpec((tq, 2 * ML_V), lambda s, h, i: (s * qb + i, h)),
        out_shape=jax.ShapeDtypeStruct((n_seq * q_len, heads2 * 2 * ML_V), BF16),
        scratch_shapes=[pltpu.VMEM((2, k_len, ML_NOPE + ML_ROPE), BF16), pltpu.VMEM((2, ML_V, k_len), BF16)],
        compiler_params=_cparams(3), name="mla_attn",
    )(qn, qr, kn, kr, v)


def mla_layer(lay, x, mods, nw, p, cache_ckv, cache_kr, cos, sin, cos2, sin2):
    n, d = x.shape
    sh1, sc1, g1 = mods[0], mods[1], mods[2]
    h = norm_mod(lay, x, nw, sc1, sh1, BF16)
    tm = lay.tile(512)
    dn = matmul(h, [p['w_down']], _epi_plain, [F32], tm=tm, tn=ML_DOWN_COLS, name="mla_down")[0]
    qlat, ckv, kr = mla_mid(lay, dn, p['qnorm_w'], p['kvnorm_w'], cos, sin)
    tm_q = lay.tile(1024)
    qn = matmul(qlat, [p['w_uq_nope']], _epi_qscale, [BF16], tm=tm_q, tn=1024, name="mla_qn")[0]
    tw = cos2.shape[1]
    extras = ((cos2, pl.BlockSpec((tm_q, tw), lambda i, j: (i, 0))),
              (sin2, pl.BlockSpec((tm_q, tw), lambda i, j: (i, 0))))
    qr = matmul(qlat, [p['w_uq_rope'], p['w_uq_rope_sw']], _epi_rope, [BF16], tm=tm_q,
                tn=p['w_uq_rope'].shape[1], extras=extras, name="mla_qr")[0]
    nc, past = lay.nc, cache_ckv.shape[1]
    ckv_b, kr_b = ckv.astype(BF16), kr.astype(BF16)
    kn_c, v_c = matmul(ckv_b[:nc], [p['w_ukn'], p['w_uv']], _epi_plain, [BF16, BF16],
                       tm=lay.tile(512), tn=512, name="mla_kv_ctx")
    o_c = attention(qn, qr, kn_c, kr_b[:nc], v_c, n_seq=lay.n_ctx, q_len=lay.ctx_len,
                    k_len=lay.ctx_len, row0=0, tq=min(256, lay.ctx_len))
    k_len = lay.lat_len + past
    ckv_l = jnp.concatenate([ckv_b[nc:].reshape(lay.n_lat, lay.lat_len, -1), cache_ckv.astype(BF16)],
                            axis=1).reshape(lay.n_lat * k_len, -1)
    kr_l = jnp.concatenate([kr_b[nc:].reshape(lay.n_lat, lay.lat_len, -1), cache_kr.astype(BF16)],
                           axis=1).reshape(lay.n_lat * k_len, -1)
    bf16_rows = 2 * SUBLANES_V7X
    tk = k_len // 2 if (k_len // 2) % bf16_rows == 0 else math.gcd(k_len, 512)
    kn_l, v_l = matmul(ckv_l, [p['w_ukn'], p['w_uv']], _epi_plain, [BF16, BF16], tm=tk, tn=1024,
                       name="mla_kv_lat")
    o_l = attention(qn, qr, kn_l, kr_l, v_l, n_seq=lay.n_lat, q_len=lay.lat_len, k_len=k_len,
                    row0=nc, tq=min(256, lay.lat_len))
    o = jnp.concatenate([o_c, o_l], axis=0)
    x = matmul_gated_residual(lay, o, p['wo'], x, g1, tm=lay.tile(1024), tn=1024, name="mla_o")
    return x, ckv[:nc], kr[:nc]


def _rope_tables(lay):
    t = lay.lat_len
    rows = t // GRID_W
    rr = jnp.broadcast_to(jnp.arange(rows, dtype=F32)[:, None], (rows, GRID_W)).reshape(-1)
    cc = jnp.broadcast_to(jnp.arange(GRID_W, dtype=F32)[None, :], (rows, GRID_W)).reshape(-1)
    nf = ML_ROPE // 4
    inv = ROPE_BASE ** (-jnp.arange(nf, dtype=F32) / nf)
    ar, ac = rr[:, None] * inv, cc[:, None] * inv
    cos = jnp.concatenate([jnp.cos(ar), jnp.cos(ar), jnp.cos(ac), jnp.cos(ac)], axis=-1)
    sin = jnp.concatenate([-jnp.sin(ar), jnp.sin(ar), -jnp.sin(ac), jnp.sin(ac)], axis=-1)
    cos = jnp.concatenate([jnp.ones((lay.nc, ML_ROPE), F32), jnp.tile(cos, (lay.n_lat, 1))], axis=0)
    sin = jnp.concatenate([jnp.zeros((lay.nc, ML_ROPE), F32), jnp.tile(sin, (lay.n_lat, 1))], axis=0)
    return cos, sin


def _swap_cols(w):
    k, c = w.shape
    w4 = w.reshape(k, c // 32, 2, 16)
    return w4[:, :, ::-1, :].reshape(k, c)


def ffn(lay, x, mods, nw, w_a, w_b, w_out):
    sh2, sc2, g2 = mods[3], mods[4], mods[5]
    h = norm_mod(lay, x, nw, sc2, sh2, BF16)
    act = matmul(h, [w_a, w_b], _epi_swiglu, [BF16], tm=lay.tile(1024), tn=512, name="ffn_in")[0]
    return matmul_gated_residual(lay, act, w_out, x, g2, tm=lay.tile(1024), tn=512, name="ffn_out")


def kernel(x_prompt, x_sample, state_rwkv, state_hgrn, cache_ckv, cache_krope, c, c_ctx, ada_w, ada_b, norm1_w, norm2_w, ffn_w_in, ffn_w_out, final_norm_w, rw_mu, rw_wr, rw_wk, rw_wv, rw_wo, rw_w0, rw_w1, rw_w2, rw_a0, rw_a1, rw_a2, rw_g1, rw_g2, rw_kk, rw_ka, rw_rk, rw_lnx_w, rw_lnx_b, hg_w_in, hg_lb, hg_norm_w, hg_wo, ml_w_down, ml_qnorm_w, ml_kvnorm_w, ml_w_uq, ml_w_ukv, ml_wo):
    n_ctx, ctx_len, d = x_prompt.shape
    n_lat, lat_len, _ = x_sample.shape
    depth = ada_w.shape[0]
    lay = Layout(n_ctx, ctx_len, n_lat, lat_len)
    d_ff = ffn_w_out.shape[1]
    x = jnp.concatenate([x_prompt.reshape(lay.nc, d), x_sample.reshape(n_lat * lat_len, d)], axis=0)

    n_cond = -(-(1 + n_lat) // SUBLANES_V7X) * SUBLANES_V7X
    cond = jnp.zeros((n_cond, d), F32).at[0].set(c_ctx).at[1:1 + n_lat].set(c)
    mod_all = adaln(cond, ada_w, ada_b)
    mod_all = mod_all.reshape(depth, n_cond, 6, 1, d).transpose(0, 2, 1, 3, 4)

    lb_table = jnp.cumsum(jax.nn.softmax(hg_lb.astype(F32), axis=0), axis=0)
    lb_table = lb_table - lb_table[0]
    cos, sin = _rope_tables(lay)
    cos2, sin2 = jnp.tile(cos, (1, 2)), jnp.tile(sin, (1, 2))
    bf = lambda t: t.astype(BF16)

    new_rwkv, new_hgrn, new_ckv, new_krope = [], [], [], []
    for l in range(depth):
        kind, j = l % 3, l // 3
        mods = mod_all[l]
        if kind == 0:
            pad1 = lambda w: jnp.pad(w, ((0, 0), (0, 0), (0, LORA_PAD - w.shape[2])))
            pad2 = lambda w: jnp.pad(w, ((0, 0), (0, LORA_PAD - w.shape[1]), (0, 0)))
            p = {'mu': rw_mu[j], 'wr': bf(rw_wr[j]), 'wk': bf(rw_wk[j]), 'wv': bf(rw_wv[j]),
                 'wo': bf(rw_wo[j]),
                 'w0': rw_w0[j].reshape(2, 1, d), 'w1': bf(pad1(rw_w1[j])), 'w2': bf(pad2(rw_w2[j])),
                 'a0': rw_a0[j].reshape(2, 1, d), 'a1': bf(pad1(rw_a1[j])), 'a2': bf(pad2(rw_a2[j])),
                 'g1': bf(rw_g1[j]), 'g2': bf(rw_g2[j]),
                 'kk': rw_kk[j].reshape(1, d), 'ka': rw_ka[j].reshape(1, d),
                 'rk': rw_rk[j].reshape(2, 1, d),
                 'lnx_w': rw_lnx_w[j].reshape(1, d), 'lnx_b': rw_lnx_b[j].reshape(1, d)}
            rw_h = state_rwkv.shape[3]
            s_lat = state_rwkv[:, j].astype(F32).reshape(n_lat, 2, rw_h // RW_GROUP, RW_GROUP, RW_HEAD, RW_HEAD)
            s0 = jnp.concatenate([jnp.zeros((1,) + s_lat.shape[1:], F32), s_lat], axis=0)
            x, sfin = rwkv_layer(lay, x, mods, norm1_w[l], p, s0)
            new_rwkv.append(sfin[:n_ctx].reshape(n_ctx, 2, rw_h, RW_HEAD, RW_HEAD))
        elif kind == 1:
            hk = d
            w_in = hg_w_in[j]
            p = {'w_in': [bf(w_in[:, i * hk:(i + 1) * hk]) for i in range(5)],
                 'norm_w': hg_norm_w[j].reshape(1, HG_K), 'wo': bf(hg_wo[j])}
            s_lat = jnp.swapaxes(state_hgrn[:, j].astype(F32), -1, -2)
            s0t = jnp.concatenate([jnp.zeros((1,) + s_lat.shape[1:], F32), s_lat], axis=0)
            x, sfin = hgrn_layer(lay, x, mods, norm1_w[l], p, lb_table[l].reshape(1, d), s0t)
            new_hgrn.append(jnp.swapaxes(sfin[:n_ctx], -1, -2))
        else:
            wd = ml_w_down[j]
            kr_w = wd[:, ML_KR_OFF:]
            zpad = jnp.zeros((d, LANES_V7X - ML_ROPE), wd.dtype)
            w_down = jnp.concatenate([wd, zpad, _swap_cols(kr_w), zpad], axis=1)
            wq = ml_w_uq[j].reshape(ML_Q_LORA, ML_H, ML_NOPE + ML_ROPE)
            wq_n = wq[:, :, :ML_NOPE].reshape(ML_Q_LORA, ML_H * ML_NOPE)
            wq_r = wq[:, :, ML_NOPE:].reshape(ML_Q_LORA, ML_H * ML_ROPE)
            wkv = ml_w_ukv[j].reshape(ML_KV_LORA, ML_H, ML_NOPE + ML_V)
            p = {'w_down': bf(w_down), 'qnorm_w': ml_qnorm_w[j].reshape(1, -1),
                 'kvnorm_w': ml_kvnorm_w[j].reshape(1, -1),
                 'w_uq_nope': bf(wq_n), 'w_uq_rope': bf(wq_r), 'w_uq_rope_sw': bf(_swap_cols(wq_r)),
                 'w_ukn': bf(wkv[:, :, :ML_NOPE].reshape(ML_KV_LORA, ML_H * ML_NOPE)),
                 'w_uv': bf(wkv[:, :, ML_NOPE:].reshape(ML_KV_LORA, ML_H * ML_V)),
                 'wo': bf(ml_wo[j])}
            x, ckv_c, kr_c = mla_layer(lay, x, mods, norm1_w[l], p, cache_ckv[:, j], cache_krope[:, j],
                                       cos, sin, cos2, sin2)
            new_ckv.append(ckv_c.reshape(n_ctx, ctx_len, ML_KV_LORA))
            new_krope.append(kr_c.reshape(n_ctx, ctx_len, ML_ROPE))
        w_in = ffn_w_in[l]
        x = ffn(lay, x, mods, norm2_w[l], bf(w_in[:, :d_ff]), bf(w_in[:, d_ff:]), bf(ffn_w_out[l]))

    y_prompt = rmsnorm_rows(lay, x, final_norm_w, 0, lay.nc).reshape(n_ctx, ctx_len, d)
    y_sample = rmsnorm_rows(lay, x, final_norm_w, lay.nc, lay.n - lay.nc).reshape(n_lat, lat_len, d)
    return (y_prompt, y_sample, jnp.stack(new_rwkv, axis=1), jnp.stack(new_hgrn, axis=1),
            jnp.stack(new_ckv, axis=1), jnp.stack(new_krope, axis=1))
```

```python
import functools
import math

import numpy as np
import jax
import jax.numpy as jnp
from jax import lax
from jax.experimental import pallas as pl
from jax.experimental.pallas import tpu as pltpu

F32 = jnp.float32
BF16 = jnp.bfloat16

LANES_V7X = 128
SUBLANES_V7X = 8
VMEM_BYTES_V7X = 64 * 1024 * 1024
VMEM_LIMIT = VMEM_BYTES_V7X - 8 * 1024 * 1024

NORM_EPS = 1e-6
RW_HEAD = 64
RW_LN_EPS = 64e-5
RW_GROUP = 4
RW_LANES = RW_GROUP * RW_HEAD
RW_CHUNK = 64
RW_GSTEP = 2
HG_K = 128
HG_CHUNK_TOKENS = 64
HG_HSTEP = 8
ML_H = 16
ML_NOPE = 128
ML_ROPE = 64
ML_V = 128
ML_Q_LORA = 512
ML_KV_LORA = 512
GRID_W = 64
ROPE_BASE = 10000.0
LORA_PAD = 128


class Layout:
    def __init__(self, n_ctx, ctx_len, n_lat, lat_len):
        self.n_ctx, self.ctx_len, self.n_lat, self.lat_len = n_ctx, ctx_len, n_lat, lat_len
        self.nc = n_ctx * ctx_len
        self.n = self.nc + n_lat * lat_len
        self.tb = min(256, ctx_len)
        assert ctx_len % self.tb == 0 and lat_len % self.tb == 0 and self.tb % RW_CHUNK == 0
        self.nb = self.n // self.tb
        self.nb_ctx = self.nc // self.tb
        self.bps_ctx = ctx_len // self.tb
        self.bps_lat = lat_len // self.tb
        self.n_seq = n_ctx + n_lat

    def tile(self, want):
        t = want
        while self.nc % t or self.lat_len % t:
            t //= 2
        return t

    def cond_of_tile(self, i, tm):
        row = i * tm
        return jnp.where(row < self.nc, 0, 1 + (row - self.nc) // self.lat_len)

    def seq_info(self, blk):
        is_ctx = blk < self.nb_ctx
        lat = blk - self.nb_ctx
        seq = jnp.where(is_ctx, blk // self.bps_ctx, self.n_ctx + lat // self.bps_lat)
        pos = jnp.where(is_ctx, blk % self.bps_ctx, lat % self.bps_lat)
        cnt = jnp.where(is_ctx, self.bps_ctx, self.bps_lat)
        return seq, pos, cnt


def _cparams(n_axes):
    return pltpu.CompilerParams(dimension_semantics=("arbitrary",) * n_axes, vmem_limit_bytes=VMEM_LIMIT)


def _bdot(a, b):
    return jnp.dot(a.astype(BF16), b.astype(BF16), preferred_element_type=F32)


def _bdot_nt(a, b):
    return lax.dot_general(a.astype(BF16), b.astype(BF16), (((1,), (1,)), ((), ())),
                           preferred_element_type=F32)


def _silu(x):
    return x * jax.nn.sigmoid(x)


def _adaln_kernel(c_ref, w_ref, b_ref, o_ref):
    a = _silu(c_ref[...]).astype(BF16)
    o_ref[...] = jnp.dot(a, w_ref[...].astype(BF16), preferred_element_type=F32) + b_ref[...]


def adaln(cond, ada_w, ada_b):
    depth, d, d6 = ada_w.shape
    r = cond.shape[0]
    tn = 1024
    return pl.pallas_call(
        _adaln_kernel,
        grid=(depth, d6 // tn),
        in_specs=[pl.BlockSpec((r, d), lambda l, j: (0, 0)),
                  pl.BlockSpec((None, d, tn), lambda l, j: (l, 0, j)),
                  pl.BlockSpec((None, 1, tn), lambda l, j: (l, 0, j))],
        out_specs=pl.BlockSpec((None, r, tn), lambda l, j: (l, 0, j)),
        out_shape=jax.ShapeDtypeStruct((depth, r, d6), F32),
        compiler_params=_cparams(2), name="adaln",
    )(cond, ada_w, ada_b.reshape(depth, 1, d6))


def _norm_mod(x, nw, sc, sh):
    y = x * lax.rsqrt(jnp.mean(x * x, axis=-1, keepdims=True) + NORM_EPS)
    return (y * nw) * (1.0 + sc) + sh


def _norm_mod_kernel(x_ref, nw_ref, sc_ref, sh_ref, o_ref):
    o_ref[...] = _norm_mod(x_ref[...], nw_ref[...], sc_ref[...], sh_ref[...]).astype(o_ref.dtype)


def norm_mod(lay, x, nw, sc, sh, out_dtype):
    n, d = x.shape
    tm = lay.tile(512)
    cmap = lambda i: (lay.cond_of_tile(i, tm), 0, 0)
    return pl.pallas_call(
        _norm_mod_kernel,
        grid=(n // tm,),
        in_specs=[pl.BlockSpec((tm, d), lambda i: (i, 0)),
                  pl.BlockSpec((1, d), lambda i: (0, 0)),
                  pl.BlockSpec((None, 1, d), cmap),
                  pl.BlockSpec((None, 1, d), cmap)],
        out_specs=pl.BlockSpec((tm, d), lambda i: (i, 0)),
        out_shape=jax.ShapeDtypeStruct((n, d), out_dtype),
        compiler_params=_cparams(1), name="norm_mod",
    )(x, nw.reshape(1, d), sc, sh)


def _rmsnorm_kernel(x_ref, w_ref, o_ref):
    x = x_ref[...]
    o_ref[...] = x * lax.rsqrt(jnp.mean(x * x, axis=-1, keepdims=True) + NORM_EPS) * w_ref[...]


def rmsnorm_rows(lay, x, w, row0, rows):
    d = x.shape[1]
    tm = lay.tile(512)
    b0 = row0 // tm
    return pl.pallas_call(
        _rmsnorm_kernel,
        grid=(rows // tm,),
        in_specs=[pl.BlockSpec((tm, d), lambda i: (b0 + i, 0)), pl.BlockSpec((1, d), lambda i: (0, 0))],
        out_specs=pl.BlockSpec((tm, d), lambda i: (i, 0)),
        out_shape=jax.ShapeDtypeStruct((rows, d), x.dtype),
        compiler_params=_cparams(1), name="final_norm",
    )(x, w.reshape(1, d))


def _mm_kernel(*refs, n_w, n_e, epi):
    a = refs[0][...]
    accs = [jnp.dot(a, refs[1 + i][...], preferred_element_type=F32) for i in range(n_w)]
    extras = [refs[1 + n_w + i][...] for i in range(n_e)]
    outs = epi(accs, extras)
    o_refs = refs[1 + n_w + n_e:]
    for o_ref, val in zip(o_refs, outs):
        o_ref[...] = val.astype(o_ref.dtype)


def matmul(a, ws, epi, out_dtypes, *, tm, tn, extras=(), name):
    m, k = a.shape
    nw = ws[0].shape[1]
    assert m % tm == 0 and nw % tn == 0
    in_specs = [pl.BlockSpec((tm, k), lambda i, j: (i, 0))]
    in_specs += [pl.BlockSpec((k, tn), lambda i, j: (0, j)) for _ in ws]
    in_specs += [spec for _, spec in extras]
    outs = pl.pallas_call(
        functools.partial(_mm_kernel, n_w=len(ws), n_e=len(extras), epi=epi),
        grid=(m // tm, nw // tn),
        in_specs=in_specs,
        out_specs=[pl.BlockSpec((tm, tn), lambda i, j: (i, j)) for _ in out_dtypes],
        out_shape=[jax.ShapeDtypeStruct((m, nw), dt) for dt in out_dtypes],
        compiler_params=_cparams(2), name=name,
    )(a, *ws, *[arr for arr, _ in extras])
    return outs


def _epi_plain(accs, extras):
    return accs


def _epi_gated_residual(accs, extras):
    x, g = extras
    return [x + g * accs[0]]


def matmul_gated_residual(lay, a, w, x, gate, *, tm, tn, name):
    extras = ((x, pl.BlockSpec((tm, tn), lambda i, j: (i, j))),
              (gate, pl.BlockSpec((None, 1, tn), lambda i, j: (lay.cond_of_tile(i, tm), 0, j))))
    return matmul(a, [w], _epi_gated_residual, [F32], tm=tm, tn=tn, extras=extras, name=name)[0]


def _epi_swiglu(accs, extras):
    return [_silu(accs[0]) * accs[1]]


RWKV_PREP_SLAB = LANES_V7X


def _rwkv_prep_kernel(x_ref, xp_ref, xn_ref, nw_ref, sc_ref, sh_ref, mu_ref, *o_refs, lay):
    i = pl.program_id(0)
    _, pos, cnt = lay.seq_info(i)
    def inv_rms(x):
        return lax.rsqrt(jnp.mean(x * x, axis=-1, keepdims=True) + NORM_EPS)

    tb, d = x_ref.shape
    inv = inv_rms(x_ref[...])
    inv_p = inv_rms(xp_ref[SUBLANES_V7X - 1:SUBLANES_V7X, :])
    inv_n = inv_rms(xn_ref[0:1, :])
    keep_p = jnp.where(pos == 0, 0.0, 1.0)
    keep_n = jnp.where(pos == cnt - 1, 0.0, 1.0)
    row = lax.broadcasted_iota(jnp.int32, (tb, RWKV_PREP_SLAB), 0)
    for c0 in range(0, d, RWKV_PREP_SLAB):
        cs = slice(c0, c0 + RWKV_PREP_SLAB)
        nw, sc, sh = nw_ref[:, cs], sc_ref[:, cs], sh_ref[:, cs]
        h = ((x_ref[:, cs] * inv) * nw) * (1.0 + sc) + sh
        hp = (((xp_ref[SUBLANES_V7X - 1:SUBLANES_V7X, cs] * inv_p) * nw) * (1.0 + sc) + sh) * keep_p
        hn = (((xn_ref[0:1, cs] * inv_n) * nw) * (1.0 + sc) + sh) * keep_n
        prev = jnp.where(row == 0, hp, pltpu.roll(h, 1, axis=0))
        nxt = jnp.where(row == tb - 1, hn, pltpu.roll(h, tb - 1, axis=0))
        xx = 0.5 * (prev + nxt) - h
        for idx, o_ref in enumerate(o_refs):
            o_ref[:, cs] = (h + xx * mu_ref[idx:idx + 1, cs]).astype(o_ref.dtype)


def rwkv_prep(lay, x, nw, sc, sh, mu):
    n, d = x.shape
    tb = lay.tb
    r8 = tb // SUBLANES_V7X
    last8 = n // SUBLANES_V7X - 1
    cmap = lambda i: (lay.cond_of_tile(i, tb), 0, 0)
    return pl.pallas_call(
        functools.partial(_rwkv_prep_kernel, lay=lay),
        grid=(n // tb,),
        in_specs=[pl.BlockSpec((tb, d), lambda i: (i, 0)),
                  pl.BlockSpec((SUBLANES_V7X, d), lambda i: (jnp.maximum(i * r8 - 1, 0), 0)),
                  pl.BlockSpec((SUBLANES_V7X, d), lambda i: (jnp.minimum((i + 1) * r8, last8), 0)),
                  pl.BlockSpec((1, d), lambda i: (0, 0)),
                  pl.BlockSpec((None, 1, d), cmap),
                  pl.BlockSpec((None, 1, d), cmap),
                  pl.BlockSpec((6, d), lambda i: (0, 0))],
        out_specs=[pl.BlockSpec((tb, d), lambda i: (i, 0))] * 6,
        out_shape=[jax.ShapeDtypeStruct((n, d), BF16)] * 6,
        compiler_params=_cparams(1), name="rwkv_prep",
    )(x, x, x, nw.reshape(1, d), sc, sh, mu)


RW_LOG_DECAY_SCALE = -math.exp(-0.5)


def _rwkv_lora_kernel(xw_ref, xa_ref, xg_ref, w1_ref, w2_ref, w0_ref, a1_ref, a2_ref, a0_ref,
                      g1_ref, g2_ref, lw_ref, a_ref, g_ref):
    xw, xa, xg = xw_ref[...], xa_ref[...], xg_ref[...]
    for d in range(2):
        t = jnp.tanh(jnp.dot(xw, w1_ref[d], preferred_element_type=F32))
        wl = w0_ref[d] + _bdot(t, w2_ref[d])
        lw_ref[d] = RW_LOG_DECAY_SCALE * jax.nn.sigmoid(wl)
        t = jnp.dot(xa, a1_ref[d], preferred_element_type=F32)
        a_ref[d] = jax.nn.sigmoid(a0_ref[d] + _bdot(t, a2_ref[d]))
    t = jax.nn.sigmoid(jnp.dot(xg, g1_ref[...], preferred_element_type=F32))
    g_ref[...] = _bdot(t, g2_ref[...])


def rwkv_lora(lay, xw, xa, xg, w1, w2, w0, a1, a2, a0, g1, g2):
    n, d = xw.shape
    tm = lay.tile(256)
    full = lambda arr: pl.BlockSpec(arr.shape, lambda i: (0,) * arr.ndim)
    row = pl.BlockSpec((tm, d), lambda i: (i, 0))
    return pl.pallas_call(
        _rwkv_lora_kernel,
        grid=(n // tm,),
        in_specs=[row, row, row] + [full(t) for t in (w1, w2, w0, a1, a2, a0, g1, g2)],
        out_specs=[pl.BlockSpec((2, tm, d), lambda i: (0, i, 0)),
                   pl.BlockSpec((2, tm, d), lambda i: (0, i, 0)),
                   row],
        out_shape=[jax.ShapeDtypeStruct((2, n, d), F32), jax.ShapeDtypeStruct((2, n, d), F32),
                   jax.ShapeDtypeStruct((n, d), F32)],
        compiler_params=_cparams(1), name="rwkv_lora",
    )(xw, xa, xg, w1, w2, w0, a1, a2, a0, g1, g2)


def _wkv_constants():
    c, g, hd = RW_CHUNK, RW_GROUP, RW_HEAD
    gc, lanes = g * c, g * hd
    t = np.arange(c)
    cum = np.stack([(t[None, :] <= t[:, None]), (t[None, :] >= t[:, None])])
    tr = np.arange(c)[:, None]
    tc = np.arange(gc)[None, :] % c
    strict = np.stack([tc < tr, tc > tr])
    incl = np.stack([tc <= tr, tc >= tr])
    head_rows = np.arange(gc)[:, None] // c == np.arange(lanes)[None, :] // hd
    blk_rows = np.arange(gc)[:, None] // c == np.arange(gc)[None, :] // c
    bd = np.arange(lanes)[:, None] // hd == np.arange(lanes)[None, :] // hd
    eye_w = tr == tc
    return dict(cum=jnp.asarray(cum, BF16), strict=jnp.asarray(strict, F32), incl=jnp.asarray(incl, F32),
                head_rows=jnp.asarray(head_rows, BF16), blk_rows=jnp.asarray(blk_rows, BF16),
                bd=jnp.asarray(bd, F32), bd_b=jnp.asarray(bd, BF16), eye_w=jnp.asarray(eye_w, F32))


def _split_dot(m01, x, passes):
    acc, rem = None, x
    for _ in range(passes):
        part = rem.astype(BF16)
        term = jnp.dot(m01, part, preferred_element_type=F32)
        acc = term if acc is None else acc + term
        rem = rem - part.astype(F32)
    return acc


def _split_dot_r(x, m01, passes):
    acc, rem = None, x
    for _ in range(passes):
        part = rem.astype(BF16)
        term = jnp.dot(part, m01, preferred_element_type=F32)
        acc = term if acc is None else acc + term
        rem = rem - part.astype(F32)
    return acc


def _tile_rows(x, mask_b):
    return jnp.concatenate([x.astype(BF16)] * RW_GROUP, axis=0) * mask_b


def _wkv_prep2(raw, cums, head_rows):
    c = RW_CHUNK
    mids = []
    for (r, v, kk, kd, a, lw, _, _, _), cum in zip(raw, cums):
        tot = jnp.sum(lw, axis=0, keepdims=True)
        kka = kk * a
        e_inv, e_rest = jnp.exp(-cum), jnp.exp(tot - cum)
        q2 = jnp.concatenate([kk * jnp.exp(cum - lw), r * jnp.exp(cum)], axis=0).astype(BF16)
        mids.append(dict(q2=q2, kdh=_tile_rows(kd * e_inv, head_rows), kkah=_tile_rows(kka * e_inv, head_rows),
                         vbd=_tile_rows(v, head_rows), v=v, decay=jnp.exp(tot),
                         kw=jnp.concatenate([kd * e_rest, -(kka * e_rest)], axis=0).astype(BF16)))
    s1s = [_bdot_nt(m['q2'], m['kdh']) for m in mids]
    s2s = [_bdot_nt(m['q2'][c:], m['kkah']) for m in mids]
    for m, u, s1, s2 in zip(mids, raw, s1s, s2s):
        strict_w, incl_w = u[7], u[8]
        m['lad'] = jnp.concatenate([jnp.where(strict_w > 0, s1[:c], 0.0),
                                    jnp.where(incl_w > 0, s1[c:], 0.0)], axis=0).astype(BF16)
        m['a_a'] = jnp.where(incl_w > 0, s2, 0.0).astype(BF16)
    return mids


def _wkv_prep3(mids):
    for m, lav in zip(mids, [_bdot(m['lad'], m['vbd']) for m in mids]):
        m['lav'] = lav
    return mids


def _wkv_adv1(states, preps):
    return [_bdot_nt(p['q2'], s) for p, s in zip(preps, states)]


def _wkv_adv2(p0s, preps, t_ws, head_rows):
    c = RW_CHUNK
    return [_bdot(t_w, _tile_rows(p0[:c] + p['lav'][:c], head_rows))
            for p0, p, t_w in zip(p0s, preps, t_ws)]


def _wkv_adv3(states, p0s, us, preps, head_rows, bd):
    c = RW_CHUNK
    upds = [_bdot(jnp.concatenate([p['v'].astype(F32), u], axis=0).T, p['kw']) for p, u in zip(preps, us)]
    aus = [_bdot(p['a_a'], _tile_rows(u, head_rows)) for p, u in zip(preps, us)]
    new_states = [s * p['decay'] + jnp.where(bd > 0, upd, 0.0) for s, p, upd in zip(states, preps, upds)]
    ys = [p0[c:] + p['lav'][c:] - au for p0, p, au in zip(p0s, preps, aus)]
    return new_states, ys


RKV_TN = RW_LANES


def _rwkv_rkv_kernel(xr_ref, xk_ref, xv_ref, wr_ref, wk_ref, wv_ref, a_ref, kkw_ref, kaw_ref, rk_ref, bd_ref,
                     r_ref, v_ref, kk_ref, kd_ref, b_ref):
    r = jnp.dot(xr_ref[...], wr_ref[...], preferred_element_type=F32)
    k = jnp.dot(xk_ref[...], wk_ref[...], preferred_element_type=F32)
    v = jnp.dot(xv_ref[...], wv_ref[...], preferred_element_type=F32)
    r_ref[...] = r
    v_ref[...] = v.astype(v_ref.dtype)
    bd = bd_ref[...]
    kk = k * kkw_ref[...]
    mix = None
    for d in range(2):
        kd = k * (1.0 + (a_ref[d] - 1.0) * kaw_ref[...])
        kd_ref[d] = kd
        term = kd * rk_ref[d]
        mix = term if mix is None else mix + term
    rm = r * mix
    for h in range(RKV_TN // RW_LANES):
        cs = slice(h * RW_LANES, (h + 1) * RW_LANES)
        kkh = kk[:, cs]
        kk_ref[:, cs] = kkh * lax.rsqrt(_split_dot_r(kkh * kkh, bd, 2) + 1e-12)
        b_ref[:, cs] = _split_dot_r(rm[:, cs], bd, 2) * v[:, cs]


def rwkv_rkv(lay, xr, xk, xv, wr, wk, wv, a, kkw, kaw, rk):
    n, kdim = xr.shape
    dm = wr.shape[1]
    tm, tn = lay.tile(1024), RKV_TN
    bd_b = _wkv_constants()['bd_b']
    lhs = pl.BlockSpec((tm, kdim), lambda i, j: (i, 0))
    rhs = pl.BlockSpec((kdim, tn), lambda i, j: (0, j))
    tok = pl.BlockSpec((tm, tn), lambda i, j: (i, j))
    two = pl.BlockSpec((2, tm, tn), lambda i, j: (0, i, j))
    par = pl.BlockSpec((1, tn), lambda i, j: (0, j))
    return pl.pallas_call(
        _rwkv_rkv_kernel,
        grid=(n // tm, dm // tn),
        in_specs=[lhs, lhs, lhs, rhs, rhs, rhs, two, par, par,
                  pl.BlockSpec((2, 1, tn), lambda i, j: (0, 0, j)),
                  pl.BlockSpec(bd_b.shape, lambda i, j: (0, 0))],
        out_specs=[tok, tok, tok, two, tok],
        out_shape=[jax.ShapeDtypeStruct((n, dm), F32), jax.ShapeDtypeStruct((n, dm), BF16),
                   jax.ShapeDtypeStruct((n, dm), F32), jax.ShapeDtypeStruct((2, n, dm), F32),
                   jax.ShapeDtypeStruct((n, dm), F32)],
        compiler_params=_cparams(2), name="rwkv_rkv",
    )(xr, xk, xv, wr, wk, wv, a, kkw, kaw, rk, bd_b)


def _tinv_begin(items, head_rows, blk_rows, eye_w):
    cums = [_split_dot(it[3], it[2], 3) for it in items]
    ops = [(kk * jnp.exp(cum - lw), _tile_rows(kk * a * jnp.exp(-cum), head_rows))
           for (kk, a, lw, _, _), cum in zip(items, cums)]
    n_ws = [-jnp.where(it[4] > 0, _bdot_nt(q, w), 0.0) for it, (q, w) in zip(items, ops)]
    n_pows = [_bdot(n_w, _tile_rows(n_w, blk_rows)) for n_w in n_ws]
    return [eye_w + n_w for n_w in n_ws], n_pows, cums


def _tinv_level(t_ws, n_pows, last, blk_rows):
    c = RW_CHUNK
    ws = [_tile_rows(n_pow, blk_rows) for n_pow in n_pows]
    if last:
        return [t_w + _bdot(t_w, w) for t_w, w in zip(t_ws, ws)], None
    boths = [_bdot(jnp.concatenate([t_w, n_pow], axis=0), w) for t_w, n_pow, w in zip(t_ws, n_pows, ws)]
    return [t_w + both[:c] for t_w, both in zip(t_ws, boths)], [both[c:] for both in boths]


RW_TINV_LEVELS = int(math.log2(RW_CHUNK))


def _wkv_fused_kernel(*refs, lay):
    (rf, vf, kkf, kdf, af, lwf, s0f, kkfn, afn, lwfn,
     rb, vb, kkb, kdb, ab, lwb, s0b, kkbn, abn, lwbn,
     cum_ref, strict_ref, incl_ref, hr_ref, br_ref, eye_ref, bd_ref,
     yf_ref, yb_ref, sff_ref, sfb_ref, sf_scr, sb_scr, t_scr, c_scr) = refs
    c = RW_CHUNK
    n_chunks = lay.tb // c
    j = pl.program_id(1)
    cur = lax.rem(j, 2)
    nxt = 1 - cur
    _, pos_f, cnt_f = lay.seq_info(j)
    _, pos_b, cnt_b = lay.seq_info(lay.nb - 1 - j)

    heads = [(gi, h, slice(h * RW_HEAD, (h + 1) * RW_HEAD)) for gi in range(RW_GSTEP) for h in range(RW_GROUP)]

    def load_states(scr, s0):
        scr[...] = jnp.zeros(scr.shape, scr.dtype)
        for gi, h, hs in heads:
            scr[gi, hs, hs] = s0[gi, h]

    def store_states(out, scr):
        for gi, h, hs in heads:
            out[gi, h] = scr[gi, hs, hs]

    @pl.when(pos_f == 0)
    def _():
        load_states(sf_scr, s0f)

    @pl.when(pos_b == cnt_b - 1)
    def _():
        load_states(sb_scr, s0b)

    hr, br, eye_w, bd = hr_ref[...], br_ref[...], eye_ref[...], bd_ref[...]
    sl_f = [slice(ci * c, (ci + 1) * c) for ci in range(n_chunks)]
    sl_b = sl_f[::-1]
    cols = [slice(gi * RW_LANES, (gi + 1) * RW_LANES) for gi in range(RW_GSTEP)]

    def inv_items(kk_f, a_f, lw_f, kk_b, a_b, lw_b, sl):
        out = []
        for cs in cols:
            out.append((kk_f[sl, cs], a_f[sl, cs], lw_f[sl, cs], cum_ref[0], strict_ref[0]))
            out.append((kk_b[sl, cs], a_b[sl, cs], lw_b[sl, cs], cum_ref[1], strict_ref[1]))
        return out

    def inv_store(scr, slot, sl, vals):
        for gi, cs in enumerate(cols):
            scr[slot, 0, sl, cs] = vals[2 * gi].astype(scr.dtype)
            scr[slot, 1, sl, cs] = vals[2 * gi + 1].astype(scr.dtype)

    @pl.when(j == 0)
    def _():
        def body(ci, carry):
            sl = pl.ds(pl.multiple_of(ci * c, c), c)
            t_ws, n_pows, cums = _tinv_begin(inv_items(kkf, af, lwf, kkb, ab, lwb, sl), hr, br, eye_w)
            inv_store(c_scr, cur, sl, cums)
            for lv in range(1, RW_TINV_LEVELS):
                t_ws, n_pows = _tinv_level(t_ws, n_pows, lv == RW_TINV_LEVELS - 1, br)
            inv_store(t_scr, cur, sl, t_ws)
            return carry
        lax.fori_loop(0, n_chunks, body, 0)

    def raw(ci):
        out = []
        for cs in cols:
            sf, sb = sl_f[ci], sl_b[ci]
            out.append((rf[sf, cs], vf[sf, cs], kkf[sf, cs], kdf[sf, cs], af[sf, cs], lwf[sf, cs],
                        cum_ref[0], strict_ref[0], incl_ref[0]))
            out.append((rb[sb, cs], vb[sb, cs], kkb[sb, cs], kdb[sb, cs], ab[sb, cs], lwb[sb, cs],
                        cum_ref[1], strict_ref[1], incl_ref[1]))
        return out

    states = []
    for gi in range(RW_GSTEP):
        states += [sf_scr[gi], sb_scr[gi]]
    def cached_cums(ci):
        out = []
        for cs in cols:
            out += [c_scr[cur, 0, sl_f[ci], cs], c_scr[cur, 1, sl_b[ci], cs]]
        return out

    preps = _wkv_prep3(_wkv_prep2(raw(0), cached_cums(0), hr))
    inv = {'lv': 0, 't': None, 'n': None}
    per = 2 * RW_GSTEP

    def inv_step():
        lv = inv['lv']
        if lv == 0:
            items = []
            for sl in sl_f:
                items += inv_items(kkfn, afn, lwfn, kkbn, abn, lwbn, sl)
            inv['t'], inv['n'], cums = _tinv_begin(items, hr, br, eye_w)
            for ci2, sl in enumerate(sl_f):
                inv_store(c_scr, nxt, sl, cums[ci2 * per:(ci2 + 1) * per])
        elif lv < RW_TINV_LEVELS:
            inv['t'], inv['n'] = _tinv_level(inv['t'], inv['n'], lv == RW_TINV_LEVELS - 1, br)
            if lv == RW_TINV_LEVELS - 1:
                for ci2, sl in enumerate(sl_f):
                    inv_store(t_scr, nxt, sl, inv['t'][ci2 * per:(ci2 + 1) * per])
        inv['lv'] = lv + 1

    for ci in range(n_chunks):
        more = ci + 1 < n_chunks
        t_ws = []
        for cs in cols:
            t_ws += [t_scr[cur, 0, sl_f[ci], cs], t_scr[cur, 1, sl_b[ci], cs]]
        p0s = _wkv_adv1(states, preps)
        if more:
            raw_n = raw(ci + 1)
            cums_n = cached_cums(ci + 1)
        inv_step()
        us = _wkv_adv2(p0s, preps, t_ws, hr)
        if more:
            mids_n = _wkv_prep2(raw_n, cums_n, hr)
        states, ys = _wkv_adv3(states, p0s, us, preps, hr, bd)
        if more:
            preps = _wkv_prep3(mids_n)
        inv_step()
        for gi, cs in enumerate(cols):
            yf_ref[sl_f[ci], cs] = ys[2 * gi]
            yb_ref[sl_b[ci], cs] = ys[2 * gi + 1]
    while inv['lv'] < RW_TINV_LEVELS:
        inv_step()
    for gi in range(RW_GSTEP):
        sf_scr[gi] = states[2 * gi]
        sb_scr[gi] = states[2 * gi + 1]

    @pl.when(pos_f == cnt_f - 1)
    def _():
        store_states(sff_ref, sf_scr)

    @pl.when(pos_b == 0)
    def _():
        store_states(sfb_ref, sb_scr)


def wkv_fused(lay, r, v, kk, kd, a, lw, s0):
    n, dm = r.shape
    tb, lanes = lay.tb, RW_LANES
    ng = dm // lanes
    width = RW_GSTEP * lanes
    k = _wkv_constants()
    full = lambda arr: pl.BlockSpec(arr.shape, lambda g, j: (0,) * arr.ndim)

    def views(d, blk, blk_next):
        tok = pl.BlockSpec((tb, width), lambda g, j: (blk(j), g))
        tok2 = pl.BlockSpec((None, tb, width), lambda g, j: (d, blk(j), g))
        tok_n = pl.BlockSpec((tb, width), lambda g, j: (blk_next(j), g))
        tok2_n = pl.BlockSpec((None, tb, width), lambda g, j: (d, blk_next(j), g))

        def s0_map(g, j):
            seq, _, _ = lay.seq_info(blk(j))
            return (jnp.maximum(seq - lay.n_ctx + 1, 0), d, g, 0, 0, 0)

        def sfin_map(g, j):
            seq, _, _ = lay.seq_info(blk(j))
            return (seq, g, 0, 0, 0)

        per_head = (RW_GSTEP, RW_GROUP, RW_HEAD, RW_HEAD)
        ins = [tok, tok, tok, tok2, tok2, tok2, pl.BlockSpec((None, None) + per_head, s0_map),
               tok_n, tok2_n, tok2_n]
        return ins, tok, pl.BlockSpec((None,) + per_head, sfin_map)

    in_f, y_f, sf_f = views(0, lambda j: j, lambda j: jnp.minimum(j + 1, lay.nb - 1))
    in_b, y_b, sf_b = views(1, lambda j: lay.nb - 1 - j, lambda j: jnp.maximum(lay.nb - 2 - j, 0))
    consts = [k['cum'], k['strict'], k['incl'], k['head_rows'], k['blk_rows'], k['eye_w'], k['bd']]
    args = [r, v, kk, kd, a, lw, s0, kk, a, lw]
    return pl.pallas_call(
        functools.partial(_wkv_fused_kernel, lay=lay),
        grid=(ng // RW_GSTEP, lay.nb),
        in_specs=in_f + in_b + [full(x) for x in consts],
        out_specs=[y_f, y_b, sf_f, sf_b],
        out_shape=[jax.ShapeDtypeStruct((n, dm), F32), jax.ShapeDtypeStruct((n, dm), F32),
                   jax.ShapeDtypeStruct((lay.n_seq, ng, RW_GROUP, RW_HEAD, RW_HEAD), F32),
                   jax.ShapeDtypeStruct((lay.n_seq, ng, RW_GROUP, RW_HEAD, RW_HEAD), F32)],
        scratch_shapes=[pltpu.VMEM((RW_GSTEP, lanes, lanes), F32), pltpu.VMEM((RW_GSTEP, lanes, lanes), F32),
                        pltpu.VMEM((2, 2, tb, width), BF16), pltpu.VMEM((2, 2, tb, width), F32)],
        compiler_params=_cparams(2), name="wkv_fused",
    )(*args, *args, *consts)


def _rwkv_post_kernel(yf_ref, yb_ref, b_ref, g_ref, lnw_ref, lnb_ref, bd_ref, o_ref):
    y = yf_ref[...] + yb_ref[...]
    bd = bd_ref[...]
    inv = 1.0 / RW_HEAD
    mu = _split_dot_r(y, bd, 2) * inv
    yc = y - mu
    var = _split_dot_r(yc * yc, bd, 2) * inv
    yn = yc * lax.rsqrt(var + RW_LN_EPS)
    out = yn * lnw_ref[...] + lnb_ref[...] + b_ref[...]
    o_ref[...] = (out * g_ref[...]).astype(o_ref.dtype)


def rwkv_post(lay, y_f, y_b, bonus, g, lnw, lnb):
    n, dm = y_f.shape
    lanes = RW_LANES
    tm = lay.tile(1024)
    bd_b = _wkv_constants()['bd_b']
    tok = pl.BlockSpec((tm, lanes), lambda i, c: (i, c))
    par = pl.BlockSpec((1, lanes), lambda i, c: (0, c))
    return pl.pallas_call(
        _rwkv_post_kernel,
        grid=(n // tm, dm // lanes),
        in_specs=[tok, tok, tok, tok, par, par, pl.BlockSpec(bd_b.shape, lambda i, c: (0, 0))],
        out_specs=tok,
        out_shape=jax.ShapeDtypeStruct((n, dm), BF16),
        compiler_params=_cparams(2), name="rwkv_post",
    )(y_f, y_b, bonus, g, lnw, lnb, bd_b)


def rwkv_layer(lay, x, mods, nw, p, s0):
    n, d = x.shape
    sh1, sc1, g1 = mods[0], mods[1], mods[2]
    xr, xw, xk, xv, xa, xg = rwkv_prep(lay, x, nw, sc1, sh1, p['mu'])
    tm = lay.tile(1024)
    lw, a, g = rwkv_lora(lay, xw, xa, xg, p['w1'], p['w2'], p['w0'], p['a1'], p['a2'], p['a0'],
                         p['g1'], p['g2'])
    r, v, kk, kd, bonus = rwkv_rkv(lay, xr, xk, xv, p['wr'], p['wk'], p['wv'], a, p['kk'], p['ka'], p['rk'])
    y_f, y_b, sfin_f, sfin_b = wkv_fused(lay, r, v, kk, kd, a, lw, s0)
    z = rwkv_post(lay, y_f, y_b, bonus, g, p['lnx_w'], p['lnx_b'])
    x = matmul_gated_residual(lay, z, p['wo'], x, g1, tm=tm, tn=1024, name="rwkv_o")
    return x, jnp.stack([sfin_f, sfin_b], axis=1)


def _hgrn_constants():
    c = HG_CHUNK_TOKENS
    t = np.arange(c)[:, None]
    j = np.arange(c)[None, :]
    cums, masks_all = [], []
    for rev in (False, True):
        masks = []
        h = 1
        while h < c:
            upper = (t % (2 * h)) >= h
            same = (t // (2 * h)) == (j // (2 * h))
            if not rev:
                mask = same & upper & ((j % (2 * h)) < h)
            else:
                mask = same & (~upper) & ((j % (2 * h)) >= h)
            masks.append(mask)
            h *= 2
        masks.append(t == j)
        cums.append((j >= t) if rev else (j <= t))
        masks_all.append(np.stack(masks, 0))
    return jnp.asarray(np.stack(cums), BF16), jnp.asarray(np.stack(masks_all).astype(np.float32))


def _hgrn_level_exponents(g, gcum, rev):
    c, kdim = g.shape
    row = lax.broadcasted_iota(jnp.int32, g.shape, 0)
    nxt = pltpu.roll(g, c - 1, axis=0)
    prv = pltpu.roll(g, 1, axis=0)
    r2, r4 = row & 1, row & 3
    if not rev:
        x1 = jnp.where(r2 == 1, g, 0.0)
        x2 = jnp.where(r4 == 0, nxt, jnp.where(r4 == 2, g, jnp.where(r4 == 3, prv + g, 0.0)))
    else:
        x1 = jnp.where(r2 == 0, g, 0.0)
        x2 = jnp.where(r4 == 0, g + nxt, jnp.where(r4 == 1, g, jnp.where(r4 == 3, prv, 0.0)))
    xs = [x1, x2]
    h = 4
    while h < c:
        gr = gcum.reshape(c // (2 * h), 2 * h, kdim)
        ref = gr[:, h:h + 1, :] if rev else gr[:, h - 1:h, :]
        upper = lax.broadcasted_iota(jnp.int32, gr.shape, 1) >= h
        diff = gr - ref
        x = jnp.where(upper, -diff, diff) if rev else jnp.where(upper, diff, -diff)
        xs.append(x.reshape(c, kdim))
        h *= 2
    return xs


def _hgrn_units(units, masks_by_dir, cum_by_dir):
    c = HG_CHUNK_TOKENS
    gs = [jnp.log(f) for _, f, _, _ in units]
    gcums = [_split_dot(cum_by_dir[rev], g, 3) for g, (_, _, _, rev) in zip(gs, units)]
    outs = []
    pend = []
    for (q, f, iv, rev), g, gcum in zip(units, gs, gcums):
        k = 1.0 - f
        tot = gcum[0:1] if rev else gcum[c - 1:c]
        es = [jnp.exp(x) for x in _hgrn_level_exponents(g, gcum, rev)]
        pend.append((q, k, iv, rev, es, jnp.exp(gcum), jnp.exp(tot - gcum), jnp.exp(tot)))
    for q, k, iv, rev, es, eg, erest, etot in pend:
        masks = masks_by_dir[rev]
        a = masks[len(es)] * _bdot_nt(q, k)
        for lv, el in enumerate(es):
            a = a + masks[lv] * _bdot_nt(q * el, k * el)
        outs.append(dict(a=a, iv=iv, qe=(q * eg).astype(BF16), kdec=(k * erest).astype(BF16), decay=etot))
    for u in outs:
        u['av'] = _bdot(u['a'], u['iv'])
    for u in outs:
        u['upd'] = _bdot(u['iv'].astype(F32).T, u['kdec'])
    return outs


def _hgrn_kernel(*refs, lay):
    (qf, ff, if_, s0f, qb, fb, ib, s0b, cum_ref, mask_ref, of_ref, ob_ref, sff_ref, sfb_ref,
     sf_scr, sb_scr) = refs
    c = HG_CHUNK_TOKENS
    n_chunks = lay.tb // c
    j = pl.program_id(1)
    _, pos_f, cnt_f = lay.seq_info(j)
    _, pos_b, cnt_b = lay.seq_info(lay.nb - 1 - j)

    @pl.when(pos_f == 0)
    def _():
        sf_scr[...] = s0f[...]

    @pl.when(pos_b == cnt_b - 1)
    def _():
        sb_scr[...] = s0b[...]

    masks_by_dir = [mask_ref[0], mask_ref[1]]
    cum_by_dir = [cum_ref[0], cum_ref[1]]
    sl_f = [slice(ci * c, (ci + 1) * c) for ci in range(n_chunks)]
    sl_b = sl_f[::-1]
    cols = [slice(hi * HG_K, (hi + 1) * HG_K) for hi in range(HG_HSTEP)]
    chains = []
    for hi, cs in enumerate(cols):
        chains.append((sf_scr, hi, of_ref, [(sl, cs) for sl in sl_f],
                       [(qf[sl, cs], ff[sl, cs], if_[sl, cs], False) for sl in sl_f]))
        chains.append((sb_scr, hi, ob_ref, [(sl, cs) for sl in sl_b],
                       [(qb[sl, cs], fb[sl, cs], ib[sl, cs], True) for sl in sl_b]))
    done = _hgrn_units([u for ch in chains for u in ch[4]], masks_by_dir, cum_by_dir)
    pend = []
    for n_ch, (scr, hi, o_ref, where, _) in enumerate(chains):
        s = scr[hi]
        for ci in range(n_chunks):
            u = done[n_ch * n_chunks + ci]
            pend.append((o_ref, where[ci], u, s))
            s = s * u['decay'] + u['upd']
        scr[hi] = s
    for o_ref, (sl, cs), u, s_prev in pend:
        o_ref[sl, cs] = u['av'] + _bdot_nt(u['qe'], s_prev)

    @pl.when(pos_f == cnt_f - 1)
    def _():
        sff_ref[...] = sf_scr[...]

    @pl.when(pos_b == 0)
    def _():
        sfb_ref[...] = sb_scr[...]


def hgrn_scan(lay, q, f_fwd, f_bwd, iv, s0t):
    n, dm = q.shape
    tb = lay.tb
    nh = dm // HG_K
    width = HG_HSTEP * HG_K
    cum_m, masks = _hgrn_constants()
    full = lambda arr: pl.BlockSpec(arr.shape, lambda h, j: (0,) * arr.ndim)

    def views(d, blk):
        tok = pl.BlockSpec((tb, width), lambda h, j: (blk(j), h))

        def s0_map(h, j):
            seq, _, _ = lay.seq_info(blk(j))
            return (jnp.maximum(seq - lay.n_ctx + 1, 0), d, h, 0, 0)

        def sfin_map(h, j):
            seq, _, _ = lay.seq_info(blk(j))
            return (seq, h, 0, 0)

        ins = [tok, tok, tok, pl.BlockSpec((None, None, HG_HSTEP, HG_K, HG_K), s0_map)]
        return ins, tok, pl.BlockSpec((None, HG_HSTEP, HG_K, HG_K), sfin_map)

    in_f, o_f, sf_f = views(0, lambda j: j)
    in_b, o_b, sf_b = views(1, lambda j: lay.nb - 1 - j)
    return pl.pallas_call(
        functools.partial(_hgrn_kernel, lay=lay),
        grid=(nh // HG_HSTEP, lay.nb),
        in_specs=in_f + in_b + [full(cum_m), full(masks)],
        out_specs=[o_f, o_b, sf_f, sf_b],
        out_shape=[jax.ShapeDtypeStruct((n, dm), F32), jax.ShapeDtypeStruct((n, dm), F32),
                   jax.ShapeDtypeStruct((lay.n_seq, nh, HG_K, HG_K), F32),
                   jax.ShapeDtypeStruct((lay.n_seq, nh, HG_K, HG_K), F32)],
        scratch_shapes=[pltpu.VMEM((HG_HSTEP, HG_K, HG_K), F32), pltpu.VMEM((HG_HSTEP, HG_K, HG_K), F32)],
        compiler_params=_cparams(2), name="hgrn_scan",
    )(q, f_fwd, iv, s0t, q, f_bwd, iv, s0t, cum_m, masks)


HG_POST_HEADS = 4


def _hgrn_post_kernel(of_ref, ob_ref, g_ref, nw_ref, z_ref):
    for h in range(HG_POST_HEADS):
        cs = slice(h * HG_K, (h + 1) * HG_K)
        o = of_ref[:, cs] + ob_ref[:, cs]
        o = o * lax.rsqrt(jnp.mean(o * o, axis=-1, keepdims=True) + NORM_EPS) * nw_ref[...] * g_ref[:, cs]
        z_ref[:, cs] = o.astype(z_ref.dtype)


def hgrn_post(lay, o_f, o_b, gs, nw):
    n, dm = o_f.shape
    tm = lay.tile(1024)
    width = HG_POST_HEADS * HG_K
    tok = pl.BlockSpec((tm, width), lambda i, h: (i, h))
    return pl.pallas_call(
        _hgrn_post_kernel,
        grid=(n // tm, dm // width),
        in_specs=[tok, tok, tok, pl.BlockSpec((1, HG_K), lambda i, h: (0, 0))],
        out_specs=tok,
        out_shape=jax.ShapeDtypeStruct((n, dm), BF16),
        compiler_params=_cparams(2), name="hgrn_post",
    )(o_f, o_b, gs, nw)


def _epi_hgrn_in(accs, extras):
    lb = extras[0]
    q = _silu(accs[0])
    f0 = lb + (1.0 - lb) * jax.nn.sigmoid(accs[1])
    f1 = lb + (1.0 - lb) * jax.nn.sigmoid(accs[2])
    return [q, f0, f1, accs[3], _silu(accs[4])]


def hgrn_layer(lay, x, mods, nw, p, lb, s0t):
    n, d = x.shape
    sh1, sc1, g1 = mods[0], mods[1], mods[2]
    h = norm_mod(lay, x, nw, sc1, sh1, BF16)
    tm, tn = lay.tile(1024), 256
    extras = ((lb, pl.BlockSpec((1, tn), lambda i, j: (0, j))),)
    q, f0, f1, iv, gs = matmul(h, p['w_in'], _epi_hgrn_in, [F32, F32, F32, BF16, F32], tm=tm, tn=tn, extras=extras,
                               name="hgrn_in")
    o_f, o_b, sfin_f, sfin_b = hgrn_scan(lay, q, f0, f1, iv, s0t)
    z = hgrn_post(lay, o_f, o_b, gs, p['norm_w'])
    x = matmul_gated_residual(lay, z, p['wo'], x, g1, tm=lay.tile(1024), tn=1024, name="hgrn_o")
    return x, jnp.stack([sfin_f, sfin_b], axis=1)


ML_DOWN_COLS = 1280
ML_KR_OFF = ML_Q_LORA + ML_KV_LORA
ML_KRS_OFF = ML_KR_OFF + LANES_V7X


def _rms(x, w):
    return x * lax.rsqrt(jnp.mean(x * x, axis=-1, keepdims=True) + NORM_EPS) * w


def _mla_mid_kernel(dn_ref, qw_ref, kvw_ref, cos_ref, sin_ref, qn_ref, ckv_ref, kr_ref):
    dn = dn_ref[...]
    qn_ref[...] = _rms(dn[:, :ML_Q_LORA], qw_ref[...]).astype(qn_ref.dtype)
    ckv_ref[...] = _rms(dn[:, ML_Q_LORA:ML_KR_OFF], kvw_ref[...])
    kr = dn[:, ML_KR_OFF:ML_KR_OFF + ML_ROPE]
    krs = dn[:, ML_KRS_OFF:ML_KRS_OFF + ML_ROPE]
    kr_ref[...] = kr * cos_ref[...] + krs * sin_ref[...]


def mla_mid(lay, dn, qw, kvw, cos, sin):
    n = dn.shape[0]
    tm = lay.tile(512)
    return pl.pallas_call(
        _mla_mid_kernel,
        grid=(n // tm,),
        in_specs=[pl.BlockSpec((tm, ML_DOWN_COLS), lambda i: (i, 0)),
                  pl.BlockSpec((1, ML_Q_LORA), lambda i: (0, 0)),
                  pl.BlockSpec((1, ML_KV_LORA), lambda i: (0, 0)),
                  pl.BlockSpec((tm, ML_ROPE), lambda i: (i, 0)),
                  pl.BlockSpec((tm, ML_ROPE), lambda i: (i, 0))],
        out_specs=[pl.BlockSpec((tm, ML_Q_LORA), lambda i: (i, 0)),
                   pl.BlockSpec((tm, ML_KV_LORA), lambda i: (i, 0)),
                   pl.BlockSpec((tm, ML_ROPE), lambda i: (i, 0))],
        out_shape=[jax.ShapeDtypeStruct((n, ML_Q_LORA), BF16),
                   jax.ShapeDtypeStruct((n, ML_KV_LORA), F32),
                   jax.ShapeDtypeStruct((n, ML_ROPE), F32)],
        compiler_params=_cparams(1), name="mla_mid",
    )(dn, qw, kvw, cos, sin)


ML_QSCALE = math.log2(math.e) / math.sqrt(ML_NOPE + ML_ROPE)


def _epi_qscale(accs, extras):
    return [accs[0] * ML_QSCALE]


def _epi_rope(accs, extras):
    cos, sin = extras
    reps = accs[0].shape[1] // cos.shape[1]
    cos, sin = jnp.concatenate([cos] * reps, axis=1), jnp.concatenate([sin] * reps, axis=1)
    return [(accs[0] * cos + accs[1] * sin) * ML_QSCALE]


ML_KEY_SPLITS = 2


def _attn_kernel(qn_ref, qr_ref, kn_ref, kr_ref, v_ref, o_ref, kc_scr, vt_scr):
    @pl.when(pl.program_id(2) == 0)
    def _():
        for h in range(2):
            kc_scr[h, :, :ML_NOPE] = kn_ref[:, h * ML_NOPE:(h + 1) * ML_NOPE]
            kc_scr[h, :, ML_NOPE:] = kr_ref[...]
            vt_scr[h] = v_ref[:, h * ML_V:(h + 1) * ML_V].astype(F32).T.astype(vt_scr.dtype)

    k_len = kc_scr.shape[1]
    kh = k_len // ML_KEY_SPLITS
    scores = []
    for h in range(2):
        q = jnp.concatenate([qn_ref[:, h * ML_NOPE:(h + 1) * ML_NOPE],
                             qr_ref[:, h * ML_ROPE:(h + 1) * ML_ROPE]], axis=1)
        scores.append([lax.dot_general(kc_scr[h, kb * kh:(kb + 1) * kh, :], q, (((1,), (1,)), ((), ())),
                                       preferred_element_type=F32) for kb in range(ML_KEY_SPLITS)])
    for h in range(2):
        m_acc = l_acc = o_acc = None
        for kb, st in enumerate(scores[h]):
            m = jnp.max(st, axis=0, keepdims=True)
            p = jnp.exp2(st - m)
            l = jnp.sum(p, axis=0, keepdims=True)
            ot = jnp.dot(vt_scr[h, :, kb * kh:(kb + 1) * kh], p.astype(BF16),
                         preferred_element_type=F32)
            if m_acc is None:
                m_acc, l_acc, o_acc = m, l, ot
            else:
                m_new = jnp.maximum(m_acc, m)
                c_old, c_new = jnp.exp2(m_acc - m_new), jnp.exp2(m - m_new)
                l_acc = l_acc * c_old + l * c_new
                o_acc = o_acc * c_old + ot * c_new
                m_acc = m_new
        o_ref[:, h * ML_V:(h + 1) * ML_V] = (o_acc / l_acc).T.astype(o_ref.dtype)


def attention(qn, qr, kn, kr, v, *, n_seq, q_len, k_len, row0, tq):
    heads2 = qn.shape[1] // (2 * ML_NOPE)
    qb = q_len // tq
    rb0 = row0 // tq
    return pl.pallas_call(
        _attn_kernel,
        grid=(n_seq, heads2, qb),
        in_specs=[pl.BlockSpec((tq, 2 * ML_NOPE), lambda s, h, i: (rb0 + s * qb + i, h)),
                  pl.BlockSpec((tq, 2 * ML_ROPE), lambda s, h, i: (rb0 + s * qb + i, h)),
                  pl.BlockSpec((k_len, 2 * ML_NOPE), lambda s, h, i: (s, h)),
                  pl.BlockSpec((k_len, ML_ROPE), lambda s, h, i: (s, 0)),
                  pl.BlockSpec((k_len, 2 * ML_V), lambda s, h, i: (s, h))],
        out_specs=pl.BlockSpec((tq, 2 * ML_V), lambda s, h, i: (s * qb + i, h)),
        out_shape=jax.ShapeDtypeStruct((n_seq * q_len, heads2 * 2 * ML_V), BF16),
        scratch_shapes=[pltpu.VMEM((2, k_len, ML_NOPE + ML_ROPE), BF16), pltpu.VMEM((2, ML_V, k_len), BF16)],
        compiler_params=_cparams(3), name="mla_attn",
    )(qn, qr, kn, kr, v)


def mla_layer(lay, x, mods, nw, p, cache_ckv, cache_kr, cos, sin, cos2, sin2):
    n, d = x.shape
    sh1, sc1, g1 = mods[0], mods[1], mods[2]
    h = norm_mod(lay, x, nw, sc1, sh1, BF16)
    tm = lay.tile(512)
    dn = matmul(h, [p['w_down']], _epi_plain, [F32], tm=tm, tn=ML_DOWN_COLS, name="mla_down")[0]
    qlat, ckv, kr = mla_mid(lay, dn, p['qnorm_w'], p['kvnorm_w'], cos, sin)
    tm_q = lay.tile(1024)
    qn = matmul(qlat, [p['w_uq_nope']], _epi_qscale, [BF16], tm=tm_q, tn=1024, name="mla_qn")[0]
    tw = cos2.shape[1]
    extras = ((cos2, pl.BlockSpec((tm_q, tw), lambda i, j: (i, 0))),
              (sin2, pl.BlockSpec((tm_q, tw), lambda i, j: (i, 0))))
    qr = matmul(qlat, [p['w_uq_rope'], p['w_uq_rope_sw']], _epi_rope, [BF16], tm=tm_q,
                tn=p['w_uq_rope'].shape[1], extras=extras, name="mla_qr")[0]
    nc, past = lay.nc, cache_ckv.shape[1]
    ckv_b, kr_b = ckv.astype(BF16), kr.astype(BF16)
    kn_c, v_c = matmul(ckv_b[:nc], [p['w_ukn'], p['w_uv']], _epi_plain, [BF16, BF16],
                       tm=lay.tile(512), tn=512, name="mla_kv_ctx")
    o_c = attention(qn, qr, kn_c, kr_b[:nc], v_c, n_seq=lay.n_ctx, q_len=lay.ctx_len,
                    k_len=lay.ctx_len, row0=0, tq=min(256, lay.ctx_len))
    k_len = lay.lat_len + past
    ckv_l = jnp.concatenate([ckv_b[nc:].reshape(lay.n_lat, lay.lat_len, -1), cache_ckv.astype(BF16)],
                            axis=1).reshape(lay.n_lat * k_len, -1)
    kr_l = jnp.concatenate([kr_b[nc:].reshape(lay.n_lat, lay.lat_len, -1), cache_kr.astype(BF16)],
                           axis=1).reshape(lay.n_lat * k_len, -1)
    bf16_rows = 2 * SUBLANES_V7X
    tk = k_len // 2 if (k_len // 2) % bf16_rows == 0 else math.gcd(k_len, 512)
    kn_l, v_l = matmul(ckv_l, [p['w_ukn'], p['w_uv']], _epi_plain, [BF16, BF16], tm=tk, tn=1024,
                       name="mla_kv_lat")
    o_l = attention(qn, qr, kn_l, kr_l, v_l, n_seq=lay.n_lat, q_len=lay.lat_len, k_len=k_len,
                    row0=nc, tq=min(256, lay.lat_len))
    o = jnp.concatenate([o_c, o_l], axis=0)
    x = matmul_gated_residual(lay, o, p['wo'], x, g1, tm=lay.tile(1024), tn=1024, name="mla_o")
    return x, ckv[:nc], kr[:nc]


def _rope_tables(lay):
    t = lay.lat_len
    rows = t // GRID_W
    rr = jnp.broadcast_to(jnp.arange(rows, dtype=F32)[:, None], (rows, GRID_W)).reshape(-1)
    cc = jnp.broadcast_to(jnp.arange(GRID_W, dtype=F32)[None, :], (rows, GRID_W)).reshape(-1)
    nf = ML_ROPE // 4
    inv = ROPE_BASE ** (-jnp.arange(nf, dtype=F32) / nf)
    ar, ac = rr[:, None] * inv, cc[:, None] * inv
    cos = jnp.concatenate([jnp.cos(ar), jnp.cos(ar), jnp.cos(ac), jnp.cos(ac)], axis=-1)
    sin = jnp.concatenate([-jnp.sin(ar), jnp.sin(ar), -jnp.sin(ac), jnp.sin(ac)], axis=-1)
    cos = jnp.concatenate([jnp.ones((lay.nc, ML_ROPE), F32), jnp.tile(cos, (lay.n_lat, 1))], axis=0)
    sin = jnp.concatenate([jnp.zeros((lay.nc, ML_ROPE), F32), jnp.tile(sin, (lay.n_lat, 1))], axis=0)
    return cos, sin


def _swap_cols(w):
    k, c = w.shape
    w4 = w.reshape(k, c // 32, 2, 16)
    return w4[:, :, ::-1, :].reshape(k, c)


def ffn(lay, x, mods, nw, w_a, w_b, w_out):
    sh2, sc2, g2 = mods[3], mods[4], mods[5]
    h = norm_mod(lay, x, nw, sc2, sh2, BF16)
    act = matmul(h, [w_a, w_b], _epi_swiglu, [BF16], tm=lay.tile(1024), tn=512, name="ffn_in")[0]
    return matmul_gated_residual(lay, act, w_out, x, g2, tm=lay.tile(1024), tn=512, name="ffn_out")


def kernel(x_prompt, x_sample, state_rwkv, state_hgrn, cache_ckv, cache_krope, c, c_ctx, ada_w, ada_b, norm1_w, norm2_w, ffn_w_in, ffn_w_out, final_norm_w, rw_mu, rw_wr, rw_wk, rw_wv, rw_wo, rw_w0, rw_w1, rw_w2, rw_a0, rw_a1, rw_a2, rw_g1, rw_g2, rw_kk, rw_ka, rw_rk, rw_lnx_w, rw_lnx_b, hg_w_in, hg_lb, hg_norm_w, hg_wo, ml_w_down, ml_qnorm_w, ml_kvnorm_w, ml_w_uq, ml_w_ukv, ml_wo):
    n_ctx, ctx_len, d = x_prompt.shape
    n_lat, lat_len, _ = x_sample.shape
    depth = ada_w.shape[0]
    lay = Layout(n_ctx, ctx_len, n_lat, lat_len)
    d_ff = ffn_w_out.shape[1]
    x = jnp.concatenate([x_prompt.reshape(lay.nc, d), x_sample.reshape(n_lat * lat_len, d)], axis=0)

    n_cond = -(-(1 + n_lat) // SUBLANES_V7X) * SUBLANES_V7X
    cond = jnp.zeros((n_cond, d), F32).at[0].set(c_ctx).at[1:1 + n_lat].set(c)
    mod_all = adaln(cond, ada_w, ada_b)
    mod_all = mod_all.reshape(depth, n_cond, 6, 1, d).transpose(0, 2, 1, 3, 4)

    lb_table = jnp.cumsum(jax.nn.softmax(hg_lb.astype(F32), axis=0), axis=0)
    lb_table = lb_table - lb_table[0]
    cos, sin = _rope_tables(lay)
    cos2, sin2 = jnp.tile(cos, (1, 2)), jnp.tile(sin, (1, 2))
    bf = lambda t: t.astype(BF16)

    new_rwkv, new_hgrn, new_ckv, new_krope = [], [], [], []
    for l in range(depth):
        kind, j = l % 3, l // 3
        mods = mod_all[l]
        if kind == 0:
            pad1 = lambda w: jnp.pad(w, ((0, 0), (0, 0), (0, LORA_PAD - w.shape[2])))
            pad2 = lambda w: jnp.pad(w, ((0, 0), (0, LORA_PAD - w.shape[1]), (0, 0)))
            p = {'mu': rw_mu[j], 'wr': bf(rw_wr[j]), 'wk': bf(rw_wk[j]), 'wv': bf(rw_wv[j]),
                 'wo': bf(rw_wo[j]),
                 'w0': rw_w0[j].reshape(2, 1, d), 'w1': bf(pad1(rw_w1[j])), 'w2': bf(pad2(rw_w2[j])),
                 'a0': rw_a0[j].reshape(2, 1, d), 'a1': bf(pad1(rw_a1[j])), 'a2': bf(pad2(rw_a2[j])),
                 'g1': bf(rw_g1[j]), 'g2': bf(rw_g2[j]),
                 'kk': rw_kk[j].reshape(1, d), 'ka': rw_ka[j].reshape(1, d),
                 'rk': rw_rk[j].reshape(2, 1, d),
                 'lnx_w': rw_lnx_w[j].reshape(1, d), 'lnx_b': rw_lnx_b[j].reshape(1, d)}
            rw_h = state_rwkv.shape[3]
            s_lat = state_rwkv[:, j].astype(F32).reshape(n_lat, 2, rw_h // RW_GROUP, RW_GROUP, RW_HEAD, RW_HEAD)
            s0 = jnp.concatenate([jnp.zeros((1,) + s_lat.shape[1:], F32), s_lat], axis=0)
            x, sfin = rwkv_layer(lay, x, mods, norm1_w[l], p, s0)
            new_rwkv.append(sfin[:n_ctx].reshape(n_ctx, 2, rw_h, RW_HEAD, RW_HEAD))
        elif kind == 1:
            hk = d
            w_in = hg_w_in[j]
            p = {'w_in': [bf(w_in[:, i * hk:(i + 1) * hk]) for i in range(5)],
                 'norm_w': hg_norm_w[j].reshape(1, HG_K), 'wo': bf(hg_wo[j])}
            s_lat = jnp.swapaxes(state_hgrn[:, j].astype(F32), -1, -2)
            s0t = jnp.concatenate([jnp.zeros((1,) + s_lat.shape[1:], F32), s_lat], axis=0)
            x, sfin = hgrn_layer(lay, x, mods, norm1_w[l], p, lb_table[l].reshape(1, d), s0t)
            new_hgrn.append(jnp.swapaxes(sfin[:n_ctx], -1, -2))
        else:
            wd = ml_w_down[j]
            kr_w = wd[:, ML_KR_OFF:]
            zpad = jnp.zeros((d, LANES_V7X - ML_ROPE), wd.dtype)
            w_down = jnp.concatenate([wd, zpad, _swap_cols(kr_w), zpad], axis=1)
            wq = ml_w_uq[j].reshape(ML_Q_LORA, ML_H, ML_NOPE + ML_ROPE)
            wq_n = wq[:, :, :ML_NOPE].reshape(ML_Q_LORA, ML_H * ML_NOPE)
            wq_r = wq[:, :, ML_NOPE:].reshape(ML_Q_LORA, ML_H * ML_ROPE)
            wkv = ml_w_ukv[j].reshape(ML_KV_LORA, ML_H, ML_NOPE + ML_V)
            p = {'w_down': bf(w_down), 'qnorm_w': ml_qnorm_w[j].reshape(1, -1),
                 'kvnorm_w': ml_kvnorm_w[j].reshape(1, -1),
                 'w_uq_nope': bf(wq_n), 'w_uq_rope': bf(wq_r), 'w_uq_rope_sw': bf(_swap_cols(wq_r)),
                 'w_ukn': bf(wkv[:, :, :ML_NOPE].reshape(ML_KV_LORA, ML_H * ML_NOPE)),
                 'w_uv': bf(wkv[:, :, ML_NOPE:].reshape(ML_KV_LORA, ML_H * ML_V)),
                 'wo': bf(ml_wo[j])}
            x, ckv_c, kr_c = mla_layer(lay, x, mods, norm1_w[l], p, cache_ckv[:, j], cache_krope[:, j],
                                       cos, sin, cos2, sin2)
            new_ckv.append(ckv_c.reshape(n_ctx, ctx_len, ML_KV_LORA))
            new_krope.append(kr_c.reshape(n_ctx, ctx_len, ML_ROPE))
        w_in = ffn_w_in[l]
        x = ffn(lay, x, mods, norm2_w[l], bf(w_in[:, :d_ff]), bf(w_in[:, d_ff:]), bf(ffn_w_out[l]))

    y_prompt = rmsnorm_rows(lay, x, final_norm_w, 0, lay.nc).reshape(n_ctx, ctx_len, d)
    y_sample = rmsnorm_rows(lay, x, final_norm_w, lay.nc, lay.n - lay.nc).reshape(n_lat, lat_len, d)
    return (y_prompt, y_sample, jnp.stack(new_rwkv, axis=1), jnp.stack(new_hgrn, axis=1),
            jnp.stack(new_ckv, axis=1), jnp.stack(new_krope, axis=1))
```
